```python
import math
import jax, jax.numpy as jnp
from jax import lax
import numpy as np

D_MODEL = 1024
BATCH = 4
SEQ = 4096
DEPTH = 1
DEC_BATCH = 32
DEC_SEQ = 1
PAST_LEN = 8192
PAGE_SIZE = 128

SSM_W = D_MODEL // 2
SSM_GROUP = 16
SSM_GROUPS = SSM_W // SSM_GROUP
SSM_STATE = 64
HEAD_DIM = 64
N_HEADS = (D_MODEL - SSM_W) // HEAD_DIM
KV_HEADS = 2
Q_PER_KV = N_HEADS // KV_HEADS
NSA_W = N_HEADS * HEAD_DIM
MIX_W = SSM_W + NSA_W
KV_W = KV_HEADS * HEAD_DIM
CMP_BLOCK = 32
CMP_STRIDE = 16
CMP_HID = 2 * HEAD_DIM
SLC_BLOCK = 64
TOP_N = 16
N_LOCAL_BLOCKS = 2
WINDOW = 512
Q_BLOCK = 128
ROPE_THETA = 500000.0
ROPE_DIM = HEAD_DIM // 4
RMS_EPS = 1e-6
NEG_INF = -1e30
IN_W = 2 * SSM_W + 2 * NSA_W + 6 * KV_W + 3 * N_HEADS
IN_SPLITS = (SSM_W, 2 * SSM_W, 2 * SSM_W + NSA_W, 2 * SSM_W + 2 * NSA_W, 2 * SSM_W + 2 * NSA_W + 6 * KV_W)

kernel_name = 'hymba_s5_nsa_decode_step'


def rms_norm(x, w):
    xf = x.astype(jnp.float32)
    y = xf * lax.rsqrt(jnp.mean(xf * xf, axis=-1, keepdims=True) + RMS_EPS)
    return (y * w.astype(jnp.float32)).astype(x.dtype)


def rope(x, pos):
    half = ROPE_DIM // 2
    inv = ROPE_THETA ** (-jnp.arange(half, dtype=jnp.float32) / half)
    ang = pos.astype(jnp.float32)[:, None] * inv
    cos = jnp.cos(ang)[:, None, :]
    sin = jnp.sin(ang)[:, None, :]
    xf = x.astype(jnp.float32)
    x1, x2, rest = xf[..., :half], xf[..., half:ROPE_DIM], xf[..., ROPE_DIM:]
    out = jnp.concatenate([x1 * cos - x2 * sin, x2 * cos + x1 * sin, rest], axis=-1)
    return out.astype(x.dtype)


def mixer_inputs(x, pos, norm_w, w_in, gate_b, q_norm_w, k_norm_w):
    B, S, _ = x.shape
    h = rms_norm(x, norm_w)
    z = jnp.einsum('bsd,de->bse', h, w_in)
    u, g_ssm, q, g_nsa, kv, gl = jnp.split(z, IN_SPLITS, axis=-1)
    q = rope(rms_norm(q.reshape(B, S, N_HEADS, HEAD_DIM), q_norm_w), pos)
    q = q.reshape(B, S, KV_HEADS, Q_PER_KV, HEAD_DIM)
    kv = kv.reshape(B, S, 6, KV_HEADS, HEAD_DIM)
    ks = rms_norm(kv[:, :, 0::2], k_norm_w[:, None, :])
    ks = rope(ks.reshape(B, S, 3 * KV_HEADS, HEAD_DIM), pos).reshape(B, S, 3, KV_HEADS, HEAD_DIM)
    vs = kv[:, :, 1::2]
    kv_rows = jnp.stack([ks[:, :, 0], vs[:, :, 0], ks[:, :, 1], vs[:, :, 1]], axis=2)
    win_rows = jnp.stack([ks[:, :, 2], vs[:, :, 2]], axis=2)
    gates = jax.nn.sigmoid(gl.reshape(B, S, N_HEADS, 3) + gate_b).reshape(B, S, KV_HEADS, Q_PER_KV, 3)
    return u, g_ssm, q, g_nsa, kv_rows, win_rows, gates


def s5_scan(u, h0_re, h0_im, lam_re, lam_im, log_step, b_re, b_im, c_re, c_im, d_skip):
    Bsz, S, _ = u.shape
    f32 = jnp.float32
    uf = u.astype(f32).reshape(Bsz, S, SSM_GROUPS, SSM_GROUP)
    dt = jnp.exp(log_step.astype(f32))[:, None]
    lr, li = lam_re.astype(f32), lam_im.astype(f32)
    mag, ang = jnp.exp(lr * dt), li * dt
    a_re, a_im = mag * jnp.cos(ang), mag * jnp.sin(ang)
    den = lr * lr + li * li
    nr, ni = a_re - 1.0, a_im
    f_re, f_im = (nr * lr + ni * li) / den, (ni * lr - nr * li) / den
    br, bi = b_re.astype(f32), b_im.astype(f32)
    bb_re = f_re[..., None] * br - f_im[..., None] * bi
    bb_im = f_re[..., None] * bi + f_im[..., None] * br
    x_re = jnp.einsum('bsgc,gnc->bsgn', uf, bb_re)
    x_im = jnp.einsum('bsgc,gnc->bsgn', uf, bb_im)
    x_re = x_re.at[:, 0].add(a_re * h0_re - a_im * h0_im)
    x_im = x_im.at[:, 0].add(a_re * h0_im + a_im * h0_re)
    A_re = jnp.broadcast_to(a_re, x_re.shape)
    A_im = jnp.broadcast_to(a_im, x_im.shape)

    def combine(l, r):
        ar1, ai1, xr1, xi1 = l
        ar2, ai2, xr2, xi2 = r
        return (ar2 * ar1 - ai2 * ai1, ar2 * ai1 + ai2 * ar1,
                ar2 * xr1 - ai2 * xi1 + xr2, ar2 * xi1 + ai2 * xr1 + xi2)

    _, _, h_re, h_im = lax.associative_scan(combine, (A_re, A_im, x_re, x_im), axis=1)
    y = (jnp.einsum('gcn,bsgn->bsgc', c_re.astype(f32), h_re)
         - jnp.einsum('gcn,bsgn->bsgc', c_im.astype(f32), h_im)
         + d_skip.astype(f32).reshape(SSM_GROUPS, SSM_GROUP) * uf)
    return y.reshape(Bsz, S, SSM_W).astype(u.dtype), h_re[:, -1], h_im[:, -1]


def compress(rows, pe, w1, b1, w2):
    B, L, G, HD = rows.shape
    ch = rows.reshape(B, L // CMP_STRIDE, CMP_STRIDE, G, HD)
    pe = pe.reshape(2, CMP_STRIDE, 1, HD)
    w1 = w1.reshape(2, CMP_STRIDE, HD, CMP_HID)
    pa = jnp.einsum('bnjgd,jdh->bngh', ch + pe[0], w1[0])
    pb = jnp.einsum('bnjgd,jdh->bngh', ch + pe[1], w1[1])
    hid = jax.nn.gelu(pa[:, :-1] + pb[:, 1:] + b1)
    return jnp.einsum('bngh,hd->bngd', hid, w2)


def nsa_context(kv_rows, cmp_pe, cmp_w1, cmp_b1, cmp_w2):
    B, L = kv_rows.shape[:2]
    Lp = -(-L // SLC_BLOCK) * SLC_BLOCK
    kv = jnp.pad(kv_rows, ((0, 0), (0, Lp - L), (0, 0), (0, 0), (0, 0)))
    kc = compress(kv[:, :, 0], cmp_pe[0], cmp_w1[0], cmp_b1[0], cmp_w2[0])
    vc = compress(kv[:, :, 1], cmp_pe[1], cmp_w1[1], cmp_b1[1], cmp_w2[1])
    sl = kv[:, :, 2:4].reshape(B, Lp // SLC_BLOCK, SLC_BLOCK, 2, KV_HEADS, HEAD_DIM)
    sl = sl.transpose(3, 0, 4, 1, 2, 5)
    return kc, vc, sl[0], sl[1]


def nsa_query_block(q, qpos, kc, vc, ksb, vsb, kw, vw, kwpos, gates):
    f32 = jnp.float32
    B, Q = q.shape[:2]
    C, NB = kc.shape[1], ksb.shape[2]
    scale = HEAD_DIM ** -0.5
    c_start = jnp.arange(C) * CMP_STRIDE
    cmask = (c_start + CMP_BLOCK - 1)[None, :] <= qpos[:, None]
    s = jnp.einsum('bqgrd,bcgd->bqgrc', q, kc, preferred_element_type=f32) * scale
    p_cmp = jax.nn.softmax(jnp.where(cmask[None, :, None, None, :], s, NEG_INF), axis=-1)
    p_cmp = jnp.where(jnp.any(cmask, axis=-1)[None, :, None, None, None], p_cmp, 0.0)
    o_cmp = jnp.einsum('bqgrc,bcgd->bqgrd', p_cmp.astype(vc.dtype), vc)
    blk = jnp.arange(NB)
    overlap = ((c_start[:, None] < (blk[None, :] + 1) * SLC_BLOCK)
               & (c_start[:, None] + CMP_BLOCK > blk[None, :] * SLC_BLOCK)).astype(f32)
    imp = jnp.einsum('bqgrc,cn->bqgn', p_cmp, overlap)
    q_blk = qpos // SLC_BLOCK
    causal = blk[None, :] <= q_blk[:, None]
    forced = (blk[None, :] == 0) | (blk[None, :] >= q_blk[:, None] - (N_LOCAL_BLOCKS - 1))
    imp = jnp.where(causal[None, :, None, :], jnp.where(forced[None, :, None, :], jnp.inf, imp), -jnp.inf)
    _, idx = lax.top_k(imp, min(TOP_N, NB))
    bi = jnp.arange(B)[:, None, None, None]
    gi = jnp.arange(KV_HEADS)[None, None, :, None]
    kg = ksb[bi, gi, idx]
    vg = vsb[bi, gi, idx]
    kpos = idx[..., None] * SLC_BLOCK + jnp.arange(SLC_BLOCK)
    smask = kpos <= qpos[None, :, None, None, None]
    s = jnp.einsum('bqgrd,bqgkjd->bqgrkj', q, kg, preferred_element_type=f32) * scale
    s = jnp.where(smask[:, :, :, None], s, NEG_INF)
    p = jax.nn.softmax(s.reshape(s.shape[:4] + (-1,)), axis=-1).reshape(s.shape)
    o_slc = jnp.einsum('bqgrkj,bqgkjd->bqgrd', p.astype(vg.dtype), vg)
    wmask = ((kwpos[None, :] <= qpos[:, None]) & (kwpos[None, :] > qpos[:, None] - WINDOW)
             & (kwpos[None, :] >= 0))
    s = jnp.einsum('bqgrd,bwgd->bqgrw', q, kw, preferred_element_type=f32) * scale
    p = jax.nn.softmax(jnp.where(wmask[None, :, None, None, :], s, NEG_INF), axis=-1)
    o_win = jnp.einsum('bqgrw,bwgd->bqgrd', p.astype(vw.dtype), vw)
    return gates[..., 0:1] * o_cmp + gates[..., 1:2] * o_slc + gates[..., 2:3] * o_win


def nsa_prompt(q, kv_rows, win_rows, gates, cmp_pe, cmp_w1, cmp_b1, cmp_w2):
    B, S = q.shape[:2]
    kc, vc, ksb, vsb = nsa_context(kv_rows, cmp_pe, cmp_w1, cmp_b1, cmp_w2)
    wpad = jnp.pad(win_rows, ((0, 0), (WINDOW, 0), (0, 0), (0, 0), (0, 0)))
    nqb = S // Q_BLOCK
    qb = q.reshape(B, nqb, Q_BLOCK, KV_HEADS, Q_PER_KV, HEAD_DIM).swapaxes(0, 1)
    gb = gates.reshape(B, nqb, Q_BLOCK, KV_HEADS, Q_PER_KV, 3).swapaxes(0, 1)

    def block(args):
        q_i, g_i, s0 = args
        qpos = s0 + jnp.arange(Q_BLOCK)
        w = lax.dynamic_slice_in_dim(wpad, s0, WINDOW + Q_BLOCK, axis=1)
        kwpos = s0 - WINDOW + jnp.arange(WINDOW + Q_BLOCK)
        return nsa_query_block(q_i, qpos, kc, vc, ksb, vsb, w[:, :, 0], w[:, :, 1], kwpos, g_i)

    o = lax.map(block, (qb, gb, jnp.arange(nqb) * Q_BLOCK))
    return o.swapaxes(0, 1).reshape(B, S, NSA_W)


def mixer_output(x, y_ssm, g_ssm, o_nsa, g_nsa, w_glu, w_out):
    a, b = jnp.split(jnp.einsum('bsc,ce->bse', y_ssm, w_glu), 2, axis=-1)
    ssm_out = a * jax.nn.sigmoid(b) * jax.nn.silu(g_ssm)
    nsa_out = o_nsa * jax.nn.silu(g_nsa)
    mix = jnp.concatenate([ssm_out, nsa_out], axis=-1)
    return x + jnp.einsum('bsc,cd->bsd', mix, w_out)


def setup_inputs(seed: int = 0) -> dict:
    key = jax.random.key(seed)
    ks = jax.random.split(key, 32)
    f32 = jnp.float32
    n_pages = PAST_LEN // PAGE_SIZE
    n_used = DEC_BATCH * n_pages
    n_phys = n_used + max(1, n_used // 4)
    win_buf = min(WINDOW, PAST_LEN)

    def nrm(k, shape, s=1.0):
        return s * jax.random.normal(k, shape, f32)

    n_idx = jnp.arange(SSM_STATE, dtype=f32)
    return {
        'x_prompt': nrm(ks[0], (BATCH, SEQ, D_MODEL)),
        'x_sample': nrm(ks[1], (DEC_BATCH, DEC_SEQ, D_MODEL)),
        'cache_kv': nrm(ks[2], (DEPTH, n_phys, PAGE_SIZE, 4, KV_HEADS, HEAD_DIM)),
        'cache_win': nrm(ks[3], (DEPTH, DEC_BATCH, win_buf, 2, KV_HEADS, HEAD_DIM)),
        'state_ssm_re': nrm(ks[4], (DEPTH, DEC_BATCH, SSM_GROUPS, SSM_STATE), 0.5),
        'state_ssm_im': nrm(ks[5], (DEPTH, DEC_BATCH, SSM_GROUPS, SSM_STATE), 0.5),
        'page_table': jax.random.permutation(ks[6], n_phys)[:n_used].reshape(DEC_BATCH, n_pages).astype(jnp.int32),
        'norm_w': 1.0 + nrm(ks[7], (DEPTH, D_MODEL), 0.01),
        'w_in': nrm(ks[8], (DEPTH, D_MODEL, IN_W), D_MODEL ** -0.5),
        'gate_b': nrm(ks[9], (DEPTH, N_HEADS, 3), 0.1),
        'q_norm_w': 1.0 + nrm(ks[10], (DEPTH, HEAD_DIM), 0.01),
        'k_norm_w': 1.0 + nrm(ks[11], (DEPTH, 3, HEAD_DIM), 0.01),
        'cmp_pe': nrm(ks[12], (DEPTH, 2, CMP_BLOCK, HEAD_DIM), 0.1),
        'cmp_w1': nrm(ks[13], (DEPTH, 2, CMP_BLOCK * HEAD_DIM, CMP_HID), (CMP_BLOCK * HEAD_DIM) ** -0.5),
        'cmp_b1': nrm(ks[14], (DEPTH, 2, CMP_HID), 0.01),
        'cmp_w2': nrm(ks[15], (DEPTH, 2, CMP_HID, HEAD_DIM), CMP_HID ** -0.5),
        'ssm_lam_re': -0.5 + nrm(ks[16], (DEPTH, SSM_GROUPS, SSM_STATE), 0.01),
        'ssm_lam_im': math.pi * n_idx + nrm(ks[17], (DEPTH, SSM_GROUPS, SSM_STATE), 0.01),
        'ssm_log_step': jax.random.uniform(ks[18], (DEPTH, SSM_GROUPS), f32, math.log(1e-3), math.log(1e-1)),
        'ssm_b_re': nrm(ks[19], (DEPTH, SSM_GROUPS, SSM_STATE, SSM_GROUP), (2 * SSM_GROUP) ** -0.5),
        'ssm_b_im': nrm(ks[20], (DEPTH, SSM_GROUPS, SSM_STATE, SSM_GROUP), (2 * SSM_GROUP) ** -0.5),
        'ssm_c_re': nrm(ks[21], (DEPTH, SSM_GROUPS, SSM_GROUP, SSM_STATE), SSM_STATE ** -0.5),
        'ssm_c_im': nrm(ks[22], (DEPTH, SSM_GROUPS, SSM_GROUP, SSM_STATE), SSM_STATE ** -0.5),
        'ssm_d': nrm(ks[23], (DEPTH, SSM_W), 1.0),
        'w_glu': nrm(ks[24], (DEPTH, SSM_W, 2 * SSM_W), SSM_W ** -0.5),
        'w_out': nrm(ks[25], (DEPTH, MIX_W, D_MODEL), MIX_W ** -0.5),
    }


def reference(x_prompt, x_sample, cache_kv, cache_win, state_ssm_re, state_ssm_im, page_table,
              norm_w, w_in, gate_b, q_norm_w, k_norm_w, cmp_pe, cmp_w1, cmp_b1, cmp_w2,
              ssm_lam_re, ssm_lam_im, ssm_log_step, ssm_b_re, ssm_b_im, ssm_c_re, ssm_c_im, ssm_d,
              w_glu, w_out):
    b_p, s_p = x_prompt.shape[:2]
    b_s, s_s = x_sample.shape[:2]
    past_len = page_table.shape[1] * cache_kv.shape[2]
    win_buf = cache_win.shape[2]
    pos_p = jnp.arange(s_p)
    pos_s = past_len + jnp.arange(s_s)
    kwpos_s = past_len - win_buf + jnp.arange(win_buf + s_s)
    h_p, h_s = x_prompt, x_sample
    kv_p_l, kv_s_l, win_p_l, win_s_l = [], [], [], []
    sre_p_l, sim_p_l, sre_s_l, sim_s_l = [], [], [], []
    for l in range(DEPTH):
        ssm_p = (ssm_lam_re[l], ssm_lam_im[l], ssm_log_step[l], ssm_b_re[l], ssm_b_im[l],
                 ssm_c_re[l], ssm_c_im[l], ssm_d[l])
        cmp_p = (cmp_pe[l], cmp_w1[l], cmp_b1[l], cmp_w2[l])
        proj = (norm_w[l], w_in[l], gate_b[l], q_norm_w[l], k_norm_w[l])
        u, g_ssm, q, g_nsa, kv_rows, win_rows, gates = mixer_inputs(h_p, pos_p, *proj)
        zeros = jnp.zeros((b_p, SSM_GROUPS, SSM_STATE), jnp.float32)
        y_ssm, hr_p, hi_p = s5_scan(u, zeros, zeros, *ssm_p)
        o_nsa = nsa_prompt(q, kv_rows, win_rows, gates, *cmp_p)
        h_p = mixer_output(h_p, y_ssm, g_ssm, o_nsa, g_nsa, w_glu[l], w_out[l])
        kv_p_l.append(kv_rows)
        win_p_l.append(win_rows[:, s_p - min(WINDOW, s_p):])
        sre_p_l.append(hr_p)
        sim_p_l.append(hi_p)
        u, g_ssm, q, g_nsa, kv_rows, win_rows, gates = mixer_inputs(h_s, pos_s, *proj)
        y_ssm, hr_s, hi_s = s5_scan(u, state_ssm_re[l], state_ssm_im[l], *ssm_p)
        past = cache_kv[l][page_table].reshape(b_s, past_len, 4, KV_HEADS, HEAD_DIM)
        kc, vc, ksb, vsb = nsa_context(jnp.concatenate([past, kv_rows], axis=1), *cmp_p)
        wrows = jnp.concatenate([cache_win[l], win_rows], axis=1)
        o_nsa = nsa_query_block(q, pos_s, kc, vc, ksb, vsb, wrows[:, :, 0], wrows[:, :, 1], kwpos_s, gates)
        h_s = mixer_output(h_s, y_ssm, g_ssm, o_nsa.reshape(b_s, s_s, NSA_W), g_nsa, w_glu[l], w_out[l])
        kv_s_l.append(kv_rows)
        win_s_l.append(wrows[:, wrows.shape[1] - min(WINDOW, wrows.shape[1]):])
        sre_s_l.append(hr_s)
        sim_s_l.append(hi_s)
    return (h_p, h_s, jnp.stack(kv_p_l), jnp.stack(kv_s_l), jnp.stack(win_p_l), jnp.stack(win_s_l),
            jnp.stack(sre_p_l), jnp.stack(sim_p_l), jnp.stack(sre_s_l), jnp.stack(sim_s_l))
```

```python
import functools
import math

import jax
import jax.numpy as jnp
from jax import lax
from jax.experimental import pallas as pl
from jax.experimental.pallas import tpu as pltpu

F32 = jnp.float32
BF16 = jnp.bfloat16
I32 = jnp.int32

LANES = 128
SUBLANES = 8
VMEM_LIMIT_BYTES = 56 * 1024 * 1024

HEAD_DIM = 64
N_HEADS = 8
KV_HEADS = 2
Q_PER_KV = N_HEADS // KV_HEADS
SSM_W = 512
SSM_GROUP = 16
SSM_STATE = 64
NSA_W = N_HEADS * HEAD_DIM
CMP_BLOCK = 32
CMP_STRIDE = 16
CMP_HID = 2 * HEAD_DIM
SLC_BLOCK = 64
TOP_N = 16
N_LOCAL_BLOCKS = 2
WINDOW = 512
ROPE_THETA = 500000.0
ROPE_DIM = HEAD_DIM // 4
RMS_EPS = 1e-6
NEG_INF = -1e30

COL_U = 0
COL_GS = SSM_W
COL_Q = 2 * SSM_W
COL_GN = 2 * SSM_W + NSA_W
COL_KV = 2 * SSM_W + 2 * NSA_W
COL_GL = COL_KV + 6 * KV_HEADS * HEAD_DIM
IN_W_PAD = COL_GL + LANES

SSM_CHUNK = 16
SSM_SLAB_GROUPS = LANES // SSM_GROUP
SSM_SLABS = SSM_W // LANES


def _cparams(sem):
    return pltpu.CompilerParams(dimension_semantics=sem, vmem_limit_bytes=VMEM_LIMIT_BYTES)


def _sigmoid(x):
    return 1.0 / (1.0 + jnp.exp(-x))


def _dot(a, b):
    return jnp.dot(a, b, preferred_element_type=F32)


def _dot_nt(a, b):
    return lax.dot_general(a, b, (((1,), (1,)), ((), ())), preferred_element_type=F32)


def _proj_body(x_ref, nw_ref, w_ref, qnw_ref, knw_ref, gb_ref, ra_ref, rb_ref, rc_ref,
               u_ref, gs_ref, gn_ref, q_ref, kv_ref, win_ref, gate_ref,
               ks_ref, vs_ref, kw_ref, vw_ref, *, tm, tiles_per_seq):
    x = x_ref[...]
    ms = jnp.mean(x * x, axis=-1, keepdims=True)
    h = (x * lax.rsqrt(ms + RMS_EPS) * nw_ref[...]).astype(BF16)

    def mm(c0, c1):
        return _dot(h, w_ref[:, c0:c1])

    zu = mm(COL_U, COL_GS)
    for j in range(SSM_SLABS):
        u_ref[j] = zu[:, j * LANES:(j + 1) * LANES]
    gs_ref[...] = mm(COL_GS, COL_Q)
    gn_ref[...] = mm(COL_GN, COL_KV)

    lane = lax.broadcasted_iota(I32, (tm, LANES), 1)
    lo = lane < HEAD_DIM
    ra = ra_ref[...]
    rb = rb_ref[...]
    rc = rc_ref[...]

    def norm_rope(s, wrow):
        s2 = s * s
        slo = jnp.sum(jnp.where(lo, s2, 0.0), axis=-1, keepdims=True)
        shi = jnp.sum(jnp.where(lo, 0.0, s2), axis=-1, keepdims=True)
        msq = jnp.where(lo, slo, shi) * (1.0 / HEAD_DIM)
        y = s * lax.rsqrt(msq + RMS_EPS) * wrow
        half = ROPE_DIM // 2
        return y * ra + pltpu.roll(y, LANES - half, 1) * rb + pltpu.roll(y, half, 1) * rc

    def hi_half(y, head):
        src = pltpu.roll(y, HEAD_DIM, 1) if head == 0 else y
        return jnp.where(lo, 0.0, src)

    def lo_half(y, head):
        return y if head == 0 else pltpu.roll(y, HEAD_DIM, 1)

    zq = mm(COL_Q, COL_GN)
    qnw = qnw_ref[...]
    scale = HEAD_DIM ** -0.5
    for j in range(N_HEADS // 2):
        y = norm_rope(zq[:, j * LANES:(j + 1) * LANES], qnw) * scale
        q_ref[2 * j] = hi_half(y, 0).astype(BF16)
        q_ref[2 * j + 1] = hi_half(y, 1).astype(BF16)

    zkv = mm(COL_KV, COL_GL)
    kc = norm_rope(zkv[:, 0:LANES], knw_ref[0:1, :])
    vc = zkv[:, LANES:2 * LANES]
    ks = norm_rope(zkv[:, 2 * LANES:3 * LANES], knw_ref[1:2, :])
    vs = zkv[:, 3 * LANES:4 * LANES]
    kw = norm_rope(zkv[:, 4 * LANES:5 * LANES], knw_ref[2:3, :])
    vw = zkv[:, 5 * LANES:6 * LANES]
    kv_ref[:, 0:LANES] = kc
    kv_ref[:, LANES:2 * LANES] = vc
    kv_ref[:, 2 * LANES:3 * LANES] = ks
    kv_ref[:, 3 * LANES:4 * LANES] = vs
    win_ref[:, 0:LANES] = kw
    win_ref[:, LANES:2 * LANES] = vw

    row = lax.broadcasted_iota(I32, (tm, LANES), 0)
    pos = (pl.program_id(0) % tiles_per_seq) * tm + row
    onehot = jnp.where(lane == lax.shift_right_logical(pos, 6), 1.0, 0.0)
    ones_col = jnp.where(lane == HEAD_DIM, 1.0, 0.0)
    for g in range(KV_HEADS):
        ks_ref[g] = jnp.where(lo, onehot, hi_half(ks, g)).astype(BF16)
        vs_ref[g] = jnp.where(lo, lo_half(vs, g), ones_col).astype(BF16)
        kw_ref[g] = hi_half(kw, g).astype(BF16)
        vw_ref[g] = jnp.where(lo, lo_half(vw, g), ones_col).astype(BF16)

    gates = _sigmoid(mm(COL_GL, IN_W_PAD) + gb_ref[...])
    for g in range(KV_HEADS):
        gate_ref[g] = gates if g == 0 else pltpu.roll(gates, LANES - 3 * Q_PER_KV * g, 1)


def _proj_call(x2d, w_pad, norm_w, qnw, knw, gb, tabs, seq_len):
    T, D = x2d.shape
    tm = min(256, T)
    assert T % tm == 0 and (seq_len % tm == 0 or seq_len <= tm)
    tiles_per_seq = max(seq_len // tm, 1)
    n_tab_tiles = tabs[0].shape[0] // tm
    row_spec = lambda w: pl.BlockSpec((tm, w), lambda i: (i, 0))
    full = lambda a: pl.BlockSpec(a.shape, lambda i: (0,) * a.ndim)
    tab_spec = pl.BlockSpec((tm, LANES), lambda i: (i % n_tab_tiles, 0))
    head_spec = lambda n: pl.BlockSpec((n, tm, LANES), lambda i: (0, i, 0))
    out_shape = (
        jax.ShapeDtypeStruct((SSM_SLABS, T, LANES), F32),
        jax.ShapeDtypeStruct((T, SSM_W), F32),
        jax.ShapeDtypeStruct((T, NSA_W), F32),
        jax.ShapeDtypeStruct((N_HEADS, T, LANES), BF16),
        jax.ShapeDtypeStruct((T, 4 * LANES), F32),
        jax.ShapeDtypeStruct((T, 2 * LANES), F32),
        jax.ShapeDtypeStruct((KV_HEADS, T, LANES), F32),
        jax.ShapeDtypeStruct((KV_HEADS, T, LANES), BF16),
        jax.ShapeDtypeStruct((KV_HEADS, T, LANES), BF16),
        jax.ShapeDtypeStruct((KV_HEADS, T, LANES), BF16),
        jax.ShapeDtypeStruct((KV_HEADS, T, LANES), BF16),
    )
    out_specs = (head_spec(SSM_SLABS), row_spec(SSM_W), row_spec(NSA_W), head_spec(N_HEADS),
                 row_spec(4 * LANES), row_spec(2 * LANES), head_spec(KV_HEADS),
                 head_spec(KV_HEADS), head_spec(KV_HEADS), head_spec(KV_HEADS), head_spec(KV_HEADS))
    return pl.pallas_call(
        functools.partial(_proj_body, tm=tm, tiles_per_seq=tiles_per_seq),
        out_shape=out_shape,
        grid=(T // tm,),
        in_specs=[row_spec(D), full(norm_w), full(w_pad), full(qnw), full(knw), full(gb),
                  tab_spec, tab_spec, tab_spec],
        out_specs=out_specs,
        compiler_params=_cparams(("parallel",)),
        name="proj",
    )(x2d, norm_w, w_pad, qnw, knw, gb, *tabs)


def _prep_params(p, l):
    w_in = p["w_in"][l]
    d_model, in_w = w_in.shape
    tile2 = lambda v: jnp.tile(v, (1, LANES // HEAD_DIM))
    return {
        "w_in": jnp.pad(w_in.astype(BF16), ((0, 0), (0, IN_W_PAD - in_w))),
        "norm_w": p["norm_w"][l].reshape(1, d_model).astype(F32),
        "qnw": tile2(p["q_norm_w"][l].reshape(1, HEAD_DIM)).astype(F32),
        "knw": tile2(p["k_norm_w"][l]).astype(F32),
        "gb": jnp.pad(p["gate_b"][l].reshape(1, -1).astype(F32), ((0, 0), (0, LANES - 3 * N_HEADS))),
        "w_glu": p["w_glu"][l].astype(BF16),
        "w_out": p["w_out"][l].astype(BF16),
    }


def _rope_tables(pos):
    half = ROPE_DIM // 2
    inv = ROPE_THETA ** (-jnp.arange(half, dtype=F32) / half)
    ang = pos.astype(F32)[:, None] * inv
    cos, sin = jnp.cos(ang), jnp.sin(ang)
    n = pos.shape[0]
    rest = HEAD_DIM - ROPE_DIM
    a = jnp.concatenate([cos, cos, jnp.ones((n, rest), F32)], axis=-1)
    b = jnp.concatenate([-sin, jnp.zeros((n, HEAD_DIM - half), F32)], axis=-1)
    c = jnp.concatenate([jnp.zeros((n, half), F32), sin, jnp.zeros((n, rest), F32)], axis=-1)
    return tuple(jnp.tile(t, (1, LANES // HEAD_DIM)) for t in (a, b, c))


def _outmix_body(x_ref, y_ref, gs_ref, o_ref, gn_ref, wg_ref, wo_ref, out_ref):
    y = jnp.concatenate([y_ref[j] for j in range(SSM_SLABS)], axis=-1)
    ab = _dot(y.astype(BF16), wg_ref[...])
    gs = gs_ref[...]
    ssm = ab[:, :SSM_W] * _sigmoid(ab[:, SSM_W:]) * (gs * _sigmoid(gs))
    gn = gn_ref[...]
    nsa = o_ref[...] * (gn * _sigmoid(gn))
    acc = _dot(ssm.astype(BF16), wo_ref[0:SSM_W, :])
    acc += _dot(nsa.astype(BF16), wo_ref[SSM_W:, :])
    out_ref[...] = x_ref[...] + acc


def _outmix_call(x2d, y_ssm, g_ssm, o_nsa, g_nsa, w_glu, w_out):
    T, D = x2d.shape
    tm = min(512, T)
    row_spec = lambda w: pl.BlockSpec((tm, w), lambda i: (i, 0))
    full = lambda a: pl.BlockSpec(a.shape, lambda i: (0,) * a.ndim)
    return pl.pallas_call(
        _outmix_body,
        out_shape=jax.ShapeDtypeStruct((T, D), F32),
        grid=(T // tm,),
        in_specs=[row_spec(D), pl.BlockSpec((SSM_SLABS, tm, LANES), lambda i: (0, i, 0)),
                  row_spec(SSM_W), row_spec(NSA_W), row_spec(NSA_W),
                  full(w_glu), full(w_out)],
        out_specs=row_spec(D),
        compiler_params=_cparams(("parallel",)),
        name="outmix",
    )(x2d, y_ssm, g_ssm, o_nsa, g_nsa, w_glu, w_out)


def _gelu_tanh(x):
    c = math.sqrt(2.0 / math.pi)
    return 0.5 * x * (1.0 + jnp.tanh(c * (x + 0.044715 * (x * x * x))))


def _compress_rows(rows_refs, pe_ref, wa_ref, wb_ref, b1_ref, w2_ref, kc_ref, vc_ref, nch):
    lane = lax.broadcasted_iota(I32, (nch, LANES), 1)
    for kvi, out_ref in ((0, kc_ref), (1, vc_ref)):
        rows_ref = rows_refs[kvi]
        pa = jnp.zeros((nch, 2 * CMP_HID), F32)
        pb = jnp.zeros((nch, 2 * CMP_HID), F32)
        for j0 in range(0, CMP_STRIDE, 2):
            xs = [rows_ref[pl.ds(j, nch, stride=CMP_STRIDE), :] for j in (j0, j0 + 1)]
            xa = jnp.concatenate([xs[i] + pe_ref[kvi, 0, j0 + i:j0 + i + 1, :] for i in range(2)], axis=-1)
            xb = jnp.concatenate([xs[i] + pe_ref[kvi, 1, j0 + i:j0 + i + 1, :] for i in range(2)], axis=-1)
            wsl = slice(j0 * LANES, (j0 + 2) * LANES)
            pa += _dot(xa.astype(BF16), wa_ref[kvi, wsl, :])
            pb += _dot(xb.astype(BF16), wb_ref[kvi, wsl, :])
        hid = _gelu_tanh(pa + pltpu.roll(pb, nch - 1, 0) + b1_ref[kvi]).astype(BF16)
        for g in range(KV_HEADS):
            o = _dot(hid, w2_ref[kvi, g])
            if kvi == 1:
                o = jnp.where(lane == HEAD_DIM, 1.0, o)
            out_ref[g] = o.astype(BF16)


def _compress_prompt_body(krows_ref, vrows_ref, pe_ref, wa_ref, wb_ref, b1_ref, w2_ref, kc_ref, vc_ref, *, nch):
    _compress_rows((krows_ref, vrows_ref), pe_ref, wa_ref, wb_ref, b1_ref, w2_ref, kc_ref, vc_ref, nch)


def _compress_prompt_call(kv_rows2d, cw, batch, seq):
    nch = seq // CMP_STRIDE
    full = lambda a: pl.BlockSpec(a.shape, lambda b: (0,) * a.ndim)
    out_spec = pl.BlockSpec((KV_HEADS, nch, LANES), lambda b: (0, b, 0))
    out_sds = jax.ShapeDtypeStruct((KV_HEADS, batch * nch, LANES), BF16)
    return pl.pallas_call(
        functools.partial(_compress_prompt_body, nch=nch),
        out_shape=(out_sds, out_sds),
        grid=(batch,),
        in_specs=[pl.BlockSpec((seq, LANES), lambda b: (b, 0)), pl.BlockSpec((seq, LANES), lambda b: (b, 1)),
                  full(cw["pe"]), full(cw["wa"]), full(cw["wb"]), full(cw["b1"]), full(cw["w2"])],
        out_specs=(out_spec, out_spec),
        compiler_params=_cparams(("parallel",)),
        name="compress_prompt",
    )(kv_rows2d, kv_rows2d, cw["pe"], cw["wa"], cw["wb"], cw["b1"], cw["w2"])


def _prep_compress(p, l):
    eye = jnp.eye(KV_HEADS, dtype=F32)
    w1 = p["cmp_w1"][l].reshape(2, 2, CMP_STRIDE, HEAD_DIM, CMP_HID)
    wexp = jnp.einsum("khjdn,ge->khjgden", w1, eye).reshape(2, 2, CMP_STRIDE * LANES, KV_HEADS * CMP_HID)
    pe = p["cmp_pe"][l].reshape(2, 2, CMP_STRIDE, HEAD_DIM)
    w2 = p["cmp_w2"][l]
    zeros = jnp.zeros_like(w2[0])
    w2k = jnp.concatenate([zeros, w2[0]], axis=-1)
    w2v = jnp.concatenate([w2[1], zeros], axis=-1)
    w2e = jnp.stack([jnp.einsum("hd,ge->gehd", w, eye).reshape(KV_HEADS, KV_HEADS * CMP_HID, LANES)
                     for w in (w2k, w2v)])
    return {
        "pe": jnp.tile(pe, (1, 1, 1, KV_HEADS)).astype(F32),
        "wa": wexp[:, 0].astype(BF16),
        "wb": wexp[:, 1].astype(BF16),
        "b1": jnp.tile(p["cmp_b1"][l].reshape(2, 1, CMP_HID), (1, 1, KV_HEADS)).astype(F32),
        "w2": w2e.astype(BF16),
    }


def _overlap_matrix(n_tok_pad, n_tok, n_blk, n_cols=LANES):
    c_start = jnp.arange(n_tok_pad) * CMP_STRIDE
    blk = jnp.arange(n_cols)
    ov = ((c_start[:, None] < (blk[None, :] + 1) * SLC_BLOCK)
          & (c_start[:, None] + CMP_BLOCK > blk[None, :] * SLC_BLOCK)
          & (jnp.arange(n_tok_pad)[:, None] < n_tok) & (blk[None, :] < n_blk))
    return ov.astype(BF16)


def _topk_select_t(imp_t, q0, tq):
    nb = imp_t.shape[0]
    n_i = lax.broadcasted_iota(I32, (nb, tq), 0)
    qblk = lax.shift_right_logical(q0 + lax.broadcasted_iota(I32, (nb, tq), 1), 6)
    causal = n_i <= qblk
    forced = (n_i == 0) | (n_i >= qblk - (N_LOCAL_BLOCKS - 1))
    w = jnp.where(causal, jnp.where(forced, jnp.inf, imp_t), -jnp.inf)
    rank = jnp.zeros((nb, tq), F32)
    for m in range(nb):
        wm = w[m:m + 1, :]
        rank += jnp.where(n_i > m, jnp.where(wm >= w, 1.0, 0.0), jnp.where(wm > w, 1.0, 0.0))
    return causal & (rank < TOP_N)


def _flash_tile(q, k, v, mask, m_ref, acc_ref):
    s = _dot_nt(q, k)
    if mask is not None:
        s = jnp.where(mask, s, NEG_INF)
    m_prev = m_ref[...]
    m_new = jnp.maximum(m_prev, jnp.max(s, axis=-1, keepdims=True))
    alpha = jnp.exp(m_prev - m_new)
    p = jnp.exp(s - m_new)
    acc_ref[...] = alpha * acc_ref[...] + _dot(p.astype(BF16), v)
    m_ref[...] = m_new


def _attn_body(q_ref, kc_ref, vc_ref, ovl_ref, ks_ref, vs_ref, kw_ref, vw_ref, gate_ref, o_ref,
               qa_ref, ocmp_ref, ms_ref, accs_ref, mw_ref, accw_ref, *, tq, ncp):
    R = Q_PER_KV
    qt = pl.program_id(2)
    q0 = qt * tq
    rows = R * tq

    c_i = lax.broadcasted_iota(I32, (tq, ncp), 1)
    qpos_c = q0 + lax.broadcasted_iota(I32, (tq, ncp), 0)
    cmask = c_i * CMP_STRIDE + (CMP_BLOCK - 1) <= qpos_c
    kc = kc_ref[...]
    vc = vc_ref[...]
    imp = jnp.zeros((tq, LANES), F32)
    for r in range(R):
        s = jnp.where(cmask, _dot_nt(q_ref[r], kc), NEG_INF)
        e = jnp.where(cmask, jnp.exp(s - jnp.max(s, axis=-1, keepdims=True)), 0.0)
        l = jnp.sum(e, axis=-1, keepdims=True)
        p = (e / jnp.where(l > 0.0, l, 1.0)).astype(BF16)
        imp += _dot(p, ovl_ref[...])
        ocmp_ref[r * tq:(r + 1) * tq, :] = _dot(p, vc)

    nbs = LANES // 2
    sel = _topk_select_t(imp.T[:nbs], q0, tq)
    bias_t = jnp.where(sel, 0.0, NEG_INF)
    bias = jnp.concatenate([bias_t, jnp.zeros((LANES - nbs, tq), F32)], axis=0).T
    for r in range(R):
        qa_ref[r * tq:(r + 1) * tq, :] = (q_ref[r].astype(F32) + bias).astype(BF16)

    ms_ref[...] = jnp.full((rows, 1), -jnp.inf, F32)
    accs_ref[...] = jnp.zeros((rows, LANES), F32)
    row_i = lax.broadcasted_iota(I32, (rows, tq), 0) & (tq - 1)
    col_i = lax.broadcasted_iota(I32, (rows, tq), 1)

    def slc_tile(j, mask):
        off = pl.multiple_of(j * tq, tq)
        _flash_tile(qa_ref[...], ks_ref[pl.ds(off, tq), :], vs_ref[pl.ds(off, tq), :], mask, ms_ref, accs_ref)

    def slc_loop(j, carry):
        slc_tile(j, None)
        return carry

    lax.fori_loop(0, qt, slc_loop, 0)
    slc_tile(qt, col_i <= row_i)

    mw_ref[...] = jnp.full((rows, 1), -jnp.inf, F32)
    accw_ref[...] = jnp.zeros((rows, LANES), F32)
    nwin = WINDOW // tq
    qs = jnp.concatenate([q_ref[r] for r in range(R)], axis=0)

    def win_tile(d, mask):
        off = pl.multiple_of((qt - d) * tq, tq)
        _flash_tile(qs, kw_ref[pl.ds(off, tq), :], vw_ref[pl.ds(off, tq), :], mask, mw_ref, accw_ref)

    for d in range(nwin, 0, -1):
        @pl.when(qt >= d)
        def _(d=d):
            win_tile(d, (col_i > row_i) if d == nwin else None)
    win_tile(0, col_i <= row_i)

    gates = gate_ref[...]
    lo = lax.broadcasted_iota(I32, (tq, LANES), 1) < HEAD_DIM
    outs = []
    for r in range(R):
        sl = slice(r * tq, (r + 1) * tq)
        a_s = accs_ref[sl, :]
        a_w = accw_ref[sl, :]
        o = (gates[:, 3 * r:3 * r + 1] * ocmp_ref[sl, :]
             + gates[:, 3 * r + 1:3 * r + 2] * (a_s / a_s[:, HEAD_DIM:HEAD_DIM + 1])
             + gates[:, 3 * r + 2:3 * r + 3] * (a_w / a_w[:, HEAD_DIM:HEAD_DIM + 1]))
        outs.append(o)
    for j in range(R // 2):
        o_ref[:, j * LANES:(j + 1) * LANES] = jnp.where(lo, outs[2 * j], pltpu.roll(outs[2 * j + 1], HEAD_DIM, 1))


def _attn_call(q, kc, vc, ovl, ks, vs, kw, vw, gates, batch, seq):
    tq = min(256, seq)
    nq = seq // tq
    ncp = kc.shape[1] // batch
    R = Q_PER_KV
    assert seq // SLC_BLOCK <= LANES // 2 and WINDOW % tq == 0
    kv_spec = pl.BlockSpec((None, seq, LANES), lambda b, g, t: (g, b, 0))
    cmp_spec = pl.BlockSpec((None, ncp, LANES), lambda b, g, t: (g, b, 0))
    return pl.pallas_call(
        functools.partial(_attn_body, tq=tq, ncp=ncp),
        out_shape=jax.ShapeDtypeStruct((batch * seq, NSA_W), F32),
        grid=(batch, KV_HEADS, nq),
        in_specs=[pl.BlockSpec((R, tq, LANES), lambda b, g, t: (g, b * nq + t, 0)),
                  cmp_spec, cmp_spec,
                  pl.BlockSpec(ovl.shape, lambda b, g, t: (0, 0)),
                  kv_spec, kv_spec, kv_spec, kv_spec,
                  pl.BlockSpec((None, tq, LANES), lambda b, g, t: (g, b * nq + t, 0))],
        out_specs=pl.BlockSpec((tq, R * HEAD_DIM), lambda b, g, t: (b * nq + t, g)),
        scratch_shapes=[pltpu.VMEM((R * tq, LANES), BF16),
                        pltpu.VMEM((R * tq, LANES), F32),
                        pltpu.VMEM((R * tq, 1), F32),
                        pltpu.VMEM((R * tq, LANES), F32),
                        pltpu.VMEM((R * tq, 1), F32),
                        pltpu.VMEM((R * tq, LANES), F32)],
        compiler_params=_cparams(("parallel", "parallel", "arbitrary")),
        name="nsa_prompt",
    )(q, kc, vc, ovl, ks, vs, kw, vw, gates)


def _s5_discretise(p, l):
    lr = p["ssm_lam_re"][l].astype(F32)
    li = p["ssm_lam_im"][l].astype(F32)
    dt = jnp.exp(p["ssm_log_step"][l].astype(F32))[:, None]

    def apow(t):
        mag, ang = jnp.exp(lr * dt * t), li * dt * t
        return mag * jnp.cos(ang), mag * jnp.sin(ang)

    a_re, a_im = apow(1.0)
    den = lr * lr + li * li
    nr, ni = a_re - 1.0, a_im
    f_re, f_im = (nr * lr + ni * li) / den, (ni * lr - nr * li) / den
    br, bi = p["ssm_b_re"][l].astype(F32), p["ssm_b_im"][l].astype(F32)
    bb_re = f_re[..., None] * br - f_im[..., None] * bi
    bb_im = f_re[..., None] * bi + f_im[..., None] * br
    return apow, bb_re, bb_im


def _prep_s5_step(p, l):
    apow, bb_re, bb_im = _s5_discretise(p, l)
    a_re, a_im = apow(1.0)
    G = bb_re.shape[0]
    eye = jnp.eye(G, dtype=F32)
    to_state = lambda bb: jnp.einsum("gnk,ge->gken", bb, eye).reshape(SSM_W, G * SSM_STATE)
    c_re, c_im = p["ssm_c_re"][l].astype(F32), p["ssm_c_im"][l].astype(F32)
    from_state = lambda c: jnp.einsum("gcn,ge->gnec", c, eye).reshape(G * SSM_STATE, SSM_W)
    return {
        "w_x": jnp.concatenate([to_state(bb_re), to_state(bb_im)], axis=1).astype(BF16),
        "w_y": jnp.concatenate([from_state(c_re), from_state(-c_im)], axis=0).astype(BF16),
        "a_re": a_re.reshape(1, -1), "a_im": a_im.reshape(1, -1),
        "d": p["ssm_d"][l].reshape(1, SSM_W).astype(F32),
    }


def _prep_s5_chunked(p, l):
    hi = lax.Precision.HIGHEST
    apow, bb_re, bb_im = _s5_discretise(p, l)
    c_re, c_im = p["ssm_c_re"][l].astype(F32), p["ssm_c_im"][l].astype(F32)
    G = bb_re.shape[0]
    T, J, E = SSM_CHUNK, SSM_SLABS, SSM_SLAB_GROUPS
    pw = [apow(float(t)) for t in range(T + 1)]
    pw_re = jnp.stack([x[0] for x in pw])
    pw_im = jnp.stack([x[1] for x in pw])
    p_re = pw_re[..., None] * bb_re - pw_im[..., None] * bb_im
    p_im = pw_re[..., None] * bb_im + pw_im[..., None] * bb_re
    eye = jnp.eye(E, dtype=F32)
    slab = lambda x: x.reshape(x.shape[0], J, E, *x.shape[2:])
    kt = (jnp.einsum("tgnk,gcn->tgkc", p_re[:T], c_re, precision=hi)
          - jnp.einsum("tgnk,gcn->tgkc", p_im[:T], c_im, precision=hi))
    kbd = jnp.einsum("tjgkc,ge->jtgkec", slab(kt), eye).reshape(J, T, LANES, LANES)
    w_col = kbd[:, ::-1].reshape(J, T * LANES, LANES)
    to_state = lambda x: jnp.einsum("sjgnk,ge->jsgken", slab(x[:T][::-1]), eye).reshape(J, T * LANES, E * SSM_STATE)
    w_st = jnp.concatenate([to_state(p_re), to_state(p_im)], axis=-1)
    cp_re = c_re[None, :, :, :] * pw_re[1:, :, None, :] - c_im[None] * pw_im[1:, :, None, :]
    cp_im = c_re[None, :, :, :] * pw_im[1:, :, None, :] + c_im[None] * pw_re[1:, :, None, :]
    from_state = lambda x: jnp.einsum("tjgcn,ge->jgntec", slab(x), eye).reshape(J, E * SSM_STATE, T * LANES)
    w_out = jnp.concatenate([from_state(cp_re), from_state(-cp_im)], axis=1)
    return {
        "w_col": w_col.astype(BF16), "w_st": w_st.astype(BF16), "w_out": w_out.astype(BF16),
        "a_re": pw_re[T].reshape(J, 1, E * SSM_STATE), "a_im": pw_im[T].reshape(J, 1, E * SSM_STATE),
        "d": p["ssm_d"][l].reshape(J, 1, LANES).astype(F32),
    }


def _s5_prompt_body(u_ref, wcol_ref, wst_ref, are_ref, aim_ref, wout_ref, d_ref,
                    y_ref, hre_ref, him_ref, xs_ref, hp_ref, *, n_chunks):
    T = SSM_CHUNK
    ns = SSM_SLAB_GROUPS * SSM_STATE
    ub = u_ref[...].astype(BF16)
    xs_ref[...] = _dot(ub, wst_ref[...])
    a_re = are_ref[...]
    a_im = aim_ref[...]

    def step(c, carry):
        hr, hi = carry
        hp_ref[pl.ds(c, 1), 0:ns] = hr
        hp_ref[pl.ds(c, 1), ns:2 * ns] = hi
        xr = xs_ref[pl.ds(c, 1), 0:ns]
        xi = xs_ref[pl.ds(c, 1), ns:2 * ns]
        return a_re * hr - a_im * hi + xr, a_re * hi + a_im * hr + xi

    zero = jnp.zeros((1, ns), F32)
    hr, hi = lax.fori_loop(0, n_chunks, step, (zero, zero))
    hre_ref[...] = jnp.broadcast_to(hr, hre_ref.shape)
    him_ref[...] = jnp.broadcast_to(hi, him_ref.shape)
    hpb = hp_ref[...].astype(BF16)
    for t in range(T):
        sl = slice(t * LANES, (t + 1) * LANES)
        y_ref[:, sl] = (_dot(ub[:, :(t + 1) * LANES], wcol_ref[(T - 1 - t) * LANES:, :])
                        + _dot(hpb, wout_ref[:, sl]) + d_ref[...] * u_ref[:, sl])


def _s5_prompt_call(u_slab, sw, batch, seq):
    T, J = SSM_CHUNK, SSM_SLABS
    n_chunks = seq // T
    ns = SSM_SLAB_GROUPS * SSM_STATE
    u16 = u_slab.reshape(J, batch * n_chunks, T * LANES)
    row_spec = pl.BlockSpec((None, n_chunks, T * LANES), lambda b, j: (j, b, 0))
    slab_spec = lambda a: pl.BlockSpec((None,) + a.shape[1:], lambda b, j: (j, 0, 0))
    st_spec = pl.BlockSpec((None, None, SUBLANES, ns), lambda b, j: (b, j, 0, 0))
    st_sds = jax.ShapeDtypeStruct((batch, J, SUBLANES, ns), F32)
    y16, hre, him = pl.pallas_call(
        functools.partial(_s5_prompt_body, n_chunks=n_chunks),
        out_shape=(jax.ShapeDtypeStruct(u16.shape, F32), st_sds, st_sds),
        grid=(batch, J),
        in_specs=[row_spec, slab_spec(sw["w_col"]), slab_spec(sw["w_st"]), slab_spec(sw["a_re"]),
                  slab_spec(sw["a_im"]), slab_spec(sw["w_out"]), slab_spec(sw["d"])],
        out_specs=(row_spec, st_spec, st_spec),
        scratch_shapes=[pltpu.VMEM((n_chunks, 2 * ns), F32), pltpu.VMEM((n_chunks, 2 * ns), F32)],
        compiler_params=_cparams(("parallel", "parallel")),
        name="s5_prompt",
    )(u16, sw["w_col"], sw["w_st"], sw["a_re"], sw["a_im"], sw["w_out"], sw["d"])
    n_groups = J * SSM_SLAB_GROUPS
    state = lambda h: h[:, :, 0, :].reshape(batch, n_groups, SSM_STATE)
    return y16.reshape(J, batch * seq, LANES), state(hre), state(him)


def _s5_step_body(u_ref, wx_ref, are_ref, aim_ref, h0re_ref, h0im_ref, wy_ref, d_ref,
                  y_ref, hre_ref, him_ref):
    u = jnp.concatenate([u_ref[j] for j in range(SSM_SLABS)], axis=-1)
    x = _dot(u.astype(BF16), wx_ref[...])
    ns = h0re_ref.shape[1]
    a_re, a_im = are_ref[...], aim_ref[...]
    h0r, h0i = h0re_ref[...], h0im_ref[...]
    hr = a_re * h0r - a_im * h0i + x[:, :ns]
    hi = a_re * h0i + a_im * h0r + x[:, ns:]
    hre_ref[...] = hr
    him_ref[...] = hi
    y = _dot(jnp.concatenate([hr, hi], axis=-1).astype(BF16), wy_ref[...]) + d_ref[...] * u
    for j in range(SSM_SLABS):
        y_ref[j] = y[:, j * LANES:(j + 1) * LANES]


def _s5_step_call(u_slab, h0_re, h0_im, sw):
    J, n_tok, _ = u_slab.shape
    ns = h0_re.shape[1]
    args = (u_slab, sw["w_x"], sw["a_re"], sw["a_im"], h0_re, h0_im, sw["w_y"], sw["d"])
    full = lambda a: pl.BlockSpec(a.shape, lambda i: (0,) * a.ndim)
    st_sds = jax.ShapeDtypeStruct((n_tok, ns), F32)
    return pl.pallas_call(
        _s5_step_body,
        out_shape=(jax.ShapeDtypeStruct(u_slab.shape, F32), st_sds, st_sds),
        grid=(1,),
        in_specs=[full(a) for a in args],
        out_specs=(full(u_slab), full(h0_re), full(h0_re)),
        compiler_params=_cparams(("arbitrary",)),
        name="s5_step",
    )(*args)


def _compress_sample_body(pt_ref, *refs, pages_per_step, page_rows, nch):
    del pt_ref
    page_refs = refs[:pages_per_step]
    pe_ref, wa_ref, wb_ref, b1_ref, w2_ref, kc_ref, vc_ref, krows_ref, vrows_ref = refs[pages_per_step:]
    pg = pl.program_id(1)
    for i, page_ref in enumerate(page_refs):
        off = pl.multiple_of((pg * pages_per_step + i) * page_rows, page_rows)
        krows_ref[pl.ds(off, page_rows), :] = page_ref[:, 0:LANES]
        vrows_ref[pl.ds(off, page_rows), :] = page_ref[:, LANES:2 * LANES]

    @pl.when(pg == pl.num_programs(1) - 1)
    def _():
        _compress_rows((krows_ref, vrows_ref), pe_ref, wa_ref, wb_ref, b1_ref, w2_ref, kc_ref, vc_ref, nch)


def _compress_sample_call(cache, page_table, cw):
    n_seq, n_pages = page_table.shape
    page_rows = cache.shape[1]
    pps = min(16, n_pages)
    assert n_pages % pps == 0
    nch = n_pages * page_rows // CMP_STRIDE
    page_spec = lambda i: pl.BlockSpec((None, page_rows, 2 * LANES),
                                       lambda b, pg, pt: (pt[b, pg * pps + i], 0, 0))
    full = lambda a: pl.BlockSpec(a.shape, lambda b, pg, pt: (0,) * a.ndim)
    out_spec = pl.BlockSpec((KV_HEADS, nch, LANES), lambda b, pg, pt: (0, b, 0))
    out_sds = jax.ShapeDtypeStruct((KV_HEADS, n_seq * nch, LANES), BF16)
    weights = (cw["pe"], cw["wa"], cw["wb"], cw["b1"], cw["w2"])
    return pl.pallas_call(
        functools.partial(_compress_sample_body, pages_per_step=pps, page_rows=page_rows, nch=nch),
        out_shape=(out_sds, out_sds),
        grid_spec=pltpu.PrefetchScalarGridSpec(
            num_scalar_prefetch=1,
            grid=(n_seq, n_pages // pps),
            in_specs=[page_spec(i) for i in range(pps)] + [full(a) for a in weights],
            out_specs=(out_spec, out_spec),
            scratch_shapes=[pltpu.VMEM((n_pages * page_rows, LANES), F32)] * 2),
        compiler_params=_cparams(("parallel", "arbitrary")),
        name="compress_sample",
    )(page_table, *([cache] * pps), *weights)


def _group_rows(x0, x1):
    row = lax.broadcasted_iota(I32, x0.shape, 0)
    return jnp.where(row < Q_PER_KV, x0, x1)


def _sample_select_body(q_ref, kc_ref, vc_ref, ovl_ref, tri_ref, ocmp_ref, idx_ref, *, ncp, qpos, nbp):
    q8 = q_ref[...]
    c_i = lax.broadcasted_iota(I32, (N_HEADS, ncp), 1)
    cmask = c_i * CMP_STRIDE + (CMP_BLOCK - 1) <= qpos
    s = _group_rows(_dot_nt(q8, kc_ref[0]), _dot_nt(q8, kc_ref[1]))
    s = jnp.where(cmask, s, NEG_INF)
    e = jnp.where(cmask, jnp.exp(s - jnp.max(s, axis=-1, keepdims=True)), 0.0)
    l = jnp.sum(e, axis=-1, keepdims=True)
    p = (e / jnp.where(l > 0.0, l, 1.0)).astype(BF16)
    ocmp_ref[...] = _group_rows(_dot(p, vc_ref[0]), _dot(p, vc_ref[1]))
    imp8 = _dot(p, ovl_ref[...])

    n_row = lax.broadcasted_iota(I32, (1, nbp), 1)
    qblk = qpos // SLC_BLOCK
    causal = n_row <= qblk
    forced = (n_row == 0) | (n_row >= qblk - (N_LOCAL_BLOCKS - 1))
    m_i = lax.broadcasted_iota(I32, (nbp, nbp), 0)
    n_i = lax.broadcasted_iota(I32, (nbp, nbp), 1)
    lane = lax.broadcasted_iota(I32, (1, LANES), 1)
    idx_rows = []
    for g in range(KV_HEADS):
        imp = jnp.sum(imp8[g * Q_PER_KV:(g + 1) * Q_PER_KV], axis=0, keepdims=True)
        w = jnp.where(causal, jnp.where(forced, jnp.inf, imp), -jnp.inf)
        w_sq = jnp.broadcast_to(w, (nbp, nbp))
        w_col = w_sq.T
        beats = jnp.where(n_i > m_i, jnp.where(w_col >= w_sq, 1.0, 0.0), jnp.where(w_col > w_sq, 1.0, 0.0))
        rank = jnp.sum(beats, axis=0, keepdims=True)
        sel = causal & (rank < TOP_N)
        self_f = jnp.where(sel, 1.0, 0.0)
        before = _dot(self_f.astype(BF16), tri_ref[...])
        idx = jnp.full((1, LANES), -1, I32)
        for k in range(TOP_N):
            hit = sel & (before == float(k))
            val = jnp.sum(jnp.where(hit, n_row.astype(F32) + 1.0, 0.0), axis=-1, keepdims=True) - 1.0
            idx = jnp.where(lane == k, val.astype(I32), idx)
        idx_rows.append(idx)
    idx_ref[...] = jnp.concatenate(idx_rows + [jnp.full((SUBLANES - KV_HEADS, LANES), -1, I32)], axis=0)


def _sample_select_call(q8, kc, vc, ovl, tri, qpos):
    n_seq = q8.shape[0]
    ncp = kc.shape[1] // n_seq
    nbp = ovl.shape[1]
    cmp_spec = pl.BlockSpec((KV_HEADS, ncp, LANES), lambda b: (0, b, 0))
    row_spec = pl.BlockSpec((None, N_HEADS, LANES), lambda b: (b, 0, 0))
    full = lambda a: pl.BlockSpec(a.shape, lambda b: (0,) * a.ndim)
    return pl.pallas_call(
        functools.partial(_sample_select_body, ncp=ncp, qpos=qpos, nbp=nbp),
        out_shape=(jax.ShapeDtypeStruct((n_seq, N_HEADS, LANES), F32),
                   jax.ShapeDtypeStruct((n_seq, SUBLANES, LANES), I32)),
        grid=(n_seq,),
        in_specs=[row_spec, cmp_spec, cmp_spec, full(ovl), full(tri)],
        out_specs=(row_spec, pl.BlockSpec((None, SUBLANES, LANES), lambda b: (b, 0, 0))),
        compiler_params=_cparams(("parallel",)),
        name="nsa_sample_select",
    )(q8, kc, vc, ovl, tri)


def _sample_attend_body(idx_ref, pt_ref, q_ref, ocmp_ref, gate_ref, ksn_ref, vsn_ref, kwn_ref, vwn_ref,
                        win_ref, *refs, n_cache_blocks, win_skip):
    del pt_ref
    blk_refs = refs[:KV_HEADS * TOP_N]
    o_ref = refs[KV_HEADS * TOP_N]
    b = pl.program_id(0)
    shape = (N_HEADS, LANES)
    row = lax.broadcasted_iota(I32, shape, 0)
    lo = lax.broadcasted_iota(I32, shape, 1) < HEAD_DIM
    qf = q_ref[...].astype(F32)
    qk = jnp.where(row < Q_PER_KV, jnp.where(lo, pltpu.roll(qf, HEAD_DIM, 1), 0.0), qf).astype(BF16)

    def self_score(kn_ref):
        return jnp.sum(qf * kn_ref[...].astype(F32), axis=-1, keepdims=True)

    def own_lanes(acc):
        return jnp.where(row < Q_PER_KV, acc, pltpu.roll(acc, HEAD_DIM, 1))

    res = []
    for g in range(KV_HEADS):
        s_blk = []
        for k in range(TOP_N):
            blk = blk_refs[g * TOP_N + k]
            s = _dot_nt(qk, blk[:, 0:LANES].astype(BF16))
            n = idx_ref[b, g * TOP_N + k]
            s_blk.append(jnp.where((n >= 0) & (n < n_cache_blocks), s, NEG_INF))
        s_self = self_score(ksn_ref)
        m = s_self
        for s in s_blk:
            m = jnp.maximum(m, jnp.max(s, axis=-1, keepdims=True))
        p_self = jnp.exp(s_self - m)
        l = p_self
        acc = p_self.astype(BF16).astype(F32) * vsn_ref[...].astype(F32)
        accv = jnp.zeros(shape, F32)
        for k in range(TOP_N):
            p = jnp.exp(s_blk[k] - m)
            l = l + jnp.sum(p, axis=-1, keepdims=True)
            accv += _dot(p.astype(BF16), blk_refs[g * TOP_N + k][:, LANES:2 * LANES].astype(BF16))
        res.append((acc + own_lanes(accv)) / l)
    o_slc = _group_rows(res[0], res[1])

    wk = win_ref[:, 0:LANES].astype(BF16)
    wv = win_ref[:, LANES:2 * LANES].astype(BF16)
    s = _dot_nt(qk, wk)
    col = lax.broadcasted_iota(I32, s.shape, 1)
    s = jnp.where(col >= win_skip, s, NEG_INF)
    s_self = self_score(kwn_ref)
    m = jnp.maximum(s_self, jnp.max(s, axis=-1, keepdims=True))
    p = jnp.exp(s - m)
    p_self = jnp.exp(s_self - m)
    l = p_self + jnp.sum(p, axis=-1, keepdims=True)
    o_win = (p_self.astype(BF16).astype(F32) * vwn_ref[...].astype(F32) + own_lanes(_dot(p.astype(BF16), wv))) / l

    gates = gate_ref[...]
    o_ref[...] = gates[:, 0:1] * ocmp_ref[...] + gates[:, 1:2] * o_slc + gates[:, 2:3] * o_win


def _sample_attend_call(idx, page_table, q8, ocmp, gates8, ksn, vsn, kwn, vwn, cache_win2d, cache_blk, win_skip):
    n_seq, n_pages = page_table.shape
    bpp = cache_blk.shape[1]
    n_cache_blocks = n_pages * bpp
    row_spec = pl.BlockSpec((None, N_HEADS, LANES), lambda b, ix, pt: (b, 0, 0))

    def blk_spec(j):
        def index_map(b, ix, pt):
            n = jnp.clip(ix[b, j], 0, n_cache_blocks - 1)
            return (pt[b, n // bpp], n % bpp, 0, 1)
        return pl.BlockSpec((None, None, SLC_BLOCK, 2 * LANES), index_map)

    win_spec = pl.BlockSpec((None,) + cache_win2d.shape[1:], lambda b, ix, pt: (b, 0, 0))
    return pl.pallas_call(
        functools.partial(_sample_attend_body, n_cache_blocks=n_cache_blocks, win_skip=win_skip),
        out_shape=jax.ShapeDtypeStruct((n_seq, N_HEADS, LANES), F32),
        grid_spec=pltpu.PrefetchScalarGridSpec(
            num_scalar_prefetch=2,
            grid=(n_seq,),
            in_specs=[row_spec] * 7 + [win_spec] + [blk_spec(j) for j in range(KV_HEADS * TOP_N)],
            out_specs=row_spec),
        compiler_params=_cparams(("parallel",)),
        name="nsa_sample_attend",
    )(idx, page_table, q8, ocmp, gates8, ksn, vsn, kwn, vwn, cache_win2d, *([cache_blk] * (KV_HEADS * TOP_N)))


def _round_up(x, m):
    return -(-x // m) * m


def _prompt_layer(h, p, l, batch, seq):
    prm = _prep_params(p, l)
    tabs = _rope_tables(jnp.arange(seq))
    u, gs, gn, q, kv, win, gates, ks, vs, kw, vw = _proj_call(
        h, prm["w_in"], prm["norm_w"], prm["qnw"], prm["knw"], prm["gb"], tabs, seq)
    y_ssm, h_re, h_im = _s5_prompt_call(u, _prep_s5_chunked(p, l), batch, seq)
    kc, vc = _compress_prompt_call(kv, _prep_compress(p, l), batch, seq)
    nch = seq // CMP_STRIDE
    ovl = _overlap_matrix(nch, nch - 1, seq // SLC_BLOCK)
    o = _attn_call(q, kc, vc, ovl, ks, vs, kw, vw, gates, batch, seq)
    h_new = _outmix_call(h, y_ssm, gs, o, gn, prm["w_glu"], prm["w_out"])
    kv_rows = kv.reshape(batch, seq, 4, KV_HEADS, HEAD_DIM)
    win_rows = win.reshape(batch, seq, 2, KV_HEADS, HEAD_DIM)[:, seq - min(WINDOW, seq):]
    return h_new, kv_rows, win_rows, h_re, h_im


def _sample_layer(h, p, l, cache_kv, cache_win, st_re, st_im, page_table):
    n_seq = h.shape[0]
    n_phys, page_rows = cache_kv.shape[:2]
    n_pages = page_table.shape[1]
    past_len = n_pages * page_rows
    win_buf = cache_win.shape[1]
    prm = _prep_params(p, l)
    tabs = _rope_tables(jnp.full((n_seq,), past_len, I32))
    u, gs, gn, q, kv, win, gates, ks, vs, kw, vw = _proj_call(
        h, prm["w_in"], prm["norm_w"], prm["qnw"], prm["knw"], prm["gb"], tabs, 1)
    n_state = st_re.shape[1] * st_re.shape[2]
    y_ssm, h_re, h_im = _s5_step_call(u, st_re.reshape(n_seq, n_state), st_im.reshape(n_seq, n_state),
                                      _prep_s5_step(p, l))
    kc, vc = _compress_sample_call(cache_kv.reshape(n_phys, page_rows, 4 * LANES), page_table,
                                   _prep_compress(p, l))
    ncp = past_len // CMP_STRIDE
    n_blk = -(-(past_len + 1) // SLC_BLOCK)
    nbp = _round_up(n_blk, LANES)
    ovl = _overlap_matrix(ncp, ncp - 1, n_blk, nbp)
    tri = (jnp.arange(nbp)[:, None] < jnp.arange(nbp)[None, :]).astype(BF16)
    q8 = q.transpose(1, 0, 2)
    ocmp, idx = _sample_select_call(q8, kc, vc, ovl, tri, past_len)
    idx = idx[:, :KV_HEADS, :TOP_N].reshape(n_seq, KV_HEADS * TOP_N)
    gates8 = gates[:, :, :3 * Q_PER_KV].reshape(KV_HEADS, n_seq, Q_PER_KV, 3).transpose(1, 0, 2, 3)
    gates8 = jnp.pad(gates8.reshape(n_seq, N_HEADS, 3), ((0, 0), (0, 0), (0, LANES - 3)))
    per_head = lambda a: jnp.repeat(a.transpose(1, 0, 2), Q_PER_KV, axis=1)
    blocks = cache_kv.reshape(n_phys, page_rows // SLC_BLOCK, SLC_BLOCK, 4 * LANES)
    o8 = _sample_attend_call(idx, page_table, q8, ocmp, gates8, per_head(ks), per_head(vs), per_head(kw),
                             per_head(vw), cache_win.reshape(n_seq, win_buf, 2 * LANES), blocks,
                             max(win_buf + 1 - WINDOW, 0))
    o = o8[:, :, :HEAD_DIM].reshape(n_seq, NSA_W)
    h_new = _outmix_call(h, y_ssm, gs, o, gn, prm["w_glu"], prm["w_out"])
    kv_rows = kv.reshape(n_seq, 1, 4, KV_HEADS, HEAD_DIM)
    wrows = jnp.concatenate([cache_win, win.reshape(n_seq, 1, 2, KV_HEADS, HEAD_DIM)], axis=1)
    wrows = wrows[:, wrows.shape[1] - min(WINDOW, wrows.shape[1]):]
    state = lambda s: s.reshape(st_re.shape)
    return h_new, kv_rows, wrows, state(h_re), state(h_im)


def kernel(x_prompt, x_sample, cache_kv, cache_win, state_ssm_re, state_ssm_im, page_table, norm_w, w_in, gate_b,
           q_norm_w, k_norm_w, cmp_pe, cmp_w1, cmp_b1, cmp_w2, ssm_lam_re, ssm_lam_im, ssm_log_step, ssm_b_re,
           ssm_b_im, ssm_c_re, ssm_c_im, ssm_d, w_glu, w_out):
    p = dict(norm_w=norm_w, w_in=w_in, gate_b=gate_b, q_norm_w=q_norm_w, k_norm_w=k_norm_w, cmp_pe=cmp_pe,
             cmp_w1=cmp_w1, cmp_b1=cmp_b1, cmp_w2=cmp_w2, ssm_lam_re=ssm_lam_re, ssm_lam_im=ssm_lam_im,
             ssm_log_step=ssm_log_step, ssm_b_re=ssm_b_re, ssm_b_im=ssm_b_im, ssm_c_re=ssm_c_re,
             ssm_c_im=ssm_c_im, ssm_d=ssm_d, w_glu=w_glu, w_out=w_out)
    b_p, s_p, d_model = x_prompt.shape
    b_s, s_s, _ = x_sample.shape
    assert s_s == 1, "the sample group decodes one token per sequence"
    h_p = x_prompt.reshape(b_p * s_p, d_model)
    h_s = x_sample.reshape(b_s, d_model)
    outs_p, outs_s = [], []
    for l in range(norm_w.shape[0]):
        h_p, *rest_p = _prompt_layer(h_p, p, l, b_p, s_p)
        h_s, *rest_s = _sample_layer(h_s, p, l, cache_kv[l], cache_win[l], state_ssm_re[l], state_ssm_im[l],
                                     page_table)
        outs_p.append(rest_p)
        outs_s.append(rest_s)
    stack = lambda outs, i: jnp.stack([o[i] for o in outs])
    return (h_p.reshape(x_prompt.shape), h_s.reshape(x_sample.shape),
            stack(outs_p, 0), stack(outs_s, 0), stack(outs_p, 1), stack(outs_s, 1),
            stack(outs_p, 2), stack(outs_p, 3), stack(outs_s, 2), stack(outs_s, 3))
```

```python
import functools
import math

import jax
import jax.numpy as jnp
from jax import lax
from jax.experimental import pallas as pl
from jax.experimental.pallas import tpu as pltpu

F32 = jnp.float32
BF16 = jnp.bfloat16
I32 = jnp.int32

LANES = 128
SUBLANES = 8
VMEM_LIMIT_BYTES = 56 * 1024 * 1024

HEAD_DIM = 64
N_HEADS = 8
KV_HEADS = 2
Q_PER_KV = N_HEADS // KV_HEADS
SSM_W = 512
SSM_GROUP = 16
SSM_STATE = 64
NSA_W = N_HEADS * HEAD_DIM
CMP_BLOCK = 32
CMP_STRIDE = 16
CMP_HID = 2 * HEAD_DIM
SLC_BLOCK = 64
TOP_N = 16
N_LOCAL_BLOCKS = 2
WINDOW = 512
ROPE_THETA = 500000.0
ROPE_DIM = HEAD_DIM // 4
RMS_EPS = 1e-6
NEG_INF = -1e30

COL_U = 0
COL_GS = SSM_W
COL_Q = 2 * SSM_W
COL_GN = 2 * SSM_W + NSA_W
COL_KV = 2 * SSM_W + 2 * NSA_W
COL_GL = COL_KV + 6 * KV_HEADS * HEAD_DIM
IN_W_PAD = COL_GL + LANES

SSM_CHUNK = 16
SSM_SLAB_GROUPS = LANES // SSM_GROUP
SSM_SLABS = SSM_W // LANES


def _cparams(sem):
    return pltpu.CompilerParams(dimension_semantics=sem, vmem_limit_bytes=VMEM_LIMIT_BYTES)


def _sigmoid(x):
    return 1.0 / (1.0 + jnp.exp(-x))


def _dot(a, b):
    return jnp.dot(a, b, preferred_element_type=F32)


def _dot_nt(a, b):
    return lax.dot_general(a, b, (((1,), (1,)), ((), ())), preferred_element_type=F32)


def _proj_body(x_ref, nw_ref, w_ref, qnw_ref, knw_ref, gb_ref, ra_ref, rb_ref, rc_ref,
               u_ref, gs_ref, gn_ref, qt_ref, kvt_ref, wint_ref, gt_ref,
               ks_ref, vst_ref, kw_ref, vwt_ref, *, tm, tiles_per_seq):
    x = x_ref[...]
    ms = jnp.mean(x * x, axis=-1, keepdims=True)
    h = (x * lax.rsqrt(ms + RMS_EPS) * nw_ref[...]).astype(BF16)

    def mm(c0, c1):
        return _dot(h, w_ref[:, c0:c1])

    zu = mm(COL_U, COL_GS)
    for j in range(SSM_SLABS):
        u_ref[j] = zu[:, j * LANES:(j + 1) * LANES]
    gs_ref[...] = mm(COL_GS, COL_Q)
    gn_ref[...] = mm(COL_GN, COL_KV)

    lane = lax.broadcasted_iota(I32, (tm, LANES), 1)
    lo = lane < HEAD_DIM
    ra = ra_ref[...]
    rb = rb_ref[...]
    rc = rc_ref[...]

    def norm_rope(s, wrow):
        s2 = s * s
        slo = jnp.sum(jnp.where(lo, s2, 0.0), axis=-1, keepdims=True)
        shi = jnp.sum(jnp.where(lo, 0.0, s2), axis=-1, keepdims=True)
        msq = jnp.where(lo, slo, shi) * (1.0 / HEAD_DIM)
        y = s * lax.rsqrt(msq + RMS_EPS) * wrow
        half = ROPE_DIM // 2
        return y * ra + pltpu.roll(y, LANES - half, 1) * rb + pltpu.roll(y, half, 1) * rc

    def hi_half(y, head):
        src = pltpu.roll(y, HEAD_DIM, 1) if head == 0 else y
        return jnp.where(lo, 0.0, src)

    def lo_half(y, head):
        return y if head == 0 else pltpu.roll(y, HEAD_DIM, 1)

    zq = mm(COL_Q, COL_GN)
    qnw = qnw_ref[...]
    scale = HEAD_DIM ** -0.5
    for j in range(N_HEADS // 2):
        y = norm_rope(zq[:, j * LANES:(j + 1) * LANES], qnw) * scale
        qt_ref[2 * j] = hi_half(y, 0).T.astype(BF16)
        qt_ref[2 * j + 1] = hi_half(y, 1).T.astype(BF16)

    zkv = mm(COL_KV, COL_GL)
    kc = norm_rope(zkv[:, 0:LANES], knw_ref[0:1, :])
    vc = zkv[:, LANES:2 * LANES]
    ks = norm_rope(zkv[:, 2 * LANES:3 * LANES], knw_ref[1:2, :])
    vs = zkv[:, 3 * LANES:4 * LANES]
    kw = norm_rope(zkv[:, 4 * LANES:5 * LANES], knw_ref[2:3, :])
    vw = zkv[:, 5 * LANES:6 * LANES]
    for i, rows in enumerate((kc, vc, ks, vs)):
        kvt_ref[i * LANES:(i + 1) * LANES, :] = rows.T
    for i, rows in enumerate((kw, vw)):
        wint_ref[i * LANES:(i + 1) * LANES, :] = rows.T

    row = lax.broadcasted_iota(I32, (tm, LANES), 0)
    pos = (pl.program_id(0) % tiles_per_seq) * tm + row
    onehot = jnp.where(lane == lax.shift_right_logical(pos, 6), 1.0, 0.0)
    ones_col = jnp.where(lane == HEAD_DIM, 1.0, 0.0)
    for g in range(KV_HEADS):
        ks_ref[g] = jnp.where(lo, onehot, hi_half(ks, g)).astype(BF16)
        vst_ref[g] = jnp.where(lo, lo_half(vs, g), ones_col).T.astype(BF16)
        kw_ref[g] = hi_half(kw, g).astype(BF16)
        vwt_ref[g] = jnp.where(lo, lo_half(vw, g), ones_col).T.astype(BF16)

    gates = _sigmoid(mm(COL_GL, IN_W_PAD) + gb_ref[...])
    for g in range(KV_HEADS):
        own = gates if g == 0 else pltpu.roll(gates, LANES - 3 * Q_PER_KV * g, 1)
        gt_ref[g] = own.T[:2 * SUBLANES]


def _proj_call(x2d, w_pad, norm_w, qnw, knw, gb, tabs, batch, seq):
    T, D = x2d.shape
    tm = min(256, seq)
    assert T == batch * seq and seq % tm == 0 and tm % LANES == 0
    tps = seq // tm
    row_spec = lambda w: pl.BlockSpec((tm, w), lambda i: (i, 0))
    full = lambda a: pl.BlockSpec(a.shape, lambda i: (0,) * a.ndim)
    tab_spec = pl.BlockSpec((tm, LANES), lambda i: (i % tps, 0))
    head_spec = lambda n: pl.BlockSpec((n, tm, LANES), lambda i: (0, i, 0))
    head_t_spec = lambda n, rows: pl.BlockSpec((n, rows, tm), lambda i: (0, 0, i))
    cache_t_spec = lambda rows: pl.BlockSpec((None, rows, tm), lambda i: (i // tps, 0, i % tps))
    tile_t_spec = pl.BlockSpec((KV_HEADS, None, LANES, tm), lambda i: (0, i, 0, 0))
    tile_t_sds = jax.ShapeDtypeStruct((KV_HEADS, T // tm, LANES, tm), BF16)
    out_shape = (
        jax.ShapeDtypeStruct((SSM_SLABS, T, LANES), F32),
        jax.ShapeDtypeStruct((T, SSM_W), F32),
        jax.ShapeDtypeStruct((T, NSA_W), F32),
        jax.ShapeDtypeStruct((N_HEADS, LANES, T), BF16),
        jax.ShapeDtypeStruct((batch, 4 * LANES, seq), F32),
        jax.ShapeDtypeStruct((batch, 2 * LANES, seq), F32),
        jax.ShapeDtypeStruct((KV_HEADS, 2 * SUBLANES, T), F32),
        jax.ShapeDtypeStruct((KV_HEADS, T, LANES), BF16),
        tile_t_sds,
        jax.ShapeDtypeStruct((KV_HEADS, T, LANES), BF16),
        tile_t_sds,
    )
    out_specs = (head_spec(SSM_SLABS), row_spec(SSM_W), row_spec(NSA_W), head_t_spec(N_HEADS, LANES),
                 cache_t_spec(4 * LANES), cache_t_spec(2 * LANES), head_t_spec(KV_HEADS, 2 * SUBLANES),
                 head_spec(KV_HEADS), tile_t_spec, head_spec(KV_HEADS), tile_t_spec)
    return pl.pallas_call(
        functools.partial(_proj_body, tm=tm, tiles_per_seq=tps),
        out_shape=out_shape,
        grid=(T // tm,),
        in_specs=[row_spec(D), full(norm_w), full(w_pad), full(qnw), full(knw), full(gb),
                  tab_spec, tab_spec, tab_spec],
        out_specs=out_specs,
        compiler_params=_cparams(("parallel",)),
        name="proj",
    )(x2d, norm_w, w_pad, qnw, knw, gb, *tabs)


def _prep_params(p, l):
    w_in = p["w_in"][l]
    d_model, in_w = w_in.shape
    tile2 = lambda v: jnp.tile(v, (1, LANES // HEAD_DIM))
    return {
        "w_in": jnp.pad(w_in.astype(BF16), ((0, 0), (0, IN_W_PAD - in_w))),
        "norm_w": p["norm_w"][l].reshape(1, d_model).astype(F32),
        "qnw": tile2(p["q_norm_w"][l].reshape(1, HEAD_DIM)).astype(F32),
        "knw": tile2(p["k_norm_w"][l]).astype(F32),
        "gb": jnp.pad(p["gate_b"][l].reshape(1, -1).astype(F32), ((0, 0), (0, LANES - 3 * N_HEADS))),
        "w_glu": p["w_glu"][l].astype(BF16),
        "w_out": p["w_out"][l].astype(BF16),
    }


def _rope_tables(pos):
    half = ROPE_DIM // 2
    inv = ROPE_THETA ** (-jnp.arange(half, dtype=F32) / half)
    ang = pos.astype(F32)[:, None] * inv
    cos, sin = jnp.cos(ang), jnp.sin(ang)
    n = pos.shape[0]
    rest = HEAD_DIM - ROPE_DIM
    a = jnp.concatenate([cos, cos, jnp.ones((n, rest), F32)], axis=-1)
    b = jnp.concatenate([-sin, jnp.zeros((n, HEAD_DIM - half), F32)], axis=-1)
    c = jnp.concatenate([jnp.zeros((n, half), F32), sin, jnp.zeros((n, rest), F32)], axis=-1)
    return tuple(jnp.tile(t, (1, LANES // HEAD_DIM)) for t in (a, b, c))


def _outmix_body(x_ref, y_ref, gs_ref, o_ref, gn_ref, wg_ref, wo_ref, out_ref):
    y = jnp.concatenate([y_ref[j] for j in range(SSM_SLABS)], axis=-1)
    ab = _dot(y.astype(BF16), wg_ref[...])
    gs = gs_ref[...]
    ssm = ab[:, :SSM_W] * _sigmoid(ab[:, SSM_W:]) * (gs * _sigmoid(gs))
    gn = gn_ref[...]
    nsa = o_ref[...] * (gn * _sigmoid(gn))
    acc = _dot(ssm.astype(BF16), wo_ref[0:SSM_W, :])
    acc += _dot(nsa.astype(BF16), wo_ref[SSM_W:, :])
    out_ref[...] = x_ref[...] + acc


def _outmix_call(x2d, y_ssm, g_ssm, o_nsa, g_nsa, w_glu, w_out):
    T, D = x2d.shape
    tm = min(512, T)
    row_spec = lambda w: pl.BlockSpec((tm, w), lambda i: (i, 0))
    full = lambda a: pl.BlockSpec(a.shape, lambda i: (0,) * a.ndim)
    return pl.pallas_call(
        _outmix_body,
        out_shape=jax.ShapeDtypeStruct((T, D), F32),
        grid=(T // tm,),
        in_specs=[row_spec(D), pl.BlockSpec((SSM_SLABS, tm, LANES), lambda i: (0, i, 0)),
                  row_spec(SSM_W), row_spec(NSA_W), row_spec(NSA_W),
                  full(w_glu), full(w_out)],
        out_specs=row_spec(D),
        compiler_params=_cparams(("parallel",)),
        name="outmix",
    )(x2d, y_ssm, g_ssm, o_nsa, g_nsa, w_glu, w_out)


def _gelu_tanh(x):
    c = math.sqrt(2.0 / math.pi)
    return 0.5 * x * (1.0 + jnp.tanh(c * (x + 0.044715 * (x * x * x))))


def _compress_rows(rows_refs, pe_ref, wa_ref, wb_ref, b1_ref, w2_ref, kc_ref, vc_ref, nch):
    lane = lax.broadcasted_iota(I32, (nch, LANES), 1)
    for kvi, out_ref in ((0, kc_ref), (1, vc_ref)):
        rows_ref = rows_refs[kvi]
        pa = jnp.zeros((nch, 2 * CMP_HID), F32)
        pb = jnp.zeros((nch, 2 * CMP_HID), F32)
        for j0 in range(0, CMP_STRIDE, 2):
            xs = [rows_ref[pl.ds(j, nch, stride=CMP_STRIDE), :] for j in (j0, j0 + 1)]
            xa = jnp.concatenate([xs[i] + pe_ref[kvi, 0, j0 + i:j0 + i + 1, :] for i in range(2)], axis=-1)
            xb = jnp.concatenate([xs[i] + pe_ref[kvi, 1, j0 + i:j0 + i + 1, :] for i in range(2)], axis=-1)
            wsl = slice(j0 * LANES, (j0 + 2) * LANES)
            pa += _dot(xa.astype(BF16), wa_ref[kvi, wsl, :])
            pb += _dot(xb.astype(BF16), wb_ref[kvi, wsl, :])
        hid = _gelu_tanh(pa + pltpu.roll(pb, nch - 1, 0) + b1_ref[kvi]).astype(BF16)
        for g in range(KV_HEADS):
            o = _dot(hid, w2_ref[kvi, g])
            if kvi == 1:
                o = jnp.where(lane == HEAD_DIM, 1.0, o).T
            out_ref[g] = o.astype(BF16)


def _compress_prompt_body(kvt_ref, pe_ref, wa_ref, wb_ref, b1_ref, w2_ref, kc_ref, vc_ref,
                          krows_ref, vrows_ref, *, nch):
    for c in range(kvt_ref.shape[1] // LANES):
        cs = slice(c * LANES, (c + 1) * LANES)
        krows_ref[cs, :] = kvt_ref[0:LANES, cs].T
        vrows_ref[cs, :] = kvt_ref[LANES:2 * LANES, cs].T
    _compress_rows((krows_ref, vrows_ref), pe_ref, wa_ref, wb_ref, b1_ref, w2_ref, kc_ref, vc_ref, nch)


def _compress_prompt_call(kv_t, cw, batch, seq):
    nch = seq // CMP_STRIDE
    full = lambda a: pl.BlockSpec(a.shape, lambda b: (0,) * a.ndim)
    out_spec = pl.BlockSpec((KV_HEADS, nch, LANES), lambda b: (0, b, 0))
    out_sds = jax.ShapeDtypeStruct((KV_HEADS, batch * nch, LANES), BF16)
    out_t_spec = pl.BlockSpec((KV_HEADS, None, LANES, nch), lambda b: (0, b, 0, 0))
    out_t_sds = jax.ShapeDtypeStruct((KV_HEADS, batch, LANES, nch), BF16)
    return pl.pallas_call(
        functools.partial(_compress_prompt_body, nch=nch),
        out_shape=(out_sds, out_t_sds),
        grid=(batch,),
        in_specs=[pl.BlockSpec((None, 2 * LANES, seq), lambda b: (b, 0, 0)),
                  full(cw["pe"]), full(cw["wa"]), full(cw["wb"]), full(cw["b1"]), full(cw["w2"])],
        out_specs=(out_spec, out_t_spec),
        scratch_shapes=[pltpu.VMEM((seq, LANES), F32)] * 2,
        compiler_params=_cparams(("parallel",)),
        name="compress_prompt",
    )(kv_t, cw["pe"], cw["wa"], cw["wb"], cw["b1"], cw["w2"])


def _prep_compress(p, l):
    eye = jnp.eye(KV_HEADS, dtype=F32)
    w1 = p["cmp_w1"][l].reshape(2, 2, CMP_STRIDE, HEAD_DIM, CMP_HID)
    wexp = jnp.einsum("khjdn,ge->khjgden", w1, eye).reshape(2, 2, CMP_STRIDE * LANES, KV_HEADS * CMP_HID)
    pe = p["cmp_pe"][l].reshape(2, 2, CMP_STRIDE, HEAD_DIM)
    w2 = p["cmp_w2"][l]
    zeros = jnp.zeros_like(w2[0])
    w2k = jnp.concatenate([zeros, w2[0]], axis=-1)
    w2v = jnp.concatenate([w2[1], zeros], axis=-1)
    w2e = jnp.stack([jnp.einsum("hd,ge->gehd", w, eye).reshape(KV_HEADS, KV_HEADS * CMP_HID, LANES)
                     for w in (w2k, w2v)])
    return {
        "pe": jnp.tile(pe, (1, 1, 1, KV_HEADS)).astype(F32),
        "wa": wexp[:, 0].astype(BF16),
        "wb": wexp[:, 1].astype(BF16),
        "b1": jnp.tile(p["cmp_b1"][l].reshape(2, 1, CMP_HID), (1, 1, KV_HEADS)).astype(F32),
        "w2": w2e.astype(BF16),
    }


def _overlap_matrix(n_tok_pad, n_tok, n_blk, n_cols=LANES):
    c_start = jnp.arange(n_tok_pad) * CMP_STRIDE
    blk = jnp.arange(n_cols)
    ov = ((c_start[:, None] < (blk[None, :] + 1) * SLC_BLOCK)
          & (c_start[:, None] + CMP_BLOCK > blk[None, :] * SLC_BLOCK)
          & (jnp.arange(n_tok_pad)[:, None] < n_tok) & (blk[None, :] < n_blk))
    return ov.astype(BF16)


def _topk_select_t(imp_t, q0, tq):
    nb = imp_t.shape[0]
    n_i = lax.broadcasted_iota(I32, (nb, tq), 0)
    qblk = lax.shift_right_logical(q0 + lax.broadcasted_iota(I32, (nb, tq), 1), 6)
    causal = n_i <= qblk
    forced = (n_i == 0) | (n_i >= qblk - (N_LOCAL_BLOCKS - 1))
    w = jnp.where(causal, jnp.where(forced, jnp.inf, imp_t), -jnp.inf)
    rank = jnp.zeros((nb, tq), F32)
    for m in range(nb):
        wm = w[m:m + 1, :]
        rank += jnp.where(n_i > m, jnp.where(wm >= w, 1.0, 0.0), jnp.where(wm > w, 1.0, 0.0))
    return causal & (rank < TOP_N)


def _flash_tile_t(k, q_t, v_t, mask, m_ref, acc_ref, r):
    s = _dot(k, q_t)
    if mask is not None:
        s = jnp.where(mask, s, NEG_INF)
    m_prev = m_ref[r]
    m_new = jnp.maximum(m_prev, jnp.max(s, axis=0, keepdims=True))
    alpha = jnp.exp(m_prev - m_new)
    p = jnp.exp(s - m_new)
    acc_ref[r] = alpha * acc_ref[r] + _dot(v_t, p.astype(BF16))
    m_ref[r] = m_new


def _attn_body(qt_ref, kc_ref, vct_ref, ovlt_ref, ks_ref, vst_ref, kw_ref, vwt_ref, gate_ref, o_ref,
               qa_ref, ocmp_ref, ms_ref, accs_ref, mw_ref, accw_ref, *, tq, ncp):
    R = Q_PER_KV
    qt = pl.program_id(2)
    q0 = qt * tq
    nbs = LANES // 2

    c_i = lax.broadcasted_iota(I32, (ncp, tq), 0)
    qpos_c = q0 + lax.broadcasted_iota(I32, (ncp, tq), 1)
    cmask = c_i * CMP_STRIDE + (CMP_BLOCK - 1) <= qpos_c
    kc = kc_ref[...]
    imp = jnp.zeros((LANES, tq), F32)
    for r in range(R):
        s = jnp.where(cmask, _dot(kc, qt_ref[r]), NEG_INF)
        e = jnp.where(cmask, jnp.exp(s - jnp.max(s, axis=0, keepdims=True)), 0.0)
        l = jnp.sum(e, axis=0, keepdims=True)
        p = (e / jnp.where(l > 0.0, l, 1.0)).astype(BF16)
        imp += _dot(ovlt_ref[...], p)
        ocmp_ref[r] = _dot(vct_ref[...], p)

    sel = _topk_select_t(imp[:nbs], q0, tq)
    bias = jnp.concatenate([jnp.where(sel, 0.0, NEG_INF), jnp.zeros((LANES - nbs, tq), F32)], axis=0)
    for r in range(R):
        qa_ref[r] = (qt_ref[r].astype(F32) + bias).astype(BF16)

    key_i = lax.broadcasted_iota(I32, (tq, tq), 0)
    qry_i = lax.broadcasted_iota(I32, (tq, tq), 1)

    ms_ref[...] = jnp.full(ms_ref.shape, -jnp.inf, F32)
    accs_ref[...] = jnp.zeros(accs_ref.shape, F32)

    def slc_tile(j, mask):
        off = pl.multiple_of(j * tq, tq)
        k = ks_ref[pl.ds(off, tq), :]
        v_t = vst_ref[j]
        for r in range(R):
            _flash_tile_t(k, qa_ref[r], v_t, mask, ms_ref, accs_ref, r)

    def slc_loop(j, carry):
        slc_tile(j, None)
        return carry

    lax.fori_loop(0, qt, slc_loop, 0)
    slc_tile(qt, key_i <= qry_i)

    mw_ref[...] = jnp.full(mw_ref.shape, -jnp.inf, F32)
    accw_ref[...] = jnp.zeros(accw_ref.shape, F32)
    nwin = WINDOW // tq

    def win_tile(d, mask):
        j = qt - d
        off = pl.multiple_of(j * tq, tq)
        k = kw_ref[pl.ds(off, tq), :]
        v_t = vwt_ref[j]
        for r in range(R):
            _flash_tile_t(k, qt_ref[r], v_t, mask, mw_ref, accw_ref, r)

    for d in range(nwin, 0, -1):
        @pl.when(qt >= d)
        def _(d=d):
            win_tile(d, (key_i > qry_i) if d == nwin else None)
    win_tile(0, key_i <= qry_i)

    outs = []
    for r in range(R):
        a_s = accs_ref[r]
        a_w = accw_ref[r]
        g = lambda k: gate_ref[3 * r + k:3 * r + k + 1, :]
        o = (g(0) * ocmp_ref[r] + g(1) * (a_s / a_s[HEAD_DIM:HEAD_DIM + 1, :])
             + g(2) * (a_w / a_w[HEAD_DIM:HEAD_DIM + 1, :]))
        outs.append(o[:HEAD_DIM])
    o_ref[...] = jnp.concatenate(outs, axis=0).T


def _attn_call(q_t, kc, vc_t, ovl_t, ks, vs_t, kw, vw_t, gates_t, batch, seq):
    tq = min(256, seq)
    nq = seq // tq
    ncp = kc.shape[1] // batch
    R = Q_PER_KV
    assert seq // SLC_BLOCK <= LANES // 2 and WINDOW % tq == 0
    k_spec = pl.BlockSpec((None, seq, LANES), lambda b, g, t: (g, b, 0))
    vt_spec = pl.BlockSpec((None, nq, LANES, tq), lambda b, g, t: (g, b, 0, 0))
    acc = pltpu.VMEM((R, LANES, tq), F32)
    stat = pltpu.VMEM((R, 1, tq), F32)
    return pl.pallas_call(
        functools.partial(_attn_body, tq=tq, ncp=ncp),
        out_shape=jax.ShapeDtypeStruct((batch * seq, NSA_W), F32),
        grid=(batch, KV_HEADS, nq),
        in_specs=[pl.BlockSpec((R, LANES, tq), lambda b, g, t: (g, 0, b * nq + t)),
                  pl.BlockSpec((None, ncp, LANES), lambda b, g, t: (g, b, 0)),
                  pl.BlockSpec((None, None, LANES, ncp), lambda b, g, t: (g, b, 0, 0)),
                  pl.BlockSpec(ovl_t.shape, lambda b, g, t: (0, 0)),
                  k_spec, vt_spec, k_spec, vt_spec,
                  pl.BlockSpec((None, 2 * SUBLANES, tq), lambda b, g, t: (g, 0, b * nq + t))],
        out_specs=pl.BlockSpec((tq, R * HEAD_DIM), lambda b, g, t: (b * nq + t, g)),
        scratch_shapes=[pltpu.VMEM((R, LANES, tq), BF16), acc, stat, acc, stat, acc],
        compiler_params=_cparams(("parallel", "parallel", "arbitrary")),
        name="nsa_prompt",
    )(q_t, kc, vc_t, ovl_t, ks, vs_t, kw, vw_t, gates_t)


def _s5_discretise(p, l):
    lr = p["ssm_lam_re"][l].astype(F32)
    li = p["ssm_lam_im"][l].astype(F32)
    dt = jnp.exp(p["ssm_log_step"][l].astype(F32))[:, None]

    def apow(t):
        mag, ang = jnp.exp(lr * dt * t), li * dt * t
        return mag * jnp.cos(ang), mag * jnp.sin(ang)

    a_re, a_im = apow(1.0)
    den = lr * lr + li * li
    nr, ni = a_re - 1.0, a_im
    f_re, f_im = (nr * lr + ni * li) / den, (ni * lr - nr * li) / den
    br, bi = p["ssm_b_re"][l].astype(F32), p["ssm_b_im"][l].astype(F32)
    bb_re = f_re[..., None] * br - f_im[..., None] * bi
    bb_im = f_re[..., None] * bi + f_im[..., None] * br
    return apow, bb_re, bb_im


def _prep_s5_step(p, l):
    apow, bb_re, bb_im = _s5_discretise(p, l)
    a_re, a_im = apow(1.0)
    G = bb_re.shape[0]
    eye = jnp.eye(G, dtype=F32)
    to_state = lambda bb: jnp.einsum("gnk,ge->gken", bb, eye).reshape(SSM_W, G * SSM_STATE)
    c_re, c_im = p["ssm_c_re"][l].astype(F32), p["ssm_c_im"][l].astype(F32)
    from_state = lambda c: jnp.einsum("gcn,ge->gnec", c, eye).reshape(G * SSM_STATE, SSM_W)
    return {
        "w_x": jnp.concatenate([to_state(bb_re), to_state(bb_im)], axis=1).astype(BF16),
        "w_y": jnp.concatenate([from_state(c_re), from_state(-c_im)], axis=0).astype(BF16),
        "a_re": a_re.reshape(1, -1), "a_im": a_im.reshape(1, -1),
        "d": p["ssm_d"][l].reshape(1, SSM_W).astype(F32),
    }


def _prep_s5_chunked(p, l):
    hi = lax.Precision.HIGHEST
    apow, bb_re, bb_im = _s5_discretise(p, l)
    c_re, c_im = p["ssm_c_re"][l].astype(F32), p["ssm_c_im"][l].astype(F32)
    T, J, E = SSM_CHUNK, SSM_SLABS, SSM_SLAB_GROUPS
    pw_re, pw_im = apow(jnp.arange(T + 1, dtype=F32)[:, None, None])
    p_re = pw_re[..., None] * bb_re - pw_im[..., None] * bb_im
    p_im = pw_re[..., None] * bb_im + pw_im[..., None] * bb_re
    eye = jnp.eye(E, dtype=F32)
    slab = lambda x: x.reshape(x.shape[0], J, E, *x.shape[2:])
    kt = (jnp.einsum("tgnk,gcn->tgkc", p_re[:T], c_re, precision=hi)
          - jnp.einsum("tgnk,gcn->tgkc", p_im[:T], c_im, precision=hi))
    kbd = jnp.einsum("tjgkc,ge->jtgkec", slab(kt), eye).reshape(J, T, LANES, LANES)
    w_col = kbd[:, ::-1].reshape(J, T * LANES, LANES)
    to_state = lambda x: jnp.einsum("sjgnk,ge->jsgken", slab(x[:T][::-1]), eye).reshape(J, T * LANES, E * SSM_STATE)
    w_st = jnp.concatenate([to_state(p_re), to_state(p_im)], axis=-1)
    cp_re = c_re[None, :, :, :] * pw_re[1:, :, None, :] - c_im[None] * pw_im[1:, :, None, :]
    cp_im = c_re[None, :, :, :] * pw_im[1:, :, None, :] + c_im[None] * pw_re[1:, :, None, :]
    from_state = lambda x: jnp.einsum("tjgcn,ge->jgntec", slab(x), eye).reshape(J, E * SSM_STATE, T * LANES)
    w_out = jnp.concatenate([from_state(cp_re), from_state(-cp_im)], axis=1)
    return {
        "w_col": w_col.astype(BF16), "w_st": w_st.astype(BF16), "w_out": w_out.astype(BF16),
        "a_re": pw_re[T].reshape(J, 1, E * SSM_STATE), "a_im": pw_im[T].reshape(J, 1, E * SSM_STATE),
        "d": p["ssm_d"][l].reshape(J, 1, LANES).astype(F32),
    }


def _s5_prompt_body(u_ref, wcol_ref, wst_ref, are_ref, aim_ref, wout_ref, d_ref,
                    y_ref, hre_ref, him_ref, xs_ref, hp_ref, *, n_chunks):
    T = SSM_CHUNK
    ns = SSM_SLAB_GROUPS * SSM_STATE
    u_pos = [u_ref[pl.ds(s, n_chunks, stride=T), :] for s in range(T)]
    ub = jnp.concatenate(u_pos, axis=-1).astype(BF16)
    xs_ref[...] = _dot(ub, wst_ref[...])
    a_re = are_ref[...]
    a_im = aim_ref[...]

    def step(c, carry):
        hr, hi = carry
        hp_ref[pl.ds(c, 1), 0:ns] = hr
        hp_ref[pl.ds(c, 1), ns:2 * ns] = hi
        xr = xs_ref[pl.ds(c, 1), 0:ns]
        xi = xs_ref[pl.ds(c, 1), ns:2 * ns]
        return a_re * hr - a_im * hi + xr, a_re * hi + a_im * hr + xi

    zero = jnp.zeros((1, ns), F32)
    hr, hi = lax.fori_loop(0, n_chunks, step, (zero, zero))
    hre_ref[...] = jnp.broadcast_to(hr, hre_ref.shape)
    him_ref[...] = jnp.broadcast_to(hi, him_ref.shape)
    hpb = hp_ref[...].astype(BF16)
    for t in range(T):
        sl = slice(t * LANES, (t + 1) * LANES)
        y_ref[pl.ds(t, n_chunks, stride=T), :] = (
            _dot(ub[:, :(t + 1) * LANES], wcol_ref[(T - 1 - t) * LANES:, :])
            + _dot(hpb, wout_ref[:, sl]) + d_ref[...] * u_pos[t])


def _s5_prompt_call(u_slab, sw, batch, seq):
    T, J = SSM_CHUNK, SSM_SLABS
    n_chunks = seq // T
    ns = SSM_SLAB_GROUPS * SSM_STATE
    row_spec = pl.BlockSpec((None, seq, LANES), lambda b, j: (j, b, 0))
    slab_spec = lambda a: pl.BlockSpec((None,) + a.shape[1:], lambda b, j: (j, 0, 0))
    st_spec = pl.BlockSpec((None, None, SUBLANES, ns), lambda b, j: (b, j, 0, 0))
    st_sds = jax.ShapeDtypeStruct((batch, J, SUBLANES, ns), F32)
    y, hre, him = pl.pallas_call(
        functools.partial(_s5_prompt_body, n_chunks=n_chunks),
        out_shape=(jax.ShapeDtypeStruct(u_slab.shape, F32), st_sds, st_sds),
        grid=(batch, J),
        in_specs=[row_spec, slab_spec(sw["w_col"]), slab_spec(sw["w_st"]), slab_spec(sw["a_re"]),
                  slab_spec(sw["a_im"]), slab_spec(sw["w_out"]), slab_spec(sw["d"])],
        out_specs=(row_spec, st_spec, st_spec),
        scratch_shapes=[pltpu.VMEM((n_chunks, 2 * ns), F32), pltpu.VMEM((n_chunks, 2 * ns), F32)],
        compiler_params=_cparams(("parallel", "parallel")),
        name="s5_prompt",
    )(u_slab, sw["w_col"], sw["w_st"], sw["a_re"], sw["a_im"], sw["w_out"], sw["d"])
    n_groups = J * SSM_SLAB_GROUPS
    state = lambda h: h[:, :, 0, :].reshape(batch, n_groups, SSM_STATE)
    return y, state(hre), state(him)


def _s5_step_body(u_ref, wx_ref, are_ref, aim_ref, h0re_ref, h0im_ref, wy_ref, d_ref,
                  y_ref, hre_ref, him_ref):
    u = jnp.concatenate([u_ref[j] for j in range(SSM_SLABS)], axis=-1)
    x = _dot(u.astype(BF16), wx_ref[...])
    ns = h0re_ref.shape[1]
    a_re, a_im = are_ref[...], aim_ref[...]
    h0r, h0i = h0re_ref[...], h0im_ref[...]
    hr = a_re * h0r - a_im * h0i + x[:, :ns]
    hi = a_re * h0i + a_im * h0r + x[:, ns:]
    hre_ref[...] = hr
    him_ref[...] = hi
    y = _dot(jnp.concatenate([hr, hi], axis=-1).astype(BF16), wy_ref[...]) + d_ref[...] * u
    for j in range(SSM_SLABS):
        y_ref[j] = y[:, j * LANES:(j + 1) * LANES]


def _s5_step_call(u_slab, h0_re, h0_im, sw):
    J, n_tok, _ = u_slab.shape
    ns = h0_re.shape[1]
    args = (u_slab, sw["w_x"], sw["a_re"], sw["a_im"], h0_re, h0_im, sw["w_y"], sw["d"])
    full = lambda a: pl.BlockSpec(a.shape, lambda i: (0,) * a.ndim)
    st_sds = jax.ShapeDtypeStruct((n_tok, ns), F32)
    return pl.pallas_call(
        _s5_step_body,
        out_shape=(jax.ShapeDtypeStruct(u_slab.shape, F32), st_sds, st_sds),
        grid=(1,),
        in_specs=[full(a) for a in args],
        out_specs=(full(u_slab), full(h0_re), full(h0_re)),
        compiler_params=_cparams(("arbitrary",)),
        name="s5_step",
    )(*args)


def _compress_sample_body(pt_ref, *refs, pages_per_step, page_rows, nch):
    del pt_ref
    page_refs = refs[:pages_per_step]
    pe_ref, wa_ref, wb_ref, b1_ref, w2_ref, kc_ref, vc_ref, krows_ref, vrows_ref = refs[pages_per_step:]
    pg = pl.program_id(1)
    for i, page_ref in enumerate(page_refs):
        off = pl.multiple_of((pg * pages_per_step + i) * page_rows, page_rows)
        krows_ref[pl.ds(off, page_rows), :] = page_ref[0].reshape(LANES, page_rows).T
        vrows_ref[pl.ds(off, page_rows), :] = page_ref[1].reshape(LANES, page_rows).T

    @pl.when(pg == pl.num_programs(1) - 1)
    def _():
        _compress_rows((krows_ref, vrows_ref), pe_ref, wa_ref, wb_ref, b1_ref, w2_ref, kc_ref, vc_ref, nch)


def _compress_sample_call(cache, page_table, cw):
    n_seq, n_pages = page_table.shape
    page_rows = cache.shape[-1]
    assert page_rows == LANES
    pps = min(16, n_pages)
    assert n_pages % pps == 0
    nch = n_pages * page_rows // CMP_STRIDE
    page_spec = lambda i: pl.BlockSpec((None, 2, KV_HEADS, HEAD_DIM, page_rows),
                                       lambda b, pg, pt: (pt[b, pg * pps + i], 0, 0, 0, 0))
    full = lambda a: pl.BlockSpec(a.shape, lambda b, pg, pt: (0,) * a.ndim)
    out_spec = pl.BlockSpec((KV_HEADS, nch, LANES), lambda b, pg, pt: (0, b, 0))
    out_sds = jax.ShapeDtypeStruct((KV_HEADS, n_seq * nch, LANES), BF16)
    out_t_spec = pl.BlockSpec((KV_HEADS, None, LANES, nch), lambda b, pg, pt: (0, b, 0, 0))
    out_t_sds = jax.ShapeDtypeStruct((KV_HEADS, n_seq, LANES, nch), BF16)
    weights = (cw["pe"], cw["wa"], cw["wb"], cw["b1"], cw["w2"])
    return pl.pallas_call(
        functools.partial(_compress_sample_body, pages_per_step=pps, page_rows=page_rows, nch=nch),
        out_shape=(out_sds, out_t_sds),
        grid_spec=pltpu.PrefetchScalarGridSpec(
            num_scalar_prefetch=1,
            grid=(n_seq, n_pages // pps),
            in_specs=[page_spec(i) for i in range(pps)] + [full(a) for a in weights],
            out_specs=(out_spec, out_t_spec),
            scratch_shapes=[pltpu.VMEM((n_pages * page_rows, LANES), F32)] * 2),
        compiler_params=_cparams(("parallel", "arbitrary")),
        name="compress_sample",
    )(page_table, *([cache] * pps), *weights)


def _group_rows(x0, x1):
    row = lax.broadcasted_iota(I32, x0.shape, 0)
    return jnp.where(row < Q_PER_KV, x0, x1)


def _sample_select_body(q_ref, kc_ref, vct_ref, ovl_ref, tri_ref, ocmp_ref, idx_ref, *, ncp, qpos, nbp):
    q8 = q_ref[...]
    c_i = lax.broadcasted_iota(I32, (N_HEADS, ncp), 1)
    cmask = c_i * CMP_STRIDE + (CMP_BLOCK - 1) <= qpos
    s = _group_rows(_dot_nt(q8, kc_ref[0]), _dot_nt(q8, kc_ref[1]))
    s = jnp.where(cmask, s, NEG_INF)
    e = jnp.where(cmask, jnp.exp(s - jnp.max(s, axis=-1, keepdims=True)), 0.0)
    l = jnp.sum(e, axis=-1, keepdims=True)
    p = (e / jnp.where(l > 0.0, l, 1.0)).astype(BF16)
    ocmp_ref[...] = _group_rows(_dot_nt(p, vct_ref[0]), _dot_nt(p, vct_ref[1]))
    imp8 = _dot(p, ovl_ref[...])

    n_row = lax.broadcasted_iota(I32, (1, nbp), 1)
    qblk = qpos // SLC_BLOCK
    causal = n_row <= qblk
    forced = (n_row == 0) | (n_row >= qblk - (N_LOCAL_BLOCKS - 1))
    m_i = lax.broadcasted_iota(I32, (nbp, nbp), 0)
    n_i = lax.broadcasted_iota(I32, (nbp, nbp), 1)
    lane = lax.broadcasted_iota(I32, (1, LANES), 1)
    idx_rows = []
    for g in range(KV_HEADS):
        imp = jnp.sum(imp8[g * Q_PER_KV:(g + 1) * Q_PER_KV], axis=0, keepdims=True)
        w = jnp.where(causal, jnp.where(forced, jnp.inf, imp), -jnp.inf)
        w_sq = jnp.broadcast_to(w, (nbp, nbp))
        w_col = w_sq.T
        beats = jnp.where(n_i > m_i, jnp.where(w_col >= w_sq, 1.0, 0.0), jnp.where(w_col > w_sq, 1.0, 0.0))
        rank = jnp.sum(beats, axis=0, keepdims=True)
        sel = causal & (rank < TOP_N)
        self_f = jnp.where(sel, 1.0, 0.0)
        before = _dot(self_f.astype(BF16), tri_ref[...])
        idx = jnp.full((1, LANES), -1, I32)
        for k in range(TOP_N):
            hit = sel & (before == float(k))
            val = jnp.sum(jnp.where(hit, n_row.astype(F32) + 1.0, 0.0), axis=-1, keepdims=True) - 1.0
            idx = jnp.where(lane == k, val.astype(I32), idx)
        idx_rows.append(idx)
    idx_ref[...] = jnp.concatenate(idx_rows + [jnp.full((SUBLANES - KV_HEADS, LANES), -1, I32)], axis=0)


def _sample_select_call(q8, kc, vc_t, ovl, tri, qpos):
    n_seq = q8.shape[0]
    ncp = kc.shape[1] // n_seq
    nbp = ovl.shape[1]
    cmp_spec = pl.BlockSpec((KV_HEADS, ncp, LANES), lambda b: (0, b, 0))
    row_spec = pl.BlockSpec((None, N_HEADS, LANES), lambda b: (b, 0, 0))
    full = lambda a: pl.BlockSpec(a.shape, lambda b: (0,) * a.ndim)
    return pl.pallas_call(
        functools.partial(_sample_select_body, ncp=ncp, qpos=qpos, nbp=nbp),
        out_shape=(jax.ShapeDtypeStruct((n_seq, N_HEADS, LANES), F32),
                   jax.ShapeDtypeStruct((n_seq, SUBLANES, LANES), I32)),
        grid=(n_seq,),
        in_specs=[row_spec, cmp_spec, pl.BlockSpec((KV_HEADS, None, LANES, ncp), lambda b: (0, b, 0, 0)),
                  full(ovl), full(tri)],
        out_specs=(row_spec, pl.BlockSpec((None, SUBLANES, LANES), lambda b: (b, 0, 0))),
        compiler_params=_cparams(("parallel",)),
        name="nsa_sample_select",
    )(q8, kc, vc_t, ovl, tri)


def _sample_attend_body(idx_ref, pt_ref, q_ref, ocmp_ref, gate_ref, ksn_ref, vsn_ref, kwn_ref, vwn_ref,
                        win_ref, *refs, n_cache_blocks, blocks_per_page, win_skip):
    del pt_ref
    n_blk = KV_HEADS * TOP_N
    k_refs, v_refs, o_ref = refs[:n_blk], refs[n_blk:2 * n_blk], refs[2 * n_blk]
    b = pl.program_id(0)
    q = q_ref[...]
    qf = q.astype(F32)
    row_g = (lax.broadcasted_iota(I32, (N_HEADS, 1), 0) >= Q_PER_KV).astype(I32)

    def attend(s_list, v_list, kn_ref, vn_ref):
        s_self = jnp.sum(qf * kn_ref[...].astype(F32), axis=-1, keepdims=True)
        m = s_self
        for s in s_list:
            m = jnp.maximum(m, jnp.max(s, axis=-1, keepdims=True))
        p_self = jnp.exp(s_self - m)
        l = p_self
        acc = p_self.astype(BF16).astype(F32) * vn_ref[...].astype(F32)
        for s, v in zip(s_list, v_list):
            p = jnp.exp(s - m)
            l = l + jnp.sum(p, axis=-1, keepdims=True)
            acc = acc + _dot_nt(p.astype(BF16), v().astype(BF16))
        return acc / l

    s_list, v_list = [], []
    for j in range(n_blk):
        s = _dot(q, k_refs[j][...].astype(BF16))
        col = lax.broadcasted_iota(I32, s.shape, 1)
        n = idx_ref[b, j]
        first = (n % blocks_per_page) * SLC_BLOCK
        ok = ((row_g == j // TOP_N) & (col >= first) & (col < first + SLC_BLOCK)
              & (n >= 0) & (n < n_cache_blocks))
        s_list.append(jnp.where(ok, s, NEG_INF))
        v_list.append(lambda j=j: v_refs[j][...])
    o_slc = attend(s_list, v_list, ksn_ref, vsn_ref)

    s_list, v_list = [], []
    for g in range(KV_HEADS):
        s = _dot(q, win_ref[0, g].astype(BF16))
        col = lax.broadcasted_iota(I32, s.shape, 1)
        s_list.append(jnp.where((row_g == g) & (col >= win_skip), s, NEG_INF))
        v_list.append(lambda g=g: win_ref[1, g])
    o_win = attend(s_list, v_list, kwn_ref, vwn_ref)

    gates = gate_ref[...]
    o_ref[...] = gates[:, 0:1] * ocmp_ref[:, 0:HEAD_DIM] + gates[:, 1:2] * o_slc + gates[:, 2:3] * o_win


def _sample_attend_call(idx, page_table, q64, ocmp, gates8, ksn, vsn, kwn, vwn, cache_win_t, cache_t, win_skip):
    n_seq, n_pages = page_table.shape
    page_rows = cache_t.shape[-1]
    bpp = page_rows // SLC_BLOCK
    n_cache_blocks = n_pages * bpp
    row_spec = lambda a: pl.BlockSpec((None,) + a.shape[1:], lambda b, ix, pt: (b,) + (0,) * (a.ndim - 1))

    def blk_spec(j, slot):
        def index_map(b, ix, pt):
            n = jnp.clip(ix[b, j], 0, n_cache_blocks - 1)
            return (pt[b, n // bpp], slot, j // TOP_N, 0, 0)
        return pl.BlockSpec((None, None, None, HEAD_DIM, page_rows), index_map)

    n_blk = KV_HEADS * TOP_N
    small = (q64, ocmp, gates8, ksn, vsn, kwn, vwn, cache_win_t)
    return pl.pallas_call(
        functools.partial(_sample_attend_body, n_cache_blocks=n_cache_blocks, blocks_per_page=bpp,
                          win_skip=win_skip),
        out_shape=jax.ShapeDtypeStruct((n_seq, N_HEADS, HEAD_DIM), F32),
        grid_spec=pltpu.PrefetchScalarGridSpec(
            num_scalar_prefetch=2,
            grid=(n_seq,),
            in_specs=([row_spec(a) for a in small] + [blk_spec(j, 2) for j in range(n_blk)]
                      + [blk_spec(j, 3) for j in range(n_blk)]),
            out_specs=pl.BlockSpec((None, N_HEADS, HEAD_DIM), lambda b, ix, pt: (b, 0, 0))),
        compiler_params=_cparams(("parallel",)),
        name="nsa_sample_attend",
    )(idx, page_table, *small, *([cache_t] * (2 * n_blk)))


def _round_up(x, m):
    return -(-x // m) * m


def _prompt_layer(h, p, l, batch, seq):
    prm = _prep_params(p, l)
    tabs = _rope_tables(jnp.arange(seq))
    u, gs, gn, q_t, kv_t, win_t, gates_t, ks, vs_t, kw, vw_t = _proj_call(
        h, prm["w_in"], prm["norm_w"], prm["qnw"], prm["knw"], prm["gb"], tabs, batch, seq)
    y_ssm, h_re, h_im = _s5_prompt_call(u, _prep_s5_chunked(p, l), batch, seq)
    kc, vc_t = _compress_prompt_call(kv_t, _prep_compress(p, l), batch, seq)
    nch = seq // CMP_STRIDE
    ovl_t = _overlap_matrix(nch, nch - 1, seq // SLC_BLOCK).T
    o = _attn_call(q_t, kc, vc_t, ovl_t, ks, vs_t, kw, vw_t, gates_t, batch, seq)
    h_new = _outmix_call(h, y_ssm, gs, o, gn, prm["w_glu"], prm["w_out"])
    rows = lambda x_t, slots: x_t.reshape(batch, slots, KV_HEADS, HEAD_DIM, seq).transpose(0, 4, 1, 2, 3)
    return h_new, rows(kv_t, 4), rows(win_t, 2)[:, seq - min(WINDOW, seq):], h_re, h_im


def _sample_layer(h, p, l, cache_kv, cache_win, st_re, st_im, page_table):
    n_seq = h.shape[0]
    n_phys, page_rows = cache_kv.shape[:2]
    n_pages = page_table.shape[1]
    past_len = n_pages * page_rows
    win_buf = cache_win.shape[1]
    prm = _prep_params(p, l)
    n_pad = _round_up(n_seq, LANES)
    tabs = _rope_tables(jnp.full((n_pad,), past_len, I32))
    h_pad = jnp.pad(h, ((0, n_pad - n_seq), (0, 0)))
    u, gs, gn, q_t, kv_t, win_t, gates_t, ks, vs_t, kw, vw_t = _proj_call(
        h_pad, prm["w_in"], prm["norm_w"], prm["qnw"], prm["knw"], prm["gb"], tabs, 1, n_pad)
    u, gs, gn = u[:, :n_seq], gs[:n_seq], gn[:n_seq]
    n_state = st_re.shape[1] * st_re.shape[2]
    y_ssm, h_re, h_im = _s5_step_call(u, st_re.reshape(n_seq, n_state), st_im.reshape(n_seq, n_state),
                                      _prep_s5_step(p, l))
    cache_t = cache_kv.transpose(0, 2, 3, 4, 1)
    kc, vc_t = _compress_sample_call(cache_t, page_table, _prep_compress(p, l))
    ncp = past_len // CMP_STRIDE
    n_blk = -(-(past_len + 1) // SLC_BLOCK)
    nbp = _round_up(n_blk, LANES)
    ovl = _overlap_matrix(ncp, ncp - 1, n_blk, nbp)
    tri = (jnp.arange(nbp)[:, None] < jnp.arange(nbp)[None, :]).astype(BF16)
    q8 = q_t[:, :, :n_seq].transpose(2, 0, 1)
    ocmp, idx = _sample_select_call(q8, kc, vc_t, ovl, tri, past_len)
    idx = idx[:, :KV_HEADS, :TOP_N].reshape(n_seq, KV_HEADS * TOP_N)
    gates8 = gates_t[:, :3 * Q_PER_KV, :n_seq].reshape(KV_HEADS, Q_PER_KV, 3, n_seq).transpose(3, 0, 1, 2)
    gates8 = jnp.pad(gates8.reshape(n_seq, N_HEADS, 3), ((0, 0), (0, 0), (0, LANES - 3)))
    per_head = lambda a: jnp.repeat(a.transpose(1, 0, 2), Q_PER_KV, axis=1)
    new_k = lambda k: per_head(k[:, :n_seq, HEAD_DIM:])
    new_v = lambda v_t: per_head(v_t[:, 0, :HEAD_DIM, :n_seq].transpose(0, 2, 1))
    o8 = _sample_attend_call(idx, page_table, q8[:, :, HEAD_DIM:], ocmp, gates8, new_k(ks), new_v(vs_t),
                             new_k(kw), new_v(vw_t), cache_win.transpose(0, 2, 3, 4, 1), cache_t,
                             max(win_buf + 1 - WINDOW, 0))
    o = o8.reshape(n_seq, NSA_W)
    h_new = _outmix_call(h, y_ssm, gs, o, gn, prm["w_glu"], prm["w_out"])
    kv_rows = kv_t[0, :, :n_seq].T.reshape(n_seq, 1, 4, KV_HEADS, HEAD_DIM)
    win_new = win_t[0, :, :n_seq].T.reshape(n_seq, 1, 2, KV_HEADS, HEAD_DIM)
    wrows = jnp.concatenate([cache_win, win_new], axis=1)
    wrows = wrows[:, wrows.shape[1] - min(WINDOW, wrows.shape[1]):]
    state = lambda s: s.reshape(st_re.shape)
    return h_new, kv_rows, wrows, state(h_re), state(h_im)


def kernel(x_prompt, x_sample, cache_kv, cache_win, state_ssm_re, state_ssm_im, page_table, norm_w, w_in, gate_b,
           q_norm_w, k_norm_w, cmp_pe, cmp_w1, cmp_b1, cmp_w2, ssm_lam_re, ssm_lam_im, ssm_log_step, ssm_b_re,
           ssm_b_im, ssm_c_re, ssm_c_im, ssm_d, w_glu, w_out):
    p = dict(norm_w=norm_w, w_in=w_in, gate_b=gate_b, q_norm_w=q_norm_w, k_norm_w=k_norm_w, cmp_pe=cmp_pe,
             cmp_w1=cmp_w1, cmp_b1=cmp_b1, cmp_w2=cmp_w2, ssm_lam_re=ssm_lam_re, ssm_lam_im=ssm_lam_im,
             ssm_log_step=ssm_log_step, ssm_b_re=ssm_b_re, ssm_b_im=ssm_b_im, ssm_c_re=ssm_c_re,
             ssm_c_im=ssm_c_im, ssm_d=ssm_d, w_glu=w_glu, w_out=w_out)
    b_p, s_p, d_model = x_prompt.shape
    b_s, s_s, _ = x_sample.shape
    assert s_s == 1, "the sample group decodes one token per sequence"
    h_p = x_prompt.reshape(b_p * s_p, d_model)
    h_s = x_sample.reshape(b_s, d_model)
    outs_p, outs_s = [], []
    for l in range(norm_w.shape[0]):
        h_p, *rest_p = _prompt_layer(h_p, p, l, b_p, s_p)
        h_s, *rest_s = _sample_layer(h_s, p, l, cache_kv[l], cache_win[l], state_ssm_re[l], state_ssm_im[l],
                                     page_table)
        outs_p.append(rest_p)
        outs_s.append(rest_s)
    stack = lambda outs, i: jnp.stack([o[i] for o in outs])
    return (h_p.reshape(x_prompt.shape), h_s.reshape(x_sample.shape),
            stack(outs_p, 0), stack(outs_s, 0), stack(outs_p, 1), stack(outs_s, 1),
            stack(outs_p, 2), stack(outs_p, 3), stack(outs_s, 2), stack(outs_s, 3))
```

```python
import functools
import math

import jax
import jax.numpy as jnp
from jax import lax
from jax.experimental import pallas as pl
from jax.experimental.pallas import tpu as pltpu

F32 = jnp.float32
BF16 = jnp.bfloat16
I32 = jnp.int32

LANES = 128
SUBLANES = 8
VMEM_LIMIT_BYTES = 56 * 1024 * 1024

HEAD_DIM = 64
N_HEADS = 8
KV_HEADS = 2
Q_PER_KV = N_HEADS // KV_HEADS
SSM_W = 512
SSM_GROUP = 16
SSM_STATE = 64
NSA_W = N_HEADS * HEAD_DIM
CMP_BLOCK = 32
CMP_STRIDE = 16
CMP_HID = 2 * HEAD_DIM
SLC_BLOCK = 64
TOP_N = 16
N_LOCAL_BLOCKS = 2
WINDOW = 512
ROPE_THETA = 500000.0
ROPE_DIM = HEAD_DIM // 4
RMS_EPS = 1e-6
NEG_INF = -1e30

COL_U = 0
COL_GS = SSM_W
COL_Q = 2 * SSM_W
COL_GN = 2 * SSM_W + NSA_W
COL_KV = 2 * SSM_W + 2 * NSA_W
COL_GL = COL_KV + 6 * KV_HEADS * HEAD_DIM
IN_W_PAD = COL_GL + LANES

SSM_CHUNK = 16
SSM_SLAB_GROUPS = LANES // SSM_GROUP
SSM_SLABS = SSM_W // LANES


def _cparams(sem):
    return pltpu.CompilerParams(dimension_semantics=sem, vmem_limit_bytes=VMEM_LIMIT_BYTES)


def _sigmoid(x):
    return 1.0 / (1.0 + jnp.exp(-x))


def _dot(a, b):
    return jnp.dot(a, b, preferred_element_type=F32)


def _dot_nt(a, b):
    return lax.dot_general(a, b, (((1,), (1,)), ((), ())), preferred_element_type=F32)


def _proj_body(x_ref, nw_ref, w_ref, qnw_ref, knw_ref, gb_ref, ra_ref, rb_ref, rc_ref,
               u_ref, gs_ref, gn_ref, qt_ref, kvt_ref, wint_ref, gt_ref,
               ks_ref, vst_ref, kw_ref, vwt_ref, *, tm, tiles_per_seq):
    x = x_ref[...]
    ms = jnp.mean(x * x, axis=-1, keepdims=True)
    h = (x * lax.rsqrt(ms + RMS_EPS) * nw_ref[...]).astype(BF16)

    def mm(c0, c1):
        return _dot(h, w_ref[:, c0:c1])

    zu = mm(COL_U, COL_GS)
    for j in range(SSM_SLABS):
        u_ref[j] = zu[:, j * LANES:(j + 1) * LANES]
    gs_ref[...] = mm(COL_GS, COL_Q)
    gn_ref[...] = mm(COL_GN, COL_KV)

    lane = lax.broadcasted_iota(I32, (tm, LANES), 1)
    lo = lane < HEAD_DIM
    ra = ra_ref[...]
    rb = rb_ref[...]
    rc = rc_ref[...]

    def norm_rope(s, wrow):
        s2 = s * s
        slo = jnp.sum(jnp.where(lo, s2, 0.0), axis=-1, keepdims=True)
        shi = jnp.sum(jnp.where(lo, 0.0, s2), axis=-1, keepdims=True)
        msq = jnp.where(lo, slo, shi) * (1.0 / HEAD_DIM)
        y = s * lax.rsqrt(msq + RMS_EPS) * wrow
        half = ROPE_DIM // 2
        return y * ra + pltpu.roll(y, LANES - half, 1) * rb + pltpu.roll(y, half, 1) * rc

    def hi_half(y, head):
        src = pltpu.roll(y, HEAD_DIM, 1) if head == 0 else y
        return jnp.where(lo, 0.0, src)

    def lo_half(y, head):
        return y if head == 0 else pltpu.roll(y, HEAD_DIM, 1)

    zq = mm(COL_Q, COL_GN)
    qnw = qnw_ref[...]
    scale = HEAD_DIM ** -0.5 * math.log2(math.e)
    for j in range(N_HEADS // 2):
        y = norm_rope(zq[:, j * LANES:(j + 1) * LANES], qnw) * scale
        qt_ref[2 * j] = hi_half(y, 0).T.astype(BF16)
        qt_ref[2 * j + 1] = hi_half(y, 1).T.astype(BF16)

    zkv = mm(COL_KV, COL_GL)
    kc = norm_rope(zkv[:, 0:LANES], knw_ref[0:1, :])
    vc = zkv[:, LANES:2 * LANES]
    ks = norm_rope(zkv[:, 2 * LANES:3 * LANES], knw_ref[1:2, :])
    vs = zkv[:, 3 * LANES:4 * LANES]
    kw = norm_rope(zkv[:, 4 * LANES:5 * LANES], knw_ref[2:3, :])
    vw = zkv[:, 5 * LANES:6 * LANES]
    for i, rows in enumerate((kc, vc, ks, vs)):
        kvt_ref[i * LANES:(i + 1) * LANES, :] = rows.T
    for i, rows in enumerate((kw, vw)):
        wint_ref[i * LANES:(i + 1) * LANES, :] = rows.T

    row = lax.broadcasted_iota(I32, (tm, LANES), 0)
    pos = (pl.program_id(0) % tiles_per_seq) * tm + row
    onehot = jnp.where(lane == lax.shift_right_logical(pos, 6), 1.0, 0.0)
    ones_col = jnp.where(lane == HEAD_DIM, 1.0, 0.0)
    for g in range(KV_HEADS):
        ks_ref[g] = jnp.where(lo, onehot, hi_half(ks, g)).astype(BF16)
        vst_ref[g] = jnp.where(lo, lo_half(vs, g), ones_col).T.astype(BF16)
        kw_ref[g] = hi_half(kw, g).astype(BF16)
        vwt_ref[g] = jnp.where(lo, lo_half(vw, g), ones_col).T.astype(BF16)

    gates = _sigmoid(mm(COL_GL, IN_W_PAD) + gb_ref[...])
    for g in range(KV_HEADS):
        own = gates if g == 0 else pltpu.roll(gates, LANES - 3 * Q_PER_KV * g, 1)
        gt_ref[g] = own.T[:2 * SUBLANES]


def _proj_call(x2d, w_pad, norm_w, qnw, knw, gb, tabs, batch, seq):
    T, D = x2d.shape
    tm = min(256, seq)
    assert T == batch * seq and seq % tm == 0 and tm % LANES == 0
    tps = seq // tm
    row_spec = lambda w: pl.BlockSpec((tm, w), lambda i: (i, 0))
    full = lambda a: pl.BlockSpec(a.shape, lambda i: (0,) * a.ndim)
    tab_spec = pl.BlockSpec((tm, LANES), lambda i: (i % tps, 0))
    head_spec = lambda n: pl.BlockSpec((n, tm, LANES), lambda i: (0, i, 0))
    head_t_spec = lambda n, rows: pl.BlockSpec((n, rows, tm), lambda i: (0, 0, i))
    cache_t_spec = lambda rows: pl.BlockSpec((None, rows, tm), lambda i: (i // tps, 0, i % tps))
    tile_t_spec = pl.BlockSpec((KV_HEADS, None, LANES, tm), lambda i: (0, i, 0, 0))
    tile_t_sds = jax.ShapeDtypeStruct((KV_HEADS, T // tm, LANES, tm), BF16)
    out_shape = (
        jax.ShapeDtypeStruct((SSM_SLABS, T, LANES), F32),
        jax.ShapeDtypeStruct((T, SSM_W), F32),
        jax.ShapeDtypeStruct((T, NSA_W), F32),
        jax.ShapeDtypeStruct((N_HEADS, LANES, T), BF16),
        jax.ShapeDtypeStruct((batch, 4 * LANES, seq), F32),
        jax.ShapeDtypeStruct((batch, 2 * LANES, seq), F32),
        jax.ShapeDtypeStruct((KV_HEADS, 2 * SUBLANES, T), F32),
        jax.ShapeDtypeStruct((KV_HEADS, T, LANES), BF16),
        tile_t_sds,
        jax.ShapeDtypeStruct((KV_HEADS, T, LANES), BF16),
        tile_t_sds,
    )
    out_specs = (head_spec(SSM_SLABS), row_spec(SSM_W), row_spec(NSA_W), head_t_spec(N_HEADS, LANES),
                 cache_t_spec(4 * LANES), cache_t_spec(2 * LANES), head_t_spec(KV_HEADS, 2 * SUBLANES),
                 head_spec(KV_HEADS), tile_t_spec, head_spec(KV_HEADS), tile_t_spec)
    return pl.pallas_call(
        functools.partial(_proj_body, tm=tm, tiles_per_seq=tps),
        out_shape=out_shape,
        grid=(T // tm,),
        in_specs=[row_spec(D), full(norm_w), full(w_pad), full(qnw), full(knw), full(gb),
                  tab_spec, tab_spec, tab_spec],
        out_specs=out_specs,
        compiler_params=_cparams(("parallel",)),
        name="proj",
    )(x2d, norm_w, w_pad, qnw, knw, gb, *tabs)


def _prep_params(p, l):
    w_in = p["w_in"][l]
    d_model, in_w = w_in.shape
    tile2 = lambda v: jnp.tile(v, (1, LANES // HEAD_DIM))
    return {
        "w_in": jnp.pad(w_in.astype(BF16), ((0, 0), (0, IN_W_PAD - in_w))),
        "norm_w": p["norm_w"][l].reshape(1, d_model).astype(F32),
        "qnw": tile2(p["q_norm_w"][l].reshape(1, HEAD_DIM)).astype(F32),
        "knw": tile2(p["k_norm_w"][l]).astype(F32),
        "gb": jnp.pad(p["gate_b"][l].reshape(1, -1).astype(F32), ((0, 0), (0, LANES - 3 * N_HEADS))),
        "w_glu": p["w_glu"][l].astype(BF16),
        "w_out": p["w_out"][l].astype(BF16),
    }


def _rope_tables(pos):
    half = ROPE_DIM // 2
    inv = ROPE_THETA ** (-jnp.arange(half, dtype=F32) / half)
    ang = pos.astype(F32)[:, None] * inv
    cos, sin = jnp.cos(ang), jnp.sin(ang)
    n = pos.shape[0]
    rest = HEAD_DIM - ROPE_DIM
    a = jnp.concatenate([cos, cos, jnp.ones((n, rest), F32)], axis=-1)
    b = jnp.concatenate([-sin, jnp.zeros((n, HEAD_DIM - half), F32)], axis=-1)
    c = jnp.concatenate([jnp.zeros((n, half), F32), sin, jnp.zeros((n, rest), F32)], axis=-1)
    return tuple(jnp.tile(t, (1, LANES // HEAD_DIM)) for t in (a, b, c))


def _outmix_body(x_ref, y_ref, gs_ref, o_ref, gn_ref, wg_ref, wo_ref, out_ref):
    y = jnp.concatenate([y_ref[j] for j in range(SSM_SLABS)], axis=-1)
    ab = _dot(y.astype(BF16), wg_ref[...])
    gs = gs_ref[...]
    ssm = ab[:, :SSM_W] * _sigmoid(ab[:, SSM_W:]) * (gs * _sigmoid(gs))
    gn = gn_ref[...]
    nsa = o_ref[...] * (gn * _sigmoid(gn))
    acc = _dot(ssm.astype(BF16), wo_ref[0:SSM_W, :])
    acc += _dot(nsa.astype(BF16), wo_ref[SSM_W:, :])
    out_ref[...] = x_ref[...] + acc


def _outmix_call(x2d, y_ssm, g_ssm, o_nsa, g_nsa, w_glu, w_out):
    T, D = x2d.shape
    tm = min(512, T)
    row_spec = lambda w: pl.BlockSpec((tm, w), lambda i: (i, 0))
    full = lambda a: pl.BlockSpec(a.shape, lambda i: (0,) * a.ndim)
    return pl.pallas_call(
        _outmix_body,
        out_shape=jax.ShapeDtypeStruct((T, D), F32),
        grid=(T // tm,),
        in_specs=[row_spec(D), pl.BlockSpec((SSM_SLABS, tm, LANES), lambda i: (0, i, 0)),
                  row_spec(SSM_W), row_spec(NSA_W), row_spec(NSA_W),
                  full(w_glu), full(w_out)],
        out_specs=row_spec(D),
        compiler_params=_cparams(("parallel",)),
        name="outmix",
    )(x2d, y_ssm, g_ssm, o_nsa, g_nsa, w_glu, w_out)


def _gelu_tanh(x):
    c = math.sqrt(2.0 / math.pi)
    return 0.5 * x * (1.0 + jnp.tanh(c * (x + 0.044715 * (x * x * x))))


def _compress_rows(rows_refs, pe_ref, wa_ref, wb_ref, b1_ref, w2_ref, kc_ref, vc_ref, nch):
    lane = lax.broadcasted_iota(I32, (nch, LANES), 1)
    for kvi, out_ref in ((0, kc_ref), (1, vc_ref)):
        rows_ref = rows_refs[kvi]
        pa = jnp.zeros((nch, 2 * CMP_HID), F32)
        pb = jnp.zeros((nch, 2 * CMP_HID), F32)
        for j0 in range(0, CMP_STRIDE, 2):
            xs = [rows_ref[pl.ds(j, nch, stride=CMP_STRIDE), :] for j in (j0, j0 + 1)]
            xa = jnp.concatenate([xs[i] + pe_ref[kvi, 0, j0 + i:j0 + i + 1, :] for i in range(2)], axis=-1)
            xb = jnp.concatenate([xs[i] + pe_ref[kvi, 1, j0 + i:j0 + i + 1, :] for i in range(2)], axis=-1)
            wsl = slice(j0 * LANES, (j0 + 2) * LANES)
            pa += _dot(xa.astype(BF16), wa_ref[kvi, wsl, :])
            pb += _dot(xb.astype(BF16), wb_ref[kvi, wsl, :])
        hid = _gelu_tanh(pa + pltpu.roll(pb, nch - 1, 0) + b1_ref[kvi]).astype(BF16)
        for g in range(KV_HEADS):
            o = _dot(hid, w2_ref[kvi, g])
            if kvi == 1:
                o = jnp.where(lane == HEAD_DIM, 1.0, o).T
            out_ref[g] = o.astype(BF16)


def _compress_prompt_body(kvt_ref, pe_ref, wa_ref, wb_ref, b1_ref, w2_ref, kc_ref, vc_ref,
                          krows_ref, vrows_ref, *, nch):
    for c in range(kvt_ref.shape[1] // LANES):
        cs = slice(c * LANES, (c + 1) * LANES)
        krows_ref[cs, :] = kvt_ref[0:LANES, cs].T
        vrows_ref[cs, :] = kvt_ref[LANES:2 * LANES, cs].T
    _compress_rows((krows_ref, vrows_ref), pe_ref, wa_ref, wb_ref, b1_ref, w2_ref, kc_ref, vc_ref, nch)


def _compress_prompt_call(kv_t, cw, batch, seq):
    nch = seq // CMP_STRIDE
    full = lambda a: pl.BlockSpec(a.shape, lambda b: (0,) * a.ndim)
    out_spec = pl.BlockSpec((KV_HEADS, nch, LANES), lambda b: (0, b, 0))
    out_sds = jax.ShapeDtypeStruct((KV_HEADS, batch * nch, LANES), BF16)
    out_t_spec = pl.BlockSpec((KV_HEADS, None, LANES, nch), lambda b: (0, b, 0, 0))
    out_t_sds = jax.ShapeDtypeStruct((KV_HEADS, batch, LANES, nch), BF16)
    return pl.pallas_call(
        functools.partial(_compress_prompt_body, nch=nch),
        out_shape=(out_sds, out_t_sds),
        grid=(batch,),
        in_specs=[pl.BlockSpec((None, 2 * LANES, seq), lambda b: (b, 0, 0)),
                  full(cw["pe"]), full(cw["wa"]), full(cw["wb"]), full(cw["b1"]), full(cw["w2"])],
        out_specs=(out_spec, out_t_spec),
        scratch_shapes=[pltpu.VMEM((seq, LANES), F32)] * 2,
        compiler_params=_cparams(("parallel",)),
        name="compress_prompt",
    )(kv_t, cw["pe"], cw["wa"], cw["wb"], cw["b1"], cw["w2"])


def _prep_compress(p, l):
    eye = jnp.eye(KV_HEADS, dtype=F32)
    w1 = p["cmp_w1"][l].reshape(2, 2, CMP_STRIDE, HEAD_DIM, CMP_HID)
    wexp = jnp.einsum("khjdn,ge->khjgden", w1, eye).reshape(2, 2, CMP_STRIDE * LANES, KV_HEADS * CMP_HID)
    pe = p["cmp_pe"][l].reshape(2, 2, CMP_STRIDE, HEAD_DIM)
    w2 = p["cmp_w2"][l]
    zeros = jnp.zeros_like(w2[0])
    w2k = jnp.concatenate([zeros, w2[0]], axis=-1)
    w2v = jnp.concatenate([w2[1], zeros], axis=-1)
    w2e = jnp.stack([jnp.einsum("hd,ge->gehd", w, eye).reshape(KV_HEADS, KV_HEADS * CMP_HID, LANES)
                     for w in (w2k, w2v)])
    return {
        "pe": jnp.tile(pe, (1, 1, 1, KV_HEADS)).astype(F32),
        "wa": wexp[:, 0].astype(BF16),
        "wb": wexp[:, 1].astype(BF16),
        "b1": jnp.tile(p["cmp_b1"][l].reshape(2, 1, CMP_HID), (1, 1, KV_HEADS)).astype(F32),
        "w2": w2e.astype(BF16),
    }


def _overlap_matrix(n_tok_pad, n_tok, n_blk, n_cols=LANES):
    c_start = jnp.arange(n_tok_pad) * CMP_STRIDE
    blk = jnp.arange(n_cols)
    ov = ((c_start[:, None] < (blk[None, :] + 1) * SLC_BLOCK)
          & (c_start[:, None] + CMP_BLOCK > blk[None, :] * SLC_BLOCK)
          & (jnp.arange(n_tok_pad)[:, None] < n_tok) & (blk[None, :] < n_blk))
    return ov.astype(BF16)


def _topk_select_t(w_ref, imp_t, q0, tq):
    nb = imp_t.shape[0]
    n_i = lax.broadcasted_iota(I32, (nb, tq), 0)
    qblk = lax.shift_right_logical(q0 + lax.broadcasted_iota(I32, (nb, tq), 1), 6)
    causal = n_i <= qblk
    forced = (n_i == 0) | (n_i >= qblk - (N_LOCAL_BLOCKS - 1))
    w_ref[...] = jnp.where(causal, jnp.where(forced, jnp.inf, imp_t), -jnp.inf)
    last_blk = lax.shift_right_logical(q0 + tq - 1, 6)
    n_grp = nb // SUBLANES
    rank = [jnp.zeros((SUBLANES, tq), F32) for _ in range(n_grp)]
    grp_i = lax.broadcasted_iota(I32, (SUBLANES, tq), 0)

    def count_group(mg, rank):
        rank = list(rank)
        for mi in range(SUBLANES):
            m = mg * SUBLANES + mi
            wm = w_ref[m:m + 1, :]
            for ng in range(n_grp):
                w = w_ref[ng * SUBLANES:(ng + 1) * SUBLANES, :]
                if ng > mg:
                    beats = jnp.where(wm >= w, 1.0, 0.0)
                elif ng < mg:
                    beats = jnp.where(wm > w, 1.0, 0.0)
                else:
                    beats = jnp.where(grp_i > mi, jnp.where(wm >= w, 1.0, 0.0), jnp.where(wm > w, 1.0, 0.0))
                rank[ng] = rank[ng] + beats
        return tuple(rank)

    rank = tuple(rank)
    for mg in range(n_grp):
        rank = lax.cond(mg * SUBLANES <= last_blk, functools.partial(count_group, mg), lambda r: r, rank)
    return causal & (jnp.concatenate(rank, axis=0) < TOP_N)


def _flash_tiles_t(k, q_ts, v_t, mask, ms, accs):
    s = [_dot(k, q_t) for q_t in q_ts]
    if mask is not None:
        s = [jnp.where(mask, x, NEG_INF) for x in s]
    m_new = [jnp.maximum(m, jnp.max(x, axis=0, keepdims=True)) for m, x in zip(ms, s)]
    alpha = [jnp.exp2(m - mn) for m, mn in zip(ms, m_new)]
    p = [jnp.exp2(x - mn).astype(BF16) for x, mn in zip(s, m_new)]
    pv = [_dot(v_t, x) for x in p]
    accs = [a * acc + x for a, acc, x in zip(alpha, accs, pv)]
    return tuple(m_new), tuple(accs)


def _attn_body(qt_ref, kc_ref, vct_ref, ovlt_ref, ks_ref, vst_ref, kw_ref, vwt_ref, gate_ref, o_ref,
               qa_ref, ocmp_ref, w_ref, *, tq, ncp):
    R = Q_PER_KV
    qt = pl.program_id(2)
    q0 = qt * tq
    nbs = LANES // 2

    c_i = lax.broadcasted_iota(I32, (ncp, tq), 0)
    qpos_c = q0 + lax.broadcasted_iota(I32, (ncp, tq), 1)
    cmask = c_i * CMP_STRIDE + (CMP_BLOCK - 1) <= qpos_c
    kc = kc_ref[...]
    imp = jnp.zeros((LANES, tq), F32)
    for r in range(R):
        s = jnp.where(cmask, _dot(kc, qt_ref[r]), NEG_INF)
        e = jnp.where(cmask, jnp.exp2(s - jnp.max(s, axis=0, keepdims=True)), 0.0)
        l = jnp.sum(e, axis=0, keepdims=True)
        p = (e / jnp.where(l > 0.0, l, 1.0)).astype(BF16)
        imp += _dot(ovlt_ref[...], p)
        ocmp_ref[r] = _dot(vct_ref[...], p)

    sel = _topk_select_t(w_ref, imp[:nbs], q0, tq)
    bias = jnp.concatenate([jnp.where(sel, 0.0, NEG_INF), jnp.zeros((LANES - nbs, tq), F32)], axis=0)
    for r in range(R):
        qa_ref[r] = (qt_ref[r].astype(F32) + bias).astype(BF16)

    key_i = lax.broadcasted_iota(I32, (tq, tq), 0)
    qry_i = lax.broadcasted_iota(I32, (tq, tq), 1)
    init = (tuple(jnp.full((1, tq), -jnp.inf, F32) for _ in range(R)),
            tuple(jnp.zeros((LANES, tq), F32) for _ in range(R)))

    def tile(k_ref, vt_ref, q_ref, j, mask, state):
        off = pl.multiple_of(j * tq, tq)
        return _flash_tiles_t(k_ref[pl.ds(off, tq), :], [q_ref[r] for r in range(R)], vt_ref[j], mask, *state)

    def slc_tile(j, mask, state):
        return tile(ks_ref, vst_ref, qa_ref, j, mask, state)

    def slc_pair(i, st):
        return slc_tile(2 * i + 1, None, slc_tile(2 * i, None, st))

    state = lax.fori_loop(0, qt // 2, slc_pair, init)
    state = lax.cond(qt % 2 == 1, lambda st: slc_tile(qt - 1, None, st), lambda st: st, state)
    _, acc_s = slc_tile(qt, key_i <= qry_i, state)

    nwin = WINDOW // tq

    def win_tile(d, mask, state):
        return tile(kw_ref, vwt_ref, qt_ref, qt - d, mask, state)

    state = init
    for d in range(nwin, 0, -1):
        state = lax.cond(qt >= d, lambda st, d=d: win_tile(d, (key_i > qry_i) if d == nwin else None, st),
                         lambda st: st, state)
    _, acc_w = win_tile(0, key_i <= qry_i, state)

    outs = []
    for r in range(R):
        a_s = acc_s[r]
        a_w = acc_w[r]
        g = lambda k: gate_ref[3 * r + k:3 * r + k + 1, :]
        o = (g(0) * ocmp_ref[r] + g(1) * (a_s / a_s[HEAD_DIM:HEAD_DIM + 1, :])
             + g(2) * (a_w / a_w[HEAD_DIM:HEAD_DIM + 1, :]))
        outs.append(o[:HEAD_DIM])
    o_ref[...] = jnp.concatenate(outs, axis=0).T


def _attn_call(q_t, kc, vc_t, ovl_t, ks, vs_t, kw, vw_t, gates_t, batch, seq):
    tq = min(256, seq)
    nq = seq // tq
    ncp = kc.shape[1] // batch
    R = Q_PER_KV
    assert seq // SLC_BLOCK <= LANES // 2 and WINDOW % tq == 0
    k_spec = pl.BlockSpec((None, seq, LANES), lambda b, g, t: (g, b, 0))
    vt_spec = pl.BlockSpec((None, nq, LANES, tq), lambda b, g, t: (g, b, 0, 0))
    acc = pltpu.VMEM((R, LANES, tq), F32)
    return pl.pallas_call(
        functools.partial(_attn_body, tq=tq, ncp=ncp),
        out_shape=jax.ShapeDtypeStruct((batch * seq, NSA_W), F32),
        grid=(batch, KV_HEADS, nq),
        in_specs=[pl.BlockSpec((R, LANES, tq), lambda b, g, t: (g, 0, b * nq + t)),
                  pl.BlockSpec((None, ncp, LANES), lambda b, g, t: (g, b, 0)),
                  pl.BlockSpec((None, None, LANES, ncp), lambda b, g, t: (g, b, 0, 0)),
                  pl.BlockSpec(ovl_t.shape, lambda b, g, t: (0, 0)),
                  k_spec, vt_spec, k_spec, vt_spec,
                  pl.BlockSpec((None, 2 * SUBLANES, tq), lambda b, g, t: (g, 0, b * nq + t))],
        out_specs=pl.BlockSpec((tq, R * HEAD_DIM), lambda b, g, t: (b * nq + t, g)),
        scratch_shapes=[pltpu.VMEM((R, LANES, tq), BF16), acc, pltpu.VMEM((LANES // 2, tq), F32)],
        compiler_params=_cparams(("parallel", "parallel", "arbitrary")),
        name="nsa_prompt",
    )(q_t, kc, vc_t, ovl_t, ks, vs_t, kw, vw_t, gates_t)


def _s5_discretise(p, l):
    lr = p["ssm_lam_re"][l].astype(F32)
    li = p["ssm_lam_im"][l].astype(F32)
    dt = jnp.exp(p["ssm_log_step"][l].astype(F32))[:, None]

    def apow(t):
        mag, ang = jnp.exp(lr * dt * t), li * dt * t
        return mag * jnp.cos(ang), mag * jnp.sin(ang)

    a_re, a_im = apow(1.0)
    den = lr * lr + li * li
    nr, ni = a_re - 1.0, a_im
    f_re, f_im = (nr * lr + ni * li) / den, (ni * lr - nr * li) / den
    br, bi = p["ssm_b_re"][l].astype(F32), p["ssm_b_im"][l].astype(F32)
    bb_re = f_re[..., None] * br - f_im[..., None] * bi
    bb_im = f_re[..., None] * bi + f_im[..., None] * br
    return apow, bb_re, bb_im


def _prep_s5_step(p, l):
    apow, bb_re, bb_im = _s5_discretise(p, l)
    a_re, a_im = apow(1.0)
    G = bb_re.shape[0]
    eye = jnp.eye(G, dtype=F32)
    to_state = lambda bb: jnp.einsum("gnk,ge->gken", bb, eye).reshape(SSM_W, G * SSM_STATE)
    c_re, c_im = p["ssm_c_re"][l].astype(F32), p["ssm_c_im"][l].astype(F32)
    from_state = lambda c: jnp.einsum("gcn,ge->gnec", c, eye).reshape(G * SSM_STATE, SSM_W)
    return {
        "w_x": jnp.concatenate([to_state(bb_re), to_state(bb_im)], axis=1).astype(BF16),
        "w_y": jnp.concatenate([from_state(c_re), from_state(-c_im)], axis=0).astype(BF16),
        "a_re": a_re.reshape(1, -1), "a_im": a_im.reshape(1, -1),
        "d": p["ssm_d"][l].reshape(1, SSM_W).astype(F32),
    }


def _prep_s5_chunked(p, l):
    hi = lax.Precision.HIGHEST
    apow, bb_re, bb_im = _s5_discretise(p, l)
    c_re, c_im = p["ssm_c_re"][l].astype(F32), p["ssm_c_im"][l].astype(F32)
    T, J, E = SSM_CHUNK, SSM_SLABS, SSM_SLAB_GROUPS
    pw_re, pw_im = apow(jnp.arange(T + 1, dtype=F32)[:, None, None])
    p_re = pw_re[..., None] * bb_re - pw_im[..., None] * bb_im
    p_im = pw_re[..., None] * bb_im + pw_im[..., None] * bb_re
    eye = jnp.eye(E, dtype=F32)
    slab = lambda x: x.reshape(x.shape[0], J, E, *x.shape[2:])
    kt = (jnp.einsum("tgnk,gcn->tgkc", p_re[:T], c_re, precision=hi)
          - jnp.einsum("tgnk,gcn->tgkc", p_im[:T], c_im, precision=hi))
    kbd = jnp.einsum("tjgkc,ge->jtgkec", slab(kt), eye).reshape(J, T, LANES, LANES)
    w_col = kbd[:, ::-1].reshape(J, T * LANES, LANES)
    to_state = lambda x: jnp.einsum("sjgnk,ge->jsgken", slab(x[:T][::-1]), eye).reshape(J, T * LANES, E * SSM_STATE)
    w_st = jnp.concatenate([to_state(p_re), to_state(p_im)], axis=-1)
    cp_re = c_re[None, :, :, :] * pw_re[1:, :, None, :] - c_im[None] * pw_im[1:, :, None, :]
    cp_im = c_re[None, :, :, :] * pw_im[1:, :, None, :] + c_im[None] * pw_re[1:, :, None, :]
    from_state = lambda x: jnp.einsum("tjgcn,ge->jgntec", slab(x), eye).reshape(J, E * SSM_STATE, T * LANES)
    w_out = jnp.concatenate([from_state(cp_re), from_state(-cp_im)], axis=1)
    return {
        "w_col": w_col.astype(BF16), "w_st": w_st.astype(BF16), "w_out": w_out.astype(BF16),
        "a_re": pw_re[T].reshape(J, 1, E * SSM_STATE), "a_im": pw_im[T].reshape(J, 1, E * SSM_STATE),
        "d": p["ssm_d"][l].reshape(J, 1, LANES).astype(F32),
    }


def _s5_prompt_body(u_ref, wcol_ref, wst_ref, are_ref, aim_ref, wout_ref, d_ref,
                    y_ref, hre_ref, him_ref, xs_ref, hp_ref, *, n_chunks):
    T = SSM_CHUNK
    ns = SSM_SLAB_GROUPS * SSM_STATE
    u_pos = [u_ref[pl.ds(s, n_chunks, stride=T), :] for s in range(T)]
    ub = jnp.concatenate(u_pos, axis=-1).astype(BF16)
    xs_ref[...] = _dot(ub, wst_ref[...])
    a_re = are_ref[...]
    a_im = aim_ref[...]

    def step(c, carry):
        hr, hi = carry
        hp_ref[pl.ds(c, 1), 0:ns] = hr
        hp_ref[pl.ds(c, 1), ns:2 * ns] = hi
        xr = xs_ref[pl.ds(c, 1), 0:ns]
        xi = xs_ref[pl.ds(c, 1), ns:2 * ns]
        return a_re * hr - a_im * hi + xr, a_re * hi + a_im * hr + xi

    zero = jnp.zeros((1, ns), F32)
    hr, hi = lax.fori_loop(0, n_chunks, step, (zero, zero))
    hre_ref[...] = jnp.broadcast_to(hr, hre_ref.shape)
    him_ref[...] = jnp.broadcast_to(hi, him_ref.shape)
    hpb = hp_ref[...].astype(BF16)
    for t in range(T):
        sl = slice(t * LANES, (t + 1) * LANES)
        y_ref[pl.ds(t, n_chunks, stride=T), :] = (
            _dot(ub[:, :(t + 1) * LANES], wcol_ref[(T - 1 - t) * LANES:, :])
            + _dot(hpb, wout_ref[:, sl]) + d_ref[...] * u_pos[t])


def _s5_prompt_call(u_slab, sw, batch, seq):
    T, J = SSM_CHUNK, SSM_SLABS
    n_chunks = seq // T
    ns = SSM_SLAB_GROUPS * SSM_STATE
    row_spec = pl.BlockSpec((None, seq, LANES), lambda b, j: (j, b, 0))
    slab_spec = lambda a: pl.BlockSpec((None,) + a.shape[1:], lambda b, j: (j, 0, 0))
    st_spec = pl.BlockSpec((None, None, SUBLANES, ns), lambda b, j: (b, j, 0, 0))
    st_sds = jax.ShapeDtypeStruct((batch, J, SUBLANES, ns), F32)
    y, hre, him = pl.pallas_call(
        functools.partial(_s5_prompt_body, n_chunks=n_chunks),
        out_shape=(jax.ShapeDtypeStruct(u_slab.shape, F32), st_sds, st_sds),
        grid=(batch, J),
        in_specs=[row_spec, slab_spec(sw["w_col"]), slab_spec(sw["w_st"]), slab_spec(sw["a_re"]),
                  slab_spec(sw["a_im"]), slab_spec(sw["w_out"]), slab_spec(sw["d"])],
        out_specs=(row_spec, st_spec, st_spec),
        scratch_shapes=[pltpu.VMEM((n_chunks, 2 * ns), F32), pltpu.VMEM((n_chunks, 2 * ns), F32)],
        compiler_params=_cparams(("parallel", "parallel")),
        name="s5_prompt",
    )(u_slab, sw["w_col"], sw["w_st"], sw["a_re"], sw["a_im"], sw["w_out"], sw["d"])
    n_groups = J * SSM_SLAB_GROUPS
    state = lambda h: h[:, :, 0, :].reshape(batch, n_groups, SSM_STATE)
    return y, state(hre), state(him)


def _s5_step_body(u_ref, wx_ref, are_ref, aim_ref, h0re_ref, h0im_ref, wy_ref, d_ref,
                  y_ref, hre_ref, him_ref):
    u = jnp.concatenate([u_ref[j] for j in range(SSM_SLABS)], axis=-1)
    x = _dot(u.astype(BF16), wx_ref[...])
    ns = h0re_ref.shape[1]
    a_re, a_im = are_ref[...], aim_ref[...]
    h0r, h0i = h0re_ref[...], h0im_ref[...]
    hr = a_re * h0r - a_im * h0i + x[:, :ns]
    hi = a_re * h0i + a_im * h0r + x[:, ns:]
    hre_ref[...] = hr
    him_ref[...] = hi
    y = _dot(jnp.concatenate([hr, hi], axis=-1).astype(BF16), wy_ref[...]) + d_ref[...] * u
    for j in range(SSM_SLABS):
        y_ref[j] = y[:, j * LANES:(j + 1) * LANES]


def _s5_step_call(u_slab, h0_re, h0_im, sw):
    J, n_tok, _ = u_slab.shape
    ns = h0_re.shape[1]
    args = (u_slab, sw["w_x"], sw["a_re"], sw["a_im"], h0_re, h0_im, sw["w_y"], sw["d"])
    full = lambda a: pl.BlockSpec(a.shape, lambda i: (0,) * a.ndim)
    st_sds = jax.ShapeDtypeStruct((n_tok, ns), F32)
    return pl.pallas_call(
        _s5_step_body,
        out_shape=(jax.ShapeDtypeStruct(u_slab.shape, F32), st_sds, st_sds),
        grid=(1,),
        in_specs=[full(a) for a in args],
        out_specs=(full(u_slab), full(h0_re), full(h0_re)),
        compiler_params=_cparams(("arbitrary",)),
        name="s5_step",
    )(*args)


def _compress_sample_body(pt_ref, *refs, pages_per_step, page_rows, nch):
    del pt_ref
    page_refs = refs[:pages_per_step]
    pe_ref, wa_ref, wb_ref, b1_ref, w2_ref, kc_ref, vc_ref, krows_ref, vrows_ref = refs[pages_per_step:]
    pg = pl.program_id(1)
    for i, page_ref in enumerate(page_refs):
        off = pl.multiple_of((pg * pages_per_step + i) * page_rows, page_rows)
        krows_ref[pl.ds(off, page_rows), :] = page_ref[0].reshape(LANES, page_rows).T
        vrows_ref[pl.ds(off, page_rows), :] = page_ref[1].reshape(LANES, page_rows).T

    @pl.when(pg == pl.num_programs(1) - 1)
    def _():
        _compress_rows((krows_ref, vrows_ref), pe_ref, wa_ref, wb_ref, b1_ref, w2_ref, kc_ref, vc_ref, nch)


def _compress_sample_call(cache, page_table, cw):
    n_seq, n_pages = page_table.shape
    page_rows = cache.shape[-1]
    assert page_rows == LANES
    pps = min(16, n_pages)
    assert n_pages % pps == 0
    nch = n_pages * page_rows // CMP_STRIDE
    page_spec = lambda i: pl.BlockSpec((None, 2, KV_HEADS, HEAD_DIM, page_rows),
                                       lambda b, pg, pt: (pt[b, pg * pps + i], 0, 0, 0, 0))
    full = lambda a: pl.BlockSpec(a.shape, lambda b, pg, pt: (0,) * a.ndim)
    out_spec = pl.BlockSpec((KV_HEADS, nch, LANES), lambda b, pg, pt: (0, b, 0))
    out_sds = jax.ShapeDtypeStruct((KV_HEADS, n_seq * nch, LANES), BF16)
    out_t_spec = pl.BlockSpec((KV_HEADS, None, LANES, nch), lambda b, pg, pt: (0, b, 0, 0))
    out_t_sds = jax.ShapeDtypeStruct((KV_HEADS, n_seq, LANES, nch), BF16)
    weights = (cw["pe"], cw["wa"], cw["wb"], cw["b1"], cw["w2"])
    return pl.pallas_call(
        functools.partial(_compress_sample_body, pages_per_step=pps, page_rows=page_rows, nch=nch),
        out_shape=(out_sds, out_t_sds),
        grid_spec=pltpu.PrefetchScalarGridSpec(
            num_scalar_prefetch=1,
            grid=(n_seq, n_pages // pps),
            in_specs=[page_spec(i) for i in range(pps)] + [full(a) for a in weights],
            out_specs=(out_spec, out_t_spec),
            scratch_shapes=[pltpu.VMEM((n_pages * page_rows, LANES), F32)] * 2),
        compiler_params=_cparams(("parallel", "arbitrary")),
        name="compress_sample",
    )(page_table, *([cache] * pps), *weights)


def _group_rows(x0, x1):
    row = lax.broadcasted_iota(I32, x0.shape, 0)
    return jnp.where(row < Q_PER_KV, x0, x1)


def _sample_select_body(q_ref, kc_ref, vct_ref, ovl_ref, tri_ref, ocmp_ref, idx_ref, *, ncp, qpos, nbp):
    q8 = q_ref[...]
    c_i = lax.broadcasted_iota(I32, (N_HEADS, ncp), 1)
    cmask = c_i * CMP_STRIDE + (CMP_BLOCK - 1) <= qpos
    s = _group_rows(_dot_nt(q8, kc_ref[0]), _dot_nt(q8, kc_ref[1]))
    s = jnp.where(cmask, s, NEG_INF)
    e = jnp.where(cmask, jnp.exp2(s - jnp.max(s, axis=-1, keepdims=True)), 0.0)
    l = jnp.sum(e, axis=-1, keepdims=True)
    p = (e / jnp.where(l > 0.0, l, 1.0)).astype(BF16)
    ocmp_ref[...] = _group_rows(_dot_nt(p, vct_ref[0]), _dot_nt(p, vct_ref[1]))
    imp8 = _dot(p, ovl_ref[...])

    n_row = lax.broadcasted_iota(I32, (1, nbp), 1)
    qblk = qpos // SLC_BLOCK
    causal = n_row <= qblk
    forced = (n_row == 0) | (n_row >= qblk - (N_LOCAL_BLOCKS - 1))
    m_i = lax.broadcasted_iota(I32, (nbp, nbp), 0)
    n_i = lax.broadcasted_iota(I32, (nbp, nbp), 1)
    lane = lax.broadcasted_iota(I32, (1, LANES), 1)
    idx_rows = []
    for g in range(KV_HEADS):
        imp = jnp.sum(imp8[g * Q_PER_KV:(g + 1) * Q_PER_KV], axis=0, keepdims=True)
        w = jnp.where(causal, jnp.where(forced, jnp.inf, imp), -jnp.inf)
        w_sq = jnp.broadcast_to(w, (nbp, nbp))
        w_col = w_sq.T
        beats = jnp.where(n_i > m_i, jnp.where(w_col >= w_sq, 1.0, 0.0), jnp.where(w_col > w_sq, 1.0, 0.0))
        rank = jnp.sum(beats, axis=0, keepdims=True)
        sel = causal & (rank < TOP_N)
        self_f = jnp.where(sel, 1.0, 0.0)
        before = _dot(self_f.astype(BF16), tri_ref[...])
        idx = jnp.full((1, LANES), -1, I32)
        for k in range(TOP_N):
            hit = sel & (before == float(k))
            val = jnp.sum(jnp.where(hit, n_row.astype(F32) + 1.0, 0.0), axis=-1, keepdims=True) - 1.0
            idx = jnp.where(lane == k, val.astype(I32), idx)
        idx_rows.append(idx)
    idx_ref[...] = jnp.concatenate(idx_rows + [jnp.full((SUBLANES - KV_HEADS, LANES), -1, I32)], axis=0)


def _sample_select_call(q8, kc, vc_t, ovl, tri, qpos):
    n_seq = q8.shape[0]
    ncp = kc.shape[1] // n_seq
    nbp = ovl.shape[1]
    cmp_spec = pl.BlockSpec((KV_HEADS, ncp, LANES), lambda b: (0, b, 0))
    row_spec = pl.BlockSpec((None, N_HEADS, LANES), lambda b: (b, 0, 0))
    full = lambda a: pl.BlockSpec(a.shape, lambda b: (0,) * a.ndim)
    return pl.pallas_call(
        functools.partial(_sample_select_body, ncp=ncp, qpos=qpos, nbp=nbp),
        out_shape=(jax.ShapeDtypeStruct((n_seq, N_HEADS, LANES), F32),
                   jax.ShapeDtypeStruct((n_seq, SUBLANES, LANES), I32)),
        grid=(n_seq,),
        in_specs=[row_spec, cmp_spec, pl.BlockSpec((KV_HEADS, None, LANES, ncp), lambda b: (0, b, 0, 0)),
                  full(ovl), full(tri)],
        out_specs=(row_spec, pl.BlockSpec((None, SUBLANES, LANES), lambda b: (b, 0, 0))),
        compiler_params=_cparams(("parallel",)),
        name="nsa_sample_select",
    )(q8, kc, vc_t, ovl, tri)


def _sample_attend_body(idx_ref, pt_ref, q_ref, ocmp_ref, gate_ref, ksn_ref, vsn_ref, kwn_ref, vwn_ref,
                        win_ref, *refs, n_cache_blocks, blocks_per_page, win_skip):
    del pt_ref
    n_blk = KV_HEADS * TOP_N
    k_refs, v_refs, o_ref = refs[:n_blk], refs[n_blk:2 * n_blk], refs[2 * n_blk]
    b = pl.program_id(0)
    q = q_ref[...]
    qf = q.astype(F32)
    row_g = (lax.broadcasted_iota(I32, (N_HEADS, 1), 0) >= Q_PER_KV).astype(I32)

    def attend(s_list, v_list, kn_ref, vn_ref):
        s_self = jnp.sum(qf * kn_ref[...].astype(F32), axis=-1, keepdims=True)
        m = s_self
        for s in s_list:
            m = jnp.maximum(m, jnp.max(s, axis=-1, keepdims=True))
        p_self = jnp.exp2(s_self - m)
        l = p_self
        acc = p_self.astype(BF16).astype(F32) * vn_ref[...].astype(F32)
        for s, v in zip(s_list, v_list):
            p = jnp.exp2(s - m)
            l = l + jnp.sum(p, axis=-1, keepdims=True)
            acc = acc + _dot_nt(p.astype(BF16), v().astype(BF16))
        return acc / l

    s_list, v_list = [], []
    for j in range(n_blk):
        s = _dot(q, k_refs[j][...].astype(BF16))
        col = lax.broadcasted_iota(I32, s.shape, 1)
        n = idx_ref[b, j]
        first = (n % blocks_per_page) * SLC_BLOCK
        ok = ((row_g == j // TOP_N) & (col >= first) & (col < first + SLC_BLOCK)
              & (n >= 0) & (n < n_cache_blocks))
        s_list.append(jnp.where(ok, s, NEG_INF))
        v_list.append(lambda j=j: v_refs[j][...])
    o_slc = attend(s_list, v_list, ksn_ref, vsn_ref)

    s_list, v_list = [], []
    for g in range(KV_HEADS):
        s = _dot(q, win_ref[0, g].astype(BF16))
        col = lax.broadcasted_iota(I32, s.shape, 1)
        s_list.append(jnp.where((row_g == g) & (col >= win_skip), s, NEG_INF))
        v_list.append(lambda g=g: win_ref[1, g])
    o_win = attend(s_list, v_list, kwn_ref, vwn_ref)

    gates = gate_ref[...]
    o_ref[...] = gates[:, 0:1] * ocmp_ref[:, 0:HEAD_DIM] + gates[:, 1:2] * o_slc + gates[:, 2:3] * o_win


def _sample_attend_call(idx, page_table, q64, ocmp, gates8, ksn, vsn, kwn, vwn, cache_win_t, cache_t, win_skip):
    n_seq, n_pages = page_table.shape
    page_rows = cache_t.shape[-1]
    bpp = page_rows // SLC_BLOCK
    n_cache_blocks = n_pages * bpp
    row_spec = lambda a: pl.BlockSpec((None,) + a.shape[1:], lambda b, ix, pt: (b,) + (0,) * (a.ndim - 1))

    def blk_spec(j, slot):
        def index_map(b, ix, pt):
            n = jnp.clip(ix[b, j], 0, n_cache_blocks - 1)
            return (pt[b, n // bpp], slot, j // TOP_N, 0, 0)
        return pl.BlockSpec((None, None, None, HEAD_DIM, page_rows), index_map)

    n_blk = KV_HEADS * TOP_N
    small = (q64, ocmp, gates8, ksn, vsn, kwn, vwn, cache_win_t)
    return pl.pallas_call(
        functools.partial(_sample_attend_body, n_cache_blocks=n_cache_blocks, blocks_per_page=bpp,
                          win_skip=win_skip),
        out_shape=jax.ShapeDtypeStruct((n_seq, N_HEADS, HEAD_DIM), F32),
        grid_spec=pltpu.PrefetchScalarGridSpec(
            num_scalar_prefetch=2,
            grid=(n_seq,),
            in_specs=([row_spec(a) for a in small] + [blk_spec(j, 2) for j in range(n_blk)]
                      + [blk_spec(j, 3) for j in range(n_blk)]),
            out_specs=pl.BlockSpec((None, N_HEADS, HEAD_DIM), lambda b, ix, pt: (b, 0, 0))),
        compiler_params=_cparams(("parallel",)),
        name="nsa_sample_attend",
    )(idx, page_table, *small, *([cache_t] * (2 * n_blk)))


def _round_up(x, m):
    return -(-x // m) * m


def _prompt_layer(h, p, l, batch, seq):
    prm = _prep_params(p, l)
    tabs = _rope_tables(jnp.arange(seq))
    u, gs, gn, q_t, kv_t, win_t, gates_t, ks, vs_t, kw, vw_t = _proj_call(
        h, prm["w_in"], prm["norm_w"], prm["qnw"], prm["knw"], prm["gb"], tabs, batch, seq)
    y_ssm, h_re, h_im = _s5_prompt_call(u, _prep_s5_chunked(p, l), batch, seq)
    kc, vc_t = _compress_prompt_call(kv_t, _prep_compress(p, l), batch, seq)
    nch = seq // CMP_STRIDE
    ovl_t = _overlap_matrix(nch, nch - 1, seq // SLC_BLOCK).T
    o = _attn_call(q_t, kc, vc_t, ovl_t, ks, vs_t, kw, vw_t, gates_t, batch, seq)
    h_new = _outmix_call(h, y_ssm, gs, o, gn, prm["w_glu"], prm["w_out"])
    rows = lambda x_t, slots: x_t.reshape(batch, slots, KV_HEADS, HEAD_DIM, seq).transpose(0, 4, 1, 2, 3)
    return h_new, rows(kv_t, 4), rows(win_t, 2)[:, seq - min(WINDOW, seq):], h_re, h_im


def _sample_layer(h, p, l, cache_kv, cache_win, st_re, st_im, page_table):
    n_seq = h.shape[0]
    n_phys, page_rows = cache_kv.shape[:2]
    n_pages = page_table.shape[1]
    past_len = n_pages * page_rows
    win_buf = cache_win.shape[1]
    prm = _prep_params(p, l)
    n_pad = _round_up(n_seq, LANES)
    tabs = _rope_tables(jnp.full((n_pad,), past_len, I32))
    h_pad = jnp.pad(h, ((0, n_pad - n_seq), (0, 0)))
    u, gs, gn, q_t, kv_t, win_t, gates_t, ks, vs_t, kw, vw_t = _proj_call(
        h_pad, prm["w_in"], prm["norm_w"], prm["qnw"], prm["knw"], prm["gb"], tabs, 1, n_pad)
    u, gs, gn = u[:, :n_seq], gs[:n_seq], gn[:n_seq]
    n_state = st_re.shape[1] * st_re.shape[2]
    y_ssm, h_re, h_im = _s5_step_call(u, st_re.reshape(n_seq, n_state), st_im.reshape(n_seq, n_state),
                                      _prep_s5_step(p, l))
    cache_t = cache_kv.transpose(0, 2, 3, 4, 1)
    kc, vc_t = _compress_sample_call(cache_t, page_table, _prep_compress(p, l))
    ncp = past_len // CMP_STRIDE
    n_blk = -(-(past_len + 1) // SLC_BLOCK)
    nbp = _round_up(n_blk, LANES)
    ovl = _overlap_matrix(ncp, ncp - 1, n_blk, nbp)
    tri = (jnp.arange(nbp)[:, None] < jnp.arange(nbp)[None, :]).astype(BF16)
    q8 = q_t[:, :, :n_seq].transpose(2, 0, 1)
    ocmp, idx = _sample_select_call(q8, kc, vc_t, ovl, tri, past_len)
    idx = idx[:, :KV_HEADS, :TOP_N].reshape(n_seq, KV_HEADS * TOP_N)
    gates8 = gates_t[:, :3 * Q_PER_KV, :n_seq].reshape(KV_HEADS, Q_PER_KV, 3, n_seq).transpose(3, 0, 1, 2)
    gates8 = jnp.pad(gates8.reshape(n_seq, N_HEADS, 3), ((0, 0), (0, 0), (0, LANES - 3)))
    per_head = lambda a: jnp.repeat(a.transpose(1, 0, 2), Q_PER_KV, axis=1)
    new_k = lambda k: per_head(k[:, :n_seq, HEAD_DIM:])
    new_v = lambda v_t: per_head(v_t[:, 0, :HEAD_DIM, :n_seq].transpose(0, 2, 1))
    o8 = _sample_attend_call(idx, page_table, q8[:, :, HEAD_DIM:], ocmp, gates8, new_k(ks), new_v(vs_t),
                             new_k(kw), new_v(vw_t), cache_win.transpose(0, 2, 3, 4, 1), cache_t,
                             max(win_buf + 1 - WINDOW, 0))
    o = o8.reshape(n_seq, NSA_W)
    h_new = _outmix_call(h, y_ssm, gs, o, gn, prm["w_glu"], prm["w_out"])
    kv_rows = kv_t[0, :, :n_seq].T.reshape(n_seq, 1, 4, KV_HEADS, HEAD_DIM)
    win_new = win_t[0, :, :n_seq].T.reshape(n_seq, 1, 2, KV_HEADS, HEAD_DIM)
    wrows = jnp.concatenate([cache_win, win_new], axis=1)
    wrows = wrows[:, wrows.shape[1] - min(WINDOW, wrows.shape[1]):]
    state = lambda s: s.reshape(st_re.shape)
    return h_new, kv_rows, wrows, state(h_re), state(h_im)


def kernel(x_prompt, x_sample, cache_kv, cache_win, state_ssm_re, state_ssm_im, page_table, norm_w, w_in, gate_b,
           q_norm_w, k_norm_w, cmp_pe, cmp_w1, cmp_b1, cmp_w2, ssm_lam_re, ssm_lam_im, ssm_log_step, ssm_b_re,
           ssm_b_im, ssm_c_re, ssm_c_im, ssm_d, w_glu, w_out):
    p = dict(norm_w=norm_w, w_in=w_in, gate_b=gate_b, q_norm_w=q_norm_w, k_norm_w=k_norm_w, cmp_pe=cmp_pe,
             cmp_w1=cmp_w1, cmp_b1=cmp_b1, cmp_w2=cmp_w2, ssm_lam_re=ssm_lam_re, ssm_lam_im=ssm_lam_im,
             ssm_log_step=ssm_log_step, ssm_b_re=ssm_b_re, ssm_b_im=ssm_b_im, ssm_c_re=ssm_c_re,
             ssm_c_im=ssm_c_im, ssm_d=ssm_d, w_glu=w_glu, w_out=w_out)
    b_p, s_p, d_model = x_prompt.shape
    b_s, s_s, _ = x_sample.shape
    assert s_s == 1, "the sample group decodes one token per sequence"
    h_p = x_prompt.reshape(b_p * s_p, d_model)
    h_s = x_sample.reshape(b_s, d_model)
    outs_p, outs_s = [], []
    for l in range(norm_w.shape[0]):
        h_p, *rest_p = _prompt_layer(h_p, p, l, b_p, s_p)
        h_s, *rest_s = _sample_layer(h_s, p, l, cache_kv[l], cache_win[l], state_ssm_re[l], state_ssm_im[l],
                                     page_table)
        outs_p.append(rest_p)
        outs_s.append(rest_s)
    stack = lambda outs, i: jnp.stack([o[i] for o in outs])
    return (h_p.reshape(x_prompt.shape), h_s.reshape(x_sample.shape),
            stack(outs_p, 0), stack(outs_s, 0), stack(outs_p, 1), stack(outs_s, 1),
            stack(outs_p, 2), stack(outs_p, 3), stack(outs_s, 2), stack(outs_s, 3))
```

```python
import functools
import math

import jax
import jax.numpy as jnp
from jax import lax
from jax.experimental import pallas as pl
from jax.experimental.pallas import tpu as pltpu

F32 = jnp.float32
BF16 = jnp.bfloat16
I32 = jnp.int32

LANES = 128
SUBLANES = 8
VMEM_LIMIT_BYTES = 56 * 1024 * 1024

HEAD_DIM = 64
N_HEADS = 8
KV_HEADS = 2
Q_PER_KV = N_HEADS // KV_HEADS
SSM_W = 512
SSM_GROUP = 16
SSM_STATE = 64
NSA_W = N_HEADS * HEAD_DIM
CMP_BLOCK = 32
CMP_STRIDE = 16
CMP_HID = 2 * HEAD_DIM
SLC_BLOCK = 64
TOP_N = 16
N_LOCAL_BLOCKS = 2
WINDOW = 512
ROPE_THETA = 500000.0
ROPE_DIM = HEAD_DIM // 4
RMS_EPS = 1e-6
NEG_INF = -1e30

COL_U = 0
COL_GS = SSM_W
COL_Q = 2 * SSM_W
COL_GN = 2 * SSM_W + NSA_W
COL_KV = 2 * SSM_W + 2 * NSA_W
COL_GL = COL_KV + 6 * KV_HEADS * HEAD_DIM
IN_W_PAD = COL_GL + LANES

SSM_CHUNK = 16
SSM_SLAB_GROUPS = LANES // SSM_GROUP
SSM_SLABS = SSM_W // LANES


def _cparams(sem):
    return pltpu.CompilerParams(dimension_semantics=sem, vmem_limit_bytes=VMEM_LIMIT_BYTES)


def _sigmoid(x):
    return 1.0 / (1.0 + jnp.exp(-x))


def _dot(a, b):
    return jnp.dot(a, b, preferred_element_type=F32)


def _dot_nt(a, b):
    return lax.dot_general(a, b, (((1,), (1,)), ((), ())), preferred_element_type=F32)


def _proj_body(x_ref, nw_ref, w_ref, qnw_ref, knw_ref, gb_ref, ra_ref, rb_ref, rc_ref,
               u_ref, gs_ref, gn_ref, qt_ref, kvt_ref, wint_ref, gt_ref,
               ks_ref, vst_ref, kw_ref, vwt_ref, *, tm, tiles_per_seq):
    x = x_ref[...]
    ms = jnp.mean(x * x, axis=-1, keepdims=True)
    h = (x * lax.rsqrt(ms + RMS_EPS) * nw_ref[...]).astype(BF16)

    def mm(c0, c1):
        return _dot(h, w_ref[:, c0:c1])

    zu = mm(COL_U, COL_GS)
    for j in range(SSM_SLABS):
        u_ref[j] = zu[:, j * LANES:(j + 1) * LANES]
    gs_ref[...] = mm(COL_GS, COL_Q)
    gn_ref[...] = mm(COL_GN, COL_KV)

    lane = lax.broadcasted_iota(I32, (tm, LANES), 1)
    lo = lane < HEAD_DIM
    ra = ra_ref[...]
    rb = rb_ref[...]
    rc = rc_ref[...]

    def norm_rope(s, wrow):
        s2 = s * s
        slo = jnp.sum(jnp.where(lo, s2, 0.0), axis=-1, keepdims=True)
        shi = jnp.sum(jnp.where(lo, 0.0, s2), axis=-1, keepdims=True)
        msq = jnp.where(lo, slo, shi) * (1.0 / HEAD_DIM)
        y = s * lax.rsqrt(msq + RMS_EPS) * wrow
        half = ROPE_DIM // 2
        return y * ra + pltpu.roll(y, LANES - half, 1) * rb + pltpu.roll(y, half, 1) * rc

    def hi_half(y, head):
        src = pltpu.roll(y, HEAD_DIM, 1) if head == 0 else y
        return jnp.where(lo, 0.0, src)

    def lo_half(y, head):
        return y if head == 0 else pltpu.roll(y, HEAD_DIM, 1)

    zq = mm(COL_Q, COL_GN)
    qnw = qnw_ref[...]
    scale = HEAD_DIM ** -0.5 * math.log2(math.e)
    for j in range(N_HEADS // 2):
        y = norm_rope(zq[:, j * LANES:(j + 1) * LANES], qnw) * scale
        qt_ref[2 * j] = hi_half(y, 0).T.astype(BF16)
        qt_ref[2 * j + 1] = hi_half(y, 1).T.astype(BF16)

    zkv = mm(COL_KV, COL_GL)
    kc = norm_rope(zkv[:, 0:LANES], knw_ref[0:1, :])
    vc = zkv[:, LANES:2 * LANES]
    ks = norm_rope(zkv[:, 2 * LANES:3 * LANES], knw_ref[1:2, :])
    vs = zkv[:, 3 * LANES:4 * LANES]
    kw = norm_rope(zkv[:, 4 * LANES:5 * LANES], knw_ref[2:3, :])
    vw = zkv[:, 5 * LANES:6 * LANES]
    for i, rows in enumerate((kc, vc, ks, vs)):
        kvt_ref[i * LANES:(i + 1) * LANES, :] = rows.T
    for i, rows in enumerate((kw, vw)):
        wint_ref[i * LANES:(i + 1) * LANES, :] = rows.T

    row = lax.broadcasted_iota(I32, (tm, LANES), 0)
    pos = (pl.program_id(0) % tiles_per_seq) * tm + row
    onehot = jnp.where(lane == lax.shift_right_logical(pos, 6), 1.0, 0.0)
    ones_col = jnp.where(lane == HEAD_DIM, 1.0, 0.0)
    for g in range(KV_HEADS):
        ks_ref[g] = jnp.where(lo, onehot, hi_half(ks, g)).astype(BF16)
        vst_ref[g] = jnp.where(lo, lo_half(vs, g), ones_col).T.astype(BF16)
        kw_ref[g] = hi_half(kw, g).astype(BF16)
        vwt_ref[g] = jnp.where(lo, lo_half(vw, g), ones_col).T.astype(BF16)

    gates = _sigmoid(mm(COL_GL, IN_W_PAD) + gb_ref[...])
    for g in range(KV_HEADS):
        own = gates if g == 0 else pltpu.roll(gates, LANES - 3 * Q_PER_KV * g, 1)
        gt_ref[g] = own.T[:2 * SUBLANES]


def _proj_call(x2d, w_pad, norm_w, qnw, knw, gb, tabs, batch, seq):
    T, D = x2d.shape
    tm = min(256, seq)
    assert T == batch * seq and seq % tm == 0 and tm % LANES == 0
    tps = seq // tm
    row_spec = lambda w: pl.BlockSpec((tm, w), lambda i: (i, 0))
    full = lambda a: pl.BlockSpec(a.shape, lambda i: (0,) * a.ndim)
    tab_spec = pl.BlockSpec((tm, LANES), lambda i: (i % tps, 0))
    head_spec = lambda n: pl.BlockSpec((n, tm, LANES), lambda i: (0, i, 0))
    head_t_spec = lambda n, rows: pl.BlockSpec((n, rows, tm), lambda i: (0, 0, i))
    cache_t_spec = lambda rows: pl.BlockSpec((None, rows, tm), lambda i: (i // tps, 0, i % tps))
    tile_t_spec = pl.BlockSpec((KV_HEADS, None, LANES, tm), lambda i: (0, i, 0, 0))
    tile_t_sds = jax.ShapeDtypeStruct((KV_HEADS, T // tm, LANES, tm), BF16)
    out_shape = (
        jax.ShapeDtypeStruct((SSM_SLABS, T, LANES), F32),
        jax.ShapeDtypeStruct((T, SSM_W), F32),
        jax.ShapeDtypeStruct((T, NSA_W), F32),
        jax.ShapeDtypeStruct((N_HEADS, LANES, T), BF16),
        jax.ShapeDtypeStruct((batch, 4 * LANES, seq), F32),
        jax.ShapeDtypeStruct((batch, 2 * LANES, seq), F32),
        jax.ShapeDtypeStruct((KV_HEADS, 2 * SUBLANES, T), F32),
        jax.ShapeDtypeStruct((KV_HEADS, T, LANES), BF16),
        tile_t_sds,
        jax.ShapeDtypeStruct((KV_HEADS, T, LANES), BF16),
        tile_t_sds,
    )
    out_specs = (head_spec(SSM_SLABS), row_spec(SSM_W), row_spec(NSA_W), head_t_spec(N_HEADS, LANES),
                 cache_t_spec(4 * LANES), cache_t_spec(2 * LANES), head_t_spec(KV_HEADS, 2 * SUBLANES),
                 head_spec(KV_HEADS), tile_t_spec, head_spec(KV_HEADS), tile_t_spec)
    return pl.pallas_call(
        functools.partial(_proj_body, tm=tm, tiles_per_seq=tps),
        out_shape=out_shape,
        grid=(T // tm,),
        in_specs=[row_spec(D), full(norm_w), full(w_pad), full(qnw), full(knw), full(gb),
                  tab_spec, tab_spec, tab_spec],
        out_specs=out_specs,
        compiler_params=_cparams(("parallel",)),
        name="proj",
    )(x2d, norm_w, w_pad, qnw, knw, gb, *tabs)


def _prep_params(p, l):
    w_in = p["w_in"][l]
    d_model, in_w = w_in.shape
    tile2 = lambda v: jnp.tile(v, (1, LANES // HEAD_DIM))
    return {
        "w_in": jnp.pad(w_in.astype(BF16), ((0, 0), (0, IN_W_PAD - in_w))),
        "norm_w": p["norm_w"][l].reshape(1, d_model).astype(F32),
        "qnw": tile2(p["q_norm_w"][l].reshape(1, HEAD_DIM)).astype(F32),
        "knw": tile2(p["k_norm_w"][l]).astype(F32),
        "gb": jnp.pad(p["gate_b"][l].reshape(1, -1).astype(F32), ((0, 0), (0, LANES - 3 * N_HEADS))),
        "w_glu": p["w_glu"][l].astype(BF16),
        "w_out": p["w_out"][l].astype(BF16),
    }


def _rope_tables(pos):
    half = ROPE_DIM // 2
    inv = ROPE_THETA ** (-jnp.arange(half, dtype=F32) / half)
    ang = pos.astype(F32)[:, None] * inv
    cos, sin = jnp.cos(ang), jnp.sin(ang)
    n = pos.shape[0]
    rest = HEAD_DIM - ROPE_DIM
    a = jnp.concatenate([cos, cos, jnp.ones((n, rest), F32)], axis=-1)
    b = jnp.concatenate([-sin, jnp.zeros((n, HEAD_DIM - half), F32)], axis=-1)
    c = jnp.concatenate([jnp.zeros((n, half), F32), sin, jnp.zeros((n, rest), F32)], axis=-1)
    return tuple(jnp.tile(t, (1, LANES // HEAD_DIM)) for t in (a, b, c))


def _outmix_body(x_ref, y_ref, gs_ref, o_ref, gn_ref, wg_ref, wo_ref, out_ref):
    y = jnp.concatenate([y_ref[j] for j in range(SSM_SLABS)], axis=-1)
    ab = _dot(y.astype(BF16), wg_ref[...])
    gs = gs_ref[...]
    ssm = ab[:, :SSM_W] * _sigmoid(ab[:, SSM_W:]) * (gs * _sigmoid(gs))
    gn = gn_ref[...]
    nsa = o_ref[...] * (gn * _sigmoid(gn))
    acc = _dot(ssm.astype(BF16), wo_ref[0:SSM_W, :])
    acc += _dot(nsa.astype(BF16), wo_ref[SSM_W:, :])
    out_ref[...] = x_ref[...] + acc


def _outmix_call(x2d, y_ssm, g_ssm, o_nsa, g_nsa, w_glu, w_out):
    T, D = x2d.shape
    tm = min(512, T)
    row_spec = lambda w: pl.BlockSpec((tm, w), lambda i: (i, 0))
    full = lambda a: pl.BlockSpec(a.shape, lambda i: (0,) * a.ndim)
    return pl.pallas_call(
        _outmix_body,
        out_shape=jax.ShapeDtypeStruct((T, D), F32),
        grid=(T // tm,),
        in_specs=[row_spec(D), pl.BlockSpec((SSM_SLABS, tm, LANES), lambda i: (0, i, 0)),
                  row_spec(SSM_W), row_spec(NSA_W), row_spec(NSA_W),
                  full(w_glu), full(w_out)],
        out_specs=row_spec(D),
        compiler_params=_cparams(("parallel",)),
        name="outmix",
    )(x2d, y_ssm, g_ssm, o_nsa, g_nsa, w_glu, w_out)


def _gelu_tanh(x):
    c = math.sqrt(2.0 / math.pi)
    return 0.5 * x * (1.0 + jnp.tanh(c * (x + 0.044715 * (x * x * x))))


def _compress_rows(rows_refs, pe_ref, wa_ref, wb_ref, b1_ref, w2_ref, kc_ref, vc_ref, nch):
    lane = lax.broadcasted_iota(I32, (nch, LANES), 1)
    for kvi, out_ref in ((0, kc_ref), (1, vc_ref)):
        rows_ref = rows_refs[kvi]
        pa = jnp.zeros((nch, 2 * CMP_HID), F32)
        pb = jnp.zeros((nch, 2 * CMP_HID), F32)
        for j0 in range(0, CMP_STRIDE, 2):
            xs = [rows_ref[pl.ds(j, nch, stride=CMP_STRIDE), :] for j in (j0, j0 + 1)]
            xa = jnp.concatenate([xs[i] + pe_ref[kvi, 0, j0 + i:j0 + i + 1, :] for i in range(2)], axis=-1)
            xb = jnp.concatenate([xs[i] + pe_ref[kvi, 1, j0 + i:j0 + i + 1, :] for i in range(2)], axis=-1)
            wsl = slice(j0 * LANES, (j0 + 2) * LANES)
            pa += _dot(xa.astype(BF16), wa_ref[kvi, wsl, :])
            pb += _dot(xb.astype(BF16), wb_ref[kvi, wsl, :])
        hid = _gelu_tanh(pa + pltpu.roll(pb, nch - 1, 0) + b1_ref[kvi]).astype(BF16)
        for g in range(KV_HEADS):
            o = _dot(hid, w2_ref[kvi, g])
            if kvi == 1:
                o = jnp.where(lane == HEAD_DIM, 1.0, o).T
            out_ref[g] = o.astype(BF16)


def _compress_prompt_body(kvt_ref, pe_ref, wa_ref, wb_ref, b1_ref, w2_ref, kc_ref, vc_ref,
                          krows_ref, vrows_ref, *, nch):
    for c in range(kvt_ref.shape[1] // LANES):
        cs = slice(c * LANES, (c + 1) * LANES)
        krows_ref[cs, :] = kvt_ref[0:LANES, cs].T
        vrows_ref[cs, :] = kvt_ref[LANES:2 * LANES, cs].T
    _compress_rows((krows_ref, vrows_ref), pe_ref, wa_ref, wb_ref, b1_ref, w2_ref, kc_ref, vc_ref, nch)


def _compress_prompt_call(kv_t, cw, batch, seq):
    nch = seq // CMP_STRIDE
    full = lambda a: pl.BlockSpec(a.shape, lambda b: (0,) * a.ndim)
    out_spec = pl.BlockSpec((KV_HEADS, nch, LANES), lambda b: (0, b, 0))
    out_sds = jax.ShapeDtypeStruct((KV_HEADS, batch * nch, LANES), BF16)
    out_t_spec = pl.BlockSpec((KV_HEADS, None, LANES, nch), lambda b: (0, b, 0, 0))
    out_t_sds = jax.ShapeDtypeStruct((KV_HEADS, batch, LANES, nch), BF16)
    return pl.pallas_call(
        functools.partial(_compress_prompt_body, nch=nch),
        out_shape=(out_sds, out_t_sds),
        grid=(batch,),
        in_specs=[pl.BlockSpec((None, 2 * LANES, seq), lambda b: (b, 0, 0)),
                  full(cw["pe"]), full(cw["wa"]), full(cw["wb"]), full(cw["b1"]), full(cw["w2"])],
        out_specs=(out_spec, out_t_spec),
        scratch_shapes=[pltpu.VMEM((seq, LANES), F32)] * 2,
        compiler_params=_cparams(("parallel",)),
        name="compress_prompt",
    )(kv_t, cw["pe"], cw["wa"], cw["wb"], cw["b1"], cw["w2"])


def _prep_compress(p, l):
    eye = jnp.eye(KV_HEADS, dtype=F32)
    w1 = p["cmp_w1"][l].reshape(2, 2, CMP_STRIDE, HEAD_DIM, CMP_HID)
    wexp = jnp.einsum("khjdn,ge->khjgden", w1, eye).reshape(2, 2, CMP_STRIDE * LANES, KV_HEADS * CMP_HID)
    pe = p["cmp_pe"][l].reshape(2, 2, CMP_STRIDE, HEAD_DIM)
    w2 = p["cmp_w2"][l]
    zeros = jnp.zeros_like(w2[0])
    w2k = jnp.concatenate([zeros, w2[0]], axis=-1)
    w2v = jnp.concatenate([w2[1], zeros], axis=-1)
    w2e = jnp.stack([jnp.einsum("hd,ge->gehd", w, eye).reshape(KV_HEADS, KV_HEADS * CMP_HID, LANES)
                     for w in (w2k, w2v)])
    return {
        "pe": jnp.tile(pe, (1, 1, 1, KV_HEADS)).astype(F32),
        "wa": wexp[:, 0].astype(BF16),
        "wb": wexp[:, 1].astype(BF16),
        "b1": jnp.tile(p["cmp_b1"][l].reshape(2, 1, CMP_HID), (1, 1, KV_HEADS)).astype(F32),
        "w2": w2e.astype(BF16),
    }


def _overlap_matrix(n_tok_pad, n_tok, n_blk, n_cols=LANES):
    c_start = jnp.arange(n_tok_pad) * CMP_STRIDE
    blk = jnp.arange(n_cols)
    ov = ((c_start[:, None] < (blk[None, :] + 1) * SLC_BLOCK)
          & (c_start[:, None] + CMP_BLOCK > blk[None, :] * SLC_BLOCK)
          & (jnp.arange(n_tok_pad)[:, None] < n_tok) & (blk[None, :] < n_blk))
    return ov.astype(BF16)


def _topk_select_t(w_ref, imp_t, q0, tq):
    nb = imp_t.shape[0]
    n_i = lax.broadcasted_iota(I32, (nb, tq), 0)
    qblk = lax.shift_right_logical(q0 + lax.broadcasted_iota(I32, (nb, tq), 1), 6)
    causal = n_i <= qblk
    forced = (n_i == 0) | (n_i >= qblk - (N_LOCAL_BLOCKS - 1))
    w_ref[...] = jnp.where(causal, jnp.where(forced, jnp.inf, imp_t), -jnp.inf)
    last_blk = lax.shift_right_logical(q0 + tq - 1, 6)
    n_grp = nb // SUBLANES
    rank = [jnp.zeros((SUBLANES, tq), F32) for _ in range(n_grp)]
    grp_i = lax.broadcasted_iota(I32, (SUBLANES, tq), 0)

    def count_group(mg, rank):
        rank = list(rank)
        for mi in range(SUBLANES):
            m = mg * SUBLANES + mi
            wm = w_ref[m:m + 1, :]
            for ng in range(n_grp):
                w = w_ref[ng * SUBLANES:(ng + 1) * SUBLANES, :]
                if ng > mg:
                    beats = jnp.where(wm >= w, 1.0, 0.0)
                elif ng < mg:
                    beats = jnp.where(wm > w, 1.0, 0.0)
                else:
                    beats = jnp.where(grp_i > mi, jnp.where(wm >= w, 1.0, 0.0), jnp.where(wm > w, 1.0, 0.0))
                rank[ng] = rank[ng] + beats
        return tuple(rank)

    rank = tuple(rank)
    for mg in range(n_grp):
        rank = lax.cond(mg * SUBLANES <= last_blk, functools.partial(count_group, mg), lambda r: r, rank)
    return causal & (jnp.concatenate(rank, axis=0) < TOP_N)


def _flash_tiles_t(tiles, q_ts, ms, accs):
    def scores(i):
        k, _, mask = tiles[i]
        s = [_dot(k, q_t) for q_t in q_ts]
        return s if mask is None else [jnp.where(mask, x, NEG_INF) for x in s]

    ahead = 2
    pending = {i: scores(i) for i in range(min(ahead, len(tiles)))}
    for i, (_, v_t, _) in enumerate(tiles):
        s = pending.pop(i)
        m_new = [jnp.maximum(m, jnp.max(x, axis=0, keepdims=True)) for m, x in zip(ms, s)]
        alpha = [jnp.exp2(m - mn) for m, mn in zip(ms, m_new)]
        p = [jnp.exp2(x - mn).astype(BF16) for x, mn in zip(s, m_new)]
        pv = [_dot(v_t, x) for x in p]
        if i + ahead < len(tiles):
            pending[i + ahead] = scores(i + ahead)
        accs = [a * acc + x for a, acc, x in zip(alpha, accs, pv)]
        ms = m_new
    return tuple(ms), tuple(accs)


def _attn_body(qt_ref, kc_ref, vct_ref, ovlt_ref, ks_ref, vst_ref, kw_ref, vwt_ref, gate_ref, o_ref,
               qa_ref, ocmp_ref, w_ref, *, tq, ncp):
    R = Q_PER_KV
    qt = pl.program_id(2)
    q0 = qt * tq
    nbs = LANES // 2

    c_i = lax.broadcasted_iota(I32, (ncp, tq), 0)
    qpos_c = q0 + lax.broadcasted_iota(I32, (ncp, tq), 1)
    cmask = c_i * CMP_STRIDE + (CMP_BLOCK - 1) <= qpos_c
    kc = kc_ref[...]
    s = [jnp.where(cmask, _dot(kc, qt_ref[r]), NEG_INF) for r in range(R)]
    e = [jnp.where(cmask, jnp.exp2(x - jnp.max(x, axis=0, keepdims=True)), 0.0) for x in s]
    l = [jnp.sum(x, axis=0, keepdims=True) for x in e]
    p = [(x * (1.0 / jnp.where(y > 0.0, y, 1.0))).astype(BF16) for x, y in zip(e, l)]
    imp = sum(_dot(ovlt_ref[...], x) for x in p)
    for r in range(R):
        ocmp_ref[r] = _dot(vct_ref[...], p[r])

    sel = _topk_select_t(w_ref, imp[:nbs], q0, tq)
    bias = jnp.concatenate([jnp.where(sel, 0.0, NEG_INF), jnp.zeros((LANES - nbs, tq), F32)], axis=0)
    for r in range(R):
        qa_ref[r] = (qt_ref[r].astype(F32) + bias).astype(BF16)

    key_i = lax.broadcasted_iota(I32, (tq, tq), 0)
    qry_i = lax.broadcasted_iota(I32, (tq, tq), 1)
    init = (tuple(jnp.full((1, tq), -jnp.inf, F32) for _ in range(R)),
            tuple(jnp.zeros((LANES, tq), F32) for _ in range(R)))

    def tiles(k_ref, vt_ref, q_ref, js, masks, state):
        ts = [(k_ref[pl.ds(pl.multiple_of(j * tq, tq), tq), :], vt_ref[j], mask) for j, mask in zip(js, masks)]
        return _flash_tiles_t(ts, [q_ref[r] for r in range(R)], *state)

    def last_tiles(k_ref, vt_ref, q_ref, n, first_mask, state):
        js = [qt - (n - 1 - t) for t in range(n)]
        masks = [first_mask] + [None] * (n - 2) + [key_i <= qry_i] if n > 1 else [key_i <= qry_i]
        return tiles(k_ref, vt_ref, q_ref, js, masks, state)

    GROUP = 4

    def slc_group(i, st):
        return tiles(ks_ref, vst_ref, qa_ref, [GROUP * i + t for t in range(GROUP)], [None] * GROUP, st)

    state = lax.fori_loop(0, qt // GROUP, slc_group, init)
    tails = [functools.partial(last_tiles, ks_ref, vst_ref, qa_ref, n, None) for n in range(1, GROUP + 1)]
    _, acc_s = lax.switch(qt % GROUP, tails, state)

    nwin = WINDOW // tq
    wins = [functools.partial(last_tiles, kw_ref, vwt_ref, qt_ref, n, (key_i > qry_i) if n == nwin + 1 else None)
            for n in range(1, nwin + 2)]
    _, acc_w = lax.switch(jnp.minimum(qt, nwin), wins, init)

    outs = []
    for r in range(R):
        a_s = acc_s[r]
        a_w = acc_w[r]
        g = lambda k: gate_ref[3 * r + k:3 * r + k + 1, :]
        o = (g(0) * ocmp_ref[r] + g(1) * (a_s / a_s[HEAD_DIM:HEAD_DIM + 1, :])
             + g(2) * (a_w / a_w[HEAD_DIM:HEAD_DIM + 1, :]))
        outs.append(o[:HEAD_DIM])
    o_ref[...] = jnp.concatenate(outs, axis=0).T


def _attn_call(q_t, kc, vc_t, ovl_t, ks, vs_t, kw, vw_t, gates_t, batch, seq):
    tq = min(256, seq)
    nq = seq // tq
    ncp = kc.shape[1] // batch
    R = Q_PER_KV
    assert seq // SLC_BLOCK <= LANES // 2 and WINDOW % tq == 0
    k_spec = pl.BlockSpec((None, seq, LANES), lambda b, g, t: (g, b, 0))
    vt_spec = pl.BlockSpec((None, nq, LANES, tq), lambda b, g, t: (g, b, 0, 0))
    acc = pltpu.VMEM((R, LANES, tq), F32)
    return pl.pallas_call(
        functools.partial(_attn_body, tq=tq, ncp=ncp),
        out_shape=jax.ShapeDtypeStruct((batch * seq, NSA_W), F32),
        grid=(batch, KV_HEADS, nq),
        in_specs=[pl.BlockSpec((R, LANES, tq), lambda b, g, t: (g, 0, b * nq + t)),
                  pl.BlockSpec((None, ncp, LANES), lambda b, g, t: (g, b, 0)),
                  pl.BlockSpec((None, None, LANES, ncp), lambda b, g, t: (g, b, 0, 0)),
                  pl.BlockSpec(ovl_t.shape, lambda b, g, t: (0, 0)),
                  k_spec, vt_spec, k_spec, vt_spec,
                  pl.BlockSpec((None, 2 * SUBLANES, tq), lambda b, g, t: (g, 0, b * nq + t))],
        out_specs=pl.BlockSpec((tq, R * HEAD_DIM), lambda b, g, t: (b * nq + t, g)),
        scratch_shapes=[pltpu.VMEM((R, LANES, tq), BF16), acc, pltpu.VMEM((LANES // 2, tq), F32)],
        compiler_params=_cparams(("parallel", "parallel", "arbitrary")),
        name="nsa_prompt",
    )(q_t, kc, vc_t, ovl_t, ks, vs_t, kw, vw_t, gates_t)


def _s5_discretise(p, l):
    lr = p["ssm_lam_re"][l].astype(F32)
    li = p["ssm_lam_im"][l].astype(F32)
    dt = jnp.exp(p["ssm_log_step"][l].astype(F32))[:, None]

    def apow(t):
        mag, ang = jnp.exp(lr * dt * t), li * dt * t
        return mag * jnp.cos(ang), mag * jnp.sin(ang)

    a_re, a_im = apow(1.0)
    den = lr * lr + li * li
    nr, ni = a_re - 1.0, a_im
    f_re, f_im = (nr * lr + ni * li) / den, (ni * lr - nr * li) / den
    br, bi = p["ssm_b_re"][l].astype(F32), p["ssm_b_im"][l].astype(F32)
    bb_re = f_re[..., None] * br - f_im[..., None] * bi
    bb_im = f_re[..., None] * bi + f_im[..., None] * br
    return apow, bb_re, bb_im


def _prep_s5_step(p, l):
    apow, bb_re, bb_im = _s5_discretise(p, l)
    a_re, a_im = apow(1.0)
    G = bb_re.shape[0]
    eye = jnp.eye(G, dtype=F32)
    to_state = lambda bb: jnp.einsum("gnk,ge->gken", bb, eye).reshape(SSM_W, G * SSM_STATE)
    c_re, c_im = p["ssm_c_re"][l].astype(F32), p["ssm_c_im"][l].astype(F32)
    from_state = lambda c: jnp.einsum("gcn,ge->gnec", c, eye).reshape(G * SSM_STATE, SSM_W)
    return {
        "w_x": jnp.concatenate([to_state(bb_re), to_state(bb_im)], axis=1).astype(BF16),
        "w_y": jnp.concatenate([from_state(c_re), from_state(-c_im)], axis=0).astype(BF16),
        "a_re": a_re.reshape(1, -1), "a_im": a_im.reshape(1, -1),
        "d": p["ssm_d"][l].reshape(1, SSM_W).astype(F32),
    }


def _prep_s5_chunked(p, l):
    hi = lax.Precision.HIGHEST
    apow, bb_re, bb_im = _s5_discretise(p, l)
    c_re, c_im = p["ssm_c_re"][l].astype(F32), p["ssm_c_im"][l].astype(F32)
    T, J, E = SSM_CHUNK, SSM_SLABS, SSM_SLAB_GROUPS
    pw_re, pw_im = apow(jnp.arange(T + 1, dtype=F32)[:, None, None])
    p_re = pw_re[..., None] * bb_re - pw_im[..., None] * bb_im
    p_im = pw_re[..., None] * bb_im + pw_im[..., None] * bb_re
    eye = jnp.eye(E, dtype=F32)
    slab = lambda x: x.reshape(x.shape[0], J, E, *x.shape[2:])
    kt = (jnp.einsum("tgnk,gcn->tgkc", p_re[:T], c_re, precision=hi)
          - jnp.einsum("tgnk,gcn->tgkc", p_im[:T], c_im, precision=hi))
    kbd = jnp.einsum("tjgkc,ge->jtgkec", slab(kt), eye).reshape(J, T, LANES, LANES)
    w_col = kbd[:, ::-1].reshape(J, T * LANES, LANES)
    to_state = lambda x: jnp.einsum("sjgnk,ge->jsgken", slab(x[:T][::-1]), eye).reshape(J, T * LANES, E * SSM_STATE)
    w_st = jnp.concatenate([to_state(p_re), to_state(p_im)], axis=-1)
    cp_re = c_re[None, :, :, :] * pw_re[1:, :, None, :] - c_im[None] * pw_im[1:, :, None, :]
    cp_im = c_re[None, :, :, :] * pw_im[1:, :, None, :] + c_im[None] * pw_re[1:, :, None, :]
    from_state = lambda x: jnp.einsum("tjgcn,ge->jgntec", slab(x), eye).reshape(J, E * SSM_STATE, T * LANES)
    w_out = jnp.concatenate([from_state(cp_re), from_state(-cp_im)], axis=1)
    return {
        "w_col": w_col.astype(BF16), "w_st": w_st.astype(BF16), "w_out": w_out.astype(BF16),
        "a_re": pw_re[T].reshape(J, 1, E * SSM_STATE), "a_im": pw_im[T].reshape(J, 1, E * SSM_STATE),
        "d": p["ssm_d"][l].reshape(J, 1, LANES).astype(F32),
    }


def _s5_prompt_body(u_ref, wcol_ref, wst_ref, are_ref, aim_ref, wout_ref, d_ref,
                    y_ref, hre_ref, him_ref, xs_ref, hp_ref, *, n_chunks):
    T = SSM_CHUNK
    ns = SSM_SLAB_GROUPS * SSM_STATE
    u_pos = [u_ref[pl.ds(s, n_chunks, stride=T), :] for s in range(T)]
    ub = jnp.concatenate(u_pos, axis=-1).astype(BF16)
    xs_ref[...] = _dot(ub, wst_ref[...])
    a_re = are_ref[...]
    a_im = aim_ref[...]

    def step(c, carry):
        hr, hi = carry
        hp_ref[pl.ds(c, 1), 0:ns] = hr
        hp_ref[pl.ds(c, 1), ns:2 * ns] = hi
        xr = xs_ref[pl.ds(c, 1), 0:ns]
        xi = xs_ref[pl.ds(c, 1), ns:2 * ns]
        return a_re * hr - a_im * hi + xr, a_re * hi + a_im * hr + xi

    zero = jnp.zeros((1, ns), F32)
    hr, hi = lax.fori_loop(0, n_chunks, step, (zero, zero))
    hre_ref[...] = jnp.broadcast_to(hr, hre_ref.shape)
    him_ref[...] = jnp.broadcast_to(hi, him_ref.shape)
    hpb = hp_ref[...].astype(BF16)
    for t in range(T):
        sl = slice(t * LANES, (t + 1) * LANES)
        y_ref[pl.ds(t, n_chunks, stride=T), :] = (
            _dot(ub[:, :(t + 1) * LANES], wcol_ref[(T - 1 - t) * LANES:, :])
            + _dot(hpb, wout_ref[:, sl]) + d_ref[...] * u_pos[t])


def _s5_prompt_call(u_slab, sw, batch, seq):
    T, J = SSM_CHUNK, SSM_SLABS
    n_chunks = seq // T
    ns = SSM_SLAB_GROUPS * SSM_STATE
    row_spec = pl.BlockSpec((None, seq, LANES), lambda b, j: (j, b, 0))
    slab_spec = lambda a: pl.BlockSpec((None,) + a.shape[1:], lambda b, j: (j, 0, 0))
    st_spec = pl.BlockSpec((None, None, SUBLANES, ns), lambda b, j: (b, j, 0, 0))
    st_sds = jax.ShapeDtypeStruct((batch, J, SUBLANES, ns), F32)
    y, hre, him = pl.pallas_call(
        functools.partial(_s5_prompt_body, n_chunks=n_chunks),
        out_shape=(jax.ShapeDtypeStruct(u_slab.shape, F32), st_sds, st_sds),
        grid=(batch, J),
        in_specs=[row_spec, slab_spec(sw["w_col"]), slab_spec(sw["w_st"]), slab_spec(sw["a_re"]),
                  slab_spec(sw["a_im"]), slab_spec(sw["w_out"]), slab_spec(sw["d"])],
        out_specs=(row_spec, st_spec, st_spec),
        scratch_shapes=[pltpu.VMEM((n_chunks, 2 * ns), F32), pltpu.VMEM((n_chunks, 2 * ns), F32)],
        compiler_params=_cparams(("parallel", "parallel")),
        name="s5_prompt",
    )(u_slab, sw["w_col"], sw["w_st"], sw["a_re"], sw["a_im"], sw["w_out"], sw["d"])
    n_groups = J * SSM_SLAB_GROUPS
    state = lambda h: h[:, :, 0, :].reshape(batch, n_groups, SSM_STATE)
    return y, state(hre), state(him)


def _s5_step_body(u_ref, wx_ref, are_ref, aim_ref, h0re_ref, h0im_ref, wy_ref, d_ref,
                  y_ref, hre_ref, him_ref):
    u = jnp.concatenate([u_ref[j] for j in range(SSM_SLABS)], axis=-1)
    x = _dot(u.astype(BF16), wx_ref[...])
    ns = h0re_ref.shape[1]
    a_re, a_im = are_ref[...], aim_ref[...]
    h0r, h0i = h0re_ref[...], h0im_ref[...]
    hr = a_re * h0r - a_im * h0i + x[:, :ns]
    hi = a_re * h0i + a_im * h0r + x[:, ns:]
    hre_ref[...] = hr
    him_ref[...] = hi
    y = _dot(jnp.concatenate([hr, hi], axis=-1).astype(BF16), wy_ref[...]) + d_ref[...] * u
    for j in range(SSM_SLABS):
        y_ref[j] = y[:, j * LANES:(j + 1) * LANES]


def _s5_step_call(u_slab, h0_re, h0_im, sw):
    J, n_tok, _ = u_slab.shape
    ns = h0_re.shape[1]
    args = (u_slab, sw["w_x"], sw["a_re"], sw["a_im"], h0_re, h0_im, sw["w_y"], sw["d"])
    full = lambda a: pl.BlockSpec(a.shape, lambda i: (0,) * a.ndim)
    st_sds = jax.ShapeDtypeStruct((n_tok, ns), F32)
    return pl.pallas_call(
        _s5_step_body,
        out_shape=(jax.ShapeDtypeStruct(u_slab.shape, F32), st_sds, st_sds),
        grid=(1,),
        in_specs=[full(a) for a in args],
        out_specs=(full(u_slab), full(h0_re), full(h0_re)),
        compiler_params=_cparams(("arbitrary",)),
        name="s5_step",
    )(*args)


def _compress_sample_body(pt_ref, *refs, n_pages, page_rows, nch):
    del pt_ref
    page_refs = refs[:n_pages]
    pe_ref, wa_ref, wb_ref, b1_ref, w2_ref, kc_ref, vc_ref, krows_ref, vrows_ref = refs[n_pages:]
    b = pl.program_id(0)

    @pl.when(b == 0)
    def _():
        krows_ref[...] = jnp.zeros(krows_ref.shape, F32)
        vrows_ref[...] = jnp.zeros(vrows_ref.shape, F32)

    stage = b % 2
    for i, page_ref in enumerate(page_refs):
        krows_ref[stage, i * page_rows:(i + 1) * page_rows, :] = page_ref[0].reshape(LANES, page_rows).T
        vrows_ref[stage, i * page_rows:(i + 1) * page_rows, :] = page_ref[1].reshape(LANES, page_rows).T
    done = 1 - stage
    _compress_rows((krows_ref.at[done], vrows_ref.at[done]), pe_ref, wa_ref, wb_ref, b1_ref, w2_ref,
                   kc_ref, vc_ref, nch)


def _compress_sample_call(cache, page_table, cw):
    n_seq, n_pages = page_table.shape
    page_rows = cache.shape[-1]
    assert page_rows == LANES
    nch = n_pages * page_rows // CMP_STRIDE
    staged = lambda b: jnp.minimum(b, n_seq - 1)
    page_spec = lambda i: pl.BlockSpec((None, 2, KV_HEADS, HEAD_DIM, page_rows),
                                       lambda b, pt: (pt[staged(b), i], 0, 0, 0, 0))
    full = lambda a: pl.BlockSpec(a.shape, lambda b, pt: (0,) * a.ndim)
    out_of = lambda b: jnp.maximum(b - 1, 0)
    out_spec = pl.BlockSpec((KV_HEADS, nch, LANES), lambda b, pt: (0, out_of(b), 0))
    out_sds = jax.ShapeDtypeStruct((KV_HEADS, n_seq * nch, LANES), BF16)
    out_t_spec = pl.BlockSpec((KV_HEADS, None, LANES, nch), lambda b, pt: (0, out_of(b), 0, 0))
    out_t_sds = jax.ShapeDtypeStruct((KV_HEADS, n_seq, LANES, nch), BF16)
    weights = (cw["pe"], cw["wa"], cw["wb"], cw["b1"], cw["w2"])
    return pl.pallas_call(
        functools.partial(_compress_sample_body, n_pages=n_pages, page_rows=page_rows, nch=nch),
        out_shape=(out_sds, out_t_sds),
        grid_spec=pltpu.PrefetchScalarGridSpec(
            num_scalar_prefetch=1,
            grid=(n_seq + 1,),
            in_specs=[page_spec(i) for i in range(n_pages)] + [full(a) for a in weights],
            out_specs=(out_spec, out_t_spec),
            scratch_shapes=[pltpu.VMEM((2, n_pages * page_rows, LANES), F32)] * 2),
        compiler_params=_cparams(("arbitrary",)),
        name="compress_sample",
    )(page_table, *([cache] * n_pages), *weights)


def _group_rows(x0, x1):
    row = lax.broadcasted_iota(I32, x0.shape, 0)
    return jnp.where(row < Q_PER_KV, x0, x1)


def _sample_select_body(q_ref, kc_ref, vct_ref, ovl_ref, tri_ref, ocmp_ref, idx_ref, *, ncp, qpos, nbp):
    q8 = q_ref[...]
    c_i = lax.broadcasted_iota(I32, (N_HEADS, ncp), 1)
    cmask = c_i * CMP_STRIDE + (CMP_BLOCK - 1) <= qpos
    s = _group_rows(_dot_nt(q8, kc_ref[0]), _dot_nt(q8, kc_ref[1]))
    s = jnp.where(cmask, s, NEG_INF)
    e = jnp.where(cmask, jnp.exp2(s - jnp.max(s, axis=-1, keepdims=True)), 0.0)
    l = jnp.sum(e, axis=-1, keepdims=True)
    p = (e / jnp.where(l > 0.0, l, 1.0)).astype(BF16)
    ocmp_ref[...] = _group_rows(_dot_nt(p, vct_ref[0]), _dot_nt(p, vct_ref[1]))
    imp8 = _dot(p, ovl_ref[...])

    n_row = lax.broadcasted_iota(I32, (1, nbp), 1)
    qblk = qpos // SLC_BLOCK
    causal = n_row <= qblk
    forced = (n_row == 0) | (n_row >= qblk - (N_LOCAL_BLOCKS - 1))
    m_i = lax.broadcasted_iota(I32, (nbp, nbp), 0)
    n_i = lax.broadcasted_iota(I32, (nbp, nbp), 1)
    lane = lax.broadcasted_iota(I32, (1, LANES), 1)
    idx_rows = []
    for g in range(KV_HEADS):
        imp = jnp.sum(imp8[g * Q_PER_KV:(g + 1) * Q_PER_KV], axis=0, keepdims=True)
        w = jnp.where(causal, jnp.where(forced, jnp.inf, imp), -jnp.inf)
        w_sq = jnp.broadcast_to(w, (nbp, nbp))
        w_col = w_sq.T
        beats = jnp.where(n_i > m_i, jnp.where(w_col >= w_sq, 1.0, 0.0), jnp.where(w_col > w_sq, 1.0, 0.0))
        rank = jnp.sum(beats, axis=0, keepdims=True)
        sel = causal & (rank < TOP_N)
        self_f = jnp.where(sel, 1.0, 0.0)
        before = _dot(self_f.astype(BF16), tri_ref[...])
        idx = jnp.full((1, LANES), -1, I32)
        for k in range(TOP_N):
            hit = sel & (before == float(k))
            val = jnp.sum(jnp.where(hit, n_row.astype(F32) + 1.0, 0.0), axis=-1, keepdims=True) - 1.0
            idx = jnp.where(lane == k, val.astype(I32), idx)
        idx_rows.append(idx)
    idx_ref[...] = jnp.concatenate(idx_rows + [jnp.full((SUBLANES - KV_HEADS, LANES), -1, I32)], axis=0)


def _sample_select_call(q8, kc, vc_t, ovl, tri, qpos):
    n_seq = q8.shape[0]
    ncp = kc.shape[1] // n_seq
    nbp = ovl.shape[1]
    cmp_spec = pl.BlockSpec((KV_HEADS, ncp, LANES), lambda b: (0, b, 0))
    row_spec = pl.BlockSpec((None, N_HEADS, LANES), lambda b: (b, 0, 0))
    full = lambda a: pl.BlockSpec(a.shape, lambda b: (0,) * a.ndim)
    return pl.pallas_call(
        functools.partial(_sample_select_body, ncp=ncp, qpos=qpos, nbp=nbp),
        out_shape=(jax.ShapeDtypeStruct((n_seq, N_HEADS, LANES), F32),
                   jax.ShapeDtypeStruct((n_seq, SUBLANES, LANES), I32)),
        grid=(n_seq,),
        in_specs=[row_spec, cmp_spec, pl.BlockSpec((KV_HEADS, None, LANES, ncp), lambda b: (0, b, 0, 0)),
                  full(ovl), full(tri)],
        out_specs=(row_spec, pl.BlockSpec((None, SUBLANES, LANES), lambda b: (b, 0, 0))),
        compiler_params=_cparams(("parallel",)),
        name="nsa_sample_select",
    )(q8, kc, vc_t, ovl, tri)


def _sample_attend_body(idx_ref, pt_ref, q_ref, ocmp_ref, gate_ref, ksn_ref, vsn_ref, kwn_ref, vwn_ref,
                        win_ref, *refs, n_cache_blocks, blocks_per_page, win_skip):
    del pt_ref
    n_blk = KV_HEADS * TOP_N
    k_refs, v_refs, o_ref = refs[:n_blk], refs[n_blk:2 * n_blk], refs[2 * n_blk]
    b = pl.program_id(0)
    q = q_ref[...]
    qf = q.astype(F32)
    row_g = (lax.broadcasted_iota(I32, (N_HEADS, 1), 0) >= Q_PER_KV).astype(I32)

    def attend(s_list, v_list, kn_ref, vn_ref):
        s_self = jnp.sum(qf * kn_ref[...].astype(F32), axis=-1, keepdims=True)
        m = s_self
        for s in s_list:
            m = jnp.maximum(m, jnp.max(s, axis=-1, keepdims=True))
        p_self = jnp.exp2(s_self - m)
        l = p_self
        acc = p_self.astype(BF16).astype(F32) * vn_ref[...].astype(F32)
        for s, v in zip(s_list, v_list):
            p = jnp.exp2(s - m)
            l = l + jnp.sum(p, axis=-1, keepdims=True)
            acc = acc + _dot_nt(p.astype(BF16), v().astype(BF16))
        return acc / l

    s_list, v_list = [], []
    for j in range(n_blk):
        s = _dot(q, k_refs[j][...].astype(BF16))
        col = lax.broadcasted_iota(I32, s.shape, 1)
        n = idx_ref[b, j]
        first = (n % blocks_per_page) * SLC_BLOCK
        ok = ((row_g == j // TOP_N) & (col >= first) & (col < first + SLC_BLOCK)
              & (n >= 0) & (n < n_cache_blocks))
        s_list.append(jnp.where(ok, s, NEG_INF))
        v_list.append(lambda j=j: v_refs[j][...])
    o_slc = attend(s_list, v_list, ksn_ref, vsn_ref)

    s_list, v_list = [], []
    for g in range(KV_HEADS):
        s = _dot(q, win_ref[0, g].astype(BF16))
        col = lax.broadcasted_iota(I32, s.shape, 1)
        s_list.append(jnp.where((row_g == g) & (col >= win_skip), s, NEG_INF))
        v_list.append(lambda g=g: win_ref[1, g])
    o_win = attend(s_list, v_list, kwn_ref, vwn_ref)

    gates = gate_ref[...]
    o_ref[...] = gates[:, 0:1] * ocmp_ref[:, 0:HEAD_DIM] + gates[:, 1:2] * o_slc + gates[:, 2:3] * o_win


def _sample_attend_call(idx, page_table, q64, ocmp, gates8, ksn, vsn, kwn, vwn, cache_win_t, cache_t, win_skip):
    n_seq, n_pages = page_table.shape
    page_rows = cache_t.shape[-1]
    bpp = page_rows // SLC_BLOCK
    n_cache_blocks = n_pages * bpp
    row_spec = lambda a: pl.BlockSpec((None,) + a.shape[1:], lambda b, ix, pt: (b,) + (0,) * (a.ndim - 1))

    def blk_spec(j, slot):
        def index_map(b, ix, pt):
            n = jnp.clip(ix[b, j], 0, n_cache_blocks - 1)
            return (pt[b, n // bpp], slot, j // TOP_N, 0, 0)
        return pl.BlockSpec((None, None, None, HEAD_DIM, page_rows), index_map)

    n_blk = KV_HEADS * TOP_N
    small = (q64, ocmp, gates8, ksn, vsn, kwn, vwn, cache_win_t)
    return pl.pallas_call(
        functools.partial(_sample_attend_body, n_cache_blocks=n_cache_blocks, blocks_per_page=bpp,
                          win_skip=win_skip),
        out_shape=jax.ShapeDtypeStruct((n_seq, N_HEADS, HEAD_DIM), F32),
        grid_spec=pltpu.PrefetchScalarGridSpec(
            num_scalar_prefetch=2,
            grid=(n_seq,),
            in_specs=([row_spec(a) for a in small] + [blk_spec(j, 2) for j in range(n_blk)]
                      + [blk_spec(j, 3) for j in range(n_blk)]),
            out_specs=pl.BlockSpec((None, N_HEADS, HEAD_DIM), lambda b, ix, pt: (b, 0, 0))),
        compiler_params=_cparams(("parallel",)),
        name="nsa_sample_attend",
    )(idx, page_table, *small, *([cache_t] * (2 * n_blk)))


def _round_up(x, m):
    return -(-x // m) * m


def _prompt_layer(h, p, l, batch, seq):
    prm = _prep_params(p, l)
    tabs = _rope_tables(jnp.arange(seq))
    u, gs, gn, q_t, kv_t, win_t, gates_t, ks, vs_t, kw, vw_t = _proj_call(
        h, prm["w_in"], prm["norm_w"], prm["qnw"], prm["knw"], prm["gb"], tabs, batch, seq)
    y_ssm, h_re, h_im = _s5_prompt_call(u, _prep_s5_chunked(p, l), batch, seq)
    kc, vc_t = _compress_prompt_call(kv_t, _prep_compress(p, l), batch, seq)
    nch = seq // CMP_STRIDE
    ovl_t = _overlap_matrix(nch, nch - 1, seq // SLC_BLOCK).T
    o = _attn_call(q_t, kc, vc_t, ovl_t, ks, vs_t, kw, vw_t, gates_t, batch, seq)
    h_new = _outmix_call(h, y_ssm, gs, o, gn, prm["w_glu"], prm["w_out"])
    rows = lambda x_t, slots: x_t.reshape(batch, slots, KV_HEADS, HEAD_DIM, seq).transpose(0, 4, 1, 2, 3)
    return h_new, rows(kv_t, 4), rows(win_t, 2)[:, seq - min(WINDOW, seq):], h_re, h_im


def _sample_layer(h, p, l, cache_kv, cache_win, st_re, st_im, page_table):
    n_seq = h.shape[0]
    n_phys, page_rows = cache_kv.shape[:2]
    n_pages = page_table.shape[1]
    past_len = n_pages * page_rows
    win_buf = cache_win.shape[1]
    prm = _prep_params(p, l)
    n_pad = _round_up(n_seq, LANES)
    tabs = _rope_tables(jnp.full((n_pad,), past_len, I32))
    h_pad = jnp.pad(h, ((0, n_pad - n_seq), (0, 0)))
    u, gs, gn, q_t, kv_t, win_t, gates_t, ks, vs_t, kw, vw_t = _proj_call(
        h_pad, prm["w_in"], prm["norm_w"], prm["qnw"], prm["knw"], prm["gb"], tabs, 1, n_pad)
    u, gs, gn = u[:, :n_seq], gs[:n_seq], gn[:n_seq]
    n_state = st_re.shape[1] * st_re.shape[2]
    y_ssm, h_re, h_im = _s5_step_call(u, st_re.reshape(n_seq, n_state), st_im.reshape(n_seq, n_state),
                                      _prep_s5_step(p, l))
    cache_t = cache_kv.transpose(0, 2, 3, 4, 1)
    kc, vc_t = _compress_sample_call(cache_t, page_table, _prep_compress(p, l))
    ncp = past_len // CMP_STRIDE
    n_blk = -(-(past_len + 1) // SLC_BLOCK)
    nbp = _round_up(n_blk, LANES)
    ovl = _overlap_matrix(ncp, ncp - 1, n_blk, nbp)
    tri = (jnp.arange(nbp)[:, None] < jnp.arange(nbp)[None, :]).astype(BF16)
    q8 = q_t[:, :, :n_seq].transpose(2, 0, 1)
    ocmp, idx = _sample_select_call(q8, kc, vc_t, ovl, tri, past_len)
    idx = idx[:, :KV_HEADS, :TOP_N].reshape(n_seq, KV_HEADS * TOP_N)
    gates8 = gates_t[:, :3 * Q_PER_KV, :n_seq].reshape(KV_HEADS, Q_PER_KV, 3, n_seq).transpose(3, 0, 1, 2)
    gates8 = jnp.pad(gates8.reshape(n_seq, N_HEADS, 3), ((0, 0), (0, 0), (0, LANES - 3)))
    per_head = lambda a: jnp.repeat(a.transpose(1, 0, 2), Q_PER_KV, axis=1)
    new_k = lambda k: per_head(k[:, :n_seq, HEAD_DIM:])
    new_v = lambda v_t: per_head(v_t[:, 0, :HEAD_DIM, :n_seq].transpose(0, 2, 1))
    o8 = _sample_attend_call(idx, page_table, q8[:, :, HEAD_DIM:], ocmp, gates8, new_k(ks), new_v(vs_t),
                             new_k(kw), new_v(vw_t), cache_win.transpose(0, 2, 3, 4, 1), cache_t,
                             max(win_buf + 1 - WINDOW, 0))
    o = o8.reshape(n_seq, NSA_W)
    h_new = _outmix_call(h, y_ssm, gs, o, gn, prm["w_glu"], prm["w_out"])
    kv_rows = kv_t[0, :, :n_seq].T.reshape(n_seq, 1, 4, KV_HEADS, HEAD_DIM)
    win_new = win_t[0, :, :n_seq].T.reshape(n_seq, 1, 2, KV_HEADS, HEAD_DIM)
    wrows = jnp.concatenate([cache_win, win_new], axis=1)
    wrows = wrows[:, wrows.shape[1] - min(WINDOW, wrows.shape[1]):]
    state = lambda s: s.reshape(st_re.shape)
    return h_new, kv_rows, wrows, state(h_re), state(h_im)


def kernel(x_prompt, x_sample, cache_kv, cache_win, state_ssm_re, state_ssm_im, page_table, norm_w, w_in, gate_b,
           q_norm_w, k_norm_w, cmp_pe, cmp_w1, cmp_b1, cmp_w2, ssm_lam_re, ssm_lam_im, ssm_log_step, ssm_b_re,
           ssm_b_im, ssm_c_re, ssm_c_im, ssm_d, w_glu, w_out):
    p = dict(norm_w=norm_w, w_in=w_in, gate_b=gate_b, q_norm_w=q_norm_w, k_norm_w=k_norm_w, cmp_pe=cmp_pe,
             cmp_w1=cmp_w1, cmp_b1=cmp_b1, cmp_w2=cmp_w2, ssm_lam_re=ssm_lam_re, ssm_lam_im=ssm_lam_im,
             ssm_log_step=ssm_log_step, ssm_b_re=ssm_b_re, ssm_b_im=ssm_b_im, ssm_c_re=ssm_c_re,
             ssm_c_im=ssm_c_im, ssm_d=ssm_d, w_glu=w_glu, w_out=w_out)
    b_p, s_p, d_model = x_prompt.shape
    b_s, s_s, _ = x_sample.shape
    assert s_s == 1, "the sample group decodes one token per sequence"
    h_p = x_prompt.reshape(b_p * s_p, d_model)
    h_s = x_sample.reshape(b_s, d_model)
    outs_p, outs_s = [], []
    for l in range(norm_w.shape[0]):
        h_p, *rest_p = _prompt_layer(h_p, p, l, b_p, s_p)
        h_s, *rest_s = _sample_layer(h_s, p, l, cache_kv[l], cache_win[l], state_ssm_re[l], state_ssm_im[l],
                                     page_table)
        outs_p.append(rest_p)
        outs_s.append(rest_s)
    stack = lambda outs, i: jnp.stack([o[i] for o in outs])
    return (h_p.reshape(x_prompt.shape), h_s.reshape(x_sample.shape),
            stack(outs_p, 0), stack(outs_s, 0), stack(outs_p, 1), stack(outs_s, 1),
            stack(outs_p, 2), stack(outs_p, 3), stack(outs_s, 2), stack(outs_s, 3))
```

```python
import functools
import math

import jax
import jax.numpy as jnp
from jax import lax
from jax.experimental import pallas as pl
from jax.experimental.pallas import tpu as pltpu

F32 = jnp.float32
BF16 = jnp.bfloat16
I32 = jnp.int32

LANES = 128
SUBLANES = 8
VMEM_LIMIT_BYTES = 56 * 1024 * 1024

HEAD_DIM = 64
N_HEADS = 8
KV_HEADS = 2
Q_PER_KV = N_HEADS // KV_HEADS
SSM_W = 512
SSM_GROUP = 16
SSM_STATE = 64
NSA_W = N_HEADS * HEAD_DIM
CMP_BLOCK = 32
CMP_STRIDE = 16
CMP_HID = 2 * HEAD_DIM
SLC_BLOCK = 64
TOP_N = 16
N_LOCAL_BLOCKS = 2
WINDOW = 512
ROPE_THETA = 500000.0
ROPE_DIM = HEAD_DIM // 4
RMS_EPS = 1e-6
NEG_INF = -1e30

COL_U = 0
COL_GS = SSM_W
COL_Q = 2 * SSM_W
COL_GN = 2 * SSM_W + NSA_W
COL_KV = 2 * SSM_W + 2 * NSA_W
COL_GL = COL_KV + 6 * KV_HEADS * HEAD_DIM
IN_W_PAD = COL_GL + LANES

PROJ_TILE = 512
ATTN_TILE = 256
SSM_CHUNK = 16
SSM_SLAB_GROUPS = LANES // SSM_GROUP
SSM_SLABS = SSM_W // LANES


def _cparams(sem):
    return pltpu.CompilerParams(dimension_semantics=sem, vmem_limit_bytes=VMEM_LIMIT_BYTES)


def _sigmoid(x):
    return 1.0 / (1.0 + jnp.exp(-x))


def _dot(a, b):
    return jnp.dot(a, b, preferred_element_type=F32)


def _dot_nt(a, b):
    return lax.dot_general(a, b, (((1,), (1,)), ((), ())), preferred_element_type=F32)


def _proj_body(x_ref, nw_ref, w_ref, qnw_ref, knw_ref, gb_ref, ra_ref, rb_ref, rc_ref,
               u_ref, gs_ref, gn_ref, qt_ref, kvt_ref, wint_ref, gt_ref,
               ks_ref, vst_ref, kw_ref, vwt_ref, *, tm, tv, tiles_per_seq):
    x = x_ref[...]
    ms = jnp.mean(x * x, axis=-1, keepdims=True)
    h = (x * lax.rsqrt(ms + RMS_EPS) * nw_ref[...]).astype(BF16)

    def mm(c0, c1):
        return _dot(h, w_ref[:, c0:c1])

    lane = lax.broadcasted_iota(I32, (tm, LANES), 1)
    lo = lane < HEAD_DIM
    ra = ra_ref[...]
    rb = rb_ref[...]
    rc = rc_ref[...]

    def norm_rope(s, wrow):
        s2 = s * s
        slo = jnp.sum(jnp.where(lo, s2, 0.0), axis=-1, keepdims=True)
        shi = jnp.sum(jnp.where(lo, 0.0, s2), axis=-1, keepdims=True)
        msq = jnp.where(lo, slo, shi) * (1.0 / HEAD_DIM)
        y = s * lax.rsqrt(msq + RMS_EPS) * wrow
        half = ROPE_DIM // 2
        return y * ra + pltpu.roll(y, LANES - half, 1) * rb + pltpu.roll(y, half, 1) * rc

    def hi_half(y, head):
        src = pltpu.roll(y, HEAD_DIM, 1) if head == 0 else y
        return jnp.where(lo, 0.0, src)

    zq = mm(COL_Q, COL_GN)
    zkv = mm(COL_KV, COL_GL)
    zgl = mm(COL_GL, IN_W_PAD)

    qnw = qnw_ref[...]
    scale = HEAD_DIM ** -0.5 * math.log2(math.e)
    zeros_t = jnp.zeros((HEAD_DIM, tm), F32)
    for j in range(N_HEADS // 2):
        y_t = (norm_rope(zq[:, j * LANES:(j + 1) * LANES], qnw) * scale).T
        for head in range(2):
            q_t = jnp.concatenate([zeros_t, y_t[head * HEAD_DIM:(head + 1) * HEAD_DIM]], axis=0)
            qt_ref[2 * j + head] = q_t.astype(BF16)

    kc = norm_rope(zkv[:, 0:LANES], knw_ref[0:1, :])
    vc = zkv[:, LANES:2 * LANES]
    ks = norm_rope(zkv[:, 2 * LANES:3 * LANES], knw_ref[1:2, :])
    vs = zkv[:, 3 * LANES:4 * LANES]
    kw = norm_rope(zkv[:, 4 * LANES:5 * LANES], knw_ref[2:3, :])
    vw = zkv[:, 5 * LANES:6 * LANES]
    vs_t, vw_t = vs.T, vw.T
    for i, rows_t in enumerate((kc.T, vc.T, ks.T, vs_t)):
        kvt_ref[i * LANES:(i + 1) * LANES, :] = rows_t
    for i, rows_t in enumerate((kw.T, vw_t)):
        wint_ref[i * LANES:(i + 1) * LANES, :] = rows_t

    row = lax.broadcasted_iota(I32, (tm, LANES), 0)
    pos = (pl.program_id(0) % tiles_per_seq) * tm + row
    onehot = jnp.where(lane == lax.shift_right_logical(pos, 6), 1.0, 0.0)
    ones_t = jnp.where(lax.broadcasted_iota(I32, (HEAD_DIM, tm), 0) == 0, 1.0, 0.0)
    for g in range(KV_HEADS):
        ks_ref[g] = jnp.where(lo, onehot, hi_half(ks, g)).astype(BF16)
        kw_ref[g] = hi_half(kw, g).astype(BF16)
        for v_t, vt_ref in ((vs_t, vst_ref), (vw_t, vwt_ref)):
            v_aug = jnp.concatenate([v_t[g * HEAD_DIM:(g + 1) * HEAD_DIM], ones_t], axis=0).astype(BF16)
            for t in range(tm // tv):
                vt_ref[g, t] = v_aug[:, t * tv:(t + 1) * tv]

    gates_t = _sigmoid(zgl + gb_ref[...]).T
    for g in range(KV_HEADS):
        gt_ref[g] = gates_t[g * 2 * SUBLANES:(g + 1) * 2 * SUBLANES]

    zu = mm(COL_U, COL_GS)
    for j in range(SSM_SLABS):
        u_ref[j] = zu[:, j * LANES:(j + 1) * LANES]
    gs_ref[...] = mm(COL_GS, COL_Q)
    gn_ref[...] = mm(COL_GN, COL_KV)


def _proj_call(x2d, w_pad, norm_w, qnw, knw, gb, tabs, batch, seq):
    T, D = x2d.shape
    tm = min(PROJ_TILE, seq)
    tv = min(ATTN_TILE, seq)
    assert T == batch * seq and seq % tm == 0 and tm % tv == 0 and tv % LANES == 0
    tps = seq // tm
    row_spec = lambda w: pl.BlockSpec((tm, w), lambda i: (i, 0))
    full = lambda a: pl.BlockSpec(a.shape, lambda i: (0,) * a.ndim)
    tab_spec = pl.BlockSpec((tm, LANES), lambda i: (i % tps, 0))
    head_spec = lambda n: pl.BlockSpec((n, tm, LANES), lambda i: (0, i, 0))
    head_t_spec = lambda n, rows: pl.BlockSpec((n, rows, tm), lambda i: (0, 0, i))
    cache_t_spec = lambda rows: pl.BlockSpec((None, rows, tm), lambda i: (i // tps, 0, i % tps))
    tile_t_spec = pl.BlockSpec((KV_HEADS, tm // tv, LANES, tv), lambda i: (0, i, 0, 0))
    tile_t_sds = jax.ShapeDtypeStruct((KV_HEADS, T // tv, LANES, tv), BF16)
    out_shape = (
        jax.ShapeDtypeStruct((SSM_SLABS, T, LANES), F32),
        jax.ShapeDtypeStruct((T, SSM_W), F32),
        jax.ShapeDtypeStruct((T, NSA_W), F32),
        jax.ShapeDtypeStruct((N_HEADS, LANES, T), BF16),
        jax.ShapeDtypeStruct((batch, 4 * LANES, seq), F32),
        jax.ShapeDtypeStruct((batch, 2 * LANES, seq), F32),
        jax.ShapeDtypeStruct((KV_HEADS, 2 * SUBLANES, T), F32),
        jax.ShapeDtypeStruct((KV_HEADS, T, LANES), BF16),
        tile_t_sds,
        jax.ShapeDtypeStruct((KV_HEADS, T, LANES), BF16),
        tile_t_sds,
    )
    out_specs = (head_spec(SSM_SLABS), row_spec(SSM_W), row_spec(NSA_W), head_t_spec(N_HEADS, LANES),
                 cache_t_spec(4 * LANES), cache_t_spec(2 * LANES), head_t_spec(KV_HEADS, 2 * SUBLANES),
                 head_spec(KV_HEADS), tile_t_spec, head_spec(KV_HEADS), tile_t_spec)
    return pl.pallas_call(
        functools.partial(_proj_body, tm=tm, tv=tv, tiles_per_seq=tps),
        out_shape=out_shape,
        grid=(T // tm,),
        in_specs=[row_spec(D), full(norm_w), full(w_pad), full(qnw), full(knw), full(gb),
                  tab_spec, tab_spec, tab_spec],
        out_specs=out_specs,
        compiler_params=_cparams(("parallel",)),
        name="proj",
    )(x2d, norm_w, w_pad, qnw, knw, gb, *tabs)


def _prep_params(p, l):
    w_in = p["w_in"][l].astype(BF16)
    d_model, in_w = w_in.shape
    tile2 = lambda v: jnp.tile(v, (1, LANES // HEAD_DIM))
    per_grp = 3 * Q_PER_KV
    grp_pad = lambda a: jnp.pad(a.reshape(a.shape[0], KV_HEADS, per_grp),
                                ((0, 0), (0, 0), (0, 2 * SUBLANES - per_grp))).reshape(a.shape[0], -1)
    lane_pad = lambda a: jnp.pad(a, ((0, 0), (0, LANES - a.shape[1])))
    return {
        "w_in": jnp.concatenate([w_in[:, :COL_GL], lane_pad(grp_pad(w_in[:, COL_GL:]))], axis=1),
        "norm_w": p["norm_w"][l].reshape(1, d_model).astype(F32),
        "qnw": tile2(p["q_norm_w"][l].reshape(1, HEAD_DIM)).astype(F32),
        "knw": tile2(p["k_norm_w"][l]).astype(F32),
        "gb": lane_pad(grp_pad(p["gate_b"][l].reshape(1, -1).astype(F32))),
        "w_glu": p["w_glu"][l].astype(BF16),
        "w_out": p["w_out"][l].astype(BF16),
    }


def _rope_tables(pos):
    half = ROPE_DIM // 2
    inv = ROPE_THETA ** (-jnp.arange(half, dtype=F32) / half)
    ang = pos.astype(F32)[:, None] * inv
    cos, sin = jnp.cos(ang), jnp.sin(ang)
    n = pos.shape[0]
    rest = HEAD_DIM - ROPE_DIM
    a = jnp.concatenate([cos, cos, jnp.ones((n, rest), F32)], axis=-1)
    b = jnp.concatenate([-sin, jnp.zeros((n, HEAD_DIM - half), F32)], axis=-1)
    c = jnp.concatenate([jnp.zeros((n, half), F32), sin, jnp.zeros((n, rest), F32)], axis=-1)
    return tuple(jnp.tile(t, (1, LANES // HEAD_DIM)) for t in (a, b, c))


def _outmix_body(x_ref, y_ref, gs_ref, o_ref, gn_ref, wg_ref, wo_ref, out_ref):
    y = jnp.concatenate([y_ref[j] for j in range(SSM_SLABS)], axis=-1)
    ab = _dot(y.astype(BF16), wg_ref[...])
    gs = gs_ref[...]
    ssm = ab[:, :SSM_W] * _sigmoid(ab[:, SSM_W:]) * (gs * _sigmoid(gs))
    gn = gn_ref[...]
    nsa = o_ref[...] * (gn * _sigmoid(gn))
    acc = _dot(ssm.astype(BF16), wo_ref[0:SSM_W, :])
    acc += _dot(nsa.astype(BF16), wo_ref[SSM_W:, :])
    out_ref[...] = x_ref[...] + acc


def _outmix_call(x2d, y_ssm, g_ssm, o_nsa, g_nsa, w_glu, w_out):
    T, D = x2d.shape
    tm = min(512, T)
    row_spec = lambda w: pl.BlockSpec((tm, w), lambda i: (i, 0))
    full = lambda a: pl.BlockSpec(a.shape, lambda i: (0,) * a.ndim)
    return pl.pallas_call(
        _outmix_body,
        out_shape=jax.ShapeDtypeStruct((T, D), F32),
        grid=(T // tm,),
        in_specs=[row_spec(D), pl.BlockSpec((SSM_SLABS, tm, LANES), lambda i: (0, i, 0)),
                  row_spec(SSM_W), row_spec(NSA_W), row_spec(NSA_W),
                  full(w_glu), full(w_out)],
        out_specs=row_spec(D),
        compiler_params=_cparams(("parallel",)),
        name="outmix",
    )(x2d, y_ssm, g_ssm, o_nsa, g_nsa, w_glu, w_out)


def _gelu_tanh(x):
    c = math.sqrt(2.0 / math.pi)
    return 0.5 * x * (1.0 + jnp.tanh(c * (x + 0.044715 * (x * x * x))))


def _compress_rows(rows_refs, pe_ref, wa_ref, wb_ref, b1_ref, w2_ref, kc_ref, vc_ref, nch):
    lane = lax.broadcasted_iota(I32, (nch, LANES), 1)
    for kvi, out_ref in ((0, kc_ref), (1, vc_ref)):
        rows_ref = rows_refs[kvi]
        pa = jnp.zeros((nch, 2 * CMP_HID), F32)
        pb = jnp.zeros((nch, 2 * CMP_HID), F32)
        for j0 in range(0, CMP_STRIDE, 2):
            xs = [rows_ref[pl.ds(j, nch, stride=CMP_STRIDE), :] for j in (j0, j0 + 1)]
            xa = jnp.concatenate([xs[i] + pe_ref[kvi, 0, j0 + i:j0 + i + 1, :] for i in range(2)], axis=-1)
            xb = jnp.concatenate([xs[i] + pe_ref[kvi, 1, j0 + i:j0 + i + 1, :] for i in range(2)], axis=-1)
            wsl = slice(j0 * LANES, (j0 + 2) * LANES)
            pa += _dot(xa.astype(BF16), wa_ref[kvi, wsl, :])
            pb += _dot(xb.astype(BF16), wb_ref[kvi, wsl, :])
        hid = _gelu_tanh(pa + pltpu.roll(pb, nch - 1, 0) + b1_ref[kvi]).astype(BF16)
        for g in range(KV_HEADS):
            o = _dot(hid, w2_ref[kvi, g])
            if kvi == 1:
                o = jnp.where(lane == HEAD_DIM, 1.0, o).T
            out_ref[g] = o.astype(BF16)


def _compress_prompt_body(kvt_ref, pe_ref, wa_ref, wb_ref, b1_ref, w2_ref, kc_ref, vc_ref,
                          krows_ref, vrows_ref, *, nch):
    for c in range(kvt_ref.shape[1] // LANES):
        cs = slice(c * LANES, (c + 1) * LANES)
        krows_ref[cs, :] = kvt_ref[0:LANES, cs].T
        vrows_ref[cs, :] = kvt_ref[LANES:2 * LANES, cs].T
    _compress_rows((krows_ref, vrows_ref), pe_ref, wa_ref, wb_ref, b1_ref, w2_ref, kc_ref, vc_ref, nch)


def _compress_prompt_call(kv_t, cw, batch, seq):
    nch = seq // CMP_STRIDE
    full = lambda a: pl.BlockSpec(a.shape, lambda b: (0,) * a.ndim)
    out_spec = pl.BlockSpec((KV_HEADS, nch, LANES), lambda b: (0, b, 0))
    out_sds = jax.ShapeDtypeStruct((KV_HEADS, batch * nch, LANES), BF16)
    out_t_spec = pl.BlockSpec((KV_HEADS, None, LANES, nch), lambda b: (0, b, 0, 0))
    out_t_sds = jax.ShapeDtypeStruct((KV_HEADS, batch, LANES, nch), BF16)
    return pl.pallas_call(
        functools.partial(_compress_prompt_body, nch=nch),
        out_shape=(out_sds, out_t_sds),
        grid=(batch,),
        in_specs=[pl.BlockSpec((None, 2 * LANES, seq), lambda b: (b, 0, 0)),
                  full(cw["pe"]), full(cw["wa"]), full(cw["wb"]), full(cw["b1"]), full(cw["w2"])],
        out_specs=(out_spec, out_t_spec),
        scratch_shapes=[pltpu.VMEM((seq, LANES), F32)] * 2,
        compiler_params=_cparams(("parallel",)),
        name="compress_prompt",
    )(kv_t, cw["pe"], cw["wa"], cw["wb"], cw["b1"], cw["w2"])


def _prep_compress(p, l):
    eye = jnp.eye(KV_HEADS, dtype=F32)
    w1 = p["cmp_w1"][l].reshape(2, 2, CMP_STRIDE, HEAD_DIM, CMP_HID)
    wexp = jnp.einsum("khjdn,ge->khjgden", w1, eye).reshape(2, 2, CMP_STRIDE * LANES, KV_HEADS * CMP_HID)
    pe = p["cmp_pe"][l].reshape(2, 2, CMP_STRIDE, HEAD_DIM)
    w2 = p["cmp_w2"][l]
    zeros = jnp.zeros_like(w2[0])
    w2k = jnp.concatenate([zeros, w2[0]], axis=-1)
    w2v = jnp.concatenate([w2[1], zeros], axis=-1)
    w2e = jnp.stack([jnp.einsum("hd,ge->gehd", w, eye).reshape(KV_HEADS, KV_HEADS * CMP_HID, LANES)
                     for w in (w2k, w2v)])
    return {
        "pe": jnp.tile(pe, (1, 1, 1, KV_HEADS)).astype(F32),
        "wa": wexp[:, 0].astype(BF16),
        "wb": wexp[:, 1].astype(BF16),
        "b1": jnp.tile(p["cmp_b1"][l].reshape(2, 1, CMP_HID), (1, 1, KV_HEADS)).astype(F32),
        "w2": w2e.astype(BF16),
    }


def _overlap_matrix(n_tok_pad, n_tok, n_blk, n_cols=LANES):
    c_start = jnp.arange(n_tok_pad) * CMP_STRIDE
    blk = jnp.arange(n_cols)
    ov = ((c_start[:, None] < (blk[None, :] + 1) * SLC_BLOCK)
          & (c_start[:, None] + CMP_BLOCK > blk[None, :] * SLC_BLOCK)
          & (jnp.arange(n_tok_pad)[:, None] < n_tok) & (blk[None, :] < n_blk))
    return ov.astype(BF16)


def _topk_select_t(w_ref, imp_t, q0, tq):
    nb = imp_t.shape[0]
    n_i = lax.broadcasted_iota(I32, (nb, tq), 0)
    qblk = lax.shift_right_logical(q0 + lax.broadcasted_iota(I32, (nb, tq), 1), 6)
    causal = n_i <= qblk
    forced = (n_i == 0) | (n_i >= qblk - (N_LOCAL_BLOCKS - 1))
    w_ref[...] = jnp.where(causal, jnp.where(forced, jnp.inf, imp_t), -jnp.inf)
    last_blk = lax.shift_right_logical(q0 + tq - 1, 6)
    n_grp = nb // SUBLANES
    rank = [jnp.zeros((SUBLANES, tq), F32) for _ in range(n_grp)]
    grp_i = lax.broadcasted_iota(I32, (SUBLANES, tq), 0)

    def count_group(mg, rank):
        rank = list(rank)
        for mi in range(SUBLANES):
            m = mg * SUBLANES + mi
            wm = w_ref[m:m + 1, :]
            for ng in range(n_grp):
                w = w_ref[ng * SUBLANES:(ng + 1) * SUBLANES, :]
                if ng > mg:
                    beats = jnp.where(wm >= w, 1.0, 0.0)
                elif ng < mg:
                    beats = jnp.where(wm > w, 1.0, 0.0)
                else:
                    beats = jnp.where(grp_i > mi, jnp.where(wm >= w, 1.0, 0.0), jnp.where(wm > w, 1.0, 0.0))
                rank[ng] = rank[ng] + beats
        return tuple(rank)

    rank = tuple(rank)
    for mg in range(n_grp):
        rank = lax.cond(mg * SUBLANES <= last_blk, functools.partial(count_group, mg), lambda r: r, rank)
    return causal & (jnp.concatenate(rank, axis=0) < TOP_N)


def _flash_tiles_t(tiles, q_ts, ms, accs):
    def scores(i):
        k, _, mask = tiles[i]
        s = [_dot(k, q_t) for q_t in q_ts]
        return s if mask is None else [jnp.where(mask, x, NEG_INF) for x in s]

    ahead = 2
    pending = {i: scores(i) for i in range(min(ahead, len(tiles)))}
    for i, (_, v_t, _) in enumerate(tiles):
        s = pending.pop(i)
        m_new = [jnp.maximum(m, jnp.max(x, axis=0, keepdims=True)) for m, x in zip(ms, s)]
        alpha = [jnp.exp2(m - mn) for m, mn in zip(ms, m_new)]
        p = [jnp.exp2(x - mn).astype(BF16) for x, mn in zip(s, m_new)]
        pv = [_dot(v_t, x) for x in p]
        if i + ahead < len(tiles):
            pending[i + ahead] = scores(i + ahead)
        accs = [a * acc + x for a, acc, x in zip(alpha, accs, pv)]
        ms = m_new
    return tuple(ms), tuple(accs)


def _attn_body(qt_ref, kc_ref, vct_ref, ovlt_ref, ks_ref, vst_ref, kw_ref, vwt_ref, gate_ref, o_ref,
               qa_ref, ocmp_ref, w_ref, *, tq, ncp):
    R = Q_PER_KV
    qt = pl.program_id(2)
    q0 = qt * tq
    nbs = LANES // 2

    c_i = lax.broadcasted_iota(I32, (ncp, tq), 0)
    qpos_c = q0 + lax.broadcasted_iota(I32, (ncp, tq), 1)
    cmask = c_i * CMP_STRIDE + (CMP_BLOCK - 1) <= qpos_c
    kc = kc_ref[...]
    s = [jnp.where(cmask, _dot(kc, qt_ref[r]), NEG_INF) for r in range(R)]
    e = [jnp.where(cmask, jnp.exp2(x - jnp.max(x, axis=0, keepdims=True)), 0.0) for x in s]
    l = [jnp.sum(x, axis=0, keepdims=True) for x in e]
    p = [(x * (1.0 / jnp.where(y > 0.0, y, 1.0))).astype(BF16) for x, y in zip(e, l)]
    imp = sum(_dot(ovlt_ref[...], x) for x in p)
    for r in range(R):
        ocmp_ref[r] = _dot(vct_ref[...], p[r])

    sel = _topk_select_t(w_ref, imp[:nbs], q0, tq)
    bias = jnp.concatenate([jnp.where(sel, 0.0, NEG_INF), jnp.zeros((LANES - nbs, tq), F32)], axis=0)
    for r in range(R):
        qa_ref[r] = (qt_ref[r].astype(F32) + bias).astype(BF16)

    key_i = lax.broadcasted_iota(I32, (tq, tq), 0)
    qry_i = lax.broadcasted_iota(I32, (tq, tq), 1)
    init = (tuple(jnp.full((1, tq), -jnp.inf, F32) for _ in range(R)),
            tuple(jnp.zeros((LANES, tq), F32) for _ in range(R)))

    def tiles(k_ref, vt_ref, q_ref, js, masks, state):
        ts = [(k_ref[pl.ds(pl.multiple_of(j * tq, tq), tq), :], vt_ref[j], mask) for j, mask in zip(js, masks)]
        return _flash_tiles_t(ts, [q_ref[r] for r in range(R)], *state)

    def last_tiles(k_ref, vt_ref, q_ref, n, first_mask, state):
        js = [qt - (n - 1 - t) for t in range(n)]
        masks = [first_mask] + [None] * (n - 2) + [key_i <= qry_i] if n > 1 else [key_i <= qry_i]
        return tiles(k_ref, vt_ref, q_ref, js, masks, state)

    GROUP = 4

    def slc_group(i, st):
        return tiles(ks_ref, vst_ref, qa_ref, [GROUP * i + t for t in range(GROUP)], [None] * GROUP, st)

    state = lax.fori_loop(0, qt // GROUP, slc_group, init)
    tails = [functools.partial(last_tiles, ks_ref, vst_ref, qa_ref, n, None) for n in range(1, GROUP + 1)]
    _, acc_s = lax.switch(qt % GROUP, tails, state)

    nwin = WINDOW // tq
    wins = [functools.partial(last_tiles, kw_ref, vwt_ref, qt_ref, n, (key_i > qry_i) if n == nwin + 1 else None)
            for n in range(1, nwin + 2)]
    _, acc_w = lax.switch(jnp.minimum(qt, nwin), wins, init)

    outs = []
    for r in range(R):
        a_s = acc_s[r]
        a_w = acc_w[r]
        g = lambda k: gate_ref[3 * r + k:3 * r + k + 1, :]
        o = (g(0) * ocmp_ref[r] + g(1) * (a_s / a_s[HEAD_DIM:HEAD_DIM + 1, :])
             + g(2) * (a_w / a_w[HEAD_DIM:HEAD_DIM + 1, :]))
        outs.append(o[:HEAD_DIM])
    o_ref[...] = jnp.concatenate(outs, axis=0).T


def _attn_call(q_t, kc, vc_t, ovl_t, ks, vs_t, kw, vw_t, gates_t, batch, seq):
    tq = min(ATTN_TILE, seq)
    nq = seq // tq
    ncp = kc.shape[1] // batch
    R = Q_PER_KV
    assert seq // SLC_BLOCK <= LANES // 2 and WINDOW % tq == 0
    k_spec = pl.BlockSpec((None, seq, LANES), lambda b, g, t: (g, b, 0))
    vt_spec = pl.BlockSpec((None, nq, LANES, tq), lambda b, g, t: (g, b, 0, 0))
    acc = pltpu.VMEM((R, LANES, tq), F32)
    return pl.pallas_call(
        functools.partial(_attn_body, tq=tq, ncp=ncp),
        out_shape=jax.ShapeDtypeStruct((batch * seq, NSA_W), F32),
        grid=(batch, KV_HEADS, nq),
        in_specs=[pl.BlockSpec((R, LANES, tq), lambda b, g, t: (g, 0, b * nq + t)),
                  pl.BlockSpec((None, ncp, LANES), lambda b, g, t: (g, b, 0)),
                  pl.BlockSpec((None, None, LANES, ncp), lambda b, g, t: (g, b, 0, 0)),
                  pl.BlockSpec(ovl_t.shape, lambda b, g, t: (0, 0)),
                  k_spec, vt_spec, k_spec, vt_spec,
                  pl.BlockSpec((None, 2 * SUBLANES, tq), lambda b, g, t: (g, 0, b * nq + t))],
        out_specs=pl.BlockSpec((tq, R * HEAD_DIM), lambda b, g, t: (b * nq + t, g)),
        scratch_shapes=[pltpu.VMEM((R, LANES, tq), BF16), acc, pltpu.VMEM((LANES // 2, tq), F32)],
        compiler_params=_cparams(("parallel", "parallel", "arbitrary")),
        name="nsa_prompt",
    )(q_t, kc, vc_t, ovl_t, ks, vs_t, kw, vw_t, gates_t)


def _s5_discretise(p, l):
    lr = p["ssm_lam_re"][l].astype(F32)
    li = p["ssm_lam_im"][l].astype(F32)
    dt = jnp.exp(p["ssm_log_step"][l].astype(F32))[:, None]

    def apow(t):
        mag, ang = jnp.exp(lr * dt * t), li * dt * t
        return mag * jnp.cos(ang), mag * jnp.sin(ang)

    a_re, a_im = apow(1.0)
    den = lr * lr + li * li
    nr, ni = a_re - 1.0, a_im
    f_re, f_im = (nr * lr + ni * li) / den, (ni * lr - nr * li) / den
    br, bi = p["ssm_b_re"][l].astype(F32), p["ssm_b_im"][l].astype(F32)
    bb_re = f_re[..., None] * br - f_im[..., None] * bi
    bb_im = f_re[..., None] * bi + f_im[..., None] * br
    return apow, bb_re, bb_im


def _prep_s5_step(p, l):
    apow, bb_re, bb_im = _s5_discretise(p, l)
    a_re, a_im = apow(1.0)
    G = bb_re.shape[0]
    eye = jnp.eye(G, dtype=F32)
    to_state = lambda bb: jnp.einsum("gnk,ge->gken", bb, eye).reshape(SSM_W, G * SSM_STATE)
    c_re, c_im = p["ssm_c_re"][l].astype(F32), p["ssm_c_im"][l].astype(F32)
    from_state = lambda c: jnp.einsum("gcn,ge->gnec", c, eye).reshape(G * SSM_STATE, SSM_W)
    return {
        "w_x": jnp.concatenate([to_state(bb_re), to_state(bb_im)], axis=1).astype(BF16),
        "w_y": jnp.concatenate([from_state(c_re), from_state(-c_im)], axis=0).astype(BF16),
        "a_re": a_re.reshape(1, -1), "a_im": a_im.reshape(1, -1),
        "d": p["ssm_d"][l].reshape(1, SSM_W).astype(F32),
    }


def _prep_s5_chunked(p, l):
    hi = lax.Precision.HIGHEST
    apow, bb_re, bb_im = _s5_discretise(p, l)
    c_re, c_im = p["ssm_c_re"][l].astype(F32), p["ssm_c_im"][l].astype(F32)
    T, J, E, N, C = SSM_CHUNK, SSM_SLABS, SSM_SLAB_GROUPS, SSM_STATE, SSM_GROUP
    pw_re, pw_im = apow(jnp.arange(T + 1, dtype=F32)[:, None, None])
    grp = jnp.arange(E)

    def block_diag(x, width):
        own = (grp[:, None, None] == (jnp.arange(E * width) // width)[None, None, :])
        return jnp.where(own, jnp.tile(x, (1,) * (x.ndim - 1) + (E,)), 0.0)

    slab = lambda x: jnp.moveaxis(x.reshape(x.shape[0], J, E, *x.shape[2:]), 1, 0)
    bt_re, bt_im = bb_re.transpose(0, 2, 1), bb_im.transpose(0, 2, 1)
    p_re = pw_re[:, :, None, :] * bt_re - pw_im[:, :, None, :] * bt_im
    p_im = pw_re[:, :, None, :] * bt_im + pw_im[:, :, None, :] * bt_re
    kt = (jnp.einsum("tgkn,gcn->tgkc", p_re[:T], c_re, precision=hi)
          - jnp.einsum("tgkn,gcn->tgkc", p_im[:T], c_im, precision=hi))
    w_col = block_diag(slab(kt[::-1]), C).reshape(J, T * LANES, LANES)
    w_st = jnp.concatenate([block_diag(slab(x[:T][::-1]), N) for x in (p_re, p_im)], axis=-1)
    w_st = w_st.reshape(J, T * LANES, 2 * E * N)
    ct_re, ct_im = c_re.transpose(0, 2, 1), c_im.transpose(0, 2, 1)
    cp_re = ct_re * pw_re[1:, :, :, None] - ct_im * pw_im[1:, :, :, None]
    cp_im = ct_re * pw_im[1:, :, :, None] + ct_im * pw_re[1:, :, :, None]
    w_out = jnp.concatenate([block_diag(slab(x), C).reshape(J, T, E * N, LANES) for x in (cp_re, -cp_im)], axis=2)
    return {
        "w_col": w_col.astype(BF16), "w_st": w_st.astype(BF16), "w_out": w_out.astype(BF16),
        "a_re": pw_re[T].reshape(J, 1, E * N), "a_im": pw_im[T].reshape(J, 1, E * N),
        "d": p["ssm_d"][l].reshape(J, 1, LANES).astype(F32),
    }


def _s5_prompt_body(u_ref, wcol_ref, wst_ref, are_ref, aim_ref, wout_ref, d_ref,
                    y_ref, hre_ref, him_ref, xs_ref, hp_ref, *, n_chunks):
    T = SSM_CHUNK
    ns = SSM_SLAB_GROUPS * SSM_STATE
    u_pos = [u_ref[pl.ds(s, n_chunks, stride=T), :] for s in range(T)]
    ub = jnp.concatenate(u_pos, axis=-1).astype(BF16)
    xs_ref[...] = _dot(ub, wst_ref[...])
    a_re = are_ref[...]
    a_im = aim_ref[...]

    def step(c, carry):
        hr, hi = carry
        hp_ref[pl.ds(c, 1), 0:ns] = hr
        hp_ref[pl.ds(c, 1), ns:2 * ns] = hi
        xr = xs_ref[pl.ds(c, 1), 0:ns]
        xi = xs_ref[pl.ds(c, 1), ns:2 * ns]
        return a_re * hr - a_im * hi + xr, a_re * hi + a_im * hr + xi

    zero = jnp.zeros((1, ns), F32)
    hr, hi = lax.fori_loop(0, n_chunks, step, (zero, zero))
    hre_ref[...] = jnp.broadcast_to(hr, hre_ref.shape)
    him_ref[...] = jnp.broadcast_to(hi, him_ref.shape)
    hpb = hp_ref[...].astype(BF16)
    for t in range(T):
        y_ref[pl.ds(t, n_chunks, stride=T), :] = (
            _dot(ub[:, :(t + 1) * LANES], wcol_ref[(T - 1 - t) * LANES:, :])
            + _dot(hpb, wout_ref[t]) + d_ref[...] * u_pos[t])


def _s5_prompt_call(u_slab, sw, batch, seq):
    T, J = SSM_CHUNK, SSM_SLABS
    n_chunks = seq // T
    ns = SSM_SLAB_GROUPS * SSM_STATE
    row_spec = pl.BlockSpec((None, seq, LANES), lambda b, j: (j, b, 0))
    slab_spec = lambda a: pl.BlockSpec((None,) + a.shape[1:], lambda b, j: (j,) + (0,) * (a.ndim - 1))
    st_spec = pl.BlockSpec((None, None, SUBLANES, ns), lambda b, j: (b, j, 0, 0))
    st_sds = jax.ShapeDtypeStruct((batch, J, SUBLANES, ns), F32)
    y, hre, him = pl.pallas_call(
        functools.partial(_s5_prompt_body, n_chunks=n_chunks),
        out_shape=(jax.ShapeDtypeStruct(u_slab.shape, F32), st_sds, st_sds),
        grid=(batch, J),
        in_specs=[row_spec, slab_spec(sw["w_col"]), slab_spec(sw["w_st"]), slab_spec(sw["a_re"]),
                  slab_spec(sw["a_im"]), slab_spec(sw["w_out"]), slab_spec(sw["d"])],
        out_specs=(row_spec, st_spec, st_spec),
        scratch_shapes=[pltpu.VMEM((n_chunks, 2 * ns), F32), pltpu.VMEM((n_chunks, 2 * ns), F32)],
        compiler_params=_cparams(("parallel", "parallel")),
        name="s5_prompt",
    )(u_slab, sw["w_col"], sw["w_st"], sw["a_re"], sw["a_im"], sw["w_out"], sw["d"])
    n_groups = J * SSM_SLAB_GROUPS
    state = lambda h: h[:, :, 0, :].reshape(batch, n_groups, SSM_STATE)
    return y, state(hre), state(him)


def _s5_step_body(u_ref, wx_ref, are_ref, aim_ref, h0re_ref, h0im_ref, wy_ref, d_ref,
                  y_ref, hre_ref, him_ref):
    u = jnp.concatenate([u_ref[j] for j in range(SSM_SLABS)], axis=-1)
    x = _dot(u.astype(BF16), wx_ref[...])
    ns = h0re_ref.shape[1]
    a_re, a_im = are_ref[...], aim_ref[...]
    h0r, h0i = h0re_ref[...], h0im_ref[...]
    hr = a_re * h0r - a_im * h0i + x[:, :ns]
    hi = a_re * h0i + a_im * h0r + x[:, ns:]
    hre_ref[...] = hr
    him_ref[...] = hi
    y = _dot(jnp.concatenate([hr, hi], axis=-1).astype(BF16), wy_ref[...]) + d_ref[...] * u
    for j in range(SSM_SLABS):
        y_ref[j] = y[:, j * LANES:(j + 1) * LANES]


def _s5_step_call(u_slab, h0_re, h0_im, sw):
    J, n_tok, _ = u_slab.shape
    ns = h0_re.shape[1]
    args = (u_slab, sw["w_x"], sw["a_re"], sw["a_im"], h0_re, h0_im, sw["w_y"], sw["d"])
    full = lambda a: pl.BlockSpec(a.shape, lambda i: (0,) * a.ndim)
    st_sds = jax.ShapeDtypeStruct((n_tok, ns), F32)
    return pl.pallas_call(
        _s5_step_body,
        out_shape=(jax.ShapeDtypeStruct(u_slab.shape, F32), st_sds, st_sds),
        grid=(1,),
        in_specs=[full(a) for a in args],
        out_specs=(full(u_slab), full(h0_re), full(h0_re)),
        compiler_params=_cparams(("arbitrary",)),
        name="s5_step",
    )(*args)


def _compress_sample_body(pt_ref, *refs, n_pages, page_rows, nch):
    del pt_ref
    page_refs = refs[:n_pages]
    pe_ref, wa_ref, wb_ref, b1_ref, w2_ref, kc_ref, vc_ref, krows_ref, vrows_ref = refs[n_pages:]
    b = pl.program_id(0)

    @pl.when(b == 0)
    def _():
        krows_ref[...] = jnp.zeros(krows_ref.shape, F32)
        vrows_ref[...] = jnp.zeros(vrows_ref.shape, F32)

    stage = b % 2
    for i, page_ref in enumerate(page_refs):
        krows_ref[stage, i * page_rows:(i + 1) * page_rows, :] = page_ref[0].reshape(LANES, page_rows).T
        vrows_ref[stage, i * page_rows:(i + 1) * page_rows, :] = page_ref[1].reshape(LANES, page_rows).T
    done = 1 - stage
    _compress_rows((krows_ref.at[done], vrows_ref.at[done]), pe_ref, wa_ref, wb_ref, b1_ref, w2_ref,
                   kc_ref, vc_ref, nch)


def _compress_sample_call(cache, page_table, cw):
    n_seq, n_pages = page_table.shape
    page_rows = cache.shape[-1]
    assert page_rows == LANES
    nch = n_pages * page_rows // CMP_STRIDE
    staged = lambda b: jnp.minimum(b, n_seq - 1)
    page_spec = lambda i: pl.BlockSpec((None, 2, KV_HEADS, HEAD_DIM, page_rows),
                                       lambda b, pt: (pt[staged(b), i], 0, 0, 0, 0))
    full = lambda a: pl.BlockSpec(a.shape, lambda b, pt: (0,) * a.ndim)
    out_of = lambda b: jnp.maximum(b - 1, 0)
    out_spec = pl.BlockSpec((KV_HEADS, nch, LANES), lambda b, pt: (0, out_of(b), 0))
    out_sds = jax.ShapeDtypeStruct((KV_HEADS, n_seq * nch, LANES), BF16)
    out_t_spec = pl.BlockSpec((KV_HEADS, None, LANES, nch), lambda b, pt: (0, out_of(b), 0, 0))
    out_t_sds = jax.ShapeDtypeStruct((KV_HEADS, n_seq, LANES, nch), BF16)
    weights = (cw["pe"], cw["wa"], cw["wb"], cw["b1"], cw["w2"])
    return pl.pallas_call(
        functools.partial(_compress_sample_body, n_pages=n_pages, page_rows=page_rows, nch=nch),
        out_shape=(out_sds, out_t_sds),
        grid_spec=pltpu.PrefetchScalarGridSpec(
            num_scalar_prefetch=1,
            grid=(n_seq + 1,),
            in_specs=[page_spec(i) for i in range(n_pages)] + [full(a) for a in weights],
            out_specs=(out_spec, out_t_spec),
            scratch_shapes=[pltpu.VMEM((2, n_pages * page_rows, LANES), F32)] * 2),
        compiler_params=_cparams(("arbitrary",)),
        name="compress_sample",
    )(page_table, *([cache] * n_pages), *weights)


def _group_rows(x0, x1):
    row = lax.broadcasted_iota(I32, x0.shape, 0)
    return jnp.where(row < Q_PER_KV, x0, x1)


def _sample_select_body(q_ref, kc_ref, vct_ref, ovl_ref, tri_ref, ocmp_ref, idx_ref, *, ncp, qpos, nbp):
    q8 = q_ref[...]
    c_i = lax.broadcasted_iota(I32, (N_HEADS, ncp), 1)
    cmask = c_i * CMP_STRIDE + (CMP_BLOCK - 1) <= qpos
    s = _group_rows(_dot_nt(q8, kc_ref[0]), _dot_nt(q8, kc_ref[1]))
    s = jnp.where(cmask, s, NEG_INF)
    e = jnp.where(cmask, jnp.exp2(s - jnp.max(s, axis=-1, keepdims=True)), 0.0)
    l = jnp.sum(e, axis=-1, keepdims=True)
    p = (e / jnp.where(l > 0.0, l, 1.0)).astype(BF16)
    ocmp_ref[...] = _group_rows(_dot_nt(p, vct_ref[0]), _dot_nt(p, vct_ref[1]))
    imp8 = _dot(p, ovl_ref[...])

    n_row = lax.broadcasted_iota(I32, (1, nbp), 1)
    qblk = qpos // SLC_BLOCK
    causal = n_row <= qblk
    forced = (n_row == 0) | (n_row >= qblk - (N_LOCAL_BLOCKS - 1))
    m_i = lax.broadcasted_iota(I32, (nbp, nbp), 0)
    n_i = lax.broadcasted_iota(I32, (nbp, nbp), 1)
    lane = lax.broadcasted_iota(I32, (1, LANES), 1)
    idx_rows = []
    for g in range(KV_HEADS):
        imp = jnp.sum(imp8[g * Q_PER_KV:(g + 1) * Q_PER_KV], axis=0, keepdims=True)
        w = jnp.where(causal, jnp.where(forced, jnp.inf, imp), -jnp.inf)
        w_sq = jnp.broadcast_to(w, (nbp, nbp))
        w_col = w_sq.T
        beats = jnp.where(n_i > m_i, jnp.where(w_col >= w_sq, 1.0, 0.0), jnp.where(w_col > w_sq, 1.0, 0.0))
        rank = jnp.sum(beats, axis=0, keepdims=True)
        sel = causal & (rank < TOP_N)
        self_f = jnp.where(sel, 1.0, 0.0)
        before = _dot(self_f.astype(BF16), tri_ref[...])
        idx = jnp.full((1, LANES), -1, I32)
        for k in range(TOP_N):
            hit = sel & (before == float(k))
            val = jnp.sum(jnp.where(hit, n_row.astype(F32) + 1.0, 0.0), axis=-1, keepdims=True) - 1.0
            idx = jnp.where(lane == k, val.astype(I32), idx)
        idx_rows.append(idx)
    idx_ref[...] = jnp.concatenate(idx_rows + [jnp.full((SUBLANES - KV_HEADS, LANES), -1, I32)], axis=0)


def _sample_select_call(q8, kc, vc_t, ovl, tri, qpos):
    n_seq = q8.shape[0]
    ncp = kc.shape[1] // n_seq
    nbp = ovl.shape[1]
    cmp_spec = pl.BlockSpec((KV_HEADS, ncp, LANES), lambda b: (0, b, 0))
    row_spec = pl.BlockSpec((None, N_HEADS, LANES), lambda b: (b, 0, 0))
    full = lambda a: pl.BlockSpec(a.shape, lambda b: (0,) * a.ndim)
    return pl.pallas_call(
        functools.partial(_sample_select_body, ncp=ncp, qpos=qpos, nbp=nbp),
        out_shape=(jax.ShapeDtypeStruct((n_seq, N_HEADS, LANES), F32),
                   jax.ShapeDtypeStruct((n_seq, SUBLANES, LANES), I32)),
        grid=(n_seq,),
        in_specs=[row_spec, cmp_spec, pl.BlockSpec((KV_HEADS, None, LANES, ncp), lambda b: (0, b, 0, 0)),
                  full(ovl), full(tri)],
        out_specs=(row_spec, pl.BlockSpec((None, SUBLANES, LANES), lambda b: (b, 0, 0))),
        compiler_params=_cparams(("parallel",)),
        name="nsa_sample_select",
    )(q8, kc, vc_t, ovl, tri)


def _sample_attend_body(idx_ref, pt_ref, q_ref, ocmp_ref, gate_ref, ksn_ref, vsn_ref, kwn_ref, vwn_ref,
                        win_ref, *refs, n_cache_blocks, blocks_per_page, win_skip):
    del pt_ref
    n_blk = KV_HEADS * TOP_N
    k_refs, v_refs, o_ref = refs[:n_blk], refs[n_blk:2 * n_blk], refs[2 * n_blk]
    b = pl.program_id(0)
    q = q_ref[...]
    qf = q.astype(F32)
    row_g = (lax.broadcasted_iota(I32, (N_HEADS, 1), 0) >= Q_PER_KV).astype(I32)

    def attend(s_list, v_list, kn_ref, vn_ref):
        s_self = jnp.sum(qf * kn_ref[...].astype(F32), axis=-1, keepdims=True)
        m = s_self
        for s in s_list:
            m = jnp.maximum(m, jnp.max(s, axis=-1, keepdims=True))
        p_self = jnp.exp2(s_self - m)
        l = p_self
        acc = p_self.astype(BF16).astype(F32) * vn_ref[...].astype(F32)
        for s, v in zip(s_list, v_list):
            p = jnp.exp2(s - m)
            l = l + jnp.sum(p, axis=-1, keepdims=True)
            acc = acc + _dot_nt(p.astype(BF16), v().astype(BF16))
        return acc / l

    s_list, v_list = [], []
    for j in range(n_blk):
        s = _dot(q, k_refs[j][...].astype(BF16))
        col = lax.broadcasted_iota(I32, s.shape, 1)
        n = idx_ref[b, j]
        first = (n % blocks_per_page) * SLC_BLOCK
        ok = ((row_g == j // TOP_N) & (col >= first) & (col < first + SLC_BLOCK)
              & (n >= 0) & (n < n_cache_blocks))
        s_list.append(jnp.where(ok, s, NEG_INF))
        v_list.append(lambda j=j: v_refs[j][...])
    o_slc = attend(s_list, v_list, ksn_ref, vsn_ref)

    s_list, v_list = [], []
    for g in range(KV_HEADS):
        s = _dot(q, win_ref[0, g].astype(BF16))
        col = lax.broadcasted_iota(I32, s.shape, 1)
        s_list.append(jnp.where((row_g == g) & (col >= win_skip), s, NEG_INF))
        v_list.append(lambda g=g: win_ref[1, g])
    o_win = attend(s_list, v_list, kwn_ref, vwn_ref)

    gates = gate_ref[...]
    o_ref[...] = gates[:, 0:1] * ocmp_ref[:, 0:HEAD_DIM] + gates[:, 1:2] * o_slc + gates[:, 2:3] * o_win


def _sample_attend_call(idx, page_table, q64, ocmp, gates8, ksn, vsn, kwn, vwn, cache_win_t, cache_t, win_skip):
    n_seq, n_pages = page_table.shape
    page_rows = cache_t.shape[-1]
    bpp = page_rows // SLC_BLOCK
    n_cache_blocks = n_pages * bpp
    row_spec = lambda a: pl.BlockSpec((None,) + a.shape[1:], lambda b, ix, pt: (b,) + (0,) * (a.ndim - 1))

    def blk_spec(j, slot):
        def index_map(b, ix, pt):
            n = jnp.clip(ix[b, j], 0, n_cache_blocks - 1)
            return (pt[b, n // bpp], slot, j // TOP_N, 0, 0)
        return pl.BlockSpec((None, None, None, HEAD_DIM, page_rows), index_map)

    n_blk = KV_HEADS * TOP_N
    small = (q64, ocmp, gates8, ksn, vsn, kwn, vwn, cache_win_t)
    return pl.pallas_call(
        functools.partial(_sample_attend_body, n_cache_blocks=n_cache_blocks, blocks_per_page=bpp,
                          win_skip=win_skip),
        out_shape=jax.ShapeDtypeStruct((n_seq, N_HEADS, HEAD_DIM), F32),
        grid_spec=pltpu.PrefetchScalarGridSpec(
            num_scalar_prefetch=2,
            grid=(n_seq,),
            in_specs=([row_spec(a) for a in small] + [blk_spec(j, 2) for j in range(n_blk)]
                      + [blk_spec(j, 3) for j in range(n_blk)]),
            out_specs=pl.BlockSpec((None, N_HEADS, HEAD_DIM), lambda b, ix, pt: (b, 0, 0))),
        compiler_params=_cparams(("parallel",)),
        name="nsa_sample_attend",
    )(idx, page_table, *small, *([cache_t] * (2 * n_blk)))


def _round_up(x, m):
    return -(-x // m) * m


def _prompt_layer(h, p, l, batch, seq):
    prm = _prep_params(p, l)
    tabs = _rope_tables(jnp.arange(seq))
    u, gs, gn, q_t, kv_t, win_t, gates_t, ks, vs_t, kw, vw_t = _proj_call(
        h, prm["w_in"], prm["norm_w"], prm["qnw"], prm["knw"], prm["gb"], tabs, batch, seq)
    y_ssm, h_re, h_im = _s5_prompt_call(u, _prep_s5_chunked(p, l), batch, seq)
    kc, vc_t = _compress_prompt_call(kv_t, _prep_compress(p, l), batch, seq)
    nch = seq // CMP_STRIDE
    ovl_t = _overlap_matrix(nch, nch - 1, seq // SLC_BLOCK).T
    o = _attn_call(q_t, kc, vc_t, ovl_t, ks, vs_t, kw, vw_t, gates_t, batch, seq)
    h_new = _outmix_call(h, y_ssm, gs, o, gn, prm["w_glu"], prm["w_out"])
    rows = lambda x_t, slots: x_t.reshape(batch, slots, KV_HEADS, HEAD_DIM, seq).transpose(0, 4, 1, 2, 3)
    return h_new, rows(kv_t, 4), rows(win_t, 2)[:, seq - min(WINDOW, seq):], h_re, h_im


def _sample_layer(h, p, l, cache_kv, cache_win, st_re, st_im, page_table):
    n_seq = h.shape[0]
    n_phys, page_rows = cache_kv.shape[:2]
    n_pages = page_table.shape[1]
    past_len = n_pages * page_rows
    win_buf = cache_win.shape[1]
    prm = _prep_params(p, l)
    n_pad = _round_up(n_seq, LANES)
    tabs = _rope_tables(jnp.full((n_pad,), past_len, I32))
    h_pad = jnp.pad(h, ((0, n_pad - n_seq), (0, 0)))
    u, gs, gn, q_t, kv_t, win_t, gates_t, ks, vs_t, kw, vw_t = _proj_call(
        h_pad, prm["w_in"], prm["norm_w"], prm["qnw"], prm["knw"], prm["gb"], tabs, 1, n_pad)
    u, gs, gn = u[:, :n_seq], gs[:n_seq], gn[:n_seq]
    n_state = st_re.shape[1] * st_re.shape[2]
    y_ssm, h_re, h_im = _s5_step_call(u, st_re.reshape(n_seq, n_state), st_im.reshape(n_seq, n_state),
                                      _prep_s5_step(p, l))
    cache_t = cache_kv.transpose(0, 2, 3, 4, 1)
    kc, vc_t = _compress_sample_call(cache_t, page_table, _prep_compress(p, l))
    ncp = past_len // CMP_STRIDE
    n_blk = -(-(past_len + 1) // SLC_BLOCK)
    nbp = _round_up(n_blk, LANES)
    ovl = _overlap_matrix(ncp, ncp - 1, n_blk, nbp)
    tri = (jnp.arange(nbp)[:, None] < jnp.arange(nbp)[None, :]).astype(BF16)
    q8 = q_t[:, :, :n_seq].transpose(2, 0, 1)
    ocmp, idx = _sample_select_call(q8, kc, vc_t, ovl, tri, past_len)
    idx = idx[:, :KV_HEADS, :TOP_N].reshape(n_seq, KV_HEADS * TOP_N)
    gates8 = gates_t[:, :3 * Q_PER_KV, :n_seq].reshape(KV_HEADS, Q_PER_KV, 3, n_seq).transpose(3, 0, 1, 2)
    gates8 = jnp.pad(gates8.reshape(n_seq, N_HEADS, 3), ((0, 0), (0, 0), (0, LANES - 3)))
    per_head = lambda a: jnp.repeat(a.transpose(1, 0, 2), Q_PER_KV, axis=1)
    new_k = lambda k: per_head(k[:, :n_seq, HEAD_DIM:])
    new_v = lambda v_t: per_head(v_t[:, 0, :HEAD_DIM, :n_seq].transpose(0, 2, 1))
    o8 = _sample_attend_call(idx, page_table, q8[:, :, HEAD_DIM:], ocmp, gates8, new_k(ks), new_v(vs_t),
                             new_k(kw), new_v(vw_t), cache_win.transpose(0, 2, 3, 4, 1), cache_t,
                             max(win_buf + 1 - WINDOW, 0))
    o = o8.reshape(n_seq, NSA_W)
    h_new = _outmix_call(h, y_ssm, gs, o, gn, prm["w_glu"], prm["w_out"])
    kv_rows = kv_t[0, :, :n_seq].T.reshape(n_seq, 1, 4, KV_HEADS, HEAD_DIM)
    win_new = win_t[0, :, :n_seq].T.reshape(n_seq, 1, 2, KV_HEADS, HEAD_DIM)
    wrows = jnp.concatenate([cache_win, win_new], axis=1)
    wrows = wrows[:, wrows.shape[1] - min(WINDOW, wrows.shape[1]):]
    state = lambda s: s.reshape(st_re.shape)
    return h_new, kv_rows, wrows, state(h_re), state(h_im)


def kernel(x_prompt, x_sample, cache_kv, cache_win, state_ssm_re, state_ssm_im, page_table, norm_w, w_in, gate_b,
           q_norm_w, k_norm_w, cmp_pe, cmp_w1, cmp_b1, cmp_w2, ssm_lam_re, ssm_lam_im, ssm_log_step, ssm_b_re,
           ssm_b_im, ssm_c_re, ssm_c_im, ssm_d, w_glu, w_out):
    p = dict(norm_w=norm_w, w_in=w_in, gate_b=gate_b, q_norm_w=q_norm_w, k_norm_w=k_norm_w, cmp_pe=cmp_pe,
             cmp_w1=cmp_w1, cmp_b1=cmp_b1, cmp_w2=cmp_w2, ssm_lam_re=ssm_lam_re, ssm_lam_im=ssm_lam_im,
             ssm_log_step=ssm_log_step, ssm_b_re=ssm_b_re, ssm_b_im=ssm_b_im, ssm_c_re=ssm_c_re,
             ssm_c_im=ssm_c_im, ssm_d=ssm_d, w_glu=w_glu, w_out=w_out)
    b_p, s_p, d_model = x_prompt.shape
    b_s, s_s, _ = x_sample.shape
    assert s_s == 1, "the sample group decodes one token per sequence"
    h_p = x_prompt.reshape(b_p * s_p, d_model)
    h_s = x_sample.reshape(b_s, d_model)
    outs_p, outs_s = [], []
    for l in range(norm_w.shape[0]):
        h_p, *rest_p = _prompt_layer(h_p, p, l, b_p, s_p)
        h_s, *rest_s = _sample_layer(h_s, p, l, cache_kv[l], cache_win[l], state_ssm_re[l], state_ssm_im[l],
                                     page_table)
        outs_p.append(rest_p)
        outs_s.append(rest_s)
    stack = lambda outs, i: jnp.stack([o[i] for o in outs])
    return (h_p.reshape(x_prompt.shape), h_s.reshape(x_sample.shape),
            stack(outs_p, 0), stack(outs_s, 0), stack(outs_p, 1), stack(outs_s, 1),
            stack(outs_p, 2), stack(outs_p, 3), stack(outs_s, 2), stack(outs_s, 3))
```

```python
import functools
import math

import jax
import jax.numpy as jnp
from jax import lax
from jax.experimental import pallas as pl
from jax.experimental.pallas import tpu as pltpu

F32 = jnp.float32
BF16 = jnp.bfloat16
I32 = jnp.int32

LANES = 128
SUBLANES = 8
VMEM_LIMIT_BYTES = 56 * 1024 * 1024

HEAD_DIM = 64
N_HEADS = 8
KV_HEADS = 2
Q_PER_KV = N_HEADS // KV_HEADS
SSM_W = 512
SSM_GROUP = 16
SSM_STATE = 64
NSA_W = N_HEADS * HEAD_DIM
CMP_BLOCK = 32
CMP_STRIDE = 16
CMP_HID = 2 * HEAD_DIM
SLC_BLOCK = 64
TOP_N = 16
N_LOCAL_BLOCKS = 2
WINDOW = 512
ROPE_THETA = 500000.0
ROPE_DIM = HEAD_DIM // 4
RMS_EPS = 1e-6
NEG_INF = -1e30

COL_U = 0
COL_GS = SSM_W
COL_Q = 2 * SSM_W
COL_GN = 2 * SSM_W + NSA_W
COL_KV = 2 * SSM_W + 2 * NSA_W
COL_GL = COL_KV + 6 * KV_HEADS * HEAD_DIM
IN_W_PAD = COL_GL + LANES

PROJ_TILE = 512
ATTN_TILE = 256
SSM_CHUNK = 16
SSM_SLAB_GROUPS = LANES // SSM_GROUP
SSM_SLABS = SSM_W // LANES


def _cparams(sem):
    return pltpu.CompilerParams(dimension_semantics=sem, vmem_limit_bytes=VMEM_LIMIT_BYTES)


def _sigmoid(x):
    return 1.0 / (1.0 + jnp.exp(-x))


def _dot(a, b):
    return jnp.dot(a, b, preferred_element_type=F32)


def _dot_nt(a, b):
    return lax.dot_general(a, b, (((1,), (1,)), ((), ())), preferred_element_type=F32)


def _proj_body(x_ref, nw_ref, w_ref, qnw_ref, knw_ref, gb_ref, ra_ref, rb_ref, rc_ref,
               u_ref, gs_ref, gn_ref, qt_ref, kvt_ref, wint_ref, gt_ref,
               ks_ref, vst_ref, kw_ref, vwt_ref, *, tm, tv, tiles_per_seq):
    x = x_ref[...]
    ms = jnp.mean(x * x, axis=-1, keepdims=True)
    h = (x * lax.rsqrt(ms + RMS_EPS) * nw_ref[...]).astype(BF16)

    def mm(c0, c1):
        return _dot(h, w_ref[:, c0:c1])

    lane = lax.broadcasted_iota(I32, (tm, LANES), 1)
    lo = lane < HEAD_DIM
    ra = ra_ref[...]
    rb = rb_ref[...]
    rc = rc_ref[...]

    def norm_rope(s, wrow):
        s2 = s * s
        slo = jnp.sum(jnp.where(lo, s2, 0.0), axis=-1, keepdims=True)
        shi = jnp.sum(jnp.where(lo, 0.0, s2), axis=-1, keepdims=True)
        msq = jnp.where(lo, slo, shi) * (1.0 / HEAD_DIM)
        y = s * lax.rsqrt(msq + RMS_EPS) * wrow
        half = ROPE_DIM // 2
        return y * ra + pltpu.roll(y, LANES - half, 1) * rb + pltpu.roll(y, half, 1) * rc

    def hi_half(y, head):
        src = pltpu.roll(y, HEAD_DIM, 1) if head == 0 else y
        return jnp.where(lo, 0.0, src)

    zq = mm(COL_Q, COL_GN)
    zkv = mm(COL_KV, COL_GL)
    zgl = mm(COL_GL, IN_W_PAD)

    qnw = qnw_ref[...]
    scale = HEAD_DIM ** -0.5 * math.log2(math.e)
    zeros_t = jnp.zeros((HEAD_DIM, tm), F32)
    for j in range(N_HEADS // 2):
        y_t = (norm_rope(zq[:, j * LANES:(j + 1) * LANES], qnw) * scale).T
        for head in range(2):
            q_t = jnp.concatenate([zeros_t, y_t[head * HEAD_DIM:(head + 1) * HEAD_DIM]], axis=0)
            qt_ref[2 * j + head] = q_t.astype(BF16)

    kc = norm_rope(zkv[:, 0:LANES], knw_ref[0:1, :])
    vc = zkv[:, LANES:2 * LANES]
    ks = norm_rope(zkv[:, 2 * LANES:3 * LANES], knw_ref[1:2, :])
    vs = zkv[:, 3 * LANES:4 * LANES]
    kw = norm_rope(zkv[:, 4 * LANES:5 * LANES], knw_ref[2:3, :])
    vw = zkv[:, 5 * LANES:6 * LANES]
    vs_t, vw_t = vs.T, vw.T
    for i, rows_t in enumerate((kc.T, vc.T, ks.T, vs_t)):
        kvt_ref[i * LANES:(i + 1) * LANES, :] = rows_t
    for i, rows_t in enumerate((kw.T, vw_t)):
        wint_ref[i * LANES:(i + 1) * LANES, :] = rows_t

    row = lax.broadcasted_iota(I32, (tm, LANES), 0)
    pos = (pl.program_id(0) % tiles_per_seq) * tm + row
    onehot = jnp.where(lane == lax.shift_right_logical(pos, 6), 1.0, 0.0)
    ones_t = jnp.where(lax.broadcasted_iota(I32, (HEAD_DIM, tm), 0) == 0, 1.0, 0.0)
    for g in range(KV_HEADS):
        ks_ref[g] = jnp.where(lo, onehot, hi_half(ks, g)).astype(BF16)
        kw_ref[g] = hi_half(kw, g).astype(BF16)
        for v_t, vt_ref in ((vs_t, vst_ref), (vw_t, vwt_ref)):
            v_aug = jnp.concatenate([v_t[g * HEAD_DIM:(g + 1) * HEAD_DIM], ones_t], axis=0).astype(BF16)
            for t in range(tm // tv):
                vt_ref[g, t] = v_aug[:, t * tv:(t + 1) * tv]

    gates_t = _sigmoid(zgl + gb_ref[...]).T
    for g in range(KV_HEADS):
        gt_ref[g] = gates_t[g * 3 * Q_PER_KV:g * 3 * Q_PER_KV + 2 * SUBLANES]

    zu = mm(COL_U, COL_GS)
    for j in range(SSM_SLABS):
        u_ref[j] = zu[:, j * LANES:(j + 1) * LANES]
    gs_ref[...] = mm(COL_GS, COL_Q)
    gn_ref[...] = mm(COL_GN, COL_KV)


def _proj_call(x2d, w_pad, norm_w, qnw, knw, gb, tabs, batch, seq):
    T, D = x2d.shape
    tm = min(PROJ_TILE, seq)
    tv = min(ATTN_TILE, seq)
    assert T == batch * seq and seq % tm == 0 and tm % tv == 0 and tv % LANES == 0
    tps = seq // tm
    row_spec = lambda w: pl.BlockSpec((tm, w), lambda i: (i, 0))
    full = lambda a: pl.BlockSpec(a.shape, lambda i: (0,) * a.ndim)
    tab_spec = pl.BlockSpec((tm, LANES), lambda i: (i % tps, 0))
    head_spec = lambda n: pl.BlockSpec((n, tm, LANES), lambda i: (0, i, 0))
    head_t_spec = lambda n, rows: pl.BlockSpec((n, rows, tm), lambda i: (0, 0, i))
    cache_t_spec = lambda rows: pl.BlockSpec((None, rows, tm), lambda i: (i // tps, 0, i % tps))
    tile_t_spec = pl.BlockSpec((KV_HEADS, tm // tv, LANES, tv), lambda i: (0, i, 0, 0))
    tile_t_sds = jax.ShapeDtypeStruct((KV_HEADS, T // tv, LANES, tv), BF16)
    out_shape = (
        jax.ShapeDtypeStruct((SSM_SLABS, T, LANES), F32),
        jax.ShapeDtypeStruct((T, SSM_W), F32),
        jax.ShapeDtypeStruct((T, NSA_W), F32),
        jax.ShapeDtypeStruct((N_HEADS, LANES, T), BF16),
        jax.ShapeDtypeStruct((batch, 4 * LANES, seq), F32),
        jax.ShapeDtypeStruct((batch, 2 * LANES, seq), F32),
        jax.ShapeDtypeStruct((KV_HEADS, 2 * SUBLANES, T), F32),
        jax.ShapeDtypeStruct((KV_HEADS, T, LANES), BF16),
        tile_t_sds,
        jax.ShapeDtypeStruct((KV_HEADS, T, LANES), BF16),
        tile_t_sds,
    )
    out_specs = (head_spec(SSM_SLABS), row_spec(SSM_W), row_spec(NSA_W), head_t_spec(N_HEADS, LANES),
                 cache_t_spec(4 * LANES), cache_t_spec(2 * LANES), head_t_spec(KV_HEADS, 2 * SUBLANES),
                 head_spec(KV_HEADS), tile_t_spec, head_spec(KV_HEADS), tile_t_spec)
    return pl.pallas_call(
        functools.partial(_proj_body, tm=tm, tv=tv, tiles_per_seq=tps),
        out_shape=out_shape,
        grid=(T // tm,),
        in_specs=[row_spec(D), full(norm_w), full(w_pad), full(qnw), full(knw), full(gb),
                  tab_spec, tab_spec, tab_spec],
        out_specs=out_specs,
        compiler_params=_cparams(("parallel",)),
        name="proj",
    )(x2d, norm_w, w_pad, qnw, knw, gb, *tabs)


def _prep_params(p, l):
    w_in = p["w_in"][l]
    d_model, in_w = w_in.shape
    tile2 = lambda v: jnp.tile(v, (1, LANES // HEAD_DIM))
    return {
        "w_in": jnp.pad(w_in.astype(BF16), ((0, 0), (0, IN_W_PAD - in_w))),
        "norm_w": p["norm_w"][l].reshape(1, d_model).astype(F32),
        "qnw": tile2(p["q_norm_w"][l].reshape(1, HEAD_DIM)).astype(F32),
        "knw": tile2(p["k_norm_w"][l]).astype(F32),
        "gb": jnp.pad(p["gate_b"][l].reshape(1, -1).astype(F32), ((0, 0), (0, LANES - 3 * N_HEADS))),
        "w_glu": p["w_glu"][l].astype(BF16),
        "w_out": p["w_out"][l].astype(BF16),
    }


def _rope_tables(pos):
    half = ROPE_DIM // 2
    inv = ROPE_THETA ** (-jnp.arange(half, dtype=F32) / half)
    ang = pos.astype(F32)[:, None] * inv
    cos, sin = jnp.cos(ang), jnp.sin(ang)
    n = pos.shape[0]
    rest = HEAD_DIM - ROPE_DIM
    a = jnp.concatenate([cos, cos, jnp.ones((n, rest), F32)], axis=-1)
    b = jnp.concatenate([-sin, jnp.zeros((n, HEAD_DIM - half), F32)], axis=-1)
    c = jnp.concatenate([jnp.zeros((n, half), F32), sin, jnp.zeros((n, rest), F32)], axis=-1)
    return tuple(jnp.tile(t, (1, LANES // HEAD_DIM)) for t in (a, b, c))


def _outmix_body(x_ref, y_ref, gs_ref, o_ref, gn_ref, wg_ref, wo_ref, out_ref):
    y = jnp.concatenate([y_ref[j] for j in range(SSM_SLABS)], axis=-1)
    ab = _dot(y.astype(BF16), wg_ref[...])
    gs = gs_ref[...]
    ssm = ab[:, :SSM_W] * _sigmoid(ab[:, SSM_W:]) * (gs * _sigmoid(gs))
    gn = gn_ref[...]
    nsa = o_ref[...] * (gn * _sigmoid(gn))
    acc = _dot(ssm.astype(BF16), wo_ref[0:SSM_W, :])
    acc += _dot(nsa.astype(BF16), wo_ref[SSM_W:, :])
    out_ref[...] = x_ref[...] + acc


def _outmix_call(x2d, y_ssm, g_ssm, o_nsa, g_nsa, w_glu, w_out):
    T, D = x2d.shape
    tm = min(512, T)
    row_spec = lambda w: pl.BlockSpec((tm, w), lambda i: (i, 0))
    full = lambda a: pl.BlockSpec(a.shape, lambda i: (0,) * a.ndim)
    return pl.pallas_call(
        _outmix_body,
        out_shape=jax.ShapeDtypeStruct((T, D), F32),
        grid=(T // tm,),
        in_specs=[row_spec(D), pl.BlockSpec((SSM_SLABS, tm, LANES), lambda i: (0, i, 0)),
                  row_spec(SSM_W), row_spec(NSA_W), row_spec(NSA_W),
                  full(w_glu), full(w_out)],
        out_specs=row_spec(D),
        compiler_params=_cparams(("parallel",)),
        name="outmix",
    )(x2d, y_ssm, g_ssm, o_nsa, g_nsa, w_glu, w_out)


def _gelu_tanh(x):
    c = math.sqrt(2.0 / math.pi)
    return 0.5 * x * (1.0 + jnp.tanh(c * (x + 0.044715 * (x * x * x))))


def _compress_rows(rows_refs, pe_ref, wa_ref, wb_ref, b1_ref, w2_ref, kc_ref, vc_ref, nch):
    lane = lax.broadcasted_iota(I32, (nch, LANES), 1)
    for kvi, out_ref in ((0, kc_ref), (1, vc_ref)):
        rows_ref = rows_refs[kvi]
        pa = jnp.zeros((nch, 2 * CMP_HID), F32)
        pb = jnp.zeros((nch, 2 * CMP_HID), F32)
        for j0 in range(0, CMP_STRIDE, 2):
            xs = [rows_ref[pl.ds(j, nch, stride=CMP_STRIDE), :] for j in (j0, j0 + 1)]
            xa = jnp.concatenate([xs[i] + pe_ref[kvi, 0, j0 + i:j0 + i + 1, :] for i in range(2)], axis=-1)
            xb = jnp.concatenate([xs[i] + pe_ref[kvi, 1, j0 + i:j0 + i + 1, :] for i in range(2)], axis=-1)
            wsl = slice(j0 * LANES, (j0 + 2) * LANES)
            pa += _dot(xa.astype(BF16), wa_ref[kvi, wsl, :])
            pb += _dot(xb.astype(BF16), wb_ref[kvi, wsl, :])
        hid = _gelu_tanh(pa + pltpu.roll(pb, nch - 1, 0) + b1_ref[kvi]).astype(BF16)
        for g in range(KV_HEADS):
            o = _dot(hid, w2_ref[kvi, g])
            if kvi == 1:
                o = jnp.where(lane == HEAD_DIM, 1.0, o).T
            out_ref[g] = o.astype(BF16)


def _compress_prompt_body(kvt_ref, pe_ref, wa_ref, wb_ref, b1_ref, w2_ref, kc_ref, vc_ref,
                          krows_ref, vrows_ref, *, nch):
    for c in range(kvt_ref.shape[1] // LANES):
        cs = slice(c * LANES, (c + 1) * LANES)
        krows_ref[cs, :] = kvt_ref[0:LANES, cs].T
        vrows_ref[cs, :] = kvt_ref[LANES:2 * LANES, cs].T
    _compress_rows((krows_ref, vrows_ref), pe_ref, wa_ref, wb_ref, b1_ref, w2_ref, kc_ref, vc_ref, nch)


def _compress_prompt_call(kv_t, cw, batch, seq):
    nch = seq // CMP_STRIDE
    full = lambda a: pl.BlockSpec(a.shape, lambda b: (0,) * a.ndim)
    out_spec = pl.BlockSpec((KV_HEADS, nch, LANES), lambda b: (0, b, 0))
    out_sds = jax.ShapeDtypeStruct((KV_HEADS, batch * nch, LANES), BF16)
    out_t_spec = pl.BlockSpec((KV_HEADS, None, LANES, nch), lambda b: (0, b, 0, 0))
    out_t_sds = jax.ShapeDtypeStruct((KV_HEADS, batch, LANES, nch), BF16)
    return pl.pallas_call(
        functools.partial(_compress_prompt_body, nch=nch),
        out_shape=(out_sds, out_t_sds),
        grid=(batch,),
        in_specs=[pl.BlockSpec((None, 2 * LANES, seq), lambda b: (b, 0, 0)),
                  full(cw["pe"]), full(cw["wa"]), full(cw["wb"]), full(cw["b1"]), full(cw["w2"])],
        out_specs=(out_spec, out_t_spec),
        scratch_shapes=[pltpu.VMEM((seq, LANES), F32)] * 2,
        compiler_params=_cparams(("parallel",)),
        name="compress_prompt",
    )(kv_t, cw["pe"], cw["wa"], cw["wb"], cw["b1"], cw["w2"])


def _prep_compress(p, l):
    eye = jnp.eye(KV_HEADS, dtype=F32)
    w1 = p["cmp_w1"][l].reshape(2, 2, CMP_STRIDE, HEAD_DIM, CMP_HID)
    wexp = jnp.einsum("khjdn,ge->khjgden", w1, eye).reshape(2, 2, CMP_STRIDE * LANES, KV_HEADS * CMP_HID)
    pe = p["cmp_pe"][l].reshape(2, 2, CMP_STRIDE, HEAD_DIM)
    w2 = p["cmp_w2"][l]
    zeros = jnp.zeros_like(w2[0])
    w2k = jnp.concatenate([zeros, w2[0]], axis=-1)
    w2v = jnp.concatenate([w2[1], zeros], axis=-1)
    w2e = jnp.stack([jnp.einsum("hd,ge->gehd", w, eye).reshape(KV_HEADS, KV_HEADS * CMP_HID, LANES)
                     for w in (w2k, w2v)])
    return {
        "pe": jnp.tile(pe, (1, 1, 1, KV_HEADS)).astype(F32),
        "wa": wexp[:, 0].astype(BF16),
        "wb": wexp[:, 1].astype(BF16),
        "b1": jnp.tile(p["cmp_b1"][l].reshape(2, 1, CMP_HID), (1, 1, KV_HEADS)).astype(F32),
        "w2": w2e.astype(BF16),
    }


def _overlap_matrix(n_tok_pad, n_tok, n_blk, n_cols=LANES):
    c_start = jnp.arange(n_tok_pad) * CMP_STRIDE
    blk = jnp.arange(n_cols)
    ov = ((c_start[:, None] < (blk[None, :] + 1) * SLC_BLOCK)
          & (c_start[:, None] + CMP_BLOCK > blk[None, :] * SLC_BLOCK)
          & (jnp.arange(n_tok_pad)[:, None] < n_tok) & (blk[None, :] < n_blk))
    return ov.astype(BF16)


def _topk_select_t(w_ref, imp_t, q0, tq):
    nb = imp_t.shape[0]
    n_i = lax.broadcasted_iota(I32, (nb, tq), 0)
    qblk = lax.shift_right_logical(q0 + lax.broadcasted_iota(I32, (nb, tq), 1), 6)
    causal = n_i <= qblk
    forced = (n_i == 0) | (n_i >= qblk - (N_LOCAL_BLOCKS - 1))
    w_ref[...] = jnp.where(causal, jnp.where(forced, jnp.inf, imp_t), -jnp.inf)
    last_blk = lax.shift_right_logical(q0 + tq - 1, 6)
    n_grp = nb // SUBLANES
    rank = [jnp.zeros((SUBLANES, tq), F32) for _ in range(n_grp)]
    grp_i = lax.broadcasted_iota(I32, (SUBLANES, tq), 0)

    def count_group(mg, rank):
        rank = list(rank)
        for mi in range(SUBLANES):
            m = mg * SUBLANES + mi
            wm = w_ref[m:m + 1, :]
            for ng in range(n_grp):
                w = w_ref[ng * SUBLANES:(ng + 1) * SUBLANES, :]
                if ng > mg:
                    beats = jnp.where(wm >= w, 1.0, 0.0)
                elif ng < mg:
                    beats = jnp.where(wm > w, 1.0, 0.0)
                else:
                    beats = jnp.where(grp_i > mi, jnp.where(wm >= w, 1.0, 0.0), jnp.where(wm > w, 1.0, 0.0))
                rank[ng] = rank[ng] + beats
        return tuple(rank)

    rank = tuple(rank)
    for mg in range(n_grp):
        rank = lax.cond(mg * SUBLANES <= last_blk, functools.partial(count_group, mg), lambda r: r, rank)
    return causal & (jnp.concatenate(rank, axis=0) < TOP_N)


def _flash_tiles_t(tiles, q_ts, ms, accs):
    def scores(i):
        k, _, mask = tiles[i]
        s = [_dot(k, q_t) for q_t in q_ts]
        return s if mask is None else [jnp.where(mask, x, NEG_INF) for x in s]

    ahead = 2
    pending = {i: scores(i) for i in range(min(ahead, len(tiles)))}
    for i, (_, v_t, _) in enumerate(tiles):
        s = pending.pop(i)
        m_new = [jnp.maximum(m, jnp.max(x, axis=0, keepdims=True)) for m, x in zip(ms, s)]
        alpha = [jnp.exp2(m - mn) for m, mn in zip(ms, m_new)]
        p = [jnp.exp2(x - mn).astype(BF16) for x, mn in zip(s, m_new)]
        pv = [_dot(v_t, x) for x in p]
        if i + ahead < len(tiles):
            pending[i + ahead] = scores(i + ahead)
        accs = [a * acc + x for a, acc, x in zip(alpha, accs, pv)]
        ms = m_new
    return tuple(ms), tuple(accs)


def _attn_body(qt_ref, kc_ref, vct_ref, ovlt_ref, ks_ref, vst_ref, kw_ref, vwt_ref, gate_ref, o_ref,
               qa_ref, ocmp_ref, w_ref, *, tq, ncp):
    R = Q_PER_KV
    qt = pl.program_id(2)
    q0 = qt * tq
    nbs = LANES // 2

    c_i = lax.broadcasted_iota(I32, (ncp, tq), 0)
    qpos_c = q0 + lax.broadcasted_iota(I32, (ncp, tq), 1)
    cmask = c_i * CMP_STRIDE + (CMP_BLOCK - 1) <= qpos_c
    kc = kc_ref[...]
    s = [jnp.where(cmask, _dot(kc, qt_ref[r]), NEG_INF) for r in range(R)]
    e = [jnp.where(cmask, jnp.exp2(x - jnp.max(x, axis=0, keepdims=True)), 0.0) for x in s]
    l = [jnp.sum(x, axis=0, keepdims=True) for x in e]
    p = [(x * (1.0 / jnp.where(y > 0.0, y, 1.0))).astype(BF16) for x, y in zip(e, l)]
    imp = sum(_dot(ovlt_ref[...], x) for x in p)
    for r in range(R):
        ocmp_ref[r] = _dot(vct_ref[...], p[r])

    sel = _topk_select_t(w_ref, imp[:nbs], q0, tq)
    bias = jnp.concatenate([jnp.where(sel, 0.0, NEG_INF), jnp.zeros((LANES - nbs, tq), F32)], axis=0)
    for r in range(R):
        qa_ref[r] = (qt_ref[r].astype(F32) + bias).astype(BF16)

    key_i = lax.broadcasted_iota(I32, (tq, tq), 0)
    qry_i = lax.broadcasted_iota(I32, (tq, tq), 1)
    init = (tuple(jnp.full((1, tq), -jnp.inf, F32) for _ in range(R)),
            tuple(jnp.zeros((LANES, tq), F32) for _ in range(R)))

    def tiles(k_ref, vt_ref, q_ref, js, masks, state):
        ts = [(k_ref[pl.ds(pl.multiple_of(j * tq, tq), tq), :], vt_ref[j], mask) for j, mask in zip(js, masks)]
        return _flash_tiles_t(ts, [q_ref[r] for r in range(R)], *state)

    def last_tiles(k_ref, vt_ref, q_ref, n, first_mask, state):
        js = [qt - (n - 1 - t) for t in range(n)]
        masks = [first_mask] + [None] * (n - 2) + [key_i <= qry_i] if n > 1 else [key_i <= qry_i]
        return tiles(k_ref, vt_ref, q_ref, js, masks, state)

    GROUP = 4

    def slc_group(i, st):
        return tiles(ks_ref, vst_ref, qa_ref, [GROUP * i + t for t in range(GROUP)], [None] * GROUP, st)

    state = lax.fori_loop(0, qt // GROUP, slc_group, init)
    tails = [functools.partial(last_tiles, ks_ref, vst_ref, qa_ref, n, None) for n in range(1, GROUP + 1)]
    _, acc_s = lax.switch(qt % GROUP, tails, state)

    nwin = WINDOW // tq
    wins = [functools.partial(last_tiles, kw_ref, vwt_ref, qt_ref, n, (key_i > qry_i) if n == nwin + 1 else None)
            for n in range(1, nwin + 2)]
    _, acc_w = lax.switch(jnp.minimum(qt, nwin), wins, init)

    outs = []
    for r in range(R):
        a_s = acc_s[r]
        a_w = acc_w[r]
        g = lambda k: gate_ref[3 * r + k:3 * r + k + 1, :]
        o = (g(0) * ocmp_ref[r] + g(1) * (a_s / a_s[HEAD_DIM:HEAD_DIM + 1, :])
             + g(2) * (a_w / a_w[HEAD_DIM:HEAD_DIM + 1, :]))
        outs.append(o[:HEAD_DIM])
    o_ref[...] = jnp.concatenate(outs, axis=0).T


def _attn_call(q_t, kc, vc_t, ovl_t, ks, vs_t, kw, vw_t, gates_t, batch, seq):
    tq = min(ATTN_TILE, seq)
    nq = seq // tq
    ncp = kc.shape[1] // batch
    R = Q_PER_KV
    assert seq // SLC_BLOCK <= LANES // 2 and WINDOW % tq == 0
    k_spec = pl.BlockSpec((None, seq, LANES), lambda b, g, t: (g, b, 0))
    vt_spec = pl.BlockSpec((None, nq, LANES, tq), lambda b, g, t: (g, b, 0, 0))
    acc = pltpu.VMEM((R, LANES, tq), F32)
    return pl.pallas_call(
        functools.partial(_attn_body, tq=tq, ncp=ncp),
        out_shape=jax.ShapeDtypeStruct((batch * seq, NSA_W), F32),
        grid=(batch, KV_HEADS, nq),
        in_specs=[pl.BlockSpec((R, LANES, tq), lambda b, g, t: (g, 0, b * nq + t)),
                  pl.BlockSpec((None, ncp, LANES), lambda b, g, t: (g, b, 0)),
                  pl.BlockSpec((None, None, LANES, ncp), lambda b, g, t: (g, b, 0, 0)),
                  pl.BlockSpec(ovl_t.shape, lambda b, g, t: (0, 0)),
                  k_spec, vt_spec, k_spec, vt_spec,
                  pl.BlockSpec((None, 2 * SUBLANES, tq), lambda b, g, t: (g, 0, b * nq + t))],
        out_specs=pl.BlockSpec((tq, R * HEAD_DIM), lambda b, g, t: (b * nq + t, g)),
        scratch_shapes=[pltpu.VMEM((R, LANES, tq), BF16), acc, pltpu.VMEM((LANES // 2, tq), F32)],
        compiler_params=_cparams(("parallel", "parallel", "arbitrary")),
        name="nsa_prompt",
    )(q_t, kc, vc_t, ovl_t, ks, vs_t, kw, vw_t, gates_t)


def _s5_discretise(p, l):
    lr = p["ssm_lam_re"][l].astype(F32)
    li = p["ssm_lam_im"][l].astype(F32)
    dt = jnp.exp(p["ssm_log_step"][l].astype(F32))[:, None]

    def apow(t):
        mag, ang = jnp.exp(lr * dt * t), li * dt * t
        return mag * jnp.cos(ang), mag * jnp.sin(ang)

    a_re, a_im = apow(1.0)
    den = lr * lr + li * li
    nr, ni = a_re - 1.0, a_im
    f_re, f_im = (nr * lr + ni * li) / den, (ni * lr - nr * li) / den
    br, bi = p["ssm_b_re"][l].astype(F32), p["ssm_b_im"][l].astype(F32)
    bb_re = f_re[..., None] * br - f_im[..., None] * bi
    bb_im = f_re[..., None] * bi + f_im[..., None] * br
    return apow, bb_re, bb_im


def _s5_expand_body(pst_ref, kt_ref, cp_ref, wcol_ref, wst_ref, wout_ref):
    T, E, N, C = SSM_CHUNK, SSM_SLAB_GROUPS, SSM_STATE, SSM_GROUP
    row = lax.broadcasted_iota(I32, (LANES, LANES), 0)
    lane = lax.broadcasted_iota(I32, (LANES, LANES), 1)
    row_grp = row // C
    low = lane < N
    own_grp = row_grp == lane // C
    lane_n = lax.broadcasted_iota(I32, (N, LANES), 1)
    for i in range(T):
        wcol_ref[i * LANES:(i + 1) * LANES, :] = jnp.where(own_grp, kt_ref[i], 0.0).astype(BF16)
        x = pst_ref[i]
        x_sw = pltpu.roll(x, N, 1)
        halves = (jnp.where(low, x, x_sw), jnp.where(low, x_sw, x))
        for ri, x2 in enumerate(halves):
            for q in range(E // 2):
                own = row_grp == 2 * q + jnp.where(low, 0, 1)
                c0 = ri * E * N + q * LANES
                wst_ref[i * LANES:(i + 1) * LANES, c0:c0 + LANES] = jnp.where(own, x2, 0.0).astype(BF16)
    for i in range(T + 1):
        m_t = cp_ref[i].T
        for ri in range(2):
            rows = m_t[ri * N:(ri + 1) * N]
            for g in range(E):
                r0 = ri * E * N + g * N
                wout_ref[i, r0:r0 + N, :] = jnp.where(lane_n // C == g, rows, 0.0).astype(BF16)


def _prep_s5_chunked(p, l):
    hi = lax.Precision.HIGHEST
    apow, bb_re, bb_im = _s5_discretise(p, l)
    c_re, c_im = p["ssm_c_re"][l].astype(F32), p["ssm_c_im"][l].astype(F32)
    T, J, E, N, C = SSM_CHUNK, SSM_SLABS, SSM_SLAB_GROUPS, SSM_STATE, SSM_GROUP
    pw_re, pw_im = apow(jnp.arange(T + 1, dtype=F32)[:, None, None])
    bt_re, bt_im = bb_re.transpose(0, 2, 1), bb_im.transpose(0, 2, 1)
    p_re = pw_re[:, :, None, :] * bt_re - pw_im[:, :, None, :] * bt_im
    p_im = pw_re[:, :, None, :] * bt_im + pw_im[:, :, None, :] * bt_re
    kt = (jnp.einsum("tgkn,gcn->tgkc", p_re[:T], jnp.tile(c_re, (1, E, 1)), precision=hi)
          - jnp.einsum("tgkn,gcn->tgkc", p_im[:T], jnp.tile(c_im, (1, E, 1)), precision=hi))
    cp_re = c_re * pw_re[:, :, None, :] - c_im * pw_im[:, :, None, :]
    cp_im = c_re * pw_im[:, :, None, :] + c_im * pw_re[:, :, None, :]
    slab = lambda x: jnp.moveaxis(x.reshape(x.shape[0], J, E * C, LANES), 1, 0)
    pst = slab(jnp.concatenate([p_re[:T][::-1], p_im[:T][::-1]], axis=-1))
    ktile = slab(kt[::-1])
    cpn = slab(jnp.concatenate([cp_re, -cp_im], axis=-1))
    blk = lambda n: pl.BlockSpec((None, n, LANES, LANES), lambda j: (j, 0, 0, 0))
    w_col, w_st, w_out = pl.pallas_call(
        _s5_expand_body,
        out_shape=(jax.ShapeDtypeStruct((J, T * LANES, LANES), BF16),
                   jax.ShapeDtypeStruct((J, T * LANES, 2 * E * N), BF16),
                   jax.ShapeDtypeStruct((J, T + 1, 2 * E * N, LANES), BF16)),
        grid=(J,),
        in_specs=[blk(T), blk(T), blk(T + 1)],
        out_specs=(pl.BlockSpec((None, T * LANES, LANES), lambda j: (j, 0, 0)),
                   pl.BlockSpec((None, T * LANES, 2 * E * N), lambda j: (j, 0, 0)),
                   pl.BlockSpec((None, T + 1, 2 * E * N, LANES), lambda j: (j, 0, 0, 0))),
        compiler_params=_cparams(("parallel",)),
        name="s5_expand",
    )(pst, ktile, cpn)
    return {
        "w_col": w_col, "w_st": w_st, "w_out": w_out,
        "a_re": pw_re[T].reshape(J, 1, E * N), "a_im": pw_im[T].reshape(J, 1, E * N),
        "a1_re": pw_re[1].reshape(J, 1, E * N), "a1_im": pw_im[1].reshape(J, 1, E * N),
        "d": p["ssm_d"][l].reshape(J, 1, LANES).astype(F32),
    }


def _s5_prompt_body(u_ref, wcol_ref, wst_ref, are_ref, aim_ref, wout_ref, d_ref,
                    y_ref, hre_ref, him_ref, xs_ref, hp_ref, *, n_chunks):
    T = SSM_CHUNK
    ns = SSM_SLAB_GROUPS * SSM_STATE
    u_pos = [u_ref[pl.ds(s, n_chunks, stride=T), :] for s in range(T)]
    ub = jnp.concatenate(u_pos, axis=-1).astype(BF16)
    xs_ref[...] = _dot(ub, wst_ref[...])
    a_re = are_ref[...]
    a_im = aim_ref[...]

    def step(c, carry):
        hr, hi = carry
        hp_ref[pl.ds(c, 1), 0:ns] = hr
        hp_ref[pl.ds(c, 1), ns:2 * ns] = hi
        xr = xs_ref[pl.ds(c, 1), 0:ns]
        xi = xs_ref[pl.ds(c, 1), ns:2 * ns]
        return a_re * hr - a_im * hi + xr, a_re * hi + a_im * hr + xi

    zero = jnp.zeros((1, ns), F32)
    hr, hi = lax.fori_loop(0, n_chunks, step, (zero, zero))
    hre_ref[...] = jnp.broadcast_to(hr, hre_ref.shape)
    him_ref[...] = jnp.broadcast_to(hi, him_ref.shape)
    hpb = hp_ref[...].astype(BF16)
    for t in range(T):
        y_ref[pl.ds(t, n_chunks, stride=T), :] = (
            _dot(ub[:, :(t + 1) * LANES], wcol_ref[(T - 1 - t) * LANES:, :])
            + _dot(hpb, wout_ref[t + 1]) + d_ref[...] * u_pos[t])


def _s5_prompt_call(u_slab, sw, batch, seq):
    T, J = SSM_CHUNK, SSM_SLABS
    n_chunks = seq // T
    ns = SSM_SLAB_GROUPS * SSM_STATE
    row_spec = pl.BlockSpec((None, seq, LANES), lambda b, j: (j, b, 0))
    slab_spec = lambda a: pl.BlockSpec((None,) + a.shape[1:], lambda b, j: (j,) + (0,) * (a.ndim - 1))
    st_spec = pl.BlockSpec((None, None, SUBLANES, ns), lambda b, j: (b, j, 0, 0))
    st_sds = jax.ShapeDtypeStruct((batch, J, SUBLANES, ns), F32)
    y, hre, him = pl.pallas_call(
        functools.partial(_s5_prompt_body, n_chunks=n_chunks),
        out_shape=(jax.ShapeDtypeStruct(u_slab.shape, F32), st_sds, st_sds),
        grid=(batch, J),
        in_specs=[row_spec, slab_spec(sw["w_col"]), slab_spec(sw["w_st"]), slab_spec(sw["a_re"]),
                  slab_spec(sw["a_im"]), slab_spec(sw["w_out"]), slab_spec(sw["d"])],
        out_specs=(row_spec, st_spec, st_spec),
        scratch_shapes=[pltpu.VMEM((n_chunks, 2 * ns), F32), pltpu.VMEM((n_chunks, 2 * ns), F32)],
        compiler_params=_cparams(("parallel", "parallel")),
        name="s5_prompt",
    )(u_slab, sw["w_col"], sw["w_st"], sw["a_re"], sw["a_im"], sw["w_out"], sw["d"])
    n_groups = J * SSM_SLAB_GROUPS
    state = lambda h: h[:, :, 0, :].reshape(batch, n_groups, SSM_STATE)
    return y, state(hre), state(him)


def _s5_step_body(u_ref, wx_ref, are_ref, aim_ref, h0re_ref, h0im_ref, wy_ref, d_ref,
                  y_ref, hre_ref, him_ref):
    ns = SSM_SLAB_GROUPS * SSM_STATE
    for j in range(SSM_SLABS):
        sl = slice(j * ns, (j + 1) * ns)
        u = u_ref[j]
        x = _dot(u.astype(BF16), wx_ref[j])
        a_re, a_im = are_ref[j], aim_ref[j]
        h0r, h0i = h0re_ref[:, sl], h0im_ref[:, sl]
        hr = a_re * h0r - a_im * h0i + x[:, :ns]
        hi = a_re * h0i + a_im * h0r + x[:, ns:]
        hre_ref[:, sl] = hr
        him_ref[:, sl] = hi
        y_ref[j] = _dot(jnp.concatenate([hr, hi], axis=-1).astype(BF16), wy_ref[j]) + d_ref[j] * u


def _s5_step_call(u_slab, h0_re, h0_im, sw):
    J, n_tok, _ = u_slab.shape
    T = SSM_CHUNK
    full = lambda a: pl.BlockSpec(a.shape, lambda i: (0,) * a.ndim)
    wx_spec = pl.BlockSpec((J, LANES, sw["w_st"].shape[2]), lambda i: (0, T - 1, 0))
    wy_spec = pl.BlockSpec((J, None) + sw["w_out"].shape[2:], lambda i: (0, 0, 0, 0))
    st_sds = jax.ShapeDtypeStruct(h0_re.shape, F32)
    return pl.pallas_call(
        _s5_step_body,
        out_shape=(jax.ShapeDtypeStruct(u_slab.shape, F32), st_sds, st_sds),
        grid=(1,),
        in_specs=[full(u_slab), wx_spec, full(sw["a1_re"]), full(sw["a1_im"]), full(h0_re), full(h0_im),
                  wy_spec, full(sw["d"])],
        out_specs=(full(u_slab), full(h0_re), full(h0_re)),
        compiler_params=_cparams(("arbitrary",)),
        name="s5_step",
    )(u_slab, sw["w_st"], sw["a1_re"], sw["a1_im"], h0_re, h0_im, sw["w_out"], sw["d"])


def _compress_sample_body(pt_ref, *refs, n_pages, page_rows, nch):
    del pt_ref
    page_refs = refs[:n_pages]
    pe_ref, wa_ref, wb_ref, b1_ref, w2_ref, kc_ref, vc_ref, krows_ref, vrows_ref = refs[n_pages:]
    b = pl.program_id(0)

    @pl.when(b == 0)
    def _():
        krows_ref[...] = jnp.zeros(krows_ref.shape, F32)
        vrows_ref[...] = jnp.zeros(vrows_ref.shape, F32)

    stage = b % 2
    for i, page_ref in enumerate(page_refs):
        krows_ref[stage, i * page_rows:(i + 1) * page_rows, :] = page_ref[0].reshape(LANES, page_rows).T
        vrows_ref[stage, i * page_rows:(i + 1) * page_rows, :] = page_ref[1].reshape(LANES, page_rows).T
    done = 1 - stage
    _compress_rows((krows_ref.at[done], vrows_ref.at[done]), pe_ref, wa_ref, wb_ref, b1_ref, w2_ref,
                   kc_ref, vc_ref, nch)


def _compress_sample_call(cache, page_table, cw):
    n_seq, n_pages = page_table.shape
    page_rows = cache.shape[-1]
    assert page_rows == LANES
    nch = n_pages * page_rows // CMP_STRIDE
    staged = lambda b: jnp.minimum(b, n_seq - 1)
    page_spec = lambda i: pl.BlockSpec((None, 2, KV_HEADS, HEAD_DIM, page_rows),
                                       lambda b, pt: (pt[staged(b), i], 0, 0, 0, 0))
    full = lambda a: pl.BlockSpec(a.shape, lambda b, pt: (0,) * a.ndim)
    out_of = lambda b: jnp.maximum(b - 1, 0)
    out_spec = pl.BlockSpec((KV_HEADS, nch, LANES), lambda b, pt: (0, out_of(b), 0))
    out_sds = jax.ShapeDtypeStruct((KV_HEADS, n_seq * nch, LANES), BF16)
    out_t_spec = pl.BlockSpec((KV_HEADS, None, LANES, nch), lambda b, pt: (0, out_of(b), 0, 0))
    out_t_sds = jax.ShapeDtypeStruct((KV_HEADS, n_seq, LANES, nch), BF16)
    weights = (cw["pe"], cw["wa"], cw["wb"], cw["b1"], cw["w2"])
    return pl.pallas_call(
        functools.partial(_compress_sample_body, n_pages=n_pages, page_rows=page_rows, nch=nch),
        out_shape=(out_sds, out_t_sds),
        grid_spec=pltpu.PrefetchScalarGridSpec(
            num_scalar_prefetch=1,
            grid=(n_seq + 1,),
            in_specs=[page_spec(i) for i in range(n_pages)] + [full(a) for a in weights],
            out_specs=(out_spec, out_t_spec),
            scratch_shapes=[pltpu.VMEM((2, n_pages * page_rows, LANES), F32)] * 2),
        compiler_params=_cparams(("arbitrary",)),
        name="compress_sample",
    )(page_table, *([cache] * n_pages), *weights)


def _group_rows(x0, x1):
    row = lax.broadcasted_iota(I32, x0.shape, 0)
    return jnp.where(row < Q_PER_KV, x0, x1)


def _sample_select_body(q_ref, kc_ref, vct_ref, ovl_ref, tri_ref, ocmp_ref, idx_ref, *, ncp, qpos, nbp):
    q8 = q_ref[...]
    c_i = lax.broadcasted_iota(I32, (N_HEADS, ncp), 1)
    cmask = c_i * CMP_STRIDE + (CMP_BLOCK - 1) <= qpos
    s = _group_rows(_dot_nt(q8, kc_ref[0]), _dot_nt(q8, kc_ref[1]))
    s = jnp.where(cmask, s, NEG_INF)
    e = jnp.where(cmask, jnp.exp2(s - jnp.max(s, axis=-1, keepdims=True)), 0.0)
    l = jnp.sum(e, axis=-1, keepdims=True)
    p = (e / jnp.where(l > 0.0, l, 1.0)).astype(BF16)
    ocmp_ref[...] = _group_rows(_dot_nt(p, vct_ref[0]), _dot_nt(p, vct_ref[1]))
    imp8 = _dot(p, ovl_ref[...])

    n_row = lax.broadcasted_iota(I32, (1, nbp), 1)
    qblk = qpos // SLC_BLOCK
    causal = n_row <= qblk
    forced = (n_row == 0) | (n_row >= qblk - (N_LOCAL_BLOCKS - 1))
    m_i = lax.broadcasted_iota(I32, (nbp, nbp), 0)
    n_i = lax.broadcasted_iota(I32, (nbp, nbp), 1)
    lane = lax.broadcasted_iota(I32, (1, LANES), 1)
    idx_rows = []
    for g in range(KV_HEADS):
        imp = jnp.sum(imp8[g * Q_PER_KV:(g + 1) * Q_PER_KV], axis=0, keepdims=True)
        w = jnp.where(causal, jnp.where(forced, jnp.inf, imp), -jnp.inf)
        w_sq = jnp.broadcast_to(w, (nbp, nbp))
        w_col = w_sq.T
        beats = jnp.where(n_i > m_i, jnp.where(w_col >= w_sq, 1.0, 0.0), jnp.where(w_col > w_sq, 1.0, 0.0))
        rank = jnp.sum(beats, axis=0, keepdims=True)
        sel = causal & (rank < TOP_N)
        self_f = jnp.where(sel, 1.0, 0.0)
        before = _dot(self_f.astype(BF16), tri_ref[...])
        idx = jnp.full((1, LANES), -1, I32)
        for k in range(TOP_N):
            hit = sel & (before == float(k))
            val = jnp.sum(jnp.where(hit, n_row.astype(F32) + 1.0, 0.0), axis=-1, keepdims=True) - 1.0
            idx = jnp.where(lane == k, val.astype(I32), idx)
        idx_rows.append(idx)
    idx_ref[...] = jnp.concatenate(idx_rows + [jnp.full((SUBLANES - KV_HEADS, LANES), -1, I32)], axis=0)


def _sample_select_call(q8, kc, vc_t, ovl, tri, qpos):
    n_seq = q8.shape[0]
    ncp = kc.shape[1] // n_seq
    nbp = ovl.shape[1]
    cmp_spec = pl.BlockSpec((KV_HEADS, ncp, LANES), lambda b: (0, b, 0))
    row_spec = pl.BlockSpec((None, N_HEADS, LANES), lambda b: (b, 0, 0))
    full = lambda a: pl.BlockSpec(a.shape, lambda b: (0,) * a.ndim)
    return pl.pallas_call(
        functools.partial(_sample_select_body, ncp=ncp, qpos=qpos, nbp=nbp),
        out_shape=(jax.ShapeDtypeStruct((n_seq, N_HEADS, LANES), F32),
                   jax.ShapeDtypeStruct((n_seq, SUBLANES, LANES), I32)),
        grid=(n_seq,),
        in_specs=[row_spec, cmp_spec, pl.BlockSpec((KV_HEADS, None, LANES, ncp), lambda b: (0, b, 0, 0)),
                  full(ovl), full(tri)],
        out_specs=(row_spec, pl.BlockSpec((None, SUBLANES, LANES), lambda b: (b, 0, 0))),
        compiler_params=_cparams(("parallel",)),
        name="nsa_sample_select",
    )(q8, kc, vc_t, ovl, tri)


def _sample_attend_body(idx_ref, pt_ref, q_ref, ocmp_ref, gate_ref, ksn_ref, vsn_ref, kwn_ref, vwn_ref,
                        win_ref, *refs, n_cache_blocks, blocks_per_page, win_skip):
    del pt_ref
    n_blk = KV_HEADS * TOP_N
    k_refs, v_refs, o_ref = refs[:n_blk], refs[n_blk:2 * n_blk], refs[2 * n_blk]
    b = pl.program_id(0)
    q = q_ref[...]
    qf = q.astype(F32)
    row_g = (lax.broadcasted_iota(I32, (N_HEADS, 1), 0) >= Q_PER_KV).astype(I32)

    def attend(s_list, v_list, kn_ref, vn_ref):
        s_self = jnp.sum(qf * kn_ref[...].astype(F32), axis=-1, keepdims=True)
        m = s_self
        for s in s_list:
            m = jnp.maximum(m, jnp.max(s, axis=-1, keepdims=True))
        p_self = jnp.exp2(s_self - m)
        l = p_self
        acc = p_self.astype(BF16).astype(F32) * vn_ref[...].astype(F32)
        for s, v in zip(s_list, v_list):
            p = jnp.exp2(s - m)
            l = l + jnp.sum(p, axis=-1, keepdims=True)
            acc = acc + _dot_nt(p.astype(BF16), v().astype(BF16))
        return acc / l

    s_list, v_list = [], []
    for j in range(n_blk):
        s = _dot(q, k_refs[j][...].astype(BF16))
        col = lax.broadcasted_iota(I32, s.shape, 1)
        n = idx_ref[b, j]
        first = (n % blocks_per_page) * SLC_BLOCK
        ok = ((row_g == j // TOP_N) & (col >= first) & (col < first + SLC_BLOCK)
              & (n >= 0) & (n < n_cache_blocks))
        s_list.append(jnp.where(ok, s, NEG_INF))
        v_list.append(lambda j=j: v_refs[j][...])
    o_slc = attend(s_list, v_list, ksn_ref, vsn_ref)

    s_list, v_list = [], []
    for g in range(KV_HEADS):
        s = _dot(q, win_ref[0, g].astype(BF16))
        col = lax.broadcasted_iota(I32, s.shape, 1)
        s_list.append(jnp.where((row_g == g) & (col >= win_skip), s, NEG_INF))
        v_list.append(lambda g=g: win_ref[1, g])
    o_win = attend(s_list, v_list, kwn_ref, vwn_ref)

    gates = gate_ref[...]
    o_ref[...] = gates[:, 0:1] * ocmp_ref[:, 0:HEAD_DIM] + gates[:, 1:2] * o_slc + gates[:, 2:3] * o_win


def _sample_attend_call(idx, page_table, q64, ocmp, gates8, ksn, vsn, kwn, vwn, cache_win_t, cache_t, win_skip):
    n_seq, n_pages = page_table.shape
    page_rows = cache_t.shape[-1]
    bpp = page_rows // SLC_BLOCK
    n_cache_blocks = n_pages * bpp
    row_spec = lambda a: pl.BlockSpec((None,) + a.shape[1:], lambda b, ix, pt: (b,) + (0,) * (a.ndim - 1))

    def blk_spec(j, slot):
        def index_map(b, ix, pt):
            n = jnp.clip(ix[b, j], 0, n_cache_blocks - 1)
            return (pt[b, n // bpp], slot, j // TOP_N, 0, 0)
        return pl.BlockSpec((None, None, None, HEAD_DIM, page_rows), index_map)

    n_blk = KV_HEADS * TOP_N
    small = (q64, ocmp, gates8, ksn, vsn, kwn, vwn, cache_win_t)
    return pl.pallas_call(
        functools.partial(_sample_attend_body, n_cache_blocks=n_cache_blocks, blocks_per_page=bpp,
                          win_skip=win_skip),
        out_shape=jax.ShapeDtypeStruct((n_seq, N_HEADS, HEAD_DIM), F32),
        grid_spec=pltpu.PrefetchScalarGridSpec(
            num_scalar_prefetch=2,
            grid=(n_seq,),
            in_specs=([row_spec(a) for a in small] + [blk_spec(j, 2) for j in range(n_blk)]
                      + [blk_spec(j, 3) for j in range(n_blk)]),
            out_specs=pl.BlockSpec((None, N_HEADS, HEAD_DIM), lambda b, ix, pt: (b, 0, 0))),
        compiler_params=_cparams(("parallel",)),
        name="nsa_sample_attend",
    )(idx, page_table, *small, *([cache_t] * (2 * n_blk)))


def _round_up(x, m):
    return -(-x // m) * m


def _prompt_layer(h, prm, cw, sw, batch, seq):
    tabs = _rope_tables(jnp.arange(seq))
    u, gs, gn, q_t, kv_t, win_t, gates_t, ks, vs_t, kw, vw_t = _proj_call(
        h, prm["w_in"], prm["norm_w"], prm["qnw"], prm["knw"], prm["gb"], tabs, batch, seq)
    y_ssm, h_re, h_im = _s5_prompt_call(u, sw, batch, seq)
    kc, vc_t = _compress_prompt_call(kv_t, cw, batch, seq)
    nch = seq // CMP_STRIDE
    ovl_t = _overlap_matrix(nch, nch - 1, seq // SLC_BLOCK).T
    o = _attn_call(q_t, kc, vc_t, ovl_t, ks, vs_t, kw, vw_t, gates_t, batch, seq)
    h_new = _outmix_call(h, y_ssm, gs, o, gn, prm["w_glu"], prm["w_out"])
    rows = lambda x_t, slots: x_t.reshape(batch, slots, KV_HEADS, HEAD_DIM, seq).transpose(0, 4, 1, 2, 3)
    return h_new, rows(kv_t, 4), rows(win_t, 2)[:, seq - min(WINDOW, seq):], h_re, h_im


def _sample_layer(h, prm, cw, sw, cache_kv, cache_win, st_re, st_im, page_table):
    n_seq = h.shape[0]
    n_phys, page_rows = cache_kv.shape[:2]
    n_pages = page_table.shape[1]
    past_len = n_pages * page_rows
    win_buf = cache_win.shape[1]
    n_pad = _round_up(n_seq, LANES)
    tabs = _rope_tables(jnp.full((n_pad,), past_len, I32))
    h_pad = jnp.pad(h, ((0, n_pad - n_seq), (0, 0)))
    u, gs, gn, q_t, kv_t, win_t, gates_t, ks, vs_t, kw, vw_t = _proj_call(
        h_pad, prm["w_in"], prm["norm_w"], prm["qnw"], prm["knw"], prm["gb"], tabs, 1, n_pad)
    u, gs, gn = u[:, :n_seq], gs[:n_seq], gn[:n_seq]
    n_state = st_re.shape[1] * st_re.shape[2]
    y_ssm, h_re, h_im = _s5_step_call(u, st_re.reshape(n_seq, n_state), st_im.reshape(n_seq, n_state), sw)
    cache_t = cache_kv.transpose(0, 2, 3, 4, 1)
    kc, vc_t = _compress_sample_call(cache_t, page_table, cw)
    ncp = past_len // CMP_STRIDE
    n_blk = -(-(past_len + 1) // SLC_BLOCK)
    nbp = _round_up(n_blk, LANES)
    ovl = _overlap_matrix(ncp, ncp - 1, n_blk, nbp)
    tri = (jnp.arange(nbp)[:, None] < jnp.arange(nbp)[None, :]).astype(BF16)
    q8 = q_t[:, :, :n_seq].transpose(2, 0, 1)
    ocmp, idx = _sample_select_call(q8, kc, vc_t, ovl, tri, past_len)
    idx = idx[:, :KV_HEADS, :TOP_N].reshape(n_seq, KV_HEADS * TOP_N)
    gates8 = gates_t[:, :3 * Q_PER_KV, :n_seq].reshape(KV_HEADS, Q_PER_KV, 3, n_seq).transpose(3, 0, 1, 2)
    gates8 = jnp.pad(gates8.reshape(n_seq, N_HEADS, 3), ((0, 0), (0, 0), (0, LANES - 3)))
    per_head = lambda a: jnp.repeat(a.transpose(1, 0, 2), Q_PER_KV, axis=1)
    new_k = lambda k: per_head(k[:, :n_seq, HEAD_DIM:])
    new_v = lambda v_t: per_head(v_t[:, 0, :HEAD_DIM, :n_seq].transpose(0, 2, 1))
    o8 = _sample_attend_call(idx, page_table, q8[:, :, HEAD_DIM:], ocmp, gates8, new_k(ks), new_v(vs_t),
                             new_k(kw), new_v(vw_t), cache_win.transpose(0, 2, 3, 4, 1), cache_t,
                             max(win_buf + 1 - WINDOW, 0))
    o = o8.reshape(n_seq, NSA_W)
    h_new = _outmix_call(h, y_ssm, gs, o, gn, prm["w_glu"], prm["w_out"])
    kv_rows = kv_t[0, :, :n_seq].T.reshape(n_seq, 1, 4, KV_HEADS, HEAD_DIM)
    win_new = win_t[0, :, :n_seq].T.reshape(n_seq, 1, 2, KV_HEADS, HEAD_DIM)
    wrows = jnp.concatenate([cache_win, win_new], axis=1)
    wrows = wrows[:, wrows.shape[1] - min(WINDOW, wrows.shape[1]):]
    state = lambda s: s.reshape(st_re.shape)
    return h_new, kv_rows, wrows, state(h_re), state(h_im)


def kernel(x_prompt, x_sample, cache_kv, cache_win, state_ssm_re, state_ssm_im, page_table, norm_w, w_in, gate_b,
           q_norm_w, k_norm_w, cmp_pe, cmp_w1, cmp_b1, cmp_w2, ssm_lam_re, ssm_lam_im, ssm_log_step, ssm_b_re,
           ssm_b_im, ssm_c_re, ssm_c_im, ssm_d, w_glu, w_out):
    p = dict(norm_w=norm_w, w_in=w_in, gate_b=gate_b, q_norm_w=q_norm_w, k_norm_w=k_norm_w, cmp_pe=cmp_pe,
             cmp_w1=cmp_w1, cmp_b1=cmp_b1, cmp_w2=cmp_w2, ssm_lam_re=ssm_lam_re, ssm_lam_im=ssm_lam_im,
             ssm_log_step=ssm_log_step, ssm_b_re=ssm_b_re, ssm_b_im=ssm_b_im, ssm_c_re=ssm_c_re,
             ssm_c_im=ssm_c_im, ssm_d=ssm_d, w_glu=w_glu, w_out=w_out)
    b_p, s_p, d_model = x_prompt.shape
    b_s, s_s, _ = x_sample.shape
    assert s_s == 1, "the sample group decodes one token per sequence"
    h_p = x_prompt.reshape(b_p * s_p, d_model)
    h_s = x_sample.reshape(b_s, d_model)
    outs_p, outs_s = [], []
    for l in range(norm_w.shape[0]):
        weights = (_prep_params(p, l), _prep_compress(p, l), _prep_s5_chunked(p, l))
        h_p, *rest_p = _prompt_layer(h_p, *weights, b_p, s_p)
        h_s, *rest_s = _sample_layer(h_s, *weights, cache_kv[l], cache_win[l], state_ssm_re[l],
                                     state_ssm_im[l], page_table)
        outs_p.append(rest_p)
        outs_s.append(rest_s)
    stack = lambda outs, i: jnp.stack([o[i] for o in outs])
    return (h_p.reshape(x_prompt.shape), h_s.reshape(x_sample.shape),
            stack(outs_p, 0), stack(outs_s, 0), stack(outs_p, 1), stack(outs_s, 1),
            stack(outs_p, 2), stack(outs_p, 3), stack(outs_s, 2), stack(outs_s, 3))
```

```python
import functools
import math

import jax
import jax.numpy as jnp
from jax import lax
from jax.experimental import pallas as pl
from jax.experimental.pallas import tpu as pltpu

F32 = jnp.float32
BF16 = jnp.bfloat16
I32 = jnp.int32

LANES = 128
SUBLANES = 8
VMEM_LIMIT_BYTES = 56 * 1024 * 1024

HEAD_DIM = 64
N_HEADS = 8
KV_HEADS = 2
Q_PER_KV = N_HEADS // KV_HEADS
SSM_W = 512
SSM_GROUP = 16
SSM_STATE = 64
NSA_W = N_HEADS * HEAD_DIM
CMP_BLOCK = 32
CMP_STRIDE = 16
CMP_HID = 2 * HEAD_DIM
SLC_BLOCK = 64
TOP_N = 16
N_LOCAL_BLOCKS = 2
WINDOW = 512
ROPE_THETA = 500000.0
ROPE_DIM = HEAD_DIM // 4
RMS_EPS = 1e-6
NEG_INF = -1e30

COL_U = 0
COL_GS = SSM_W
COL_Q = 2 * SSM_W
COL_GN = 2 * SSM_W + NSA_W
COL_KV = 2 * SSM_W + 2 * NSA_W
COL_GL = COL_KV + 6 * KV_HEADS * HEAD_DIM
IN_W_PAD = COL_GL + LANES

PROJ_TILE = 512
ATTN_TILE = 256
SSM_CHUNK = 16
SSM_SLAB_GROUPS = LANES // SSM_GROUP
SSM_SLABS = SSM_W // LANES


def _cparams(sem):
    return pltpu.CompilerParams(dimension_semantics=sem, vmem_limit_bytes=VMEM_LIMIT_BYTES)


def _sigmoid(x):
    return 1.0 / (1.0 + jnp.exp(-x))


def _dot(a, b):
    return jnp.dot(a, b, preferred_element_type=F32)


def _dot_nt(a, b):
    return lax.dot_general(a, b, (((1,), (1,)), ((), ())), preferred_element_type=F32)


def _proj_body(x_ref, nw_ref, w_ref, qnw_ref, knw_ref, gb_ref, ra_ref, rb_ref, rc_ref,
               u_ref, gs_ref, gn_ref, qt_ref, kvt_ref, wint_ref, gt_ref,
               ks_ref, vst_ref, kw_ref, vwt_ref, *, tm, tv, tiles_per_seq):
    x = x_ref[...]
    ms = jnp.mean(x * x, axis=-1, keepdims=True)
    h = (x * lax.rsqrt(ms + RMS_EPS) * nw_ref[...]).astype(BF16)

    def mm(c0, c1):
        return _dot(h, w_ref[:, c0:c1])

    lane = lax.broadcasted_iota(I32, (tm, LANES), 1)
    lo = lane < HEAD_DIM
    ra = ra_ref[...]
    rb = rb_ref[...]
    rc = rc_ref[...]

    def norm_rope(s, wrow):
        s2 = s * s
        slo = jnp.sum(jnp.where(lo, s2, 0.0), axis=-1, keepdims=True)
        shi = jnp.sum(jnp.where(lo, 0.0, s2), axis=-1, keepdims=True)
        msq = jnp.where(lo, slo, shi) * (1.0 / HEAD_DIM)
        y = s * lax.rsqrt(msq + RMS_EPS) * wrow
        half = ROPE_DIM // 2
        return y * ra + pltpu.roll(y, LANES - half, 1) * rb + pltpu.roll(y, half, 1) * rc

    def hi_half(y, head):
        src = pltpu.roll(y, HEAD_DIM, 1) if head == 0 else y
        return jnp.where(lo, 0.0, src)

    zq = mm(COL_Q, COL_GN)
    zkv = mm(COL_KV, COL_GL)
    zgl = mm(COL_GL, IN_W_PAD)

    qnw = qnw_ref[...]
    scale = HEAD_DIM ** -0.5 * math.log2(math.e)
    zeros_t = jnp.zeros((HEAD_DIM, tm), F32)
    for j in range(N_HEADS // 2):
        y_t = (norm_rope(zq[:, j * LANES:(j + 1) * LANES], qnw) * scale).T
        for head in range(2):
            q_t = jnp.concatenate([zeros_t, y_t[head * HEAD_DIM:(head + 1) * HEAD_DIM]], axis=0)
            qt_ref[2 * j + head] = q_t.astype(BF16)

    kc = norm_rope(zkv[:, 0:LANES], knw_ref[0:1, :])
    vc = zkv[:, LANES:2 * LANES]
    ks = norm_rope(zkv[:, 2 * LANES:3 * LANES], knw_ref[1:2, :])
    vs = zkv[:, 3 * LANES:4 * LANES]
    kw = norm_rope(zkv[:, 4 * LANES:5 * LANES], knw_ref[2:3, :])
    vw = zkv[:, 5 * LANES:6 * LANES]
    vs_t, vw_t = vs.T, vw.T
    for i, rows_t in enumerate((kc.T, vc.T, ks.T, vs_t)):
        kvt_ref[i * LANES:(i + 1) * LANES, :] = rows_t
    for i, rows_t in enumerate((kw.T, vw_t)):
        wint_ref[i * LANES:(i + 1) * LANES, :] = rows_t

    row = lax.broadcasted_iota(I32, (tm, LANES), 0)
    pos = (pl.program_id(0) % tiles_per_seq) * tm + row
    onehot = jnp.where(lane == lax.shift_right_logical(pos, 6), 1.0, 0.0)
    ones_t = jnp.where(lax.broadcasted_iota(I32, (HEAD_DIM, tm), 0) == 0, 1.0, 0.0)
    for g in range(KV_HEADS):
        ks_ref[g] = jnp.where(lo, onehot, hi_half(ks, g)).astype(BF16)
        kw_ref[g] = hi_half(kw, g).astype(BF16)
        for v_t, vt_ref in ((vs_t, vst_ref), (vw_t, vwt_ref)):
            v_aug = jnp.concatenate([v_t[g * HEAD_DIM:(g + 1) * HEAD_DIM], ones_t], axis=0).astype(BF16)
            for t in range(tm // tv):
                vt_ref[g, t] = v_aug[:, t * tv:(t + 1) * tv]

    gates_t = _sigmoid(zgl + gb_ref[...]).T
    for g in range(KV_HEADS):
        gt_ref[g] = gates_t[g * 3 * Q_PER_KV:g * 3 * Q_PER_KV + 2 * SUBLANES]

    zu = mm(COL_U, COL_GS)
    for j in range(SSM_SLABS):
        u_ref[j] = zu[:, j * LANES:(j + 1) * LANES]
    gs_ref[...] = mm(COL_GS, COL_Q)
    gn_ref[...] = mm(COL_GN, COL_KV)


def _proj_call(x2d, w_pad, norm_w, qnw, knw, gb, tabs, batch, seq):
    T, D = x2d.shape
    tm = min(PROJ_TILE, seq)
    tv = min(ATTN_TILE, seq)
    assert T == batch * seq and seq % tm == 0 and tm % tv == 0 and tv % LANES == 0
    tps = seq // tm
    row_spec = lambda w: pl.BlockSpec((tm, w), lambda i: (i, 0))
    full = lambda a: pl.BlockSpec(a.shape, lambda i: (0,) * a.ndim)
    tab_spec = pl.BlockSpec((tm, LANES), lambda i: (i % tps, 0))
    head_spec = lambda n: pl.BlockSpec((n, tm, LANES), lambda i: (0, i, 0))
    head_t_spec = lambda n, rows: pl.BlockSpec((n, rows, tm), lambda i: (0, 0, i))
    cache_t_spec = lambda rows: pl.BlockSpec((None, rows, tm), lambda i: (i // tps, 0, i % tps))
    tile_t_spec = pl.BlockSpec((KV_HEADS, tm // tv, LANES, tv), lambda i: (0, i, 0, 0))
    tile_t_sds = jax.ShapeDtypeStruct((KV_HEADS, T // tv, LANES, tv), BF16)
    out_shape = (
        jax.ShapeDtypeStruct((SSM_SLABS, T, LANES), F32),
        jax.ShapeDtypeStruct((T, SSM_W), F32),
        jax.ShapeDtypeStruct((T, NSA_W), F32),
        jax.ShapeDtypeStruct((N_HEADS, LANES, T), BF16),
        jax.ShapeDtypeStruct((batch, 4 * LANES, seq), F32),
        jax.ShapeDtypeStruct((batch, 2 * LANES, seq), F32),
        jax.ShapeDtypeStruct((KV_HEADS, 2 * SUBLANES, T), F32),
        jax.ShapeDtypeStruct((KV_HEADS, T, LANES), BF16),
        tile_t_sds,
        jax.ShapeDtypeStruct((KV_HEADS, T, LANES), BF16),
        tile_t_sds,
    )
    out_specs = (head_spec(SSM_SLABS), row_spec(SSM_W), row_spec(NSA_W), head_t_spec(N_HEADS, LANES),
                 cache_t_spec(4 * LANES), cache_t_spec(2 * LANES), head_t_spec(KV_HEADS, 2 * SUBLANES),
                 head_spec(KV_HEADS), tile_t_spec, head_spec(KV_HEADS), tile_t_spec)
    return pl.pallas_call(
        functools.partial(_proj_body, tm=tm, tv=tv, tiles_per_seq=tps),
        out_shape=out_shape,
        grid=(T // tm,),
        in_specs=[row_spec(D), full(norm_w), full(w_pad), full(qnw), full(knw), full(gb),
                  tab_spec, tab_spec, tab_spec],
        out_specs=out_specs,
        compiler_params=_cparams(("parallel",)),
        name="proj",
    )(x2d, norm_w, w_pad, qnw, knw, gb, *tabs)


def _prep_params(p, l):
    w_in = p["w_in"][l]
    d_model, in_w = w_in.shape
    tile2 = lambda v: jnp.tile(v, (1, LANES // HEAD_DIM))
    return {
        "w_in": jnp.pad(w_in.astype(BF16), ((0, 0), (0, IN_W_PAD - in_w))),
        "norm_w": p["norm_w"][l].reshape(1, d_model).astype(F32),
        "qnw": tile2(p["q_norm_w"][l].reshape(1, HEAD_DIM)).astype(F32),
        "knw": tile2(p["k_norm_w"][l]).astype(F32),
        "gb": jnp.pad(p["gate_b"][l].reshape(1, -1).astype(F32), ((0, 0), (0, LANES - 3 * N_HEADS))),
        "w_glu": p["w_glu"][l].astype(BF16),
        "w_out": p["w_out"][l].astype(BF16),
    }


def _rope_tables(pos):
    half = ROPE_DIM // 2
    inv = ROPE_THETA ** (-jnp.arange(half, dtype=F32) / half)
    ang = pos.astype(F32)[:, None] * inv
    cos, sin = jnp.cos(ang), jnp.sin(ang)
    n = pos.shape[0]
    rest = HEAD_DIM - ROPE_DIM
    a = jnp.concatenate([cos, cos, jnp.ones((n, rest), F32)], axis=-1)
    b = jnp.concatenate([-sin, jnp.zeros((n, HEAD_DIM - half), F32)], axis=-1)
    c = jnp.concatenate([jnp.zeros((n, half), F32), sin, jnp.zeros((n, rest), F32)], axis=-1)
    return tuple(jnp.tile(t, (1, LANES // HEAD_DIM)) for t in (a, b, c))


def _outmix_body(x_ref, y_ref, gs_ref, o_ref, gn_ref, wg_ref, wo_ref, out_ref):
    y = jnp.concatenate([y_ref[j] for j in range(SSM_SLABS)], axis=-1)
    ab = _dot(y.astype(BF16), wg_ref[...])
    gs = gs_ref[...]
    ssm = ab[:, :SSM_W] * _sigmoid(ab[:, SSM_W:]) * (gs * _sigmoid(gs))
    gn = gn_ref[...]
    nsa = o_ref[...] * (gn * _sigmoid(gn))
    acc = _dot(ssm.astype(BF16), wo_ref[0:SSM_W, :])
    acc += _dot(nsa.astype(BF16), wo_ref[SSM_W:, :])
    out_ref[...] = x_ref[...] + acc


def _outmix_call(x2d, y_ssm, g_ssm, o_nsa, g_nsa, w_glu, w_out):
    T, D = x2d.shape
    tm = min(512, T)
    row_spec = lambda w: pl.BlockSpec((tm, w), lambda i: (i, 0))
    full = lambda a: pl.BlockSpec(a.shape, lambda i: (0,) * a.ndim)
    return pl.pallas_call(
        _outmix_body,
        out_shape=jax.ShapeDtypeStruct((T, D), F32),
        grid=(T // tm,),
        in_specs=[row_spec(D), pl.BlockSpec((SSM_SLABS, tm, LANES), lambda i: (0, i, 0)),
                  row_spec(SSM_W), row_spec(NSA_W), row_spec(NSA_W),
                  full(w_glu), full(w_out)],
        out_specs=row_spec(D),
        compiler_params=_cparams(("parallel",)),
        name="outmix",
    )(x2d, y_ssm, g_ssm, o_nsa, g_nsa, w_glu, w_out)


def _gelu_tanh(x):
    c = math.sqrt(2.0 / math.pi)
    return 0.5 * x * (1.0 + jnp.tanh(c * (x + 0.044715 * (x * x * x))))


def _compress_rows(rows_refs, pe_ref, wa_ref, wb_ref, b1_ref, w2_ref, kc_ref, vc_ref, nch):
    lane = lax.broadcasted_iota(I32, (nch, LANES), 1)
    for kvi, out_ref in ((0, kc_ref), (1, vc_ref)):
        rows_ref = rows_refs[kvi]
        pa = jnp.zeros((nch, 2 * CMP_HID), F32)
        pb = jnp.zeros((nch, 2 * CMP_HID), F32)
        for j0 in range(0, CMP_STRIDE, 2):
            xs = [rows_ref[pl.ds(j, nch, stride=CMP_STRIDE), :] for j in (j0, j0 + 1)]
            xa = jnp.concatenate([xs[i] + pe_ref[kvi, 0, j0 + i:j0 + i + 1, :] for i in range(2)], axis=-1)
            xb = jnp.concatenate([xs[i] + pe_ref[kvi, 1, j0 + i:j0 + i + 1, :] for i in range(2)], axis=-1)
            wsl = slice(j0 * LANES, (j0 + 2) * LANES)
            pa += _dot(xa.astype(BF16), wa_ref[kvi, wsl, :])
            pb += _dot(xb.astype(BF16), wb_ref[kvi, wsl, :])
        hid = _gelu_tanh(pa + pltpu.roll(pb, nch - 1, 0) + b1_ref[kvi]).astype(BF16)
        for g in range(KV_HEADS):
            o = _dot(hid, w2_ref[kvi, g])
            if kvi == 1:
                o = jnp.where(lane == HEAD_DIM, 1.0, o).T
            out_ref[g] = o.astype(BF16)


def _compress_prompt_body(kvt_ref, pe_ref, wa_ref, wb_ref, b1_ref, w2_ref, kc_ref, vc_ref,
                          krows_ref, vrows_ref, *, nch):
    for c in range(kvt_ref.shape[1] // LANES):
        cs = slice(c * LANES, (c + 1) * LANES)
        krows_ref[cs, :] = kvt_ref[0:LANES, cs].T
        vrows_ref[cs, :] = kvt_ref[LANES:2 * LANES, cs].T
    _compress_rows((krows_ref, vrows_ref), pe_ref, wa_ref, wb_ref, b1_ref, w2_ref, kc_ref, vc_ref, nch)


def _compress_prompt_call(kv_t, cw, batch, seq):
    nch = seq // CMP_STRIDE
    full = lambda a: pl.BlockSpec(a.shape, lambda b: (0,) * a.ndim)
    out_spec = pl.BlockSpec((KV_HEADS, nch, LANES), lambda b: (0, b, 0))
    out_sds = jax.ShapeDtypeStruct((KV_HEADS, batch * nch, LANES), BF16)
    out_t_spec = pl.BlockSpec((KV_HEADS, None, LANES, nch), lambda b: (0, b, 0, 0))
    out_t_sds = jax.ShapeDtypeStruct((KV_HEADS, batch, LANES, nch), BF16)
    return pl.pallas_call(
        functools.partial(_compress_prompt_body, nch=nch),
        out_shape=(out_sds, out_t_sds),
        grid=(batch,),
        in_specs=[pl.BlockSpec((None, 2 * LANES, seq), lambda b: (b, 0, 0)),
                  full(cw["pe"]), full(cw["wa"]), full(cw["wb"]), full(cw["b1"]), full(cw["w2"])],
        out_specs=(out_spec, out_t_spec),
        scratch_shapes=[pltpu.VMEM((seq, LANES), F32)] * 2,
        compiler_params=_cparams(("parallel",)),
        name="compress_prompt",
    )(kv_t, cw["pe"], cw["wa"], cw["wb"], cw["b1"], cw["w2"])


def _prep_compress(p, l):
    eye = jnp.eye(KV_HEADS, dtype=F32)
    w1 = p["cmp_w1"][l].reshape(2, 2, CMP_STRIDE, HEAD_DIM, CMP_HID)
    wexp = jnp.einsum("khjdn,ge->khjgden", w1, eye).reshape(2, 2, CMP_STRIDE * LANES, KV_HEADS * CMP_HID)
    pe = p["cmp_pe"][l].reshape(2, 2, CMP_STRIDE, HEAD_DIM)
    w2 = p["cmp_w2"][l]
    zeros = jnp.zeros_like(w2[0])
    w2k = jnp.concatenate([zeros, w2[0]], axis=-1)
    w2v = jnp.concatenate([w2[1], zeros], axis=-1)
    w2e = jnp.stack([jnp.einsum("hd,ge->gehd", w, eye).reshape(KV_HEADS, KV_HEADS * CMP_HID, LANES)
                     for w in (w2k, w2v)])
    return {
        "pe": jnp.tile(pe, (1, 1, 1, KV_HEADS)).astype(F32),
        "wa": wexp[:, 0].astype(BF16),
        "wb": wexp[:, 1].astype(BF16),
        "b1": jnp.tile(p["cmp_b1"][l].reshape(2, 1, CMP_HID), (1, 1, KV_HEADS)).astype(F32),
        "w2": w2e.astype(BF16),
    }


def _overlap_matrix(n_tok_pad, n_tok, n_blk, n_cols=LANES):
    c_start = jnp.arange(n_tok_pad) * CMP_STRIDE
    blk = jnp.arange(n_cols)
    ov = ((c_start[:, None] < (blk[None, :] + 1) * SLC_BLOCK)
          & (c_start[:, None] + CMP_BLOCK > blk[None, :] * SLC_BLOCK)
          & (jnp.arange(n_tok_pad)[:, None] < n_tok) & (blk[None, :] < n_blk))
    return ov.astype(BF16)


def _topk_select_t(w_ref, imp_t, q0, tq):
    nb = imp_t.shape[0]
    n_i = lax.broadcasted_iota(I32, (nb, tq), 0)
    qblk = lax.shift_right_logical(q0 + lax.broadcasted_iota(I32, (nb, tq), 1), 6)
    causal = n_i <= qblk
    forced = (n_i == 0) | (n_i >= qblk - (N_LOCAL_BLOCKS - 1))
    w_ref[...] = jnp.where(causal, jnp.where(forced, jnp.inf, imp_t), -jnp.inf)
    last_blk = lax.shift_right_logical(q0 + tq - 1, 6)
    n_grp = nb // SUBLANES
    rank = [jnp.zeros((SUBLANES, tq), F32) for _ in range(n_grp)]
    grp_i = lax.broadcasted_iota(I32, (SUBLANES, tq), 0)

    def count_group(mg, rank):
        rank = list(rank)
        for mi in range(SUBLANES):
            m = mg * SUBLANES + mi
            wm = w_ref[m:m + 1, :]
            for ng in range(n_grp):
                w = w_ref[ng * SUBLANES:(ng + 1) * SUBLANES, :]
                if ng > mg:
                    beats = jnp.where(wm >= w, 1.0, 0.0)
                elif ng < mg:
                    beats = jnp.where(wm > w, 1.0, 0.0)
                else:
                    beats = jnp.where(grp_i > mi, jnp.where(wm >= w, 1.0, 0.0), jnp.where(wm > w, 1.0, 0.0))
                rank[ng] = rank[ng] + beats
        return tuple(rank)

    rank = tuple(rank)
    for mg in range(n_grp):
        rank = lax.cond(mg * SUBLANES <= last_blk, functools.partial(count_group, mg), lambda r: r, rank)
    return causal & (jnp.concatenate(rank, axis=0) < TOP_N)


def _flash_tiles_t(tiles, q_ts, ms, accs):
    def scores(i):
        k, _, mask = tiles[i]
        s = [_dot(k, q_t) for q_t in q_ts]
        return s if mask is None else [jnp.where(mask, x, NEG_INF) for x in s]

    ahead = 2
    pending = {i: scores(i) for i in range(min(ahead, len(tiles)))}
    for i, (_, v_t, _) in enumerate(tiles):
        s = pending.pop(i)
        m_new = [jnp.maximum(m, jnp.max(x, axis=0, keepdims=True)) for m, x in zip(ms, s)]
        alpha = [jnp.exp2(m - mn) for m, mn in zip(ms, m_new)]
        p = [jnp.exp2(x - mn).astype(BF16) for x, mn in zip(s, m_new)]
        pv = [_dot(v_t, x) for x in p]
        if i + ahead < len(tiles):
            pending[i + ahead] = scores(i + ahead)
        accs = [a * acc + x for a, acc, x in zip(alpha, accs, pv)]
        ms = m_new
    return tuple(ms), tuple(accs)


def _attn_body(qt_ref, kc_ref, vct_ref, ovlt_ref, ks_ref, vst_ref, kw_ref, vwt_ref, gate_ref, o_ref,
               qa_ref, ocmp_ref, w_ref, *, tq, ncp):
    R = Q_PER_KV
    qt = pl.program_id(2)
    q0 = qt * tq
    nbs = LANES // 2

    c_i = lax.broadcasted_iota(I32, (ncp, tq), 0)
    qpos_c = q0 + lax.broadcasted_iota(I32, (ncp, tq), 1)
    cmask = c_i * CMP_STRIDE + (CMP_BLOCK - 1) <= qpos_c
    kc = kc_ref[...]
    s = [jnp.where(cmask, _dot(kc, qt_ref[r]), NEG_INF) for r in range(R)]
    e = [jnp.where(cmask, jnp.exp2(x - jnp.max(x, axis=0, keepdims=True)), 0.0) for x in s]
    l = [jnp.sum(x, axis=0, keepdims=True) for x in e]
    p = [(x * (1.0 / jnp.where(y > 0.0, y, 1.0))).astype(BF16) for x, y in zip(e, l)]
    imp = sum(_dot(ovlt_ref[...], x) for x in p)
    for r in range(R):
        ocmp_ref[r] = _dot(vct_ref[...], p[r])

    sel = _topk_select_t(w_ref, imp[:nbs], q0, tq)
    bias = jnp.concatenate([jnp.where(sel, 0.0, NEG_INF), jnp.zeros((LANES - nbs, tq), F32)], axis=0)
    for r in range(R):
        qa_ref[r] = (qt_ref[r].astype(F32) + bias).astype(BF16)

    key_i = lax.broadcasted_iota(I32, (tq, tq), 0)
    qry_i = lax.broadcasted_iota(I32, (tq, tq), 1)
    init = (tuple(jnp.full((1, tq), -jnp.inf, F32) for _ in range(R)),
            tuple(jnp.zeros((LANES, tq), F32) for _ in range(R)))

    def tiles(k_ref, vt_ref, q_ref, js, masks, state):
        ts = [(k_ref[pl.ds(pl.multiple_of(j * tq, tq), tq), :], vt_ref[j], mask) for j, mask in zip(js, masks)]
        return _flash_tiles_t(ts, [q_ref[r] for r in range(R)], *state)

    def last_tiles(k_ref, vt_ref, q_ref, n, first_mask, state):
        js = [qt - (n - 1 - t) for t in range(n)]
        masks = [first_mask] + [None] * (n - 2) + [key_i <= qry_i] if n > 1 else [key_i <= qry_i]
        return tiles(k_ref, vt_ref, q_ref, js, masks, state)

    GROUP = 4

    def slc_group(i, st):
        return tiles(ks_ref, vst_ref, qa_ref, [GROUP * i + t for t in range(GROUP)], [None] * GROUP, st)

    state = lax.fori_loop(0, qt // GROUP, slc_group, init)
    tails = [functools.partial(last_tiles, ks_ref, vst_ref, qa_ref, n, None) for n in range(1, GROUP + 1)]
    _, acc_s = lax.switch(qt % GROUP, tails, state)

    nwin = WINDOW // tq
    wins = [functools.partial(last_tiles, kw_ref, vwt_ref, qt_ref, n, (key_i > qry_i) if n == nwin + 1 else None)
            for n in range(1, nwin + 2)]
    _, acc_w = lax.switch(jnp.minimum(qt, nwin), wins, init)

    outs = []
    for r in range(R):
        a_s = acc_s[r]
        a_w = acc_w[r]
        g = lambda k: gate_ref[3 * r + k:3 * r + k + 1, :]
        o = (g(0) * ocmp_ref[r] + g(1) * (a_s / a_s[HEAD_DIM:HEAD_DIM + 1, :])
             + g(2) * (a_w / a_w[HEAD_DIM:HEAD_DIM + 1, :]))
        outs.append(o[:HEAD_DIM])
    o_ref[...] = jnp.concatenate(outs, axis=0).T


def _attn_call(q_t, kc, vc_t, ovl_t, ks, vs_t, kw, vw_t, gates_t, batch, seq):
    tq = min(ATTN_TILE, seq)
    nq = seq // tq
    ncp = kc.shape[1] // batch
    R = Q_PER_KV
    assert seq // SLC_BLOCK <= LANES // 2 and WINDOW % tq == 0
    k_spec = pl.BlockSpec((None, seq, LANES), lambda b, g, t: (g, b, 0))
    vt_spec = pl.BlockSpec((None, nq, LANES, tq), lambda b, g, t: (g, b, 0, 0))
    acc = pltpu.VMEM((R, LANES, tq), F32)
    return pl.pallas_call(
        functools.partial(_attn_body, tq=tq, ncp=ncp),
        out_shape=jax.ShapeDtypeStruct((batch * seq, NSA_W), F32),
        grid=(batch, KV_HEADS, nq),
        in_specs=[pl.BlockSpec((R, LANES, tq), lambda b, g, t: (g, 0, b * nq + t)),
                  pl.BlockSpec((None, ncp, LANES), lambda b, g, t: (g, b, 0)),
                  pl.BlockSpec((None, None, LANES, ncp), lambda b, g, t: (g, b, 0, 0)),
                  pl.BlockSpec(ovl_t.shape, lambda b, g, t: (0, 0)),
                  k_spec, vt_spec, k_spec, vt_spec,
                  pl.BlockSpec((None, 2 * SUBLANES, tq), lambda b, g, t: (g, 0, b * nq + t))],
        out_specs=pl.BlockSpec((tq, R * HEAD_DIM), lambda b, g, t: (b * nq + t, g)),
        scratch_shapes=[pltpu.VMEM((R, LANES, tq), BF16), acc, pltpu.VMEM((LANES // 2, tq), F32)],
        compiler_params=_cparams(("parallel", "parallel", "arbitrary")),
        name="nsa_prompt",
    )(q_t, kc, vc_t, ovl_t, ks, vs_t, kw, vw_t, gates_t)


def _s5_discretise(p, l):
    lr = p["ssm_lam_re"][l].astype(F32)
    li = p["ssm_lam_im"][l].astype(F32)
    dt = jnp.exp(p["ssm_log_step"][l].astype(F32))[:, None]

    def apow(t):
        mag, ang = jnp.exp(lr * dt * t), li * dt * t
        return mag * jnp.cos(ang), mag * jnp.sin(ang)

    a_re, a_im = apow(1.0)
    den = lr * lr + li * li
    nr, ni = a_re - 1.0, a_im
    f_re, f_im = (nr * lr + ni * li) / den, (ni * lr - nr * li) / den
    br, bi = p["ssm_b_re"][l].astype(F32), p["ssm_b_im"][l].astype(F32)
    bb_re = f_re[..., None] * br - f_im[..., None] * bi
    bb_im = f_re[..., None] * bi + f_im[..., None] * br
    return apow, bb_re, bb_im


def _s5_expand_body(pst_ref, kt_ref, cp_ref, wcol_ref, wst_ref, wout_ref):
    T, E, N, C = SSM_CHUNK, SSM_SLAB_GROUPS, SSM_STATE, SSM_GROUP
    row = lax.broadcasted_iota(I32, (LANES, LANES), 0)
    lane = lax.broadcasted_iota(I32, (LANES, LANES), 1)
    row_grp = row // C
    low = lane < N
    own_grp = row_grp == lane // C
    lane_n = lax.broadcasted_iota(I32, (N, LANES), 1)
    for i in range(T):
        wcol_ref[i * LANES:(i + 1) * LANES, :] = jnp.where(own_grp, kt_ref[i], 0.0).astype(BF16)
        x = pst_ref[i]
        x_sw = pltpu.roll(x, N, 1)
        halves = (jnp.where(low, x, x_sw), jnp.where(low, x_sw, x))
        for ri, x2 in enumerate(halves):
            for q in range(E // 2):
                own = row_grp == 2 * q + jnp.where(low, 0, 1)
                c0 = ri * E * N + q * LANES
                wst_ref[i * LANES:(i + 1) * LANES, c0:c0 + LANES] = jnp.where(own, x2, 0.0).astype(BF16)
    for i in range(T + 1):
        m_t = cp_ref[i].T
        for ri in range(2):
            rows = m_t[ri * N:(ri + 1) * N]
            for g in range(E):
                r0 = ri * E * N + g * N
                wout_ref[i, r0:r0 + N, :] = jnp.where(lane_n // C == g, rows, 0.0).astype(BF16)


def _prep_s5_chunked(p, l):
    hi = lax.Precision.HIGHEST
    apow, bb_re, bb_im = _s5_discretise(p, l)
    c_re, c_im = p["ssm_c_re"][l].astype(F32), p["ssm_c_im"][l].astype(F32)
    T, J, E, N, C = SSM_CHUNK, SSM_SLABS, SSM_SLAB_GROUPS, SSM_STATE, SSM_GROUP
    pw_re, pw_im = apow(jnp.arange(T + 1, dtype=F32)[:, None, None])
    bt_re, bt_im = bb_re.transpose(0, 2, 1), bb_im.transpose(0, 2, 1)
    p_re = pw_re[:, :, None, :] * bt_re - pw_im[:, :, None, :] * bt_im
    p_im = pw_re[:, :, None, :] * bt_im + pw_im[:, :, None, :] * bt_re
    kt = (jnp.einsum("tgkn,gcn->tgkc", p_re[:T], jnp.tile(c_re, (1, E, 1)), precision=hi)
          - jnp.einsum("tgkn,gcn->tgkc", p_im[:T], jnp.tile(c_im, (1, E, 1)), precision=hi))
    cp_re = c_re * pw_re[:, :, None, :] - c_im * pw_im[:, :, None, :]
    cp_im = c_re * pw_im[:, :, None, :] + c_im * pw_re[:, :, None, :]
    slab = lambda x: jnp.moveaxis(x.reshape(x.shape[0], J, E * C, LANES), 1, 0)
    pst = slab(jnp.concatenate([p_re[:T][::-1], p_im[:T][::-1]], axis=-1))
    ktile = slab(kt[::-1])
    cpn = slab(jnp.concatenate([cp_re, -cp_im], axis=-1))
    blk = lambda n: pl.BlockSpec((None, n, LANES, LANES), lambda j: (j, 0, 0, 0))
    w_col, w_st, w_out = pl.pallas_call(
        _s5_expand_body,
        out_shape=(jax.ShapeDtypeStruct((J, T * LANES, LANES), BF16),
                   jax.ShapeDtypeStruct((J, T * LANES, 2 * E * N), BF16),
                   jax.ShapeDtypeStruct((J, T + 1, 2 * E * N, LANES), BF16)),
        grid=(J,),
        in_specs=[blk(T), blk(T), blk(T + 1)],
        out_specs=(pl.BlockSpec((None, T * LANES, LANES), lambda j: (j, 0, 0)),
                   pl.BlockSpec((None, T * LANES, 2 * E * N), lambda j: (j, 0, 0)),
                   pl.BlockSpec((None, T + 1, 2 * E * N, LANES), lambda j: (j, 0, 0, 0))),
        compiler_params=_cparams(("parallel",)),
        name="s5_expand",
    )(pst, ktile, cpn)
    return {
        "w_col": w_col, "w_st": w_st, "w_out": w_out,
        "a_re": pw_re[T].reshape(J, 1, E * N), "a_im": pw_im[T].reshape(J, 1, E * N),
        "a1_re": pw_re[1].reshape(J, 1, E * N), "a1_im": pw_im[1].reshape(J, 1, E * N),
        "d": p["ssm_d"][l].reshape(J, 1, LANES).astype(F32),
    }


def _s5_prompt_body(u_ref, wcol_ref, wst_ref, are_ref, aim_ref, wout_ref, d_ref,
                    y_ref, hre_ref, him_ref, xs_ref, hp_ref, *, n_chunks):
    T = SSM_CHUNK
    ns = SSM_SLAB_GROUPS * SSM_STATE
    u_pos = [u_ref[pl.ds(s, n_chunks, stride=T), :] for s in range(T)]
    ub = jnp.concatenate(u_pos, axis=-1).astype(BF16)
    xs_ref[...] = _dot(ub, wst_ref[...])
    a_re = are_ref[...]
    a_im = aim_ref[...]

    def step(c, carry):
        hr, hi = carry
        hp_ref[pl.ds(c, 1), 0:ns] = hr
        hp_ref[pl.ds(c, 1), ns:2 * ns] = hi
        xr = xs_ref[pl.ds(c, 1), 0:ns]
        xi = xs_ref[pl.ds(c, 1), ns:2 * ns]
        return a_re * hr - a_im * hi + xr, a_re * hi + a_im * hr + xi

    zero = jnp.zeros((1, ns), F32)
    hr, hi = lax.fori_loop(0, n_chunks, step, (zero, zero))
    hre_ref[...] = jnp.broadcast_to(hr, hre_ref.shape)
    him_ref[...] = jnp.broadcast_to(hi, him_ref.shape)
    hpb = hp_ref[...].astype(BF16)
    for t in range(T):
        y_ref[pl.ds(t, n_chunks, stride=T), :] = (
            _dot(ub[:, :(t + 1) * LANES], wcol_ref[(T - 1 - t) * LANES:, :])
            + _dot(hpb, wout_ref[t + 1]) + d_ref[...] * u_pos[t])


def _s5_prompt_call(u_slab, sw, batch, seq):
    T, J = SSM_CHUNK, SSM_SLABS
    n_chunks = seq // T
    ns = SSM_SLAB_GROUPS * SSM_STATE
    row_spec = pl.BlockSpec((None, seq, LANES), lambda j, b: (j, b, 0))
    slab_spec = lambda a: pl.BlockSpec((None,) + a.shape[1:], lambda j, b: (j,) + (0,) * (a.ndim - 1))
    st_spec = pl.BlockSpec((None, None, SUBLANES, ns), lambda j, b: (b, j, 0, 0))
    st_sds = jax.ShapeDtypeStruct((batch, J, SUBLANES, ns), F32)
    y, hre, him = pl.pallas_call(
        functools.partial(_s5_prompt_body, n_chunks=n_chunks),
        out_shape=(jax.ShapeDtypeStruct(u_slab.shape, F32), st_sds, st_sds),
        grid=(J, batch),
        in_specs=[row_spec, slab_spec(sw["w_col"]), slab_spec(sw["w_st"]), slab_spec(sw["a_re"]),
                  slab_spec(sw["a_im"]), slab_spec(sw["w_out"]), slab_spec(sw["d"])],
        out_specs=(row_spec, st_spec, st_spec),
        scratch_shapes=[pltpu.VMEM((n_chunks, 2 * ns), F32), pltpu.VMEM((n_chunks, 2 * ns), F32)],
        compiler_params=_cparams(("parallel", "parallel")),
        name="s5_prompt",
    )(u_slab, sw["w_col"], sw["w_st"], sw["a_re"], sw["a_im"], sw["w_out"], sw["d"])
    n_groups = J * SSM_SLAB_GROUPS
    state = lambda h: h[:, :, 0, :].reshape(batch, n_groups, SSM_STATE)
    return y, state(hre), state(him)


def _s5_step_body(u_ref, wx_ref, are_ref, aim_ref, h0re_ref, h0im_ref, wy_ref, d_ref,
                  y_ref, hre_ref, him_ref):
    ns = SSM_SLAB_GROUPS * SSM_STATE
    for j in range(SSM_SLABS):
        sl = slice(j * ns, (j + 1) * ns)
        u = u_ref[j]
        x = _dot(u.astype(BF16), wx_ref[j])
        a_re, a_im = are_ref[j], aim_ref[j]
        h0r, h0i = h0re_ref[:, sl], h0im_ref[:, sl]
        hr = a_re * h0r - a_im * h0i + x[:, :ns]
        hi = a_re * h0i + a_im * h0r + x[:, ns:]
        hre_ref[:, sl] = hr
        him_ref[:, sl] = hi
        y_ref[j] = _dot(jnp.concatenate([hr, hi], axis=-1).astype(BF16), wy_ref[j]) + d_ref[j] * u


def _s5_step_call(u_slab, h0_re, h0_im, sw):
    J, n_tok, _ = u_slab.shape
    T = SSM_CHUNK
    full = lambda a: pl.BlockSpec(a.shape, lambda i: (0,) * a.ndim)
    wx_spec = pl.BlockSpec((J, LANES, sw["w_st"].shape[2]), lambda i: (0, T - 1, 0))
    wy_spec = pl.BlockSpec((J, None) + sw["w_out"].shape[2:], lambda i: (0, 0, 0, 0))
    st_sds = jax.ShapeDtypeStruct(h0_re.shape, F32)
    return pl.pallas_call(
        _s5_step_body,
        out_shape=(jax.ShapeDtypeStruct(u_slab.shape, F32), st_sds, st_sds),
        grid=(1,),
        in_specs=[full(u_slab), wx_spec, full(sw["a1_re"]), full(sw["a1_im"]), full(h0_re), full(h0_im),
                  wy_spec, full(sw["d"])],
        out_specs=(full(u_slab), full(h0_re), full(h0_re)),
        compiler_params=_cparams(("arbitrary",)),
        name="s5_step",
    )(u_slab, sw["w_st"], sw["a1_re"], sw["a1_im"], h0_re, h0_im, sw["w_out"], sw["d"])


def _compress_sample_body(pt_ref, *refs, n_pages, page_rows, nch):
    del pt_ref
    page_refs = refs[:n_pages]
    pe_ref, wa_ref, wb_ref, b1_ref, w2_ref, kc_ref, vc_ref, krows_ref, vrows_ref = refs[n_pages:]
    b = pl.program_id(0)

    @pl.when(b == 0)
    def _():
        krows_ref[...] = jnp.zeros(krows_ref.shape, F32)
        vrows_ref[...] = jnp.zeros(vrows_ref.shape, F32)

    stage = b % 2
    for i, page_ref in enumerate(page_refs):
        krows_ref[stage, i * page_rows:(i + 1) * page_rows, :] = page_ref[0].reshape(LANES, page_rows).T
        vrows_ref[stage, i * page_rows:(i + 1) * page_rows, :] = page_ref[1].reshape(LANES, page_rows).T
    done = 1 - stage
    _compress_rows((krows_ref.at[done], vrows_ref.at[done]), pe_ref, wa_ref, wb_ref, b1_ref, w2_ref,
                   kc_ref, vc_ref, nch)


def _compress_sample_call(cache, page_table, cw):
    n_seq, n_pages = page_table.shape
    page_rows = cache.shape[-1]
    assert page_rows == LANES
    nch = n_pages * page_rows // CMP_STRIDE
    staged = lambda b: jnp.minimum(b, n_seq - 1)
    page_spec = lambda i: pl.BlockSpec((None, 2, KV_HEADS, HEAD_DIM, page_rows),
                                       lambda b, pt: (pt[staged(b), i], 0, 0, 0, 0))
    full = lambda a: pl.BlockSpec(a.shape, lambda b, pt: (0,) * a.ndim)
    out_of = lambda b: jnp.maximum(b - 1, 0)
    out_spec = pl.BlockSpec((KV_HEADS, nch, LANES), lambda b, pt: (0, out_of(b), 0))
    out_sds = jax.ShapeDtypeStruct((KV_HEADS, n_seq * nch, LANES), BF16)
    out_t_spec = pl.BlockSpec((KV_HEADS, None, LANES, nch), lambda b, pt: (0, out_of(b), 0, 0))
    out_t_sds = jax.ShapeDtypeStruct((KV_HEADS, n_seq, LANES, nch), BF16)
    weights = (cw["pe"], cw["wa"], cw["wb"], cw["b1"], cw["w2"])
    return pl.pallas_call(
        functools.partial(_compress_sample_body, n_pages=n_pages, page_rows=page_rows, nch=nch),
        out_shape=(out_sds, out_t_sds),
        grid_spec=pltpu.PrefetchScalarGridSpec(
            num_scalar_prefetch=1,
            grid=(n_seq + 1,),
            in_specs=[page_spec(i) for i in range(n_pages)] + [full(a) for a in weights],
            out_specs=(out_spec, out_t_spec),
            scratch_shapes=[pltpu.VMEM((2, n_pages * page_rows, LANES), F32)] * 2),
        compiler_params=_cparams(("arbitrary",)),
        name="compress_sample",
    )(page_table, *([cache] * n_pages), *weights)


def _group_rows(x0, x1):
    row = lax.broadcasted_iota(I32, x0.shape, 0)
    return jnp.where(row < Q_PER_KV, x0, x1)


def _sample_select_body(q_ref, kc_ref, vct_ref, ovl_ref, tri_ref, ocmp_ref, idx_ref, *, ncp, qpos, nbp):
    q8 = q_ref[...]
    c_i = lax.broadcasted_iota(I32, (N_HEADS, ncp), 1)
    cmask = c_i * CMP_STRIDE + (CMP_BLOCK - 1) <= qpos
    s = _group_rows(_dot_nt(q8, kc_ref[0]), _dot_nt(q8, kc_ref[1]))
    s = jnp.where(cmask, s, NEG_INF)
    e = jnp.where(cmask, jnp.exp2(s - jnp.max(s, axis=-1, keepdims=True)), 0.0)
    l = jnp.sum(e, axis=-1, keepdims=True)
    p = (e / jnp.where(l > 0.0, l, 1.0)).astype(BF16)
    ocmp_ref[...] = _group_rows(_dot_nt(p, vct_ref[0]), _dot_nt(p, vct_ref[1]))
    imp8 = _dot(p, ovl_ref[...])

    n_row = lax.broadcasted_iota(I32, (1, nbp), 1)
    qblk = qpos // SLC_BLOCK
    causal = n_row <= qblk
    forced = (n_row == 0) | (n_row >= qblk - (N_LOCAL_BLOCKS - 1))
    m_i = lax.broadcasted_iota(I32, (nbp, nbp), 0)
    n_i = lax.broadcasted_iota(I32, (nbp, nbp), 1)
    lane = lax.broadcasted_iota(I32, (1, LANES), 1)
    idx_rows = []
    for g in range(KV_HEADS):
        imp = jnp.sum(imp8[g * Q_PER_KV:(g + 1) * Q_PER_KV], axis=0, keepdims=True)
        w = jnp.where(causal, jnp.where(forced, jnp.inf, imp), -jnp.inf)
        w_sq = jnp.broadcast_to(w, (nbp, nbp))
        w_col = w_sq.T
        beats = jnp.where(n_i > m_i, jnp.where(w_col >= w_sq, 1.0, 0.0), jnp.where(w_col > w_sq, 1.0, 0.0))
        rank = jnp.sum(beats, axis=0, keepdims=True)
        sel = causal & (rank < TOP_N)
        self_f = jnp.where(sel, 1.0, 0.0)
        before = _dot(self_f.astype(BF16), tri_ref[...])
        idx = jnp.full((1, LANES), -1, I32)
        for k in range(TOP_N):
            hit = sel & (before == float(k))
            val = jnp.sum(jnp.where(hit, n_row.astype(F32) + 1.0, 0.0), axis=-1, keepdims=True) - 1.0
            idx = jnp.where(lane == k, val.astype(I32), idx)
        idx_rows.append(idx)
    idx_ref[...] = jnp.concatenate(idx_rows + [jnp.full((SUBLANES - KV_HEADS, LANES), -1, I32)], axis=0)


def _sample_select_call(q8, kc, vc_t, ovl, tri, qpos):
    n_seq = q8.shape[0]
    ncp = kc.shape[1] // n_seq
    nbp = ovl.shape[1]
    cmp_spec = pl.BlockSpec((KV_HEADS, ncp, LANES), lambda b: (0, b, 0))
    row_spec = pl.BlockSpec((None, N_HEADS, LANES), lambda b: (b, 0, 0))
    full = lambda a: pl.BlockSpec(a.shape, lambda b: (0,) * a.ndim)
    return pl.pallas_call(
        functools.partial(_sample_select_body, ncp=ncp, qpos=qpos, nbp=nbp),
        out_shape=(jax.ShapeDtypeStruct((n_seq, N_HEADS, LANES), F32),
                   jax.ShapeDtypeStruct((n_seq, SUBLANES, LANES), I32)),
        grid=(n_seq,),
        in_specs=[row_spec, cmp_spec, pl.BlockSpec((KV_HEADS, None, LANES, ncp), lambda b: (0, b, 0, 0)),
                  full(ovl), full(tri)],
        out_specs=(row_spec, pl.BlockSpec((None, SUBLANES, LANES), lambda b: (b, 0, 0))),
        compiler_params=_cparams(("parallel",)),
        name="nsa_sample_select",
    )(q8, kc, vc_t, ovl, tri)


def _sample_attend_body(idx_ref, pt_ref, q_ref, ocmp_ref, gate_ref, ksn_ref, vsn_ref, kwn_ref, vwn_ref,
                        win_ref, *refs, n_cache_blocks, blocks_per_page, win_skip):
    del pt_ref
    n_blk = KV_HEADS * TOP_N
    kv_refs, o_ref = refs[:n_blk], refs[n_blk]
    b = pl.program_id(0)
    q = q_ref[...]
    qf = q.astype(F32)
    row_g = (lax.broadcasted_iota(I32, (N_HEADS, 1), 0) >= Q_PER_KV).astype(I32)

    def attend(s_list, v_list, kn_ref, vn_ref):
        s_self = jnp.sum(qf * kn_ref[...].astype(F32), axis=-1, keepdims=True)
        m = s_self
        for s in s_list:
            m = jnp.maximum(m, jnp.max(s, axis=-1, keepdims=True))
        p_self = jnp.exp2(s_self - m)
        l = p_self
        acc = p_self.astype(BF16).astype(F32) * vn_ref[...].astype(F32)
        for s, v in zip(s_list, v_list):
            p = jnp.exp2(s - m)
            l = l + jnp.sum(p, axis=-1, keepdims=True)
            acc = acc + _dot_nt(p.astype(BF16), v().astype(BF16))
        return acc / l

    s_list, v_list = [], []
    for j in range(n_blk):
        s = _dot(q, kv_refs[j][0].astype(BF16))
        col = lax.broadcasted_iota(I32, s.shape, 1)
        n = idx_ref[b, j]
        first = (n & (blocks_per_page - 1)) * SLC_BLOCK
        ok = ((row_g == j // TOP_N) & (col >= first) & (col < first + SLC_BLOCK)
              & (n >= 0) & (n < n_cache_blocks))
        s_list.append(jnp.where(ok, s, NEG_INF))
        v_list.append(lambda j=j: kv_refs[j][1])
    o_slc = attend(s_list, v_list, ksn_ref, vsn_ref)

    s_list, v_list = [], []
    for g in range(KV_HEADS):
        s = _dot(q, win_ref[0, g].astype(BF16))
        col = lax.broadcasted_iota(I32, s.shape, 1)
        s_list.append(jnp.where((row_g == g) & (col >= win_skip), s, NEG_INF))
        v_list.append(lambda g=g: win_ref[1, g])
    o_win = attend(s_list, v_list, kwn_ref, vwn_ref)

    gates = gate_ref[...]
    o_ref[...] = gates[:, 0:1] * ocmp_ref[:, 0:HEAD_DIM] + gates[:, 1:2] * o_slc + gates[:, 2:3] * o_win


def _sample_attend_call(idx, page_table, q64, ocmp, gates8, ksn, vsn, kwn, vwn, cache_win_t, cache_t, win_skip):
    n_seq, n_pages = page_table.shape
    page_rows = cache_t.shape[-1]
    bpp = page_rows // SLC_BLOCK
    n_cache_blocks = n_pages * bpp
    row_spec = lambda a: pl.BlockSpec((None,) + a.shape[1:], lambda b, ix, pt: (b,) + (0,) * (a.ndim - 1))

    assert bpp & (bpp - 1) == 0
    bpp_shift = bpp.bit_length() - 1

    def blk_spec(j):
        def index_map(b, ix, pt):
            n = jnp.minimum(jnp.maximum(ix[b, j], 0), n_cache_blocks - 1)
            return (pt[b, lax.shift_right_logical(n, bpp_shift)], 1, j // TOP_N, 0, 0)
        return pl.BlockSpec((None, 2, None, HEAD_DIM, page_rows), index_map)

    n_blk = KV_HEADS * TOP_N
    small = (q64, ocmp, gates8, ksn, vsn, kwn, vwn, cache_win_t)
    return pl.pallas_call(
        functools.partial(_sample_attend_body, n_cache_blocks=n_cache_blocks, blocks_per_page=bpp,
                          win_skip=win_skip),
        out_shape=jax.ShapeDtypeStruct((n_seq, N_HEADS, HEAD_DIM), F32),
        grid_spec=pltpu.PrefetchScalarGridSpec(
            num_scalar_prefetch=2,
            grid=(n_seq,),
            in_specs=[row_spec(a) for a in small] + [blk_spec(j) for j in range(n_blk)],
            out_specs=pl.BlockSpec((None, N_HEADS, HEAD_DIM), lambda b, ix, pt: (b, 0, 0))),
        compiler_params=_cparams(("parallel",)),
        name="nsa_sample_attend",
    )(idx, page_table, *small, *([cache_t] * n_blk))


def _round_up(x, m):
    return -(-x // m) * m


def _prompt_layer(h, prm, cw, sw, batch, seq):
    tabs = _rope_tables(jnp.arange(seq))
    u, gs, gn, q_t, kv_t, win_t, gates_t, ks, vs_t, kw, vw_t = _proj_call(
        h, prm["w_in"], prm["norm_w"], prm["qnw"], prm["knw"], prm["gb"], tabs, batch, seq)
    y_ssm, h_re, h_im = _s5_prompt_call(u, sw, batch, seq)
    kc, vc_t = _compress_prompt_call(kv_t, cw, batch, seq)
    nch = seq // CMP_STRIDE
    ovl_t = _overlap_matrix(nch, nch - 1, seq // SLC_BLOCK).T
    o = _attn_call(q_t, kc, vc_t, ovl_t, ks, vs_t, kw, vw_t, gates_t, batch, seq)
    h_new = _outmix_call(h, y_ssm, gs, o, gn, prm["w_glu"], prm["w_out"])
    rows = lambda x_t, slots: x_t.reshape(batch, slots, KV_HEADS, HEAD_DIM, seq).transpose(0, 4, 1, 2, 3)
    return h_new, rows(kv_t, 4), rows(win_t, 2)[:, seq - min(WINDOW, seq):], h_re, h_im


def _sample_layer(h, prm, cw, sw, cache_kv, cache_win, st_re, st_im, page_table):
    n_seq = h.shape[0]
    n_phys, page_rows = cache_kv.shape[:2]
    n_pages = page_table.shape[1]
    past_len = n_pages * page_rows
    win_buf = cache_win.shape[1]
    n_pad = _round_up(n_seq, LANES)
    tabs = _rope_tables(jnp.full((n_pad,), past_len, I32))
    h_pad = jnp.pad(h, ((0, n_pad - n_seq), (0, 0)))
    u, gs, gn, q_t, kv_t, win_t, gates_t, ks, vs_t, kw, vw_t = _proj_call(
        h_pad, prm["w_in"], prm["norm_w"], prm["qnw"], prm["knw"], prm["gb"], tabs, 1, n_pad)
    u, gs, gn = u[:, :n_seq], gs[:n_seq], gn[:n_seq]
    n_state = st_re.shape[1] * st_re.shape[2]
    y_ssm, h_re, h_im = _s5_step_call(u, st_re.reshape(n_seq, n_state), st_im.reshape(n_seq, n_state), sw)
    cache_t = cache_kv.transpose(0, 2, 3, 4, 1)
    kc, vc_t = _compress_sample_call(cache_t, page_table, cw)
    ncp = past_len // CMP_STRIDE
    n_blk = -(-(past_len + 1) // SLC_BLOCK)
    nbp = _round_up(n_blk, LANES)
    ovl = _overlap_matrix(ncp, ncp - 1, n_blk, nbp)
    tri = (jnp.arange(nbp)[:, None] < jnp.arange(nbp)[None, :]).astype(BF16)
    q8 = q_t[:, :, :n_seq].transpose(2, 0, 1)
    ocmp, idx = _sample_select_call(q8, kc, vc_t, ovl, tri, past_len)
    idx = idx[:, :KV_HEADS, :TOP_N].reshape(n_seq, KV_HEADS * TOP_N)
    gates8 = gates_t[:, :3 * Q_PER_KV, :n_seq].reshape(KV_HEADS, Q_PER_KV, 3, n_seq).transpose(3, 0, 1, 2)
    gates8 = jnp.pad(gates8.reshape(n_seq, N_HEADS, 3), ((0, 0), (0, 0), (0, LANES - 3)))
    per_head = lambda a: jnp.repeat(a.transpose(1, 0, 2), Q_PER_KV, axis=1)
    new_k = lambda k: per_head(k[:, :n_seq, HEAD_DIM:])
    new_v = lambda v_t: per_head(v_t[:, 0, :HEAD_DIM, :n_seq].transpose(0, 2, 1))
    o8 = _sample_attend_call(idx, page_table, q8[:, :, HEAD_DIM:], ocmp, gates8, new_k(ks), new_v(vs_t),
                             new_k(kw), new_v(vw_t), cache_win.transpose(0, 2, 3, 4, 1), cache_t,
                             max(win_buf + 1 - WINDOW, 0))
    o = o8.reshape(n_seq, NSA_W)
    h_new = _outmix_call(h, y_ssm, gs, o, gn, prm["w_glu"], prm["w_out"])
    kv_rows = kv_t[0, :, :n_seq].T.reshape(n_seq, 1, 4, KV_HEADS, HEAD_DIM)
    win_new = win_t[0, :, :n_seq].T.reshape(n_seq, 1, 2, KV_HEADS, HEAD_DIM)
    wrows = jnp.concatenate([cache_win, win_new], axis=1)
    wrows = wrows[:, wrows.shape[1] - min(WINDOW, wrows.shape[1]):]
    state = lambda s: s.reshape(st_re.shape)
    return h_new, kv_rows, wrows, state(h_re), state(h_im)


def kernel(x_prompt, x_sample, cache_kv, cache_win, state_ssm_re, state_ssm_im, page_table, norm_w, w_in, gate_b,
           q_norm_w, k_norm_w, cmp_pe, cmp_w1, cmp_b1, cmp_w2, ssm_lam_re, ssm_lam_im, ssm_log_step, ssm_b_re,
           ssm_b_im, ssm_c_re, ssm_c_im, ssm_d, w_glu, w_out):
    p = dict(norm_w=norm_w, w_in=w_in, gate_b=gate_b, q_norm_w=q_norm_w, k_norm_w=k_norm_w, cmp_pe=cmp_pe,
             cmp_w1=cmp_w1, cmp_b1=cmp_b1, cmp_w2=cmp_w2, ssm_lam_re=ssm_lam_re, ssm_lam_im=ssm_lam_im,
             ssm_log_step=ssm_log_step, ssm_b_re=ssm_b_re, ssm_b_im=ssm_b_im, ssm_c_re=ssm_c_re,
             ssm_c_im=ssm_c_im, ssm_d=ssm_d, w_glu=w_glu, w_out=w_out)
    b_p, s_p, d_model = x_prompt.shape
    b_s, s_s, _ = x_sample.shape
    assert s_s == 1, "the sample group decodes one token per sequence"
    h_p = x_prompt.reshape(b_p * s_p, d_model)
    h_s = x_sample.reshape(b_s, d_model)
    outs_p, outs_s = [], []
    for l in range(norm_w.shape[0]):
        weights = (_prep_params(p, l), _prep_compress(p, l), _prep_s5_chunked(p, l))
        h_p, *rest_p = _prompt_layer(h_p, *weights, b_p, s_p)
        h_s, *rest_s = _sample_layer(h_s, *weights, cache_kv[l], cache_win[l], state_ssm_re[l],
                                     state_ssm_im[l], page_table)
        outs_p.append(rest_p)
        outs_s.append(rest_s)
    stack = lambda outs, i: jnp.stack([o[i] for o in outs])
    return (h_p.reshape(x_prompt.shape), h_s.reshape(x_sample.shape),
            stack(outs_p, 0), stack(outs_s, 0), stack(outs_p, 1), stack(outs_s, 1),
            stack(outs_p, 2), stack(outs_p, 3), stack(outs_s, 2), stack(outs_s, 3))
```

```python
import functools
import math

import jax
import jax.numpy as jnp
from jax import lax
from jax.experimental import pallas as pl
from jax.experimental.pallas import tpu as pltpu

F32 = jnp.float32
BF16 = jnp.bfloat16
I32 = jnp.int32

LANES = 128
SUBLANES = 8
VMEM_LIMIT_BYTES = 56 * 1024 * 1024

HEAD_DIM = 64
N_HEADS = 8
KV_HEADS = 2
Q_PER_KV = N_HEADS // KV_HEADS
SSM_W = 512
SSM_GROUP = 16
SSM_STATE = 64
NSA_W = N_HEADS * HEAD_DIM
CMP_BLOCK = 32
CMP_STRIDE = 16
CMP_HID = 2 * HEAD_DIM
SLC_BLOCK = 64
TOP_N = 16
N_LOCAL_BLOCKS = 2
WINDOW = 512
ROPE_THETA = 500000.0
ROPE_DIM = HEAD_DIM // 4
RMS_EPS = 1e-6
NEG_INF = -1e30

COL_U = 0
COL_GS = SSM_W
COL_Q = 2 * SSM_W
COL_GN = 2 * SSM_W + NSA_W
COL_KV = 2 * SSM_W + 2 * NSA_W
COL_GL = COL_KV + 6 * KV_HEADS * HEAD_DIM
IN_W_PAD = COL_GL + LANES

PROJ_TILE = 512
ATTN_TILE = 256
SSM_CHUNK = 16
SSM_SLAB_GROUPS = LANES // SSM_GROUP
SSM_SLABS = SSM_W // LANES


def _cparams(sem):
    return pltpu.CompilerParams(dimension_semantics=sem, vmem_limit_bytes=VMEM_LIMIT_BYTES)


def _sigmoid(x):
    return 1.0 / (1.0 + jnp.exp(-x))


def _dot(a, b):
    return jnp.dot(a, b, preferred_element_type=F32)


def _dot_nt(a, b):
    return lax.dot_general(a, b, (((1,), (1,)), ((), ())), preferred_element_type=F32)


def _proj_body(x_ref, nw_ref, w_ref, qnw_ref, knw_ref, gb_ref, ra_ref, rb_ref, rc_ref,
               u_ref, gs_ref, gn_ref, qt_ref, kvt_ref, wint_ref, gt_ref,
               ks_ref, vst_ref, kw_ref, vwt_ref, *, tm, tv, tiles_per_seq):
    x = x_ref[...]
    ms = jnp.mean(x * x, axis=-1, keepdims=True)
    h = (x * lax.rsqrt(ms + RMS_EPS) * nw_ref[...]).astype(BF16)

    def mm(c0, c1):
        return _dot(h, w_ref[:, c0:c1])

    lane = lax.broadcasted_iota(I32, (tm, LANES), 1)
    lo = lane < HEAD_DIM
    ra = ra_ref[...]
    rb = rb_ref[...]
    rc = rc_ref[...]

    def norm_rope(s, wrow):
        s2 = s * s
        slo = jnp.sum(jnp.where(lo, s2, 0.0), axis=-1, keepdims=True)
        shi = jnp.sum(jnp.where(lo, 0.0, s2), axis=-1, keepdims=True)
        msq = jnp.where(lo, slo, shi) * (1.0 / HEAD_DIM)
        y = s * lax.rsqrt(msq + RMS_EPS) * wrow
        half = ROPE_DIM // 2
        return y * ra + pltpu.roll(y, LANES - half, 1) * rb + pltpu.roll(y, half, 1) * rc

    def hi_half(y, head):
        src = pltpu.roll(y, HEAD_DIM, 1) if head == 0 else y
        return jnp.where(lo, 0.0, src)

    zq = mm(COL_Q, COL_GN)
    zkv = mm(COL_KV, COL_GL)
    zgl = mm(COL_GL, IN_W_PAD)

    qnw = qnw_ref[...]
    scale = HEAD_DIM ** -0.5 * math.log2(math.e)
    zeros_t = jnp.zeros((HEAD_DIM, tm), F32)
    for j in range(N_HEADS // 2):
        y_t = (norm_rope(zq[:, j * LANES:(j + 1) * LANES], qnw) * scale).T
        for head in range(2):
            q_t = jnp.concatenate([zeros_t, y_t[head * HEAD_DIM:(head + 1) * HEAD_DIM]], axis=0)
            qt_ref[2 * j + head] = q_t.astype(BF16)

    kc = norm_rope(zkv[:, 0:LANES], knw_ref[0:1, :])
    vc = zkv[:, LANES:2 * LANES]
    ks = norm_rope(zkv[:, 2 * LANES:3 * LANES], knw_ref[1:2, :])
    vs = zkv[:, 3 * LANES:4 * LANES]
    kw = norm_rope(zkv[:, 4 * LANES:5 * LANES], knw_ref[2:3, :])
    vw = zkv[:, 5 * LANES:6 * LANES]
    vs_t, vw_t = vs.T, vw.T
    for i, rows_t in enumerate((kc.T, vc.T, ks.T, vs_t)):
        kvt_ref[i * LANES:(i + 1) * LANES, :] = rows_t
    for i, rows_t in enumerate((kw.T, vw_t)):
        wint_ref[i * LANES:(i + 1) * LANES, :] = rows_t

    row = lax.broadcasted_iota(I32, (tm, LANES), 0)
    pos = (pl.program_id(0) % tiles_per_seq) * tm + row
    onehot = jnp.where(lane == lax.shift_right_logical(pos, 6), 1.0, 0.0)
    ones_t = jnp.where(lax.broadcasted_iota(I32, (HEAD_DIM, tm), 0) == 0, 1.0, 0.0)
    for g in range(KV_HEADS):
        ks_ref[g] = jnp.where(lo, onehot, hi_half(ks, g)).astype(BF16)
        kw_ref[g] = hi_half(kw, g).astype(BF16)
        for v_t, vt_ref in ((vs_t, vst_ref), (vw_t, vwt_ref)):
            v_aug = jnp.concatenate([v_t[g * HEAD_DIM:(g + 1) * HEAD_DIM], ones_t], axis=0).astype(BF16)
            for t in range(tm // tv):
                vt_ref[g, t] = v_aug[:, t * tv:(t + 1) * tv]

    gates_t = _sigmoid(zgl + gb_ref[...]).T
    for g in range(KV_HEADS):
        gt_ref[g] = gates_t[g * 3 * Q_PER_KV:g * 3 * Q_PER_KV + 2 * SUBLANES]

    zu = mm(COL_U, COL_GS)
    for j in range(SSM_SLABS):
        u_ref[j] = zu[:, j * LANES:(j + 1) * LANES]
    for g_ref, cols in ((gs_ref, (COL_GS, COL_Q)), (gn_ref, (COL_GN, COL_KV))):
        g = mm(*cols)
        g_ref[...] = (g * _sigmoid(g)).astype(BF16)


def _proj_call(x2d, w_pad, norm_w, qnw, knw, gb, tabs, batch, seq):
    T, D = x2d.shape
    tm = min(PROJ_TILE, seq)
    tv = min(ATTN_TILE, seq)
    assert T == batch * seq and seq % tm == 0 and tm % tv == 0 and tv % LANES == 0
    tps = seq // tm
    row_spec = lambda w: pl.BlockSpec((tm, w), lambda i: (i, 0))
    full = lambda a: pl.BlockSpec(a.shape, lambda i: (0,) * a.ndim)
    tab_spec = pl.BlockSpec((tm, LANES), lambda i: (i % tps, 0))
    head_spec = lambda n: pl.BlockSpec((n, tm, LANES), lambda i: (0, i, 0))
    head_t_spec = lambda n, rows: pl.BlockSpec((n, rows, tm), lambda i: (0, 0, i))
    cache_t_spec = lambda rows: pl.BlockSpec((None, rows, tm), lambda i: (i // tps, 0, i % tps))
    tile_t_spec = pl.BlockSpec((KV_HEADS, tm // tv, LANES, tv), lambda i: (0, i, 0, 0))
    tile_t_sds = jax.ShapeDtypeStruct((KV_HEADS, T // tv, LANES, tv), BF16)
    out_shape = (
        jax.ShapeDtypeStruct((SSM_SLABS, T, LANES), F32),
        jax.ShapeDtypeStruct((T, SSM_W), BF16),
        jax.ShapeDtypeStruct((T, NSA_W), BF16),
        jax.ShapeDtypeStruct((N_HEADS, LANES, T), BF16),
        jax.ShapeDtypeStruct((batch, 4 * LANES, seq), F32),
        jax.ShapeDtypeStruct((batch, 2 * LANES, seq), F32),
        jax.ShapeDtypeStruct((KV_HEADS, 2 * SUBLANES, T), F32),
        jax.ShapeDtypeStruct((KV_HEADS, T, LANES), BF16),
        tile_t_sds,
        jax.ShapeDtypeStruct((KV_HEADS, T, LANES), BF16),
        tile_t_sds,
    )
    out_specs = (head_spec(SSM_SLABS), row_spec(SSM_W), row_spec(NSA_W), head_t_spec(N_HEADS, LANES),
                 cache_t_spec(4 * LANES), cache_t_spec(2 * LANES), head_t_spec(KV_HEADS, 2 * SUBLANES),
                 head_spec(KV_HEADS), tile_t_spec, head_spec(KV_HEADS), tile_t_spec)
    return pl.pallas_call(
        functools.partial(_proj_body, tm=tm, tv=tv, tiles_per_seq=tps),
        out_shape=out_shape,
        grid=(T // tm,),
        in_specs=[row_spec(D), full(norm_w), full(w_pad), full(qnw), full(knw), full(gb),
                  tab_spec, tab_spec, tab_spec],
        out_specs=out_specs,
        compiler_params=_cparams(("parallel",)),
        name="proj",
    )(x2d, norm_w, w_pad, qnw, knw, gb, *tabs)


def _prep_params(p, l):
    w_in = p["w_in"][l]
    d_model, in_w = w_in.shape
    tile2 = lambda v: jnp.tile(v, (1, LANES // HEAD_DIM))
    return {
        "w_in": jnp.pad(w_in.astype(BF16), ((0, 0), (0, IN_W_PAD - in_w))),
        "norm_w": p["norm_w"][l].reshape(1, d_model).astype(F32),
        "qnw": tile2(p["q_norm_w"][l].reshape(1, HEAD_DIM)).astype(F32),
        "knw": tile2(p["k_norm_w"][l]).astype(F32),
        "gb": jnp.pad(p["gate_b"][l].reshape(1, -1).astype(F32), ((0, 0), (0, LANES - 3 * N_HEADS))),
        "w_glu": p["w_glu"][l].astype(BF16),
        "w_out": p["w_out"][l].astype(BF16),
    }


def _rope_tables(pos):
    half = ROPE_DIM // 2
    inv = ROPE_THETA ** (-jnp.arange(half, dtype=F32) / half)
    ang = pos.astype(F32)[:, None] * inv
    cos, sin = jnp.cos(ang), jnp.sin(ang)
    n = pos.shape[0]
    rest = HEAD_DIM - ROPE_DIM
    a = jnp.concatenate([cos, cos, jnp.ones((n, rest), F32)], axis=-1)
    b = jnp.concatenate([-sin, jnp.zeros((n, HEAD_DIM - half), F32)], axis=-1)
    c = jnp.concatenate([jnp.zeros((n, half), F32), sin, jnp.zeros((n, rest), F32)], axis=-1)
    return tuple(jnp.tile(t, (1, LANES // HEAD_DIM)) for t in (a, b, c))


def _outmix_body(x_ref, y_ref, gs_ref, o_ref, gn_ref, wg_ref, wo_ref, out_ref):
    y = jnp.concatenate([y_ref[j] for j in range(SSM_SLABS)], axis=-1)
    ab = _dot(y.astype(BF16), wg_ref[...])
    ssm = ab[:, :SSM_W] * _sigmoid(ab[:, SSM_W:]) * gs_ref[...].astype(F32)
    nsa = o_ref[...].astype(F32) * gn_ref[...].astype(F32)
    acc = _dot(ssm.astype(BF16), wo_ref[0:SSM_W, :])
    acc += _dot(nsa.astype(BF16), wo_ref[SSM_W:, :])
    out_ref[...] = x_ref[...] + acc


def _outmix_call(x2d, y_ssm, g_ssm, o_nsa, g_nsa, w_glu, w_out):
    T, D = x2d.shape
    tm = min(512, T)
    row_spec = lambda w: pl.BlockSpec((tm, w), lambda i: (i, 0))
    full = lambda a: pl.BlockSpec(a.shape, lambda i: (0,) * a.ndim)
    return pl.pallas_call(
        _outmix_body,
        out_shape=jax.ShapeDtypeStruct((T, D), F32),
        grid=(T // tm,),
        in_specs=[row_spec(D), pl.BlockSpec((SSM_SLABS, tm, LANES), lambda i: (0, i, 0)),
                  row_spec(SSM_W), row_spec(NSA_W), row_spec(NSA_W),
                  full(w_glu), full(w_out)],
        out_specs=row_spec(D),
        compiler_params=_cparams(("parallel",)),
        name="outmix",
    )(x2d, y_ssm, g_ssm, o_nsa, g_nsa, w_glu, w_out)


def _gelu_tanh(x):
    c = math.sqrt(2.0 / math.pi)
    return 0.5 * x * (1.0 + jnp.tanh(c * (x + 0.044715 * (x * x * x))))


def _compress_rows(rows_refs, pe_ref, wa_ref, wb_ref, b1_ref, w2_ref, kc_ref, vc_ref, nch):
    lane = lax.broadcasted_iota(I32, (nch, LANES), 1)
    for kvi, out_ref in ((0, kc_ref), (1, vc_ref)):
        rows_ref = rows_refs[kvi]
        pa = jnp.zeros((nch, 2 * CMP_HID), F32)
        pb = jnp.zeros((nch, 2 * CMP_HID), F32)
        for j0 in range(0, CMP_STRIDE, 2):
            xs = [rows_ref[pl.ds(j, nch, stride=CMP_STRIDE), :] for j in (j0, j0 + 1)]
            xa = jnp.concatenate([xs[i] + pe_ref[kvi, 0, j0 + i:j0 + i + 1, :] for i in range(2)], axis=-1)
            xb = jnp.concatenate([xs[i] + pe_ref[kvi, 1, j0 + i:j0 + i + 1, :] for i in range(2)], axis=-1)
            wsl = slice(j0 * LANES, (j0 + 2) * LANES)
            pa += _dot(xa.astype(BF16), wa_ref[kvi, wsl, :])
            pb += _dot(xb.astype(BF16), wb_ref[kvi, wsl, :])
        hid = _gelu_tanh(pa + pltpu.roll(pb, nch - 1, 0) + b1_ref[kvi]).astype(BF16)
        for g in range(KV_HEADS):
            o = _dot(hid, w2_ref[kvi, g])
            if kvi == 1:
                o = jnp.where(lane == HEAD_DIM, 1.0, o).T
            out_ref[g] = o.astype(BF16)


def _compress_prompt_body(kvt_ref, pe_ref, wa_ref, wb_ref, b1_ref, w2_ref, kc_ref, vc_ref,
                          krows_ref, vrows_ref, *, nch):
    for c in range(kvt_ref.shape[1] // LANES):
        cs = slice(c * LANES, (c + 1) * LANES)
        krows_ref[cs, :] = kvt_ref[0:LANES, cs].T
        vrows_ref[cs, :] = kvt_ref[LANES:2 * LANES, cs].T
    _compress_rows((krows_ref, vrows_ref), pe_ref, wa_ref, wb_ref, b1_ref, w2_ref, kc_ref, vc_ref, nch)


def _compress_prompt_call(kv_t, cw, batch, seq):
    nch = seq // CMP_STRIDE
    full = lambda a: pl.BlockSpec(a.shape, lambda b: (0,) * a.ndim)
    out_spec = pl.BlockSpec((KV_HEADS, nch, LANES), lambda b: (0, b, 0))
    out_sds = jax.ShapeDtypeStruct((KV_HEADS, batch * nch, LANES), BF16)
    out_t_spec = pl.BlockSpec((KV_HEADS, None, LANES, nch), lambda b: (0, b, 0, 0))
    out_t_sds = jax.ShapeDtypeStruct((KV_HEADS, batch, LANES, nch), BF16)
    return pl.pallas_call(
        functools.partial(_compress_prompt_body, nch=nch),
        out_shape=(out_sds, out_t_sds),
        grid=(batch,),
        in_specs=[pl.BlockSpec((None, 2 * LANES, seq), lambda b: (b, 0, 0)),
                  full(cw["pe"]), full(cw["wa"]), full(cw["wb"]), full(cw["b1"]), full(cw["w2"])],
        out_specs=(out_spec, out_t_spec),
        scratch_shapes=[pltpu.VMEM((seq, LANES), F32)] * 2,
        compiler_params=_cparams(("parallel",)),
        name="compress_prompt",
    )(kv_t, cw["pe"], cw["wa"], cw["wb"], cw["b1"], cw["w2"])


def _prep_compress(p, l):
    eye = jnp.eye(KV_HEADS, dtype=F32)
    w1 = p["cmp_w1"][l].reshape(2, 2, CMP_STRIDE, HEAD_DIM, CMP_HID)
    wexp = jnp.einsum("khjdn,ge->khjgden", w1, eye).reshape(2, 2, CMP_STRIDE * LANES, KV_HEADS * CMP_HID)
    pe = p["cmp_pe"][l].reshape(2, 2, CMP_STRIDE, HEAD_DIM)
    w2 = p["cmp_w2"][l]
    zeros = jnp.zeros_like(w2[0])
    w2k = jnp.concatenate([zeros, w2[0]], axis=-1)
    w2v = jnp.concatenate([w2[1], zeros], axis=-1)
    w2e = jnp.stack([jnp.einsum("hd,ge->gehd", w, eye).reshape(KV_HEADS, KV_HEADS * CMP_HID, LANES)
                     for w in (w2k, w2v)])
    return {
        "pe": jnp.tile(pe, (1, 1, 1, KV_HEADS)).astype(F32),
        "wa": wexp[:, 0].astype(BF16),
        "wb": wexp[:, 1].astype(BF16),
        "b1": jnp.tile(p["cmp_b1"][l].reshape(2, 1, CMP_HID), (1, 1, KV_HEADS)).astype(F32),
        "w2": w2e.astype(BF16),
    }


def _overlap_matrix(n_tok_pad, n_tok, n_blk, n_cols=LANES):
    c_start = jnp.arange(n_tok_pad) * CMP_STRIDE
    blk = jnp.arange(n_cols)
    ov = ((c_start[:, None] < (blk[None, :] + 1) * SLC_BLOCK)
          & (c_start[:, None] + CMP_BLOCK > blk[None, :] * SLC_BLOCK)
          & (jnp.arange(n_tok_pad)[:, None] < n_tok) & (blk[None, :] < n_blk))
    return ov.astype(BF16)


def _topk_select_t(w_ref, imp_t, q0, tq):
    nb = imp_t.shape[0]
    n_i = lax.broadcasted_iota(I32, (nb, tq), 0)
    qblk = lax.shift_right_logical(q0 + lax.broadcasted_iota(I32, (nb, tq), 1), 6)
    causal = n_i <= qblk
    forced = (n_i == 0) | (n_i >= qblk - (N_LOCAL_BLOCKS - 1))
    w_ref[...] = jnp.where(causal, jnp.where(forced, jnp.inf, imp_t), -jnp.inf)
    last_blk = lax.shift_right_logical(q0 + tq - 1, 6)
    n_grp = nb // SUBLANES
    rank = [jnp.zeros((SUBLANES, tq), F32) for _ in range(n_grp)]
    grp_i = lax.broadcasted_iota(I32, (SUBLANES, tq), 0)

    def count_group(mg, rank):
        rank = list(rank)
        for mi in range(SUBLANES):
            m = mg * SUBLANES + mi
            wm = w_ref[m:m + 1, :]
            for ng in range(n_grp):
                w = w_ref[ng * SUBLANES:(ng + 1) * SUBLANES, :]
                if ng > mg:
                    beats = jnp.where(wm >= w, 1.0, 0.0)
                elif ng < mg:
                    beats = jnp.where(wm > w, 1.0, 0.0)
                else:
                    beats = jnp.where(grp_i > mi, jnp.where(wm >= w, 1.0, 0.0), jnp.where(wm > w, 1.0, 0.0))
                rank[ng] = rank[ng] + beats
        return tuple(rank)

    rank = tuple(rank)
    for mg in range(n_grp):
        rank = lax.cond(mg * SUBLANES <= last_blk, functools.partial(count_group, mg), lambda r: r, rank)
    return causal & (jnp.concatenate(rank, axis=0) < TOP_N)


def _flash_tiles_t(tiles, q_ts, ms, accs):
    def scores(i):
        k, _, mask = tiles[i]
        s = [_dot(k, q_t) for q_t in q_ts]
        return s if mask is None else [jnp.where(mask, x, NEG_INF) for x in s]

    ahead = 2
    pending = {i: scores(i) for i in range(min(ahead, len(tiles)))}
    for i, (_, v_t, _) in enumerate(tiles):
        s = pending.pop(i)
        m_new = [jnp.maximum(m, jnp.max(x, axis=0, keepdims=True)) for m, x in zip(ms, s)]
        alpha = [jnp.exp2(m - mn) for m, mn in zip(ms, m_new)]
        p = [jnp.exp2(x - mn).astype(BF16) for x, mn in zip(s, m_new)]
        pv = [_dot(v_t, x) for x in p]
        if i + ahead < len(tiles):
            pending[i + ahead] = scores(i + ahead)
        accs = [a * acc + x for a, acc, x in zip(alpha, accs, pv)]
        ms = m_new
    return tuple(ms), tuple(accs)


def _attn_body(qt_ref, kc_ref, vct_ref, ovlt_ref, ks_ref, vst_ref, kw_ref, vwt_ref, gate_ref, o_ref,
               qa_ref, ocmp_ref, w_ref, *, tq, ncp):
    R = Q_PER_KV
    qt = pl.program_id(2)
    q0 = qt * tq
    nbs = LANES // 2

    c_i = lax.broadcasted_iota(I32, (ncp, tq), 0)
    qpos_c = q0 + lax.broadcasted_iota(I32, (ncp, tq), 1)
    cmask = c_i * CMP_STRIDE + (CMP_BLOCK - 1) <= qpos_c
    kc = kc_ref[...]
    s = [jnp.where(cmask, _dot(kc, qt_ref[r]), NEG_INF) for r in range(R)]
    e = [jnp.where(cmask, jnp.exp2(x - jnp.max(x, axis=0, keepdims=True)), 0.0) for x in s]
    l = [jnp.sum(x, axis=0, keepdims=True) for x in e]
    p = [(x * (1.0 / jnp.where(y > 0.0, y, 1.0))).astype(BF16) for x, y in zip(e, l)]
    imp = sum(_dot(ovlt_ref[...], x) for x in p)
    for r in range(R):
        ocmp_ref[r] = _dot(vct_ref[...], p[r])

    sel = _topk_select_t(w_ref, imp[:nbs], q0, tq)
    bias = jnp.concatenate([jnp.where(sel, 0.0, NEG_INF), jnp.zeros((LANES - nbs, tq), F32)], axis=0)
    for r in range(R):
        qa_ref[r] = (qt_ref[r].astype(F32) + bias).astype(BF16)

    key_i = lax.broadcasted_iota(I32, (tq, tq), 0)
    qry_i = lax.broadcasted_iota(I32, (tq, tq), 1)
    init = (tuple(jnp.full((1, tq), -jnp.inf, F32) for _ in range(R)),
            tuple(jnp.zeros((LANES, tq), F32) for _ in range(R)))

    def tiles(k_ref, vt_ref, q_ref, js, masks, state):
        ts = [(k_ref[pl.ds(pl.multiple_of(j * tq, tq), tq), :], vt_ref[j], mask) for j, mask in zip(js, masks)]
        return _flash_tiles_t(ts, [q_ref[r] for r in range(R)], *state)

    def last_tiles(k_ref, vt_ref, q_ref, n, first_mask, state):
        js = [qt - (n - 1 - t) for t in range(n)]
        masks = [first_mask] + [None] * (n - 2) + [key_i <= qry_i] if n > 1 else [key_i <= qry_i]
        return tiles(k_ref, vt_ref, q_ref, js, masks, state)

    GROUP = 4

    def slc_group(i, st):
        return tiles(ks_ref, vst_ref, qa_ref, [GROUP * i + t for t in range(GROUP)], [None] * GROUP, st)

    state = lax.fori_loop(0, qt // GROUP, slc_group, init)
    tails = [functools.partial(last_tiles, ks_ref, vst_ref, qa_ref, n, None) for n in range(1, GROUP + 1)]
    _, acc_s = lax.switch(qt % GROUP, tails, state)

    nwin = WINDOW // tq
    wins = [functools.partial(last_tiles, kw_ref, vwt_ref, qt_ref, n, (key_i > qry_i) if n == nwin + 1 else None)
            for n in range(1, nwin + 2)]
    _, acc_w = lax.switch(jnp.minimum(qt, nwin), wins, init)

    outs = []
    for r in range(R):
        a_s = acc_s[r]
        a_w = acc_w[r]
        g = lambda k: gate_ref[3 * r + k:3 * r + k + 1, :]
        o = (g(0) * ocmp_ref[r] + g(1) * (a_s / a_s[HEAD_DIM:HEAD_DIM + 1, :])
             + g(2) * (a_w / a_w[HEAD_DIM:HEAD_DIM + 1, :]))
        outs.append(o[:HEAD_DIM])
    o_ref[...] = jnp.concatenate(outs, axis=0).T.astype(o_ref.dtype)


def _attn_call(q_t, kc, vc_t, ovl_t, ks, vs_t, kw, vw_t, gates_t, batch, seq):
    tq = min(ATTN_TILE, seq)
    nq = seq // tq
    ncp = kc.shape[1] // batch
    R = Q_PER_KV
    assert seq // SLC_BLOCK <= LANES // 2 and WINDOW % tq == 0
    k_spec = pl.BlockSpec((None, seq, LANES), lambda b, g, t: (g, b, 0))
    vt_spec = pl.BlockSpec((None, nq, LANES, tq), lambda b, g, t: (g, b, 0, 0))
    acc = pltpu.VMEM((R, LANES, tq), F32)
    return pl.pallas_call(
        functools.partial(_attn_body, tq=tq, ncp=ncp),
        out_shape=jax.ShapeDtypeStruct((batch * seq, NSA_W), BF16),
        grid=(batch, KV_HEADS, nq),
        in_specs=[pl.BlockSpec((R, LANES, tq), lambda b, g, t: (g, 0, b * nq + t)),
                  pl.BlockSpec((None, ncp, LANES), lambda b, g, t: (g, b, 0)),
                  pl.BlockSpec((None, None, LANES, ncp), lambda b, g, t: (g, b, 0, 0)),
                  pl.BlockSpec(ovl_t.shape, lambda b, g, t: (0, 0)),
                  k_spec, vt_spec, k_spec, vt_spec,
                  pl.BlockSpec((None, 2 * SUBLANES, tq), lambda b, g, t: (g, 0, b * nq + t))],
        out_specs=pl.BlockSpec((tq, R * HEAD_DIM), lambda b, g, t: (b * nq + t, g)),
        scratch_shapes=[pltpu.VMEM((R, LANES, tq), BF16), acc, pltpu.VMEM((LANES // 2, tq), F32)],
        compiler_params=_cparams(("parallel", "parallel", "arbitrary")),
        name="nsa_prompt",
    )(q_t, kc, vc_t, ovl_t, ks, vs_t, kw, vw_t, gates_t)


def _s5_discretise(p, l):
    lr = p["ssm_lam_re"][l].astype(F32)
    li = p["ssm_lam_im"][l].astype(F32)
    dt = jnp.exp(p["ssm_log_step"][l].astype(F32))[:, None]

    def apow(t):
        mag, ang = jnp.exp(lr * dt * t), li * dt * t
        return mag * jnp.cos(ang), mag * jnp.sin(ang)

    a_re, a_im = apow(1.0)
    den = lr * lr + li * li
    nr, ni = a_re - 1.0, a_im
    f_re, f_im = (nr * lr + ni * li) / den, (ni * lr - nr * li) / den
    br, bi = p["ssm_b_re"][l].astype(F32), p["ssm_b_im"][l].astype(F32)
    bb_re = f_re[..., None] * br - f_im[..., None] * bi
    bb_im = f_re[..., None] * bi + f_im[..., None] * br
    return apow, bb_re, bb_im


def _s5_expand_body(pst_ref, kt_ref, cp_ref, wcol_ref, wst_ref, wout_ref):
    T, E, N, C = SSM_CHUNK, SSM_SLAB_GROUPS, SSM_STATE, SSM_GROUP
    row = lax.broadcasted_iota(I32, (LANES, LANES), 0)
    lane = lax.broadcasted_iota(I32, (LANES, LANES), 1)
    row_grp = row // C
    low = lane < N
    own_grp = row_grp == lane // C
    lane_n = lax.broadcasted_iota(I32, (N, LANES), 1)
    wcol_ref[T * LANES:(T + 1) * LANES, :] = jnp.zeros((LANES, LANES), BF16)
    wout_ref[0, :, LANES:2 * LANES] = jnp.zeros((2 * E * N, LANES), BF16)
    for i in range(T):
        wcol_ref[i * LANES:(i + 1) * LANES, :] = jnp.where(own_grp, kt_ref[i], 0.0).astype(BF16)
        x = pst_ref[i]
        x_sw = pltpu.roll(x, N, 1)
        halves = (jnp.where(low, x, x_sw), jnp.where(low, x_sw, x))
        for ri, x2 in enumerate(halves):
            for q in range(E // 2):
                own = row_grp == 2 * q + jnp.where(low, 0, 1)
                c0 = ri * E * N + q * LANES
                wst_ref[i * LANES:(i + 1) * LANES, c0:c0 + LANES] = jnp.where(own, x2, 0.0).astype(BF16)
    for i in range(T + 1):
        pair, half = (0, 0) if i == 0 else ((i + 1) // 2, (i - 1) % 2)
        m_t = cp_ref[i].T
        for ri in range(2):
            rows = m_t[ri * N:(ri + 1) * N]
            for g in range(E):
                r0 = ri * E * N + g * N
                wout_ref[pair, r0:r0 + N, half * LANES:(half + 1) * LANES] = (
                    jnp.where(lane_n // C == g, rows, 0.0).astype(BF16))


def _prep_s5_chunked(p, l):
    hi = lax.Precision.HIGHEST
    apow, bb_re, bb_im = _s5_discretise(p, l)
    c_re, c_im = p["ssm_c_re"][l].astype(F32), p["ssm_c_im"][l].astype(F32)
    T, J, E, N, C = SSM_CHUNK, SSM_SLABS, SSM_SLAB_GROUPS, SSM_STATE, SSM_GROUP
    pw_re, pw_im = apow(jnp.arange(T + 1, dtype=F32)[:, None, None])
    bt_re, bt_im = bb_re.transpose(0, 2, 1), bb_im.transpose(0, 2, 1)
    p_re = pw_re[:, :, None, :] * bt_re - pw_im[:, :, None, :] * bt_im
    p_im = pw_re[:, :, None, :] * bt_im + pw_im[:, :, None, :] * bt_re
    kt = (jnp.einsum("tgkn,gcn->tgkc", p_re[:T], jnp.tile(c_re, (1, E, 1)), precision=hi)
          - jnp.einsum("tgkn,gcn->tgkc", p_im[:T], jnp.tile(c_im, (1, E, 1)), precision=hi))
    cp_re = c_re * pw_re[:, :, None, :] - c_im * pw_im[:, :, None, :]
    cp_im = c_re * pw_im[:, :, None, :] + c_im * pw_re[:, :, None, :]
    slab = lambda x: jnp.moveaxis(x.reshape(x.shape[0], J, E * C, LANES), 1, 0)
    pst = slab(jnp.concatenate([p_re[:T][::-1], p_im[:T][::-1]], axis=-1))
    ktile = slab(kt[::-1])
    cpn = slab(jnp.concatenate([cp_re, -cp_im], axis=-1))
    blk = lambda n: pl.BlockSpec((None, n, LANES, LANES), lambda j: (j, 0, 0, 0))
    w_col, w_st, w_out = pl.pallas_call(
        _s5_expand_body,
        out_shape=(jax.ShapeDtypeStruct((J, (T + 1) * LANES, LANES), BF16),
                   jax.ShapeDtypeStruct((J, T * LANES, 2 * E * N), BF16),
                   jax.ShapeDtypeStruct((J, T // 2 + 1, 2 * E * N, 2 * LANES), BF16)),
        grid=(J,),
        in_specs=[blk(T), blk(T), blk(T + 1)],
        out_specs=(pl.BlockSpec((None, (T + 1) * LANES, LANES), lambda j: (j, 0, 0)),
                   pl.BlockSpec((None, T * LANES, 2 * E * N), lambda j: (j, 0, 0)),
                   pl.BlockSpec((None, T // 2 + 1, 2 * E * N, 2 * LANES), lambda j: (j, 0, 0, 0))),
        compiler_params=_cparams(("parallel",)),
        name="s5_expand",
    )(pst, ktile, cpn)
    return {
        "w_col": w_col, "w_st": w_st, "w_out": w_out,
        "a_re": pw_re[T].reshape(J, 1, E * N), "a_im": pw_im[T].reshape(J, 1, E * N),
        "a1_re": pw_re[1].reshape(J, 1, E * N), "a1_im": pw_im[1].reshape(J, 1, E * N),
        "d": p["ssm_d"][l].reshape(J, 1, LANES).astype(F32),
    }


def _s5_prompt_body(u_ref, wcol_ref, wst_ref, are_ref, aim_ref, wout_ref, d_ref,
                    y_ref, hre_ref, him_ref, xs_ref, hp_ref, *, n_chunks):
    T = SSM_CHUNK
    ns = SSM_SLAB_GROUPS * SSM_STATE
    u_pos = [u_ref[pl.ds(s, n_chunks, stride=T), :] for s in range(T)]
    ub = jnp.concatenate(u_pos, axis=-1).astype(BF16)
    xs_ref[...] = _dot(ub, wst_ref[...])
    a_re = are_ref[...]
    a_im = aim_ref[...]

    def step(c, carry):
        hr, hi = carry
        hp_ref[pl.ds(c, 1), 0:ns] = hr
        hp_ref[pl.ds(c, 1), ns:2 * ns] = hi
        xr = xs_ref[pl.ds(c, 1), 0:ns]
        xi = xs_ref[pl.ds(c, 1), ns:2 * ns]
        return a_re * hr - a_im * hi + xr, a_re * hi + a_im * hr + xi

    zero = jnp.zeros((1, ns), F32)
    hr, hi = lax.fori_loop(0, n_chunks, step, (zero, zero))
    hre_ref[...] = jnp.broadcast_to(hr, hre_ref.shape)
    him_ref[...] = jnp.broadcast_to(hi, him_ref.shape)
    hpb = hp_ref[...].astype(BF16)
    for t in range(0, T, 2):
        k_rows = (t + 2) * LANES
        w_pair = jnp.concatenate([wcol_ref[(T - 1 - t) * LANES:(T - 1 - t) * LANES + k_rows, :],
                                  wcol_ref[(T - 2 - t) * LANES:T * LANES, :]], axis=1)
        y_pair = _dot(ub[:, :k_rows], w_pair) + _dot(hpb, wout_ref[t // 2 + 1])
        for i in range(2):
            y_ref[pl.ds(t + i, n_chunks, stride=T), :] = (
                y_pair[:, i * LANES:(i + 1) * LANES] + d_ref[...] * u_pos[t + i])


def _s5_prompt_call(u_slab, sw, batch, seq):
    T, J = SSM_CHUNK, SSM_SLABS
    n_chunks = seq // T
    ns = SSM_SLAB_GROUPS * SSM_STATE
    row_spec = pl.BlockSpec((None, seq, LANES), lambda j, b: (j, b, 0))
    slab_spec = lambda a: pl.BlockSpec((None,) + a.shape[1:], lambda j, b: (j,) + (0,) * (a.ndim - 1))
    st_spec = pl.BlockSpec((None, None, SUBLANES, ns), lambda j, b: (b, j, 0, 0))
    st_sds = jax.ShapeDtypeStruct((batch, J, SUBLANES, ns), F32)
    y, hre, him = pl.pallas_call(
        functools.partial(_s5_prompt_body, n_chunks=n_chunks),
        out_shape=(jax.ShapeDtypeStruct(u_slab.shape, F32), st_sds, st_sds),
        grid=(J, batch),
        in_specs=[row_spec, slab_spec(sw["w_col"]), slab_spec(sw["w_st"]), slab_spec(sw["a_re"]),
                  slab_spec(sw["a_im"]), slab_spec(sw["w_out"]), slab_spec(sw["d"])],
        out_specs=(row_spec, st_spec, st_spec),
        scratch_shapes=[pltpu.VMEM((n_chunks, 2 * ns), F32), pltpu.VMEM((n_chunks, 2 * ns), F32)],
        compiler_params=_cparams(("parallel", "parallel")),
        name="s5_prompt",
    )(u_slab, sw["w_col"], sw["w_st"], sw["a_re"], sw["a_im"], sw["w_out"], sw["d"])
    n_groups = J * SSM_SLAB_GROUPS
    state = lambda h: h[:, :, 0, :].reshape(batch, n_groups, SSM_STATE)
    return y, state(hre), state(him)


def _s5_step_body(u_ref, wx_ref, are_ref, aim_ref, h0re_ref, h0im_ref, wy_ref, d_ref,
                  y_ref, hre_ref, him_ref):
    ns = SSM_SLAB_GROUPS * SSM_STATE
    for j in range(SSM_SLABS):
        sl = slice(j * ns, (j + 1) * ns)
        u = u_ref[j]
        x = _dot(u.astype(BF16), wx_ref[j])
        a_re, a_im = are_ref[j], aim_ref[j]
        h0r, h0i = h0re_ref[:, sl], h0im_ref[:, sl]
        hr = a_re * h0r - a_im * h0i + x[:, :ns]
        hi = a_re * h0i + a_im * h0r + x[:, ns:]
        hre_ref[:, sl] = hr
        him_ref[:, sl] = hi
        y_ref[j] = _dot(jnp.concatenate([hr, hi], axis=-1).astype(BF16), wy_ref[j]) + d_ref[j] * u


def _s5_step_call(u_slab, h0_re, h0_im, sw):
    J, n_tok, _ = u_slab.shape
    T = SSM_CHUNK
    full = lambda a: pl.BlockSpec(a.shape, lambda i: (0,) * a.ndim)
    wx_spec = pl.BlockSpec((J, LANES, sw["w_st"].shape[2]), lambda i: (0, T - 1, 0))
    wy_spec = pl.BlockSpec((J, None, sw["w_out"].shape[2], LANES), lambda i: (0, 0, 0, 0))
    st_sds = jax.ShapeDtypeStruct(h0_re.shape, F32)
    return pl.pallas_call(
        _s5_step_body,
        out_shape=(jax.ShapeDtypeStruct(u_slab.shape, F32), st_sds, st_sds),
        grid=(1,),
        in_specs=[full(u_slab), wx_spec, full(sw["a1_re"]), full(sw["a1_im"]), full(h0_re), full(h0_im),
                  wy_spec, full(sw["d"])],
        out_specs=(full(u_slab), full(h0_re), full(h0_re)),
        compiler_params=_cparams(("arbitrary",)),
        name="s5_step",
    )(u_slab, sw["w_st"], sw["a1_re"], sw["a1_im"], h0_re, h0_im, sw["w_out"], sw["d"])


def _compress_sample_body(pt_ref, *refs, n_pages, page_rows, nch):
    del pt_ref
    page_refs = refs[:n_pages]
    pe_ref, wa_ref, wb_ref, b1_ref, w2_ref, kc_ref, vc_ref, krows_ref, vrows_ref = refs[n_pages:]
    b = pl.program_id(0)

    @pl.when(b == 0)
    def _():
        krows_ref[...] = jnp.zeros(krows_ref.shape, F32)
        vrows_ref[...] = jnp.zeros(vrows_ref.shape, F32)

    stage = b % 2
    for i, page_ref in enumerate(page_refs):
        krows_ref[stage, i * page_rows:(i + 1) * page_rows, :] = page_ref[0].reshape(LANES, page_rows).T
        vrows_ref[stage, i * page_rows:(i + 1) * page_rows, :] = page_ref[1].reshape(LANES, page_rows).T
    done = 1 - stage
    _compress_rows((krows_ref.at[done], vrows_ref.at[done]), pe_ref, wa_ref, wb_ref, b1_ref, w2_ref,
                   kc_ref, vc_ref, nch)


def _compress_sample_call(cache, page_table, cw):
    n_seq, n_pages = page_table.shape
    page_rows = cache.shape[-1]
    assert page_rows == LANES
    nch = n_pages * page_rows // CMP_STRIDE
    staged = lambda b: jnp.minimum(b, n_seq - 1)
    page_spec = lambda i: pl.BlockSpec((None, 2, KV_HEADS, HEAD_DIM, page_rows),
                                       lambda b, pt: (pt[staged(b), i], 0, 0, 0, 0))
    full = lambda a: pl.BlockSpec(a.shape, lambda b, pt: (0,) * a.ndim)
    out_of = lambda b: jnp.maximum(b - 1, 0)
    out_spec = pl.BlockSpec((KV_HEADS, nch, LANES), lambda b, pt: (0, out_of(b), 0))
    out_sds = jax.ShapeDtypeStruct((KV_HEADS, n_seq * nch, LANES), BF16)
    out_t_spec = pl.BlockSpec((KV_HEADS, None, LANES, nch), lambda b, pt: (0, out_of(b), 0, 0))
    out_t_sds = jax.ShapeDtypeStruct((KV_HEADS, n_seq, LANES, nch), BF16)
    weights = (cw["pe"], cw["wa"], cw["wb"], cw["b1"], cw["w2"])
    return pl.pallas_call(
        functools.partial(_compress_sample_body, n_pages=n_pages, page_rows=page_rows, nch=nch),
        out_shape=(out_sds, out_t_sds),
        grid_spec=pltpu.PrefetchScalarGridSpec(
            num_scalar_prefetch=1,
            grid=(n_seq + 1,),
            in_specs=[page_spec(i) for i in range(n_pages)] + [full(a) for a in weights],
            out_specs=(out_spec, out_t_spec),
            scratch_shapes=[pltpu.VMEM((2, n_pages * page_rows, LANES), F32)] * 2),
        compiler_params=_cparams(("arbitrary",)),
        name="compress_sample",
    )(page_table, *([cache] * n_pages), *weights)


def _group_rows(x0, x1):
    row = lax.broadcasted_iota(I32, x0.shape, 0)
    return jnp.where(row < Q_PER_KV, x0, x1)


def _sample_select_body(q_ref, kc_ref, vct_ref, ovl_ref, tri_ref, ocmp_ref, idx_ref, *, ncp, qpos, nbp):
    q8 = q_ref[...]
    c_i = lax.broadcasted_iota(I32, (N_HEADS, ncp), 1)
    cmask = c_i * CMP_STRIDE + (CMP_BLOCK - 1) <= qpos
    s = _group_rows(_dot_nt(q8, kc_ref[0]), _dot_nt(q8, kc_ref[1]))
    s = jnp.where(cmask, s, NEG_INF)
    e = jnp.where(cmask, jnp.exp2(s - jnp.max(s, axis=-1, keepdims=True)), 0.0)
    l = jnp.sum(e, axis=-1, keepdims=True)
    p = (e / jnp.where(l > 0.0, l, 1.0)).astype(BF16)
    ocmp_ref[...] = _group_rows(_dot_nt(p, vct_ref[0]), _dot_nt(p, vct_ref[1]))
    imp8 = _dot(p, ovl_ref[...])

    n_row = lax.broadcasted_iota(I32, (1, nbp), 1)
    qblk = qpos // SLC_BLOCK
    causal = n_row <= qblk
    forced = (n_row == 0) | (n_row >= qblk - (N_LOCAL_BLOCKS - 1))
    m_i = lax.broadcasted_iota(I32, (nbp, nbp), 0)
    n_i = lax.broadcasted_iota(I32, (nbp, nbp), 1)
    lane = lax.broadcasted_iota(I32, (1, LANES), 1)
    idx_rows = []
    for g in range(KV_HEADS):
        imp = jnp.sum(imp8[g * Q_PER_KV:(g + 1) * Q_PER_KV], axis=0, keepdims=True)
        w = jnp.where(causal, jnp.where(forced, jnp.inf, imp), -jnp.inf)
        w_sq = jnp.broadcast_to(w, (nbp, nbp))
        w_col = w_sq.T
        beats = jnp.where(n_i > m_i, jnp.where(w_col >= w_sq, 1.0, 0.0), jnp.where(w_col > w_sq, 1.0, 0.0))
        rank = jnp.sum(beats, axis=0, keepdims=True)
        sel = causal & (rank < TOP_N)
        self_f = jnp.where(sel, 1.0, 0.0)
        before = _dot(self_f.astype(BF16), tri_ref[...])
        idx = jnp.full((1, LANES), -1, I32)
        for k in range(TOP_N):
            hit = sel & (before == float(k))
            val = jnp.sum(jnp.where(hit, n_row.astype(F32) + 1.0, 0.0), axis=-1, keepdims=True) - 1.0
            idx = jnp.where(lane == k, val.astype(I32), idx)
        idx_rows.append(idx)
    idx_ref[...] = jnp.concatenate(idx_rows + [jnp.full((SUBLANES - KV_HEADS, LANES), -1, I32)], axis=0)


def _sample_select_call(q8, kc, vc_t, ovl, tri, qpos):
    n_seq = q8.shape[0]
    ncp = kc.shape[1] // n_seq
    nbp = ovl.shape[1]
    cmp_spec = pl.BlockSpec((KV_HEADS, ncp, LANES), lambda b: (0, b, 0))
    row_spec = pl.BlockSpec((None, N_HEADS, LANES), lambda b: (b, 0, 0))
    full = lambda a: pl.BlockSpec(a.shape, lambda b: (0,) * a.ndim)
    return pl.pallas_call(
        functools.partial(_sample_select_body, ncp=ncp, qpos=qpos, nbp=nbp),
        out_shape=(jax.ShapeDtypeStruct((n_seq, N_HEADS, LANES), F32),
                   jax.ShapeDtypeStruct((n_seq, SUBLANES, LANES), I32)),
        grid=(n_seq,),
        in_specs=[row_spec, cmp_spec, pl.BlockSpec((KV_HEADS, None, LANES, ncp), lambda b: (0, b, 0, 0)),
                  full(ovl), full(tri)],
        out_specs=(row_spec, pl.BlockSpec((None, SUBLANES, LANES), lambda b: (b, 0, 0))),
        compiler_params=_cparams(("parallel",)),
        name="nsa_sample_select",
    )(q8, kc, vc_t, ovl, tri)


def _sample_attend_body(idx_ref, pt_ref, q_ref, ocmp_ref, gate_ref, ksn_ref, vsn_ref, kwn_ref, vwn_ref,
                        win_ref, *refs, n_cache_blocks, blocks_per_page, win_skip):
    del pt_ref
    n_blk = KV_HEADS * TOP_N
    kv_refs, o_ref = refs[:n_blk], refs[n_blk]
    b = pl.program_id(0)
    q = q_ref[...]
    qf = q.astype(F32)
    row_g = (lax.broadcasted_iota(I32, (N_HEADS, 1), 0) >= Q_PER_KV).astype(I32)

    def attend(s_list, v_list, kn_ref, vn_ref):
        s_self = jnp.sum(qf * kn_ref[...].astype(F32), axis=-1, keepdims=True)
        m = s_self
        for s in s_list:
            m = jnp.maximum(m, jnp.max(s, axis=-1, keepdims=True))
        p_self = jnp.exp2(s_self - m)
        l = p_self
        acc = p_self.astype(BF16).astype(F32) * vn_ref[...].astype(F32)
        for s, v in zip(s_list, v_list):
            p = jnp.exp2(s - m)
            l = l + jnp.sum(p, axis=-1, keepdims=True)
            acc = acc + _dot_nt(p.astype(BF16), v().astype(BF16))
        return acc / l

    s_list, v_list = [], []
    for j in range(n_blk):
        s = _dot(q, kv_refs[j][0].astype(BF16))
        col = lax.broadcasted_iota(I32, s.shape, 1)
        n = idx_ref[b, j]
        first = (n & (blocks_per_page - 1)) * SLC_BLOCK
        ok = ((row_g == j // TOP_N) & (col >= first) & (col < first + SLC_BLOCK)
              & (n >= 0) & (n < n_cache_blocks))
        s_list.append(jnp.where(ok, s, NEG_INF))
        v_list.append(lambda j=j: kv_refs[j][1])
    o_slc = attend(s_list, v_list, ksn_ref, vsn_ref)

    s_list, v_list = [], []
    for g in range(KV_HEADS):
        s = _dot(q, win_ref[0, g].astype(BF16))
        col = lax.broadcasted_iota(I32, s.shape, 1)
        s_list.append(jnp.where((row_g == g) & (col >= win_skip), s, NEG_INF))
        v_list.append(lambda g=g: win_ref[1, g])
    o_win = attend(s_list, v_list, kwn_ref, vwn_ref)

    gates = gate_ref[...]
    o_ref[...] = gates[:, 0:1] * ocmp_ref[:, 0:HEAD_DIM] + gates[:, 1:2] * o_slc + gates[:, 2:3] * o_win


def _sample_attend_call(idx, page_table, q64, ocmp, gates8, ksn, vsn, kwn, vwn, cache_win_t, cache_t, win_skip):
    n_seq, n_pages = page_table.shape
    page_rows = cache_t.shape[-1]
    bpp = page_rows // SLC_BLOCK
    n_cache_blocks = n_pages * bpp
    row_spec = lambda a: pl.BlockSpec((None,) + a.shape[1:], lambda b, ix, pt: (b,) + (0,) * (a.ndim - 1))

    assert bpp & (bpp - 1) == 0
    bpp_shift = bpp.bit_length() - 1

    def blk_spec(j):
        def index_map(b, ix, pt):
            n = jnp.minimum(jnp.maximum(ix[b, j], 0), n_cache_blocks - 1)
            return (pt[b, lax.shift_right_logical(n, bpp_shift)], 1, j // TOP_N, 0, 0)
        return pl.BlockSpec((None, 2, None, HEAD_DIM, page_rows), index_map)

    n_blk = KV_HEADS * TOP_N
    small = (q64, ocmp, gates8, ksn, vsn, kwn, vwn, cache_win_t)
    return pl.pallas_call(
        functools.partial(_sample_attend_body, n_cache_blocks=n_cache_blocks, blocks_per_page=bpp,
                          win_skip=win_skip),
        out_shape=jax.ShapeDtypeStruct((n_seq, N_HEADS, HEAD_DIM), F32),
        grid_spec=pltpu.PrefetchScalarGridSpec(
            num_scalar_prefetch=2,
            grid=(n_seq,),
            in_specs=[row_spec(a) for a in small] + [blk_spec(j) for j in range(n_blk)],
            out_specs=pl.BlockSpec((None, N_HEADS, HEAD_DIM), lambda b, ix, pt: (b, 0, 0))),
        compiler_params=_cparams(("parallel",)),
        name="nsa_sample_attend",
    )(idx, page_table, *small, *([cache_t] * n_blk))


def _round_up(x, m):
    return -(-x // m) * m


def _prompt_layer(h, prm, cw, sw, batch, seq):
    tabs = _rope_tables(jnp.arange(seq))
    u, gs, gn, q_t, kv_t, win_t, gates_t, ks, vs_t, kw, vw_t = _proj_call(
        h, prm["w_in"], prm["norm_w"], prm["qnw"], prm["knw"], prm["gb"], tabs, batch, seq)
    y_ssm, h_re, h_im = _s5_prompt_call(u, sw, batch, seq)
    kc, vc_t = _compress_prompt_call(kv_t, cw, batch, seq)
    nch = seq // CMP_STRIDE
    ovl_t = _overlap_matrix(nch, nch - 1, seq // SLC_BLOCK).T
    o = _attn_call(q_t, kc, vc_t, ovl_t, ks, vs_t, kw, vw_t, gates_t, batch, seq)
    h_new = _outmix_call(h, y_ssm, gs, o, gn, prm["w_glu"], prm["w_out"])
    rows = lambda x_t, slots: x_t.reshape(batch, slots, KV_HEADS, HEAD_DIM, seq).transpose(0, 4, 1, 2, 3)
    return h_new, rows(kv_t, 4), rows(win_t, 2)[:, seq - min(WINDOW, seq):], h_re, h_im


def _sample_layer(h, prm, cw, sw, cache_kv, cache_win, st_re, st_im, page_table):
    n_seq = h.shape[0]
    n_phys, page_rows = cache_kv.shape[:2]
    n_pages = page_table.shape[1]
    past_len = n_pages * page_rows
    win_buf = cache_win.shape[1]
    n_pad = _round_up(n_seq, LANES)
    tabs = _rope_tables(jnp.full((n_pad,), past_len, I32))
    h_pad = jnp.pad(h, ((0, n_pad - n_seq), (0, 0)))
    u, gs, gn, q_t, kv_t, win_t, gates_t, ks, vs_t, kw, vw_t = _proj_call(
        h_pad, prm["w_in"], prm["norm_w"], prm["qnw"], prm["knw"], prm["gb"], tabs, 1, n_pad)
    u, gs, gn = u[:, :n_seq], gs[:n_seq], gn[:n_seq]
    n_state = st_re.shape[1] * st_re.shape[2]
    y_ssm, h_re, h_im = _s5_step_call(u, st_re.reshape(n_seq, n_state), st_im.reshape(n_seq, n_state), sw)
    cache_t = cache_kv.transpose(0, 2, 3, 4, 1)
    kc, vc_t = _compress_sample_call(cache_t, page_table, cw)
    ncp = past_len // CMP_STRIDE
    n_blk = -(-(past_len + 1) // SLC_BLOCK)
    nbp = _round_up(n_blk, LANES)
    ovl = _overlap_matrix(ncp, ncp - 1, n_blk, nbp)
    tri = (jnp.arange(nbp)[:, None] < jnp.arange(nbp)[None, :]).astype(BF16)
    q8 = q_t[:, :, :n_seq].transpose(2, 0, 1)
    ocmp, idx = _sample_select_call(q8, kc, vc_t, ovl, tri, past_len)
    idx = idx[:, :KV_HEADS, :TOP_N].reshape(n_seq, KV_HEADS * TOP_N)
    gates8 = gates_t[:, :3 * Q_PER_KV, :n_seq].reshape(KV_HEADS, Q_PER_KV, 3, n_seq).transpose(3, 0, 1, 2)
    gates8 = jnp.pad(gates8.reshape(n_seq, N_HEADS, 3), ((0, 0), (0, 0), (0, LANES - 3)))
    per_head = lambda a: jnp.repeat(a.transpose(1, 0, 2), Q_PER_KV, axis=1)
    new_k = lambda k: per_head(k[:, :n_seq, HEAD_DIM:])
    new_v = lambda v_t: per_head(v_t[:, 0, :HEAD_DIM, :n_seq].transpose(0, 2, 1))
    o8 = _sample_attend_call(idx, page_table, q8[:, :, HEAD_DIM:], ocmp, gates8, new_k(ks), new_v(vs_t),
                             new_k(kw), new_v(vw_t), cache_win.transpose(0, 2, 3, 4, 1), cache_t,
                             max(win_buf + 1 - WINDOW, 0))
    o = o8.reshape(n_seq, NSA_W)
    h_new = _outmix_call(h, y_ssm, gs, o, gn, prm["w_glu"], prm["w_out"])
    kv_rows = kv_t[0, :, :n_seq].T.reshape(n_seq, 1, 4, KV_HEADS, HEAD_DIM)
    win_new = win_t[0, :, :n_seq].T.reshape(n_seq, 1, 2, KV_HEADS, HEAD_DIM)
    wrows = jnp.concatenate([cache_win, win_new], axis=1)
    wrows = wrows[:, wrows.shape[1] - min(WINDOW, wrows.shape[1]):]
    state = lambda s: s.reshape(st_re.shape)
    return h_new, kv_rows, wrows, state(h_re), state(h_im)


def kernel(x_prompt, x_sample, cache_kv, cache_win, state_ssm_re, state_ssm_im, page_table, norm_w, w_in, gate_b,
           q_norm_w, k_norm_w, cmp_pe, cmp_w1, cmp_b1, cmp_w2, ssm_lam_re, ssm_lam_im, ssm_log_step, ssm_b_re,
           ssm_b_im, ssm_c_re, ssm_c_im, ssm_d, w_glu, w_out):
    p = dict(norm_w=norm_w, w_in=w_in, gate_b=gate_b, q_norm_w=q_norm_w, k_norm_w=k_norm_w, cmp_pe=cmp_pe,
             cmp_w1=cmp_w1, cmp_b1=cmp_b1, cmp_w2=cmp_w2, ssm_lam_re=ssm_lam_re, ssm_lam_im=ssm_lam_im,
             ssm_log_step=ssm_log_step, ssm_b_re=ssm_b_re, ssm_b_im=ssm_b_im, ssm_c_re=ssm_c_re,
             ssm_c_im=ssm_c_im, ssm_d=ssm_d, w_glu=w_glu, w_out=w_out)
    b_p, s_p, d_model = x_prompt.shape
    b_s, s_s, _ = x_sample.shape
    assert s_s == 1, "the sample group decodes one token per sequence"
    h_p = x_prompt.reshape(b_p * s_p, d_model)
    h_s = x_sample.reshape(b_s, d_model)
    outs_p, outs_s = [], []
    for l in range(norm_w.shape[0]):
        weights = (_prep_params(p, l), _prep_compress(p, l), _prep_s5_chunked(p, l))
        h_p, *rest_p = _prompt_layer(h_p, *weights, b_p, s_p)
        h_s, *rest_s = _sample_layer(h_s, *weights, cache_kv[l], cache_win[l], state_ssm_re[l],
                                     state_ssm_im[l], page_table)
        outs_p.append(rest_p)
        outs_s.append(rest_s)
    stack = lambda outs, i: jnp.stack([o[i] for o in outs])
    return (h_p.reshape(x_prompt.shape), h_s.reshape(x_sample.shape),
            stack(outs_p, 0), stack(outs_s, 0), stack(outs_p, 1), stack(outs_s, 1),
            stack(outs_p, 2), stack(outs_p, 3), stack(outs_s, 2), stack(outs_s, 3))
```

```python
import functools
import math

import jax
import jax.numpy as jnp
from jax import lax
from jax.experimental import pallas as pl
from jax.experimental.pallas import tpu as pltpu

F32 = jnp.float32
BF16 = jnp.bfloat16
I32 = jnp.int32

LANES = 128
SUBLANES = 8
VMEM_LIMIT_BYTES = 56 * 1024 * 1024

HEAD_DIM = 64
N_HEADS = 8
KV_HEADS = 2
Q_PER_KV = N_HEADS // KV_HEADS
SSM_W = 512
SSM_GROUP = 16
SSM_STATE = 64
NSA_W = N_HEADS * HEAD_DIM
CMP_BLOCK = 32
CMP_STRIDE = 16
CMP_HID = 2 * HEAD_DIM
SLC_BLOCK = 64
TOP_N = 16
N_LOCAL_BLOCKS = 2
WINDOW = 512
ROPE_THETA = 500000.0
ROPE_DIM = HEAD_DIM // 4
RMS_EPS = 1e-6
NEG_INF = -1e30

COL_U = 0
COL_GS = SSM_W
COL_Q = 2 * SSM_W
COL_GN = 2 * SSM_W + NSA_W
COL_KV = 2 * SSM_W + 2 * NSA_W
COL_GL = COL_KV + 6 * KV_HEADS * HEAD_DIM
IN_W_PAD = COL_GL + LANES

PROJ_TILE = 512
ATTN_TILE = 256
SSM_CHUNK = 16
SSM_SLAB_GROUPS = LANES // SSM_GROUP
SSM_SLABS = SSM_W // LANES


def _cparams(sem):
    return pltpu.CompilerParams(dimension_semantics=sem, vmem_limit_bytes=VMEM_LIMIT_BYTES)


def _sigmoid(x):
    return 1.0 / (1.0 + jnp.exp(-x))


def _dot(a, b):
    return jnp.dot(a, b, preferred_element_type=F32)


def _dot_nt(a, b):
    return lax.dot_general(a, b, (((1,), (1,)), ((), ())), preferred_element_type=F32)


def _proj_body(x_ref, nw_ref, w_ref, qnw_ref, knw_ref, gb_ref, ra_ref, rb_ref, rc_ref,
               u_ref, gs_ref, gn_ref, qt_ref, kvt_ref, wint_ref, gt_ref,
               ks_ref, vst_ref, kw_ref, vwt_ref, *, tm, tv, tiles_per_seq):
    x = x_ref[...]
    ms = jnp.mean(x * x, axis=-1, keepdims=True)
    h = (x * lax.rsqrt(ms + RMS_EPS) * nw_ref[...]).astype(BF16)

    def mm(c0, c1):
        return _dot(h, w_ref[:, c0:c1])

    lane = lax.broadcasted_iota(I32, (tm, LANES), 1)
    lo = lane < HEAD_DIM
    ra = ra_ref[...]
    rb = rb_ref[...]
    rc = rc_ref[...]

    def norm_rope(s, wrow):
        s2 = s * s
        slo = jnp.sum(jnp.where(lo, s2, 0.0), axis=-1, keepdims=True)
        shi = jnp.sum(jnp.where(lo, 0.0, s2), axis=-1, keepdims=True)
        msq = jnp.where(lo, slo, shi) * (1.0 / HEAD_DIM)
        y = s * lax.rsqrt(msq + RMS_EPS) * wrow
        half = ROPE_DIM // 2
        return y * ra + pltpu.roll(y, LANES - half, 1) * rb + pltpu.roll(y, half, 1) * rc

    def hi_half(y, head):
        src = pltpu.roll(y, HEAD_DIM, 1) if head == 0 else y
        return jnp.where(lo, 0.0, src)

    zq = mm(COL_Q, COL_GN)
    zkv = mm(COL_KV, COL_GL)
    zgl = mm(COL_GL, IN_W_PAD)

    qnw = qnw_ref[...]
    scale = HEAD_DIM ** -0.5 * math.log2(math.e)
    zeros_t = jnp.zeros((HEAD_DIM, tm), F32)
    for j in range(N_HEADS // 2):
        y_t = (norm_rope(zq[:, j * LANES:(j + 1) * LANES], qnw) * scale).T
        for head in range(2):
            q_t = jnp.concatenate([zeros_t, y_t[head * HEAD_DIM:(head + 1) * HEAD_DIM]], axis=0)
            qt_ref[2 * j + head] = q_t.astype(BF16)

    kc = norm_rope(zkv[:, 0:LANES], knw_ref[0:1, :])
    vc = zkv[:, LANES:2 * LANES]
    ks = norm_rope(zkv[:, 2 * LANES:3 * LANES], knw_ref[1:2, :])
    vs = zkv[:, 3 * LANES:4 * LANES]
    kw = norm_rope(zkv[:, 4 * LANES:5 * LANES], knw_ref[2:3, :])
    vw = zkv[:, 5 * LANES:6 * LANES]
    vs_t, vw_t = vs.T, vw.T
    for i, rows_t in enumerate((kc.T, vc.T, ks.T, vs_t)):
        kvt_ref[i * LANES:(i + 1) * LANES, :] = rows_t
    for i, rows_t in enumerate((kw.T, vw_t)):
        wint_ref[i * LANES:(i + 1) * LANES, :] = rows_t

    row = lax.broadcasted_iota(I32, (tm, LANES), 0)
    pos = (pl.program_id(0) % tiles_per_seq) * tm + row
    onehot = jnp.where(lane == lax.shift_right_logical(pos, 6), 1.0, 0.0)
    ones_t = jnp.where(lax.broadcasted_iota(I32, (HEAD_DIM, tm), 0) == 0, 1.0, 0.0)
    for g in range(KV_HEADS):
        ks_ref[g] = jnp.where(lo, onehot, hi_half(ks, g)).astype(BF16)
        kw_ref[g] = hi_half(kw, g).astype(BF16)
        for v_t, vt_ref in ((vs_t, vst_ref), (vw_t, vwt_ref)):
            v_aug = jnp.concatenate([v_t[g * HEAD_DIM:(g + 1) * HEAD_DIM], ones_t], axis=0).astype(BF16)
            for t in range(tm // tv):
                vt_ref[g, t] = v_aug[:, t * tv:(t + 1) * tv]

    gates_t = _sigmoid(zgl + gb_ref[...]).T
    for g in range(KV_HEADS):
        gt_ref[g] = gates_t[g * 3 * Q_PER_KV:g * 3 * Q_PER_KV + 2 * SUBLANES]

    zu = mm(COL_U, COL_GS)
    for j in range(SSM_SLABS):
        u_ref[j] = zu[:, j * LANES:(j + 1) * LANES]
    for g_ref, cols in ((gs_ref, (COL_GS, COL_Q)), (gn_ref, (COL_GN, COL_KV))):
        g = mm(*cols)
        g_ref[...] = (g * _sigmoid(g)).astype(BF16)


def _proj_call(x2d, w_pad, norm_w, qnw, knw, gb, tabs, batch, seq):
    T, D = x2d.shape
    tm = min(PROJ_TILE, seq)
    tv = min(ATTN_TILE, seq)
    assert T == batch * seq and seq % tm == 0 and tm % tv == 0 and tv % LANES == 0
    tps = seq // tm
    row_spec = lambda w: pl.BlockSpec((tm, w), lambda i: (i, 0))
    full = lambda a: pl.BlockSpec(a.shape, lambda i: (0,) * a.ndim)
    tab_spec = pl.BlockSpec((tm, LANES), lambda i: (i % tps, 0))
    head_spec = lambda n: pl.BlockSpec((n, tm, LANES), lambda i: (0, i, 0))
    head_t_spec = lambda n, rows: pl.BlockSpec((n, rows, tm), lambda i: (0, 0, i))
    cache_t_spec = lambda rows: pl.BlockSpec((None, rows, tm), lambda i: (i // tps, 0, i % tps))
    tile_t_spec = pl.BlockSpec((KV_HEADS, tm // tv, LANES, tv), lambda i: (0, i, 0, 0))
    tile_t_sds = jax.ShapeDtypeStruct((KV_HEADS, T // tv, LANES, tv), BF16)
    out_shape = (
        jax.ShapeDtypeStruct((SSM_SLABS, T, LANES), F32),
        jax.ShapeDtypeStruct((T, SSM_W), BF16),
        jax.ShapeDtypeStruct((T, NSA_W), BF16),
        jax.ShapeDtypeStruct((N_HEADS, LANES, T), BF16),
        jax.ShapeDtypeStruct((batch, 4 * LANES, seq), F32),
        jax.ShapeDtypeStruct((batch, 2 * LANES, seq), F32),
        jax.ShapeDtypeStruct((KV_HEADS, 2 * SUBLANES, T), F32),
        jax.ShapeDtypeStruct((KV_HEADS, T, LANES), BF16),
        tile_t_sds,
        jax.ShapeDtypeStruct((KV_HEADS, T, LANES), BF16),
        tile_t_sds,
    )
    out_specs = (head_spec(SSM_SLABS), row_spec(SSM_W), row_spec(NSA_W), head_t_spec(N_HEADS, LANES),
                 cache_t_spec(4 * LANES), cache_t_spec(2 * LANES), head_t_spec(KV_HEADS, 2 * SUBLANES),
                 head_spec(KV_HEADS), tile_t_spec, head_spec(KV_HEADS), tile_t_spec)
    return pl.pallas_call(
        functools.partial(_proj_body, tm=tm, tv=tv, tiles_per_seq=tps),
        out_shape=out_shape,
        grid=(T // tm,),
        in_specs=[row_spec(D), full(norm_w), full(w_pad), full(qnw), full(knw), full(gb),
                  tab_spec, tab_spec, tab_spec],
        out_specs=out_specs,
        compiler_params=_cparams(("parallel",)),
        name="proj",
    )(x2d, norm_w, w_pad, qnw, knw, gb, *tabs)


def _prep_params(p, l):
    w_in = p["w_in"][l]
    d_model, in_w = w_in.shape
    tile2 = lambda v: jnp.tile(v, (1, LANES // HEAD_DIM))
    return {
        "w_in": jnp.pad(w_in.astype(BF16), ((0, 0), (0, IN_W_PAD - in_w))),
        "norm_w": p["norm_w"][l].reshape(1, d_model).astype(F32),
        "qnw": tile2(p["q_norm_w"][l].reshape(1, HEAD_DIM)).astype(F32),
        "knw": tile2(p["k_norm_w"][l]).astype(F32),
        "gb": jnp.pad(p["gate_b"][l].reshape(1, -1).astype(F32), ((0, 0), (0, LANES - 3 * N_HEADS))),
        "w_glu": p["w_glu"][l].astype(BF16),
        "w_out": p["w_out"][l].astype(BF16),
    }


def _rope_tables(pos):
    half = ROPE_DIM // 2
    inv = ROPE_THETA ** (-jnp.arange(half, dtype=F32) / half)
    ang = pos.astype(F32)[:, None] * inv
    cos, sin = jnp.cos(ang), jnp.sin(ang)
    n = pos.shape[0]
    rest = HEAD_DIM - ROPE_DIM
    a = jnp.concatenate([cos, cos, jnp.ones((n, rest), F32)], axis=-1)
    b = jnp.concatenate([-sin, jnp.zeros((n, HEAD_DIM - half), F32)], axis=-1)
    c = jnp.concatenate([jnp.zeros((n, half), F32), sin, jnp.zeros((n, rest), F32)], axis=-1)
    return tuple(jnp.tile(t, (1, LANES // HEAD_DIM)) for t in (a, b, c))


def _outmix_body(x_ref, y_ref, gs_ref, o_ref, gn_ref, wg_ref, wo_ref, out_ref):
    y = jnp.concatenate([y_ref[j] for j in range(SSM_SLABS)], axis=-1)
    ab = _dot(y.astype(BF16), wg_ref[...])
    ssm = ab[:, :SSM_W] * _sigmoid(ab[:, SSM_W:]) * gs_ref[...].astype(F32)
    nsa = o_ref[...].astype(F32) * gn_ref[...].astype(F32)
    acc = _dot(ssm.astype(BF16), wo_ref[0:SSM_W, :])
    acc += _dot(nsa.astype(BF16), wo_ref[SSM_W:, :])
    out_ref[...] = x_ref[...] + acc


def _outmix_call(x2d, y_ssm, g_ssm, o_nsa, g_nsa, w_glu, w_out):
    T, D = x2d.shape
    tm = min(512, T)
    row_spec = lambda w: pl.BlockSpec((tm, w), lambda i: (i, 0))
    full = lambda a: pl.BlockSpec(a.shape, lambda i: (0,) * a.ndim)
    return pl.pallas_call(
        _outmix_body,
        out_shape=jax.ShapeDtypeStruct((T, D), F32),
        grid=(T // tm,),
        in_specs=[row_spec(D), pl.BlockSpec((SSM_SLABS, tm, LANES), lambda i: (0, i, 0)),
                  row_spec(SSM_W), row_spec(NSA_W), row_spec(NSA_W),
                  full(w_glu), full(w_out)],
        out_specs=row_spec(D),
        compiler_params=_cparams(("parallel",)),
        name="outmix",
    )(x2d, y_ssm, g_ssm, o_nsa, g_nsa, w_glu, w_out)


def _gelu_tanh(x):
    c = math.sqrt(2.0 / math.pi)
    return 0.5 * x * (1.0 + jnp.tanh(c * (x + 0.044715 * (x * x * x))))


def _compress_rows(rows_refs, pe_ref, wa_ref, wb_ref, b1_ref, w2_ref, kc_ref, vc_ref, nch):
    lane = lax.broadcasted_iota(I32, (nch, LANES), 1)
    for kvi, out_ref in ((0, kc_ref), (1, vc_ref)):
        rows_ref = rows_refs[kvi]
        pa = jnp.zeros((nch, 2 * CMP_HID), F32)
        pb = jnp.zeros((nch, 2 * CMP_HID), F32)
        for j0 in range(0, CMP_STRIDE, 2):
            xs = [rows_ref[pl.ds(j, nch, stride=CMP_STRIDE), :] for j in (j0, j0 + 1)]
            xa = jnp.concatenate([xs[i] + pe_ref[kvi, 0, j0 + i:j0 + i + 1, :] for i in range(2)], axis=-1)
            xb = jnp.concatenate([xs[i] + pe_ref[kvi, 1, j0 + i:j0 + i + 1, :] for i in range(2)], axis=-1)
            wsl = slice(j0 * LANES, (j0 + 2) * LANES)
            pa += _dot(xa.astype(BF16), wa_ref[kvi, wsl, :])
            pb += _dot(xb.astype(BF16), wb_ref[kvi, wsl, :])
        hid = _gelu_tanh(pa + pltpu.roll(pb, nch - 1, 0) + b1_ref[kvi]).astype(BF16)
        for g in range(KV_HEADS):
            o = _dot(hid, w2_ref[kvi, g])
            if kvi == 1:
                o = jnp.where(lane == HEAD_DIM, 1.0, o).T
            out_ref[g] = o.astype(BF16)


def _compress_prompt_body(kvt_ref, pe_ref, wa_ref, wb_ref, b1_ref, w2_ref, kc_ref, vc_ref,
                          krows_ref, vrows_ref, *, nch):
    for c in range(kvt_ref.shape[1] // LANES):
        cs = slice(c * LANES, (c + 1) * LANES)
        krows_ref[cs, :] = kvt_ref[0:LANES, cs].T
        vrows_ref[cs, :] = kvt_ref[LANES:2 * LANES, cs].T
    _compress_rows((krows_ref, vrows_ref), pe_ref, wa_ref, wb_ref, b1_ref, w2_ref, kc_ref, vc_ref, nch)


def _compress_prompt_call(kv_t, cw, batch, seq):
    nch = seq // CMP_STRIDE
    full = lambda a: pl.BlockSpec(a.shape, lambda b: (0,) * a.ndim)
    out_spec = pl.BlockSpec((KV_HEADS, nch, LANES), lambda b: (0, b, 0))
    out_sds = jax.ShapeDtypeStruct((KV_HEADS, batch * nch, LANES), BF16)
    out_t_spec = pl.BlockSpec((KV_HEADS, None, LANES, nch), lambda b: (0, b, 0, 0))
    out_t_sds = jax.ShapeDtypeStruct((KV_HEADS, batch, LANES, nch), BF16)
    return pl.pallas_call(
        functools.partial(_compress_prompt_body, nch=nch),
        out_shape=(out_sds, out_t_sds),
        grid=(batch,),
        in_specs=[pl.BlockSpec((None, 2 * LANES, seq), lambda b: (b, 0, 0)),
                  full(cw["pe"]), full(cw["wa"]), full(cw["wb"]), full(cw["b1"]), full(cw["w2"])],
        out_specs=(out_spec, out_t_spec),
        scratch_shapes=[pltpu.VMEM((seq, LANES), F32)] * 2,
        compiler_params=_cparams(("parallel",)),
        name="compress_prompt",
    )(kv_t, cw["pe"], cw["wa"], cw["wb"], cw["b1"], cw["w2"])


def _prep_compress(p, l):
    eye = jnp.eye(KV_HEADS, dtype=F32)
    w1 = p["cmp_w1"][l].reshape(2, 2, CMP_STRIDE, HEAD_DIM, CMP_HID)
    wexp = jnp.einsum("khjdn,ge->khjgden", w1, eye).reshape(2, 2, CMP_STRIDE * LANES, KV_HEADS * CMP_HID)
    pe = p["cmp_pe"][l].reshape(2, 2, CMP_STRIDE, HEAD_DIM)
    w2 = p["cmp_w2"][l]
    zeros = jnp.zeros_like(w2[0])
    w2k = jnp.concatenate([zeros, w2[0]], axis=-1)
    w2v = jnp.concatenate([w2[1], zeros], axis=-1)
    w2e = jnp.stack([jnp.einsum("hd,ge->gehd", w, eye).reshape(KV_HEADS, KV_HEADS * CMP_HID, LANES)
                     for w in (w2k, w2v)])
    return {
        "pe": jnp.tile(pe, (1, 1, 1, KV_HEADS)).astype(F32),
        "wa": wexp[:, 0].astype(BF16),
        "wb": wexp[:, 1].astype(BF16),
        "b1": jnp.tile(p["cmp_b1"][l].reshape(2, 1, CMP_HID), (1, 1, KV_HEADS)).astype(F32),
        "w2": w2e.astype(BF16),
    }


def _overlap_matrix(n_tok_pad, n_tok, n_blk, n_cols=LANES):
    c_start = jnp.arange(n_tok_pad) * CMP_STRIDE
    blk = jnp.arange(n_cols)
    ov = ((c_start[:, None] < (blk[None, :] + 1) * SLC_BLOCK)
          & (c_start[:, None] + CMP_BLOCK > blk[None, :] * SLC_BLOCK)
          & (jnp.arange(n_tok_pad)[:, None] < n_tok) & (blk[None, :] < n_blk))
    return ov.astype(BF16)


def _topk_select_t(w_ref, imp_t, q0, tq):
    nb = imp_t.shape[0]
    n_i = lax.broadcasted_iota(I32, (nb, tq), 0)
    qblk = lax.shift_right_logical(q0 + lax.broadcasted_iota(I32, (nb, tq), 1), 6)
    causal = n_i <= qblk
    forced = (n_i == 0) | (n_i >= qblk - (N_LOCAL_BLOCKS - 1))
    w_ref[...] = jnp.where(causal, jnp.where(forced, jnp.inf, imp_t), -jnp.inf)
    last_blk = lax.shift_right_logical(q0 + tq - 1, 6)
    n_grp = nb // SUBLANES
    rank = [jnp.zeros((SUBLANES, tq), F32) for _ in range(n_grp)]
    grp_i = lax.broadcasted_iota(I32, (SUBLANES, tq), 0)

    def count_group(mg, rank):
        rank = list(rank)
        for mi in range(SUBLANES):
            m = mg * SUBLANES + mi
            wm = w_ref[m:m + 1, :]
            for ng in range(n_grp):
                w = w_ref[ng * SUBLANES:(ng + 1) * SUBLANES, :]
                if ng > mg:
                    beats = jnp.where(wm >= w, 1.0, 0.0)
                elif ng < mg:
                    beats = jnp.where(wm > w, 1.0, 0.0)
                else:
                    beats = jnp.where(grp_i > mi, jnp.where(wm >= w, 1.0, 0.0), jnp.where(wm > w, 1.0, 0.0))
                rank[ng] = rank[ng] + beats
        return tuple(rank)

    rank = tuple(rank)
    for mg in range(n_grp):
        rank = lax.cond(mg * SUBLANES <= last_blk, functools.partial(count_group, mg), lambda r: r, rank)
    return causal & (jnp.concatenate(rank, axis=0) < TOP_N)


def _flash_tiles_t(tiles, q_ts, ms, accs):
    def scores(i):
        k, _, mask = tiles[i]
        s = [_dot(k, q_t) for q_t in q_ts]
        return s if mask is None else [jnp.where(mask, x, NEG_INF) for x in s]

    ahead = 2
    pending = {i: scores(i) for i in range(min(ahead, len(tiles)))}
    for i, (_, v_t, _) in enumerate(tiles):
        s = pending.pop(i)
        m_new = [jnp.maximum(m, jnp.max(x, axis=0, keepdims=True)) for m, x in zip(ms, s)]
        alpha = [jnp.exp2(m - mn) for m, mn in zip(ms, m_new)]
        p = [jnp.exp2(x - mn).astype(BF16) for x, mn in zip(s, m_new)]
        pv = [_dot(v_t, x) for x in p]
        if i + ahead < len(tiles):
            pending[i + ahead] = scores(i + ahead)
        accs = [a * acc + x for a, acc, x in zip(alpha, accs, pv)]
        ms = m_new
    return tuple(ms), tuple(accs)


def _attn_body(qt_ref, kc_ref, vct_ref, ovlt_ref, ks_ref, vst_ref, kw_ref, vwt_ref, gate_ref, o_ref,
               qa_ref, ocmp_ref, w_ref, *, tq, ncp):
    R = Q_PER_KV
    qt = pl.program_id(2)
    q0 = qt * tq
    nbs = LANES // 2

    c_i = lax.broadcasted_iota(I32, (ncp, tq), 0)
    qpos_c = q0 + lax.broadcasted_iota(I32, (ncp, tq), 1)
    cmask = c_i * CMP_STRIDE + (CMP_BLOCK - 1) <= qpos_c
    kc = kc_ref[...]
    s = [jnp.where(cmask, _dot(kc, qt_ref[r]), NEG_INF) for r in range(R)]
    e = [jnp.where(cmask, jnp.exp2(x - jnp.max(x, axis=0, keepdims=True)), 0.0) for x in s]
    l = [jnp.sum(x, axis=0, keepdims=True) for x in e]
    p = [(x * (1.0 / jnp.where(y > 0.0, y, 1.0))).astype(BF16) for x, y in zip(e, l)]
    imp = sum(_dot(ovlt_ref[...], x) for x in p)
    for r in range(R):
        ocmp_ref[r] = _dot(vct_ref[...], p[r])

    sel = _topk_select_t(w_ref, imp[:nbs], q0, tq)
    bias = jnp.concatenate([jnp.where(sel, 0.0, NEG_INF), jnp.zeros((LANES - nbs, tq), F32)], axis=0)
    for r in range(R):
        qa_ref[r] = (qt_ref[r].astype(F32) + bias).astype(BF16)

    key_i = lax.broadcasted_iota(I32, (tq, tq), 0)
    qry_i = lax.broadcasted_iota(I32, (tq, tq), 1)
    init = (tuple(jnp.full((1, tq), -jnp.inf, F32) for _ in range(R)),
            tuple(jnp.zeros((LANES, tq), F32) for _ in range(R)))

    def tiles(k_ref, vt_ref, q_ref, js, masks, state):
        ts = [(k_ref[pl.ds(pl.multiple_of(j * tq, tq), tq), :], vt_ref[j], mask) for j, mask in zip(js, masks)]
        return _flash_tiles_t(ts, [q_ref[r] for r in range(R)], *state)

    def last_tiles(k_ref, vt_ref, q_ref, n, first_mask, state):
        js = [qt - (n - 1 - t) for t in range(n)]
        masks = [first_mask] + [None] * (n - 2) + [key_i <= qry_i] if n > 1 else [key_i <= qry_i]
        return tiles(k_ref, vt_ref, q_ref, js, masks, state)

    GROUP = 4

    def slc_group(i, st):
        return tiles(ks_ref, vst_ref, qa_ref, [GROUP * i + t for t in range(GROUP)], [None] * GROUP, st)

    state = lax.fori_loop(0, qt // GROUP, slc_group, init)
    tails = [functools.partial(last_tiles, ks_ref, vst_ref, qa_ref, n, None) for n in range(1, GROUP + 1)]
    _, acc_s = lax.switch(qt % GROUP, tails, state)

    nwin = WINDOW // tq
    wins = [functools.partial(last_tiles, kw_ref, vwt_ref, qt_ref, n, (key_i > qry_i) if n == nwin + 1 else None)
            for n in range(1, nwin + 2)]
    _, acc_w = lax.switch(jnp.minimum(qt, nwin), wins, init)

    outs = []
    for r in range(R):
        a_s = acc_s[r]
        a_w = acc_w[r]
        g = lambda k: gate_ref[3 * r + k:3 * r + k + 1, :]
        o = (g(0) * ocmp_ref[r] + g(1) * (a_s / a_s[HEAD_DIM:HEAD_DIM + 1, :])
             + g(2) * (a_w / a_w[HEAD_DIM:HEAD_DIM + 1, :]))
        outs.append(o[:HEAD_DIM])
    o_ref[...] = jnp.concatenate(outs, axis=0).T.astype(o_ref.dtype)


def _attn_call(q_t, kc, vc_t, ovl_t, ks, vs_t, kw, vw_t, gates_t, batch, seq):
    tq = min(ATTN_TILE, seq)
    nq = seq // tq
    ncp = kc.shape[1] // batch
    R = Q_PER_KV
    assert seq // SLC_BLOCK <= LANES // 2 and WINDOW % tq == 0
    k_spec = pl.BlockSpec((None, seq, LANES), lambda b, g, t: (g, b, 0))
    vt_spec = pl.BlockSpec((None, nq, LANES, tq), lambda b, g, t: (g, b, 0, 0))
    acc = pltpu.VMEM((R, LANES, tq), F32)
    return pl.pallas_call(
        functools.partial(_attn_body, tq=tq, ncp=ncp),
        out_shape=jax.ShapeDtypeStruct((batch * seq, NSA_W), BF16),
        grid=(batch, KV_HEADS, nq),
        in_specs=[pl.BlockSpec((R, LANES, tq), lambda b, g, t: (g, 0, b * nq + t)),
                  pl.BlockSpec((None, ncp, LANES), lambda b, g, t: (g, b, 0)),
                  pl.BlockSpec((None, None, LANES, ncp), lambda b, g, t: (g, b, 0, 0)),
                  pl.BlockSpec(ovl_t.shape, lambda b, g, t: (0, 0)),
                  k_spec, vt_spec, k_spec, vt_spec,
                  pl.BlockSpec((None, 2 * SUBLANES, tq), lambda b, g, t: (g, 0, b * nq + t))],
        out_specs=pl.BlockSpec((tq, R * HEAD_DIM), lambda b, g, t: (b * nq + t, g)),
        scratch_shapes=[pltpu.VMEM((R, LANES, tq), BF16), acc, pltpu.VMEM((LANES // 2, tq), F32)],
        compiler_params=_cparams(("parallel", "parallel", "arbitrary")),
        name="nsa_prompt",
    )(q_t, kc, vc_t, ovl_t, ks, vs_t, kw, vw_t, gates_t)


def _s5_discretise(p, l):
    lr = p["ssm_lam_re"][l].astype(F32)
    li = p["ssm_lam_im"][l].astype(F32)
    dt = jnp.exp(p["ssm_log_step"][l].astype(F32))[:, None]

    def apow(t):
        mag, ang = jnp.exp(lr * dt * t), li * dt * t
        return mag * jnp.cos(ang), mag * jnp.sin(ang)

    a_re, a_im = apow(1.0)
    den = lr * lr + li * li
    nr, ni = a_re - 1.0, a_im
    f_re, f_im = (nr * lr + ni * li) / den, (ni * lr - nr * li) / den
    br, bi = p["ssm_b_re"][l].astype(F32), p["ssm_b_im"][l].astype(F32)
    bb_re = f_re[..., None] * br - f_im[..., None] * bi
    bb_im = f_re[..., None] * bi + f_im[..., None] * br
    return apow, bb_re, bb_im


def _s5_expand_body(pst_ref, kt_ref, cp_ref, wcol_ref, wst_ref, wout_ref):
    T, E, N, C = SSM_CHUNK, SSM_SLAB_GROUPS, SSM_STATE, SSM_GROUP
    row = lax.broadcasted_iota(I32, (LANES, LANES), 0)
    lane = lax.broadcasted_iota(I32, (LANES, LANES), 1)
    row_grp = row // C
    low = lane < N
    own_grp = row_grp == lane // C
    lane_n = lax.broadcasted_iota(I32, (N, LANES), 1)
    wcol_ref[T * LANES:(T + 1) * LANES, :] = jnp.zeros((LANES, LANES), BF16)
    wout_ref[0, :, LANES:2 * LANES] = jnp.zeros((2 * E * N, LANES), BF16)
    for i in range(T):
        wcol_ref[i * LANES:(i + 1) * LANES, :] = jnp.where(own_grp, kt_ref[i], 0.0).astype(BF16)
        x = pst_ref[i]
        x_sw = pltpu.roll(x, N, 1)
        halves = (jnp.where(low, x, x_sw), jnp.where(low, x_sw, x))
        for ri, x2 in enumerate(halves):
            for q in range(E // 2):
                own = row_grp == 2 * q + jnp.where(low, 0, 1)
                c0 = ri * E * N + q * LANES
                wst_ref[i * LANES:(i + 1) * LANES, c0:c0 + LANES] = jnp.where(own, x2, 0.0).astype(BF16)
    for i in range(T + 1):
        pair, half = (0, 0) if i == 0 else ((i + 1) // 2, (i - 1) % 2)
        m_t = cp_ref[i].T
        for ri in range(2):
            rows = m_t[ri * N:(ri + 1) * N]
            for g in range(E):
                r0 = ri * E * N + g * N
                wout_ref[pair, r0:r0 + N, half * LANES:(half + 1) * LANES] = (
                    jnp.where(lane_n // C == g, rows, 0.0).astype(BF16))


def _prep_s5_chunked(p, l):
    hi = lax.Precision.HIGHEST
    apow, bb_re, bb_im = _s5_discretise(p, l)
    c_re, c_im = p["ssm_c_re"][l].astype(F32), p["ssm_c_im"][l].astype(F32)
    T, J, E, N, C = SSM_CHUNK, SSM_SLABS, SSM_SLAB_GROUPS, SSM_STATE, SSM_GROUP
    pw_re, pw_im = apow(jnp.arange(T + 1, dtype=F32)[:, None, None])
    bt_re, bt_im = bb_re.transpose(0, 2, 1), bb_im.transpose(0, 2, 1)
    p_re = pw_re[:, :, None, :] * bt_re - pw_im[:, :, None, :] * bt_im
    p_im = pw_re[:, :, None, :] * bt_im + pw_im[:, :, None, :] * bt_re
    kt = (jnp.einsum("tgkn,gcn->tgkc", p_re[:T], jnp.tile(c_re, (1, E, 1)), precision=hi)
          - jnp.einsum("tgkn,gcn->tgkc", p_im[:T], jnp.tile(c_im, (1, E, 1)), precision=hi))
    cp_re = c_re * pw_re[:, :, None, :] - c_im * pw_im[:, :, None, :]
    cp_im = c_re * pw_im[:, :, None, :] + c_im * pw_re[:, :, None, :]
    slab = lambda x: jnp.moveaxis(x.reshape(x.shape[0], J, E * C, LANES), 1, 0)
    pst = slab(jnp.concatenate([p_re[:T][::-1], p_im[:T][::-1]], axis=-1))
    ktile = slab(kt[::-1])
    cpn = slab(jnp.concatenate([cp_re, -cp_im], axis=-1))
    blk = lambda n: pl.BlockSpec((None, n, LANES, LANES), lambda j: (j, 0, 0, 0))
    w_col, w_st, w_out = pl.pallas_call(
        _s5_expand_body,
        out_shape=(jax.ShapeDtypeStruct((J, (T + 1) * LANES, LANES), BF16),
                   jax.ShapeDtypeStruct((J, T * LANES, 2 * E * N), BF16),
                   jax.ShapeDtypeStruct((J, T // 2 + 1, 2 * E * N, 2 * LANES), BF16)),
        grid=(J,),
        in_specs=[blk(T), blk(T), blk(T + 1)],
        out_specs=(pl.BlockSpec((None, (T + 1) * LANES, LANES), lambda j: (j, 0, 0)),
                   pl.BlockSpec((None, T * LANES, 2 * E * N), lambda j: (j, 0, 0)),
                   pl.BlockSpec((None, T // 2 + 1, 2 * E * N, 2 * LANES), lambda j: (j, 0, 0, 0))),
        compiler_params=_cparams(("parallel",)),
        name="s5_expand",
    )(pst, ktile, cpn)
    return {
        "w_col": w_col, "w_st": w_st, "w_out": w_out,
        "a_re": pw_re[T].reshape(J, 1, E * N), "a_im": pw_im[T].reshape(J, 1, E * N),
        "a1_re": pw_re[1].reshape(J, 1, E * N), "a1_im": pw_im[1].reshape(J, 1, E * N),
        "d": p["ssm_d"][l].reshape(J, 1, LANES).astype(F32),
    }


def _s5_prompt_body(u_ref, wcol_ref, wst_ref, are_ref, aim_ref, wout_ref, d_ref,
                    y_ref, hre_ref, him_ref, xs_ref, hp_ref, *, n_chunks):
    T = SSM_CHUNK
    ns = SSM_SLAB_GROUPS * SSM_STATE
    u_pos = [u_ref[pl.ds(s, n_chunks, stride=T), :] for s in range(T)]
    ub = jnp.concatenate(u_pos, axis=-1).astype(BF16)
    xs_ref[...] = _dot(ub, wst_ref[...])
    a_re = are_ref[...]
    a_im = aim_ref[...]

    def step(c, carry):
        hr, hi = carry
        hp_ref[pl.ds(c, 1), 0:ns] = hr
        hp_ref[pl.ds(c, 1), ns:2 * ns] = hi
        xr = xs_ref[pl.ds(c, 1), 0:ns]
        xi = xs_ref[pl.ds(c, 1), ns:2 * ns]
        return a_re * hr - a_im * hi + xr, a_re * hi + a_im * hr + xi

    zero = jnp.zeros((1, ns), F32)
    hr, hi = lax.fori_loop(0, n_chunks, step, (zero, zero))
    hre_ref[...] = jnp.broadcast_to(hr, hre_ref.shape)
    him_ref[...] = jnp.broadcast_to(hi, him_ref.shape)
    hpb = hp_ref[...].astype(BF16)
    for t in range(0, T, 2):
        k_rows = (t + 2) * LANES
        w_pair = jnp.concatenate([wcol_ref[(T - 1 - t) * LANES:(T - 1 - t) * LANES + k_rows, :],
                                  wcol_ref[(T - 2 - t) * LANES:T * LANES, :]], axis=1)
        y_pair = _dot(ub[:, :k_rows], w_pair) + _dot(hpb, wout_ref[t // 2 + 1])
        for i in range(2):
            y_ref[pl.ds(t + i, n_chunks, stride=T), :] = (
                y_pair[:, i * LANES:(i + 1) * LANES] + d_ref[...] * u_pos[t + i])


def _s5_prompt_call(u_slab, sw, batch, seq):
    T, J = SSM_CHUNK, SSM_SLABS
    n_chunks = seq // T
    ns = SSM_SLAB_GROUPS * SSM_STATE
    row_spec = pl.BlockSpec((None, seq, LANES), lambda j, b: (j, b, 0))
    slab_spec = lambda a: pl.BlockSpec((None,) + a.shape[1:], lambda j, b: (j,) + (0,) * (a.ndim - 1))
    st_spec = pl.BlockSpec((None, None, SUBLANES, ns), lambda j, b: (b, j, 0, 0))
    st_sds = jax.ShapeDtypeStruct((batch, J, SUBLANES, ns), F32)
    y, hre, him = pl.pallas_call(
        functools.partial(_s5_prompt_body, n_chunks=n_chunks),
        out_shape=(jax.ShapeDtypeStruct(u_slab.shape, F32), st_sds, st_sds),
        grid=(J, batch),
        in_specs=[row_spec, slab_spec(sw["w_col"]), slab_spec(sw["w_st"]), slab_spec(sw["a_re"]),
                  slab_spec(sw["a_im"]), slab_spec(sw["w_out"]), slab_spec(sw["d"])],
        out_specs=(row_spec, st_spec, st_spec),
        scratch_shapes=[pltpu.VMEM((n_chunks, 2 * ns), F32), pltpu.VMEM((n_chunks, 2 * ns), F32)],
        compiler_params=_cparams(("parallel", "parallel")),
        name="s5_prompt",
    )(u_slab, sw["w_col"], sw["w_st"], sw["a_re"], sw["a_im"], sw["w_out"], sw["d"])
    n_groups = J * SSM_SLAB_GROUPS
    state = lambda h: h[:, :, 0, :].reshape(batch, n_groups, SSM_STATE)
    return y, state(hre), state(him)


def _s5_step_body(u_ref, wx_ref, are_ref, aim_ref, h0re_ref, h0im_ref, wy_ref, d_ref,
                  y_ref, hre_ref, him_ref):
    ns = SSM_SLAB_GROUPS * SSM_STATE
    for j in range(SSM_SLABS):
        sl = slice(j * ns, (j + 1) * ns)
        u = u_ref[j]
        x = _dot(u.astype(BF16), wx_ref[j])
        a_re, a_im = are_ref[j], aim_ref[j]
        h0r, h0i = h0re_ref[:, sl], h0im_ref[:, sl]
        hr = a_re * h0r - a_im * h0i + x[:, :ns]
        hi = a_re * h0i + a_im * h0r + x[:, ns:]
        hre_ref[:, sl] = hr
        him_ref[:, sl] = hi
        y_ref[j] = _dot(jnp.concatenate([hr, hi], axis=-1).astype(BF16), wy_ref[j]) + d_ref[j] * u


def _s5_step_call(u_slab, h0_re, h0_im, sw):
    J, n_tok, _ = u_slab.shape
    T = SSM_CHUNK
    full = lambda a: pl.BlockSpec(a.shape, lambda i: (0,) * a.ndim)
    wx_spec = pl.BlockSpec((J, LANES, sw["w_st"].shape[2]), lambda i: (0, T - 1, 0))
    wy_spec = pl.BlockSpec((J, None, sw["w_out"].shape[2], LANES), lambda i: (0, 0, 0, 0))
    st_sds = jax.ShapeDtypeStruct(h0_re.shape, F32)
    return pl.pallas_call(
        _s5_step_body,
        out_shape=(jax.ShapeDtypeStruct(u_slab.shape, F32), st_sds, st_sds),
        grid=(1,),
        in_specs=[full(u_slab), wx_spec, full(sw["a1_re"]), full(sw["a1_im"]), full(h0_re), full(h0_im),
                  wy_spec, full(sw["d"])],
        out_specs=(full(u_slab), full(h0_re), full(h0_re)),
        compiler_params=_cparams(("arbitrary",)),
        name="s5_step",
    )(u_slab, sw["w_st"], sw["a1_re"], sw["a1_im"], h0_re, h0_im, sw["w_out"], sw["d"])


def _compress_sample_body(pt_ref, *refs, n_pages, page_rows, nch):
    del pt_ref
    page_refs = refs[:n_pages]
    pe_ref, wa_ref, wb_ref, b1_ref, w2_ref, kc_ref, vc_ref = refs[n_pages:n_pages + 7]
    buffers = refs[n_pages + 7:]
    b = pl.program_id(0)

    @pl.when(b == 0)
    def _():
        for buf in buffers[2:]:
            buf[...] = jnp.zeros(buf.shape, F32)

    def step(stage, done):
        for i, page_ref in enumerate(page_refs):
            for rows_ref, slot in zip(stage, range(2)):
                rows_ref[i * page_rows:(i + 1) * page_rows, :] = page_ref[slot].reshape(LANES, page_rows).T
        _compress_rows(done, pe_ref, wa_ref, wb_ref, b1_ref, w2_ref, kc_ref, vc_ref, nch)

    @pl.when(b % 2 == 0)
    def _():
        step(buffers[:2], buffers[2:])

    @pl.when(b % 2 == 1)
    def _():
        step(buffers[2:], buffers[:2])


def _compress_sample_call(cache, page_table, cw):
    n_seq, n_pages = page_table.shape
    page_rows = cache.shape[-1]
    assert page_rows == LANES
    nch = n_pages * page_rows // CMP_STRIDE
    staged = lambda b: jnp.minimum(b, n_seq - 1)
    page_spec = lambda i: pl.BlockSpec((None, 2, KV_HEADS, HEAD_DIM, page_rows),
                                       lambda b, pt: (pt[staged(b), i], 0, 0, 0, 0))
    full = lambda a: pl.BlockSpec(a.shape, lambda b, pt: (0,) * a.ndim)
    out_of = lambda b: jnp.maximum(b - 1, 0)
    out_spec = pl.BlockSpec((KV_HEADS, nch, LANES), lambda b, pt: (0, out_of(b), 0))
    out_sds = jax.ShapeDtypeStruct((KV_HEADS, n_seq * nch, LANES), BF16)
    out_t_spec = pl.BlockSpec((KV_HEADS, None, LANES, nch), lambda b, pt: (0, out_of(b), 0, 0))
    out_t_sds = jax.ShapeDtypeStruct((KV_HEADS, n_seq, LANES, nch), BF16)
    weights = (cw["pe"], cw["wa"], cw["wb"], cw["b1"], cw["w2"])
    return pl.pallas_call(
        functools.partial(_compress_sample_body, n_pages=n_pages, page_rows=page_rows, nch=nch),
        out_shape=(out_sds, out_t_sds),
        grid_spec=pltpu.PrefetchScalarGridSpec(
            num_scalar_prefetch=1,
            grid=(n_seq + 1,),
            in_specs=[page_spec(i) for i in range(n_pages)] + [full(a) for a in weights],
            out_specs=(out_spec, out_t_spec),
            scratch_shapes=[pltpu.VMEM((n_pages * page_rows, LANES), F32)] * 4),
        compiler_params=_cparams(("arbitrary",)),
        name="compress_sample",
    )(page_table, *([cache] * n_pages), *weights)


def _group_rows(x0, x1):
    row = lax.broadcasted_iota(I32, x0.shape, 0)
    return jnp.where(row < Q_PER_KV, x0, x1)


def _sample_select_body(q_ref, kc_ref, vct_ref, ovl_ref, tri_ref, ocmp_ref, idx_ref, *, ncp, qpos, nbp):
    q8 = q_ref[...]
    c_i = lax.broadcasted_iota(I32, (N_HEADS, ncp), 1)
    cmask = c_i * CMP_STRIDE + (CMP_BLOCK - 1) <= qpos
    s = _group_rows(_dot_nt(q8, kc_ref[0]), _dot_nt(q8, kc_ref[1]))
    s = jnp.where(cmask, s, NEG_INF)
    e = jnp.where(cmask, jnp.exp2(s - jnp.max(s, axis=-1, keepdims=True)), 0.0)
    l = jnp.sum(e, axis=-1, keepdims=True)
    p = (e / jnp.where(l > 0.0, l, 1.0)).astype(BF16)
    ocmp_ref[...] = _group_rows(_dot_nt(p, vct_ref[0]), _dot_nt(p, vct_ref[1]))
    imp8 = _dot(p, ovl_ref[...])

    n_row = lax.broadcasted_iota(I32, (1, nbp), 1)
    qblk = qpos // SLC_BLOCK
    causal = n_row <= qblk
    forced = (n_row == 0) | (n_row >= qblk - (N_LOCAL_BLOCKS - 1))
    m_i = lax.broadcasted_iota(I32, (nbp, nbp), 0)
    n_i = lax.broadcasted_iota(I32, (nbp, nbp), 1)
    lane = lax.broadcasted_iota(I32, (1, LANES), 1)
    idx_rows = []
    for g in range(KV_HEADS):
        imp = jnp.sum(imp8[g * Q_PER_KV:(g + 1) * Q_PER_KV], axis=0, keepdims=True)
        w = jnp.where(causal, jnp.where(forced, jnp.inf, imp), -jnp.inf)
        w_sq = jnp.broadcast_to(w, (nbp, nbp))
        w_col = w_sq.T
        beats = jnp.where(n_i > m_i, jnp.where(w_col >= w_sq, 1.0, 0.0), jnp.where(w_col > w_sq, 1.0, 0.0))
        rank = jnp.sum(beats, axis=0, keepdims=True)
        sel = causal & (rank < TOP_N)
        self_f = jnp.where(sel, 1.0, 0.0)
        before = _dot(self_f.astype(BF16), tri_ref[...])
        idx = jnp.full((1, LANES), -1, I32)
        for k in range(TOP_N):
            hit = sel & (before == float(k))
            val = jnp.sum(jnp.where(hit, n_row.astype(F32) + 1.0, 0.0), axis=-1, keepdims=True) - 1.0
            idx = jnp.where(lane == k, val.astype(I32), idx)
        idx_rows.append(idx)
    idx_ref[...] = jnp.concatenate(idx_rows + [jnp.full((SUBLANES - KV_HEADS, LANES), -1, I32)], axis=0)


def _sample_select_call(q8, kc, vc_t, ovl, tri, qpos):
    n_seq = q8.shape[0]
    ncp = kc.shape[1] // n_seq
    nbp = ovl.shape[1]
    cmp_spec = pl.BlockSpec((KV_HEADS, ncp, LANES), lambda b: (0, b, 0))
    row_spec = pl.BlockSpec((None, N_HEADS, LANES), lambda b: (b, 0, 0))
    full = lambda a: pl.BlockSpec(a.shape, lambda b: (0,) * a.ndim)
    return pl.pallas_call(
        functools.partial(_sample_select_body, ncp=ncp, qpos=qpos, nbp=nbp),
        out_shape=(jax.ShapeDtypeStruct((n_seq, N_HEADS, LANES), F32),
                   jax.ShapeDtypeStruct((n_seq, SUBLANES, LANES), I32)),
        grid=(n_seq,),
        in_specs=[row_spec, cmp_spec, pl.BlockSpec((KV_HEADS, None, LANES, ncp), lambda b: (0, b, 0, 0)),
                  full(ovl), full(tri)],
        out_specs=(row_spec, pl.BlockSpec((None, SUBLANES, LANES), lambda b: (b, 0, 0))),
        compiler_params=_cparams(("parallel",)),
        name="nsa_sample_select",
    )(q8, kc, vc_t, ovl, tri)


def _sample_attend_body(idx_ref, pt_ref, q_ref, ocmp_ref, gate_ref, ksn_ref, vsn_ref, kwn_ref, vwn_ref,
                        win_ref, *refs, n_cache_blocks, blocks_per_page, win_skip):
    del pt_ref
    n_blk = KV_HEADS * TOP_N
    kv_refs, o_ref = refs[:n_blk], refs[n_blk]
    b = pl.program_id(0)
    q = q_ref[...]
    qf = q.astype(F32)
    row_g = (lax.broadcasted_iota(I32, (N_HEADS, 1), 0) >= Q_PER_KV).astype(I32)

    def attend(s_list, v_list, kn_ref, vn_ref):
        s_self = jnp.sum(qf * kn_ref[...].astype(F32), axis=-1, keepdims=True)
        m = s_self
        for s in s_list:
            m = jnp.maximum(m, jnp.max(s, axis=-1, keepdims=True))
        p_self = jnp.exp2(s_self - m)
        l = p_self
        acc = p_self.astype(BF16).astype(F32) * vn_ref[...].astype(F32)
        for s, v in zip(s_list, v_list):
            p = jnp.exp2(s - m)
            l = l + jnp.sum(p, axis=-1, keepdims=True)
            acc = acc + _dot_nt(p.astype(BF16), v().astype(BF16))
        return acc / l

    s_list, v_list = [], []
    for j in range(n_blk):
        s = _dot(q, kv_refs[j][0].astype(BF16))
        col = lax.broadcasted_iota(I32, s.shape, 1)
        n = idx_ref[b, j]
        first = (n & (blocks_per_page - 1)) * SLC_BLOCK
        ok = ((row_g == j // TOP_N) & (col >= first) & (col < first + SLC_BLOCK)
              & (n >= 0) & (n < n_cache_blocks))
        s_list.append(jnp.where(ok, s, NEG_INF))
        v_list.append(lambda j=j: kv_refs[j][1])
    o_slc = attend(s_list, v_list, ksn_ref, vsn_ref)

    s_list, v_list = [], []
    for g in range(KV_HEADS):
        s = _dot(q, win_ref[0, g].astype(BF16))
        col = lax.broadcasted_iota(I32, s.shape, 1)
        s_list.append(jnp.where((row_g == g) & (col >= win_skip), s, NEG_INF))
        v_list.append(lambda g=g: win_ref[1, g])
    o_win = attend(s_list, v_list, kwn_ref, vwn_ref)

    gates = gate_ref[...]
    o_ref[...] = gates[:, 0:1] * ocmp_ref[:, 0:HEAD_DIM] + gates[:, 1:2] * o_slc + gates[:, 2:3] * o_win


def _sample_attend_call(idx, page_table, q64, ocmp, gates8, ksn, vsn, kwn, vwn, cache_win_t, cache_t, win_skip):
    n_seq, n_pages = page_table.shape
    page_rows = cache_t.shape[-1]
    bpp = page_rows // SLC_BLOCK
    n_cache_blocks = n_pages * bpp
    row_spec = lambda a: pl.BlockSpec((None,) + a.shape[1:], lambda b, ix, pt: (b,) + (0,) * (a.ndim - 1))

    assert bpp & (bpp - 1) == 0
    bpp_shift = bpp.bit_length() - 1

    def blk_spec(j):
        def index_map(b, ix, pt):
            n = jnp.minimum(jnp.maximum(ix[b, j], 0), n_cache_blocks - 1)
            return (pt[b, lax.shift_right_logical(n, bpp_shift)], 1, j // TOP_N, 0, 0)
        return pl.BlockSpec((None, 2, None, HEAD_DIM, page_rows), index_map)

    n_blk = KV_HEADS * TOP_N
    small = (q64, ocmp, gates8, ksn, vsn, kwn, vwn, cache_win_t)
    return pl.pallas_call(
        functools.partial(_sample_attend_body, n_cache_blocks=n_cache_blocks, blocks_per_page=bpp,
                          win_skip=win_skip),
        out_shape=jax.ShapeDtypeStruct((n_seq, N_HEADS, HEAD_DIM), F32),
        grid_spec=pltpu.PrefetchScalarGridSpec(
            num_scalar_prefetch=2,
            grid=(n_seq,),
            in_specs=[row_spec(a) for a in small] + [blk_spec(j) for j in range(n_blk)],
            out_specs=pl.BlockSpec((None, N_HEADS, HEAD_DIM), lambda b, ix, pt: (b, 0, 0))),
        compiler_params=_cparams(("parallel",)),
        name="nsa_sample_attend",
    )(idx, page_table, *small, *([cache_t] * n_blk))


def _round_up(x, m):
    return -(-x // m) * m


def _prompt_layer(h, prm, cw, sw, batch, seq):
    tabs = _rope_tables(jnp.arange(seq))
    u, gs, gn, q_t, kv_t, win_t, gates_t, ks, vs_t, kw, vw_t = _proj_call(
        h, prm["w_in"], prm["norm_w"], prm["qnw"], prm["knw"], prm["gb"], tabs, batch, seq)
    y_ssm, h_re, h_im = _s5_prompt_call(u, sw, batch, seq)
    kc, vc_t = _compress_prompt_call(kv_t, cw, batch, seq)
    nch = seq // CMP_STRIDE
    ovl_t = _overlap_matrix(nch, nch - 1, seq // SLC_BLOCK).T
    o = _attn_call(q_t, kc, vc_t, ovl_t, ks, vs_t, kw, vw_t, gates_t, batch, seq)
    h_new = _outmix_call(h, y_ssm, gs, o, gn, prm["w_glu"], prm["w_out"])
    rows = lambda x_t, slots: x_t.reshape(batch, slots, KV_HEADS, HEAD_DIM, seq).transpose(0, 4, 1, 2, 3)
    return h_new, rows(kv_t, 4), rows(win_t, 2)[:, seq - min(WINDOW, seq):], h_re, h_im


def _sample_layer(h, prm, cw, sw, cache_kv, cache_win, st_re, st_im, page_table):
    n_seq = h.shape[0]
    n_phys, page_rows = cache_kv.shape[:2]
    n_pages = page_table.shape[1]
    past_len = n_pages * page_rows
    win_buf = cache_win.shape[1]
    n_pad = _round_up(n_seq, LANES)
    tabs = _rope_tables(jnp.full((n_pad,), past_len, I32))
    h_pad = jnp.pad(h, ((0, n_pad - n_seq), (0, 0)))
    u, gs, gn, q_t, kv_t, win_t, gates_t, ks, vs_t, kw, vw_t = _proj_call(
        h_pad, prm["w_in"], prm["norm_w"], prm["qnw"], prm["knw"], prm["gb"], tabs, 1, n_pad)
    u, gs, gn = u[:, :n_seq], gs[:n_seq], gn[:n_seq]
    n_state = st_re.shape[1] * st_re.shape[2]
    y_ssm, h_re, h_im = _s5_step_call(u, st_re.reshape(n_seq, n_state), st_im.reshape(n_seq, n_state), sw)
    cache_t = cache_kv.transpose(0, 2, 3, 4, 1)
    kc, vc_t = _compress_sample_call(cache_t, page_table, cw)
    ncp = past_len // CMP_STRIDE
    n_blk = -(-(past_len + 1) // SLC_BLOCK)
    nbp = _round_up(n_blk, LANES)
    ovl = _overlap_matrix(ncp, ncp - 1, n_blk, nbp)
    tri = (jnp.arange(nbp)[:, None] < jnp.arange(nbp)[None, :]).astype(BF16)
    q8 = q_t[:, :, :n_seq].transpose(2, 0, 1)
    ocmp, idx = _sample_select_call(q8, kc, vc_t, ovl, tri, past_len)
    idx = idx[:, :KV_HEADS, :TOP_N].reshape(n_seq, KV_HEADS * TOP_N)
    gates8 = gates_t[:, :3 * Q_PER_KV, :n_seq].reshape(KV_HEADS, Q_PER_KV, 3, n_seq).transpose(3, 0, 1, 2)
    gates8 = jnp.pad(gates8.reshape(n_seq, N_HEADS, 3), ((0, 0), (0, 0), (0, LANES - 3)))
    per_head = lambda a: jnp.repeat(a.transpose(1, 0, 2), Q_PER_KV, axis=1)
    new_k = lambda k: per_head(k[:, :n_seq, HEAD_DIM:])
    new_v = lambda v_t: per_head(v_t[:, 0, :HEAD_DIM, :n_seq].transpose(0, 2, 1))
    o8 = _sample_attend_call(idx, page_table, q8[:, :, HEAD_DIM:], ocmp, gates8, new_k(ks), new_v(vs_t),
                             new_k(kw), new_v(vw_t), cache_win.transpose(0, 2, 3, 4, 1), cache_t,
                             max(win_buf + 1 - WINDOW, 0))
    o = o8.reshape(n_seq, NSA_W)
    h_new = _outmix_call(h, y_ssm, gs, o, gn, prm["w_glu"], prm["w_out"])
    kv_rows = kv_t[0, :, :n_seq].T.reshape(n_seq, 1, 4, KV_HEADS, HEAD_DIM)
    win_new = win_t[0, :, :n_seq].T.reshape(n_seq, 1, 2, KV_HEADS, HEAD_DIM)
    wrows = jnp.concatenate([cache_win, win_new], axis=1)
    wrows = wrows[:, wrows.shape[1] - min(WINDOW, wrows.shape[1]):]
    state = lambda s: s.reshape(st_re.shape)
    return h_new, kv_rows, wrows, state(h_re), state(h_im)


def kernel(x_prompt, x_sample, cache_kv, cache_win, state_ssm_re, state_ssm_im, page_table, norm_w, w_in, gate_b,
           q_norm_w, k_norm_w, cmp_pe, cmp_w1, cmp_b1, cmp_w2, ssm_lam_re, ssm_lam_im, ssm_log_step, ssm_b_re,
           ssm_b_im, ssm_c_re, ssm_c_im, ssm_d, w_glu, w_out):
    p = dict(norm_w=norm_w, w_in=w_in, gate_b=gate_b, q_norm_w=q_norm_w, k_norm_w=k_norm_w, cmp_pe=cmp_pe,
             cmp_w1=cmp_w1, cmp_b1=cmp_b1, cmp_w2=cmp_w2, ssm_lam_re=ssm_lam_re, ssm_lam_im=ssm_lam_im,
             ssm_log_step=ssm_log_step, ssm_b_re=ssm_b_re, ssm_b_im=ssm_b_im, ssm_c_re=ssm_c_re,
             ssm_c_im=ssm_c_im, ssm_d=ssm_d, w_glu=w_glu, w_out=w_out)
    b_p, s_p, d_model = x_prompt.shape
    b_s, s_s, _ = x_sample.shape
    assert s_s == 1, "the sample group decodes one token per sequence"
    h_p = x_prompt.reshape(b_p * s_p, d_model)
    h_s = x_sample.reshape(b_s, d_model)
    outs_p, outs_s = [], []
    for l in range(norm_w.shape[0]):
        weights = (_prep_params(p, l), _prep_compress(p, l), _prep_s5_chunked(p, l))
        h_p, *rest_p = _prompt_layer(h_p, *weights, b_p, s_p)
        h_s, *rest_s = _sample_layer(h_s, *weights, cache_kv[l], cache_win[l], state_ssm_re[l],
                                     state_ssm_im[l], page_table)
        outs_p.append(rest_p)
        outs_s.append(rest_s)
    stack = lambda outs, i: jnp.stack([o[i] for o in outs])
    return (h_p.reshape(x_prompt.shape), h_s.reshape(x_sample.shape),
            stack(outs_p, 0), stack(outs_s, 0), stack(outs_p, 1), stack(outs_s, 1),
            stack(outs_p, 2), stack(outs_p, 3), stack(outs_s, 2), stack(outs_s, 3))
```

```python
import functools
import math

import jax
import jax.numpy as jnp
from jax import lax
from jax.experimental import pallas as pl
from jax.experimental.pallas import tpu as pltpu

F32 = jnp.float32
BF16 = jnp.bfloat16
I32 = jnp.int32

LANES = 128
SUBLANES = 8
VMEM_LIMIT_BYTES = 56 * 1024 * 1024

HEAD_DIM = 64
N_HEADS = 8
KV_HEADS = 2
Q_PER_KV = N_HEADS // KV_HEADS
SSM_W = 512
SSM_GROUP = 16
SSM_STATE = 64
NSA_W = N_HEADS * HEAD_DIM
CMP_BLOCK = 32
CMP_STRIDE = 16
CMP_HID = 2 * HEAD_DIM
SLC_BLOCK = 64
TOP_N = 16
N_LOCAL_BLOCKS = 2
WINDOW = 512
ROPE_THETA = 500000.0
ROPE_DIM = HEAD_DIM // 4
RMS_EPS = 1e-6
NEG_INF = -1e30

COL_U = 0
COL_GS = SSM_W
COL_Q = 2 * SSM_W
COL_GN = 2 * SSM_W + NSA_W
COL_KV = 2 * SSM_W + 2 * NSA_W
COL_GL = COL_KV + 6 * KV_HEADS * HEAD_DIM
IN_W_PAD = COL_GL + LANES

PROJ_TILE = 512
ATTN_TILE = 256
V_ROWS = HEAD_DIM + 16
SSM_CHUNK = 16
SSM_SLAB_GROUPS = LANES // SSM_GROUP
SSM_SLABS = SSM_W // LANES


def _cparams(sem):
    return pltpu.CompilerParams(dimension_semantics=sem, vmem_limit_bytes=VMEM_LIMIT_BYTES)


def _sigmoid(x):
    return 1.0 / (1.0 + jnp.exp(-x))


def _dot(a, b):
    return jnp.dot(a, b, preferred_element_type=F32)


def _dot_nt(a, b):
    return lax.dot_general(a, b, (((1,), (1,)), ((), ())), preferred_element_type=F32)


def _proj_body(x_ref, nw_ref, w_ref, qnw_ref, knw_ref, gb_ref, ra_ref, rb_ref, rc_ref,
               u_ref, gs_ref, gn_ref, qt_ref, kvt_ref, wint_ref, gt_ref,
               ks_ref, vst_ref, kw_ref, vwt_ref, *, tm, tv, tiles_per_seq):
    x = x_ref[...]
    ms = jnp.mean(x * x, axis=-1, keepdims=True)
    h = (x * lax.rsqrt(ms + RMS_EPS) * nw_ref[...]).astype(BF16)

    def mm(c0, c1):
        return _dot(h, w_ref[:, c0:c1])

    lane = lax.broadcasted_iota(I32, (tm, LANES), 1)
    lo = lane < HEAD_DIM
    ra = ra_ref[...]
    rb = rb_ref[...]
    rc = rc_ref[...]

    def norm_rope(s, wrow):
        s2 = s * s
        slo = jnp.sum(jnp.where(lo, s2, 0.0), axis=-1, keepdims=True)
        shi = jnp.sum(jnp.where(lo, 0.0, s2), axis=-1, keepdims=True)
        msq = jnp.where(lo, slo, shi) * (1.0 / HEAD_DIM)
        y = s * lax.rsqrt(msq + RMS_EPS) * wrow
        half = ROPE_DIM // 2
        return y * ra + pltpu.roll(y, LANES - half, 1) * rb + pltpu.roll(y, half, 1) * rc

    def hi_half(y, head):
        src = pltpu.roll(y, HEAD_DIM, 1) if head == 0 else y
        return jnp.where(lo, 0.0, src)

    zq = mm(COL_Q, COL_GN)
    zkv = mm(COL_KV, COL_GL)
    zgl = mm(COL_GL, IN_W_PAD)

    qnw = qnw_ref[...]
    scale = HEAD_DIM ** -0.5 * math.log2(math.e)
    zeros_t = jnp.zeros((HEAD_DIM, tm), F32)
    for j in range(N_HEADS // 2):
        y_t = (norm_rope(zq[:, j * LANES:(j + 1) * LANES], qnw) * scale).T
        for head in range(2):
            q_t = jnp.concatenate([zeros_t, y_t[head * HEAD_DIM:(head + 1) * HEAD_DIM]], axis=0)
            qt_ref[2 * j + head] = q_t.astype(BF16)

    kc = norm_rope(zkv[:, 0:LANES], knw_ref[0:1, :])
    vc = zkv[:, LANES:2 * LANES]
    ks = norm_rope(zkv[:, 2 * LANES:3 * LANES], knw_ref[1:2, :])
    vs = zkv[:, 3 * LANES:4 * LANES]
    kw = norm_rope(zkv[:, 4 * LANES:5 * LANES], knw_ref[2:3, :])
    vw = zkv[:, 5 * LANES:6 * LANES]
    vs_t, vw_t = vs.T, vw.T
    for i, rows_t in enumerate((kc.T, vc.T, ks.T, vs_t)):
        kvt_ref[i * LANES:(i + 1) * LANES, :] = rows_t
    for i, rows_t in enumerate((kw.T, vw_t)):
        wint_ref[i * LANES:(i + 1) * LANES, :] = rows_t

    row = lax.broadcasted_iota(I32, (tm, LANES), 0)
    pos = (pl.program_id(0) % tiles_per_seq) * tm + row
    onehot = jnp.where(lane == lax.shift_right_logical(pos, 6), 1.0, 0.0)
    ones_t = jnp.where(lax.broadcasted_iota(I32, (V_ROWS - HEAD_DIM, tm), 0) == 0, 1.0, 0.0)
    for g in range(KV_HEADS):
        ks_ref[g] = jnp.where(lo, onehot, hi_half(ks, g)).astype(BF16)
        kw_ref[g] = hi_half(kw, g).astype(BF16)
        for v_t, vt_ref in ((vs_t, vst_ref), (vw_t, vwt_ref)):
            v_aug = jnp.concatenate([v_t[g * HEAD_DIM:(g + 1) * HEAD_DIM], ones_t], axis=0).astype(BF16)
            for t in range(tm // tv):
                vt_ref[g, t] = v_aug[:, t * tv:(t + 1) * tv]

    gates_t = _sigmoid(zgl + gb_ref[...]).T
    for g in range(KV_HEADS):
        gt_ref[g] = gates_t[g * 3 * Q_PER_KV:g * 3 * Q_PER_KV + 2 * SUBLANES]

    zu = mm(COL_U, COL_GS)
    for j in range(SSM_SLABS):
        u_ref[j] = zu[:, j * LANES:(j + 1) * LANES]
    for g_ref, cols in ((gs_ref, (COL_GS, COL_Q)), (gn_ref, (COL_GN, COL_KV))):
        g = mm(*cols)
        g_ref[...] = (g * _sigmoid(g)).astype(BF16)


def _proj_call(x2d, w_pad, norm_w, qnw, knw, gb, tabs, batch, seq):
    T, D = x2d.shape
    tm = min(PROJ_TILE, seq)
    tv = min(ATTN_TILE, seq)
    assert T == batch * seq and seq % tm == 0 and tm % tv == 0 and tv % LANES == 0
    tps = seq // tm
    row_spec = lambda w: pl.BlockSpec((tm, w), lambda i: (i, 0))
    full = lambda a: pl.BlockSpec(a.shape, lambda i: (0,) * a.ndim)
    tab_spec = pl.BlockSpec((tm, LANES), lambda i: (i % tps, 0))
    head_spec = lambda n: pl.BlockSpec((n, tm, LANES), lambda i: (0, i, 0))
    head_t_spec = lambda n, rows: pl.BlockSpec((n, rows, tm), lambda i: (0, 0, i))
    cache_t_spec = lambda rows: pl.BlockSpec((None, rows, tm), lambda i: (i // tps, 0, i % tps))
    tile_t_spec = pl.BlockSpec((KV_HEADS, tm // tv, V_ROWS, tv), lambda i: (0, i, 0, 0))
    tile_t_sds = jax.ShapeDtypeStruct((KV_HEADS, T // tv, V_ROWS, tv), BF16)
    out_shape = (
        jax.ShapeDtypeStruct((SSM_SLABS, T, LANES), F32),
        jax.ShapeDtypeStruct((T, SSM_W), BF16),
        jax.ShapeDtypeStruct((T, NSA_W), BF16),
        jax.ShapeDtypeStruct((N_HEADS, LANES, T), BF16),
        jax.ShapeDtypeStruct((batch, 4 * LANES, seq), F32),
        jax.ShapeDtypeStruct((batch, 2 * LANES, seq), F32),
        jax.ShapeDtypeStruct((KV_HEADS, 2 * SUBLANES, T), F32),
        jax.ShapeDtypeStruct((KV_HEADS, T, LANES), BF16),
        tile_t_sds,
        jax.ShapeDtypeStruct((KV_HEADS, T, LANES), BF16),
        tile_t_sds,
    )
    out_specs = (head_spec(SSM_SLABS), row_spec(SSM_W), row_spec(NSA_W), head_t_spec(N_HEADS, LANES),
                 cache_t_spec(4 * LANES), cache_t_spec(2 * LANES), head_t_spec(KV_HEADS, 2 * SUBLANES),
                 head_spec(KV_HEADS), tile_t_spec, head_spec(KV_HEADS), tile_t_spec)
    return pl.pallas_call(
        functools.partial(_proj_body, tm=tm, tv=tv, tiles_per_seq=tps),
        out_shape=out_shape,
        grid=(T // tm,),
        in_specs=[row_spec(D), full(norm_w), full(w_pad), full(qnw), full(knw), full(gb),
                  tab_spec, tab_spec, tab_spec],
        out_specs=out_specs,
        compiler_params=_cparams(("parallel",)),
        name="proj",
    )(x2d, norm_w, w_pad, qnw, knw, gb, *tabs)


def _prep_params(p, l):
    w_in = p["w_in"][l]
    d_model, in_w = w_in.shape
    tile2 = lambda v: jnp.tile(v, (1, LANES // HEAD_DIM))
    return {
        "w_in": jnp.pad(w_in.astype(BF16), ((0, 0), (0, IN_W_PAD - in_w))),
        "norm_w": p["norm_w"][l].reshape(1, d_model).astype(F32),
        "qnw": tile2(p["q_norm_w"][l].reshape(1, HEAD_DIM)).astype(F32),
        "knw": tile2(p["k_norm_w"][l]).astype(F32),
        "gb": jnp.pad(p["gate_b"][l].reshape(1, -1).astype(F32), ((0, 0), (0, LANES - 3 * N_HEADS))),
        "w_glu": p["w_glu"][l].astype(BF16),
        "w_out": p["w_out"][l].astype(BF16),
    }


def _rope_tables(pos):
    half = ROPE_DIM // 2
    inv = ROPE_THETA ** (-jnp.arange(half, dtype=F32) / half)
    ang = pos.astype(F32)[:, None] * inv
    cos, sin = jnp.cos(ang), jnp.sin(ang)
    n = pos.shape[0]
    rest = HEAD_DIM - ROPE_DIM
    a = jnp.concatenate([cos, cos, jnp.ones((n, rest), F32)], axis=-1)
    b = jnp.concatenate([-sin, jnp.zeros((n, HEAD_DIM - half), F32)], axis=-1)
    c = jnp.concatenate([jnp.zeros((n, half), F32), sin, jnp.zeros((n, rest), F32)], axis=-1)
    return tuple(jnp.tile(t, (1, LANES // HEAD_DIM)) for t in (a, b, c))


def _outmix_body(x_ref, y_ref, gs_ref, o_ref, gn_ref, wg_ref, wo_ref, out_ref):
    y = jnp.concatenate([y_ref[j] for j in range(SSM_SLABS)], axis=-1)
    ab = _dot(y.astype(BF16), wg_ref[...])
    ssm = ab[:, :SSM_W] * _sigmoid(ab[:, SSM_W:]) * gs_ref[...].astype(F32)
    nsa = o_ref[...].astype(F32) * gn_ref[...].astype(F32)
    acc = _dot(ssm.astype(BF16), wo_ref[0:SSM_W, :])
    acc += _dot(nsa.astype(BF16), wo_ref[SSM_W:, :])
    out_ref[...] = x_ref[...] + acc


def _outmix_call(x2d, y_ssm, g_ssm, o_nsa, g_nsa, w_glu, w_out):
    T, D = x2d.shape
    tm = min(512, T)
    row_spec = lambda w: pl.BlockSpec((tm, w), lambda i: (i, 0))
    full = lambda a: pl.BlockSpec(a.shape, lambda i: (0,) * a.ndim)
    return pl.pallas_call(
        _outmix_body,
        out_shape=jax.ShapeDtypeStruct((T, D), F32),
        grid=(T // tm,),
        in_specs=[row_spec(D), pl.BlockSpec((SSM_SLABS, tm, LANES), lambda i: (0, i, 0)),
                  row_spec(SSM_W), row_spec(NSA_W), row_spec(NSA_W),
                  full(w_glu), full(w_out)],
        out_specs=row_spec(D),
        compiler_params=_cparams(("parallel",)),
        name="outmix",
    )(x2d, y_ssm, g_ssm, o_nsa, g_nsa, w_glu, w_out)


def _gelu_tanh(x):
    c = math.sqrt(2.0 / math.pi)
    return 0.5 * x * (1.0 + jnp.tanh(c * (x + 0.044715 * (x * x * x))))


def _compress_rows(rows_refs, pe_ref, wa_ref, wb_ref, b1_ref, w2_ref, kc_ref, vc_ref, nch):
    lane = lax.broadcasted_iota(I32, (nch, LANES), 1)
    for kvi, out_ref in ((0, kc_ref), (1, vc_ref)):
        rows_ref = rows_refs[kvi]
        pa = jnp.zeros((nch, 2 * CMP_HID), F32)
        pb = jnp.zeros((nch, 2 * CMP_HID), F32)
        for j0 in range(0, CMP_STRIDE, 2):
            xs = [rows_ref[pl.ds(j, nch, stride=CMP_STRIDE), :] for j in (j0, j0 + 1)]
            xa = jnp.concatenate([xs[i] + pe_ref[kvi, 0, j0 + i:j0 + i + 1, :] for i in range(2)], axis=-1)
            xb = jnp.concatenate([xs[i] + pe_ref[kvi, 1, j0 + i:j0 + i + 1, :] for i in range(2)], axis=-1)
            wsl = slice(j0 * LANES, (j0 + 2) * LANES)
            pa += _dot(xa.astype(BF16), wa_ref[kvi, wsl, :])
            pb += _dot(xb.astype(BF16), wb_ref[kvi, wsl, :])
        hid = _gelu_tanh(pa + pltpu.roll(pb, nch - 1, 0) + b1_ref[kvi]).astype(BF16)
        for g in range(KV_HEADS):
            o = _dot(hid, w2_ref[kvi, g])
            if kvi == 1:
                o = jnp.where(lane == HEAD_DIM, 1.0, o).T
            out_ref[g] = o.astype(BF16)


def _compress_prompt_body(kvt_ref, pe_ref, wa_ref, wb_ref, b1_ref, w2_ref, kc_ref, vc_ref,
                          krows_ref, vrows_ref, *, nch):
    for c in range(kvt_ref.shape[1] // LANES):
        cs = slice(c * LANES, (c + 1) * LANES)
        krows_ref[cs, :] = kvt_ref[0:LANES, cs].T
        vrows_ref[cs, :] = kvt_ref[LANES:2 * LANES, cs].T
    _compress_rows((krows_ref, vrows_ref), pe_ref, wa_ref, wb_ref, b1_ref, w2_ref, kc_ref, vc_ref, nch)


def _compress_prompt_call(kv_t, cw, batch, seq):
    nch = seq // CMP_STRIDE
    full = lambda a: pl.BlockSpec(a.shape, lambda b: (0,) * a.ndim)
    out_spec = pl.BlockSpec((KV_HEADS, nch, LANES), lambda b: (0, b, 0))
    out_sds = jax.ShapeDtypeStruct((KV_HEADS, batch * nch, LANES), BF16)
    out_t_spec = pl.BlockSpec((KV_HEADS, None, LANES, nch), lambda b: (0, b, 0, 0))
    out_t_sds = jax.ShapeDtypeStruct((KV_HEADS, batch, LANES, nch), BF16)
    return pl.pallas_call(
        functools.partial(_compress_prompt_body, nch=nch),
        out_shape=(out_sds, out_t_sds),
        grid=(batch,),
        in_specs=[pl.BlockSpec((None, 2 * LANES, seq), lambda b: (b, 0, 0)),
                  full(cw["pe"]), full(cw["wa"]), full(cw["wb"]), full(cw["b1"]), full(cw["w2"])],
        out_specs=(out_spec, out_t_spec),
        scratch_shapes=[pltpu.VMEM((seq, LANES), F32)] * 2,
        compiler_params=_cparams(("parallel",)),
        name="compress_prompt",
    )(kv_t, cw["pe"], cw["wa"], cw["wb"], cw["b1"], cw["w2"])


def _prep_compress(p, l):
    eye = jnp.eye(KV_HEADS, dtype=F32)
    w1 = p["cmp_w1"][l].reshape(2, 2, CMP_STRIDE, HEAD_DIM, CMP_HID)
    wexp = jnp.einsum("khjdn,ge->khjgden", w1, eye).reshape(2, 2, CMP_STRIDE * LANES, KV_HEADS * CMP_HID)
    pe = p["cmp_pe"][l].reshape(2, 2, CMP_STRIDE, HEAD_DIM)
    w2 = p["cmp_w2"][l]
    zeros = jnp.zeros_like(w2[0])
    w2k = jnp.concatenate([zeros, w2[0]], axis=-1)
    w2v = jnp.concatenate([w2[1], zeros], axis=-1)
    w2e = jnp.stack([jnp.einsum("hd,ge->gehd", w, eye).reshape(KV_HEADS, KV_HEADS * CMP_HID, LANES)
                     for w in (w2k, w2v)])
    return {
        "pe": jnp.tile(pe, (1, 1, 1, KV_HEADS)).astype(F32),
        "wa": wexp[:, 0].astype(BF16),
        "wb": wexp[:, 1].astype(BF16),
        "b1": jnp.tile(p["cmp_b1"][l].reshape(2, 1, CMP_HID), (1, 1, KV_HEADS)).astype(F32),
        "w2": w2e.astype(BF16),
    }


def _overlap_matrix(n_tok_pad, n_tok, n_blk, n_cols=LANES):
    c_start = jnp.arange(n_tok_pad) * CMP_STRIDE
    blk = jnp.arange(n_cols)
    ov = ((c_start[:, None] < (blk[None, :] + 1) * SLC_BLOCK)
          & (c_start[:, None] + CMP_BLOCK > blk[None, :] * SLC_BLOCK)
          & (jnp.arange(n_tok_pad)[:, None] < n_tok) & (blk[None, :] < n_blk))
    return ov.astype(BF16)


def _topk_select_t(w_ref, imp_t, q0, tq):
    nb = imp_t.shape[0]
    n_i = lax.broadcasted_iota(I32, (nb, tq), 0)
    qblk = lax.shift_right_logical(q0 + lax.broadcasted_iota(I32, (nb, tq), 1), 6)
    causal = n_i <= qblk
    forced = (n_i == 0) | (n_i >= qblk - (N_LOCAL_BLOCKS - 1))
    w_ref[...] = jnp.where(causal, jnp.where(forced, jnp.inf, imp_t), -jnp.inf)
    last_blk = lax.shift_right_logical(q0 + tq - 1, 6)
    n_grp = nb // SUBLANES
    rank = [jnp.zeros((SUBLANES, tq), F32) for _ in range(n_grp)]
    grp_i = lax.broadcasted_iota(I32, (SUBLANES, tq), 0)

    def count_group(mg, rank):
        rank = list(rank)
        for mi in range(SUBLANES):
            m = mg * SUBLANES + mi
            wm = w_ref[m:m + 1, :]
            for ng in range(n_grp):
                w = w_ref[ng * SUBLANES:(ng + 1) * SUBLANES, :]
                if ng > mg:
                    beats = jnp.where(wm >= w, 1.0, 0.0)
                elif ng < mg:
                    beats = jnp.where(wm > w, 1.0, 0.0)
                else:
                    beats = jnp.where(grp_i > mi, jnp.where(wm >= w, 1.0, 0.0), jnp.where(wm > w, 1.0, 0.0))
                rank[ng] = rank[ng] + beats
        return tuple(rank)

    rank = tuple(rank)
    for mg in range(n_grp):
        rank = lax.cond(mg * SUBLANES <= last_blk, functools.partial(count_group, mg), lambda r: r, rank)
    return causal & (jnp.concatenate(rank, axis=0) < TOP_N)


def _flash_tiles_t(tiles, q_ts, ms, accs):
    def scores(i):
        k, _, mask = tiles[i]
        s = [_dot(k, q_t) for q_t in q_ts]
        return s if mask is None else [jnp.where(mask, x, NEG_INF) for x in s]

    ahead = 2
    pending = {i: scores(i) for i in range(min(ahead, len(tiles)))}
    for i, (_, v_t, _) in enumerate(tiles):
        s = pending.pop(i)
        m_new = [jnp.maximum(m, jnp.max(x, axis=0, keepdims=True)) for m, x in zip(ms, s)]
        alpha = [jnp.exp2(m - mn) for m, mn in zip(ms, m_new)]
        p = [jnp.exp2(x - mn).astype(BF16) for x, mn in zip(s, m_new)]
        pv = [_dot(v_t, x) for x in p]
        if i + ahead < len(tiles):
            pending[i + ahead] = scores(i + ahead)
        accs = [a * acc + x for a, acc, x in zip(alpha, accs, pv)]
        ms = m_new
    return tuple(ms), tuple(accs)


def _attn_body(qt_ref, kc_ref, vct_ref, ovlt_ref, ks_ref, vst_ref, kw_ref, vwt_ref, gate_ref, o_ref,
               qa_ref, ocmp_ref, w_ref, *, tq, ncp):
    R = Q_PER_KV
    qt = pl.program_id(2)
    q0 = qt * tq
    nbs = LANES // 2

    def compressed(n_c):
        c_i = lax.broadcasted_iota(I32, (n_c, tq), 0)
        qpos_c = q0 + lax.broadcasted_iota(I32, (n_c, tq), 1)
        cmask = c_i * CMP_STRIDE + (CMP_BLOCK - 1) <= qpos_c
        kc = kc_ref[0:n_c, :]
        s = [jnp.where(cmask, _dot(kc, qt_ref[r]), NEG_INF) for r in range(R)]
        e = [jnp.where(cmask, jnp.exp2(x - jnp.max(x, axis=0, keepdims=True)), 0.0) for x in s]
        l = [jnp.sum(x, axis=0, keepdims=True) for x in e]
        p = [(x * (1.0 / jnp.where(y > 0.0, y, 1.0))).astype(BF16) for x, y in zip(e, l)]
        for r in range(R):
            ocmp_ref[r] = _dot(vct_ref[:, 0:n_c], p[r])
        return sum(_dot(ovlt_ref[:, 0:n_c], x) for x in p)

    half = ncp // 2
    if half % LANES == 0:
        imp = lax.cond((q0 + tq) // CMP_STRIDE <= half, lambda: compressed(half), lambda: compressed(ncp))
    else:
        imp = compressed(ncp)

    sel = _topk_select_t(w_ref, imp[:nbs], q0, tq)
    bias = jnp.concatenate([jnp.where(sel, 0.0, NEG_INF), jnp.zeros((LANES - nbs, tq), F32)], axis=0)
    for r in range(R):
        qa_ref[r] = (qt_ref[r].astype(F32) + bias).astype(BF16)

    key_i = lax.broadcasted_iota(I32, (tq, tq), 0)
    qry_i = lax.broadcasted_iota(I32, (tq, tq), 1)
    init = (tuple(jnp.full((1, tq), -jnp.inf, F32) for _ in range(R)),
            tuple(jnp.zeros((V_ROWS, tq), F32) for _ in range(R)))

    def tiles(k_ref, vt_ref, q_ref, js, masks, state):
        ts = [(k_ref[pl.ds(pl.multiple_of(j * tq, tq), tq), :], vt_ref[j], mask) for j, mask in zip(js, masks)]
        return _flash_tiles_t(ts, [q_ref[r] for r in range(R)], *state)

    def last_tiles(k_ref, vt_ref, q_ref, n, first_mask, state):
        js = [qt - (n - 1 - t) for t in range(n)]
        masks = [first_mask] + [None] * (n - 2) + [key_i <= qry_i] if n > 1 else [key_i <= qry_i]
        return tiles(k_ref, vt_ref, q_ref, js, masks, state)

    GROUP = 4

    def slc_group(i, st):
        return tiles(ks_ref, vst_ref, qa_ref, [GROUP * i + t for t in range(GROUP)], [None] * GROUP, st)

    state = lax.fori_loop(0, qt // GROUP, slc_group, init)
    tails = [functools.partial(last_tiles, ks_ref, vst_ref, qa_ref, n, None) for n in range(1, GROUP + 1)]
    _, acc_s = lax.switch(qt % GROUP, tails, state)

    nwin = WINDOW // tq
    wins = [functools.partial(last_tiles, kw_ref, vwt_ref, qt_ref, n, (key_i > qry_i) if n == nwin + 1 else None)
            for n in range(1, nwin + 2)]
    _, acc_w = lax.switch(jnp.minimum(qt, nwin), wins, init)

    outs = []
    for r in range(R):
        a_s = acc_s[r]
        a_w = acc_w[r]
        g = lambda k: gate_ref[3 * r + k:3 * r + k + 1, :]
        head = lambda a: a[:HEAD_DIM] * (1.0 / a[HEAD_DIM:HEAD_DIM + 1, :])
        outs.append(g(0) * ocmp_ref[r][:HEAD_DIM] + g(1) * head(a_s) + g(2) * head(a_w))
    o_ref[...] = jnp.concatenate(outs, axis=0).T.astype(o_ref.dtype)


def _attn_call(q_t, kc, vc_t, ovl_t, ks, vs_t, kw, vw_t, gates_t, batch, seq):
    tq = min(ATTN_TILE, seq)
    nq = seq // tq
    ncp = kc.shape[1] // batch
    R = Q_PER_KV
    assert seq // SLC_BLOCK <= LANES // 2 and WINDOW % tq == 0
    k_spec = pl.BlockSpec((None, seq, LANES), lambda b, g, t: (g, b, 0))
    vt_spec = pl.BlockSpec((None, nq, V_ROWS, tq), lambda b, g, t: (g, b, 0, 0))
    acc = pltpu.VMEM((R, LANES, tq), F32)
    return pl.pallas_call(
        functools.partial(_attn_body, tq=tq, ncp=ncp),
        out_shape=jax.ShapeDtypeStruct((batch * seq, NSA_W), BF16),
        grid=(batch, KV_HEADS, nq),
        in_specs=[pl.BlockSpec((R, LANES, tq), lambda b, g, t: (g, 0, b * nq + t)),
                  pl.BlockSpec((None, ncp, LANES), lambda b, g, t: (g, b, 0)),
                  pl.BlockSpec((None, None, LANES, ncp), lambda b, g, t: (g, b, 0, 0)),
                  pl.BlockSpec(ovl_t.shape, lambda b, g, t: (0, 0)),
                  k_spec, vt_spec, k_spec, vt_spec,
                  pl.BlockSpec((None, 2 * SUBLANES, tq), lambda b, g, t: (g, 0, b * nq + t))],
        out_specs=pl.BlockSpec((tq, R * HEAD_DIM), lambda b, g, t: (b * nq + t, g)),
        scratch_shapes=[pltpu.VMEM((R, LANES, tq), BF16), acc, pltpu.VMEM((LANES // 2, tq), F32)],
        compiler_params=_cparams(("parallel", "parallel", "arbitrary")),
        name="nsa_prompt",
    )(q_t, kc, vc_t, ovl_t, ks, vs_t, kw, vw_t, gates_t)


def _s5_discretise(p, l):
    lr = p["ssm_lam_re"][l].astype(F32)
    li = p["ssm_lam_im"][l].astype(F32)
    dt = jnp.exp(p["ssm_log_step"][l].astype(F32))[:, None]

    def apow(t):
        mag, ang = jnp.exp(lr * dt * t), li * dt * t
        return mag * jnp.cos(ang), mag * jnp.sin(ang)

    a_re, a_im = apow(1.0)
    den = lr * lr + li * li
    nr, ni = a_re - 1.0, a_im
    f_re, f_im = (nr * lr + ni * li) / den, (ni * lr - nr * li) / den
    br, bi = p["ssm_b_re"][l].astype(F32), p["ssm_b_im"][l].astype(F32)
    bb_re = f_re[..., None] * br - f_im[..., None] * bi
    bb_im = f_re[..., None] * bi + f_im[..., None] * br
    return apow, bb_re, bb_im


def _s5_expand_body(pst_ref, kt_ref, cp_ref, wcol_ref, wst_ref, wout_ref):
    T, E, N, C = SSM_CHUNK, SSM_SLAB_GROUPS, SSM_STATE, SSM_GROUP
    row = lax.broadcasted_iota(I32, (LANES, LANES), 0)
    lane = lax.broadcasted_iota(I32, (LANES, LANES), 1)
    row_grp = row // C
    low = lane < N
    own_grp = row_grp == lane // C
    lane_n = lax.broadcasted_iota(I32, (N, LANES), 1)
    wcol_ref[T * LANES:(T + 1) * LANES, :] = jnp.zeros((LANES, LANES), BF16)
    wout_ref[0, :, LANES:2 * LANES] = jnp.zeros((2 * E * N, LANES), BF16)
    for i in range(T):
        wcol_ref[i * LANES:(i + 1) * LANES, :] = jnp.where(own_grp, kt_ref[i], 0.0).astype(BF16)
        x = pst_ref[i]
        x_sw = pltpu.roll(x, N, 1)
        halves = (jnp.where(low, x, x_sw), jnp.where(low, x_sw, x))
        for ri, x2 in enumerate(halves):
            for q in range(E // 2):
                own = row_grp == 2 * q + jnp.where(low, 0, 1)
                c0 = ri * E * N + q * LANES
                wst_ref[i * LANES:(i + 1) * LANES, c0:c0 + LANES] = jnp.where(own, x2, 0.0).astype(BF16)
    for i in range(T + 1):
        pair, half = (0, 0) if i == 0 else ((i + 1) // 2, (i - 1) % 2)
        m_t = cp_ref[i].T
        for ri in range(2):
            rows = m_t[ri * N:(ri + 1) * N]
            for g in range(E):
                r0 = ri * E * N + g * N
                wout_ref[pair, r0:r0 + N, half * LANES:(half + 1) * LANES] = (
                    jnp.where(lane_n // C == g, rows, 0.0).astype(BF16))


def _prep_s5_chunked(p, l):
    hi = lax.Precision.HIGHEST
    apow, bb_re, bb_im = _s5_discretise(p, l)
    c_re, c_im = p["ssm_c_re"][l].astype(F32), p["ssm_c_im"][l].astype(F32)
    T, J, E, N, C = SSM_CHUNK, SSM_SLABS, SSM_SLAB_GROUPS, SSM_STATE, SSM_GROUP
    pw_re, pw_im = apow(jnp.arange(T + 1, dtype=F32)[:, None, None])
    bt_re, bt_im = bb_re.transpose(0, 2, 1), bb_im.transpose(0, 2, 1)
    p_re = pw_re[:, :, None, :] * bt_re - pw_im[:, :, None, :] * bt_im
    p_im = pw_re[:, :, None, :] * bt_im + pw_im[:, :, None, :] * bt_re
    kt = (jnp.einsum("tgkn,gcn->tgkc", p_re[:T], jnp.tile(c_re, (1, E, 1)), precision=hi)
          - jnp.einsum("tgkn,gcn->tgkc", p_im[:T], jnp.tile(c_im, (1, E, 1)), precision=hi))
    cp_re = c_re * pw_re[:, :, None, :] - c_im * pw_im[:, :, None, :]
    cp_im = c_re * pw_im[:, :, None, :] + c_im * pw_re[:, :, None, :]
    slab = lambda x: jnp.moveaxis(x.reshape(x.shape[0], J, E * C, LANES), 1, 0)
    pst = slab(jnp.concatenate([p_re[:T][::-1], p_im[:T][::-1]], axis=-1))
    ktile = slab(kt[::-1])
    cpn = slab(jnp.concatenate([cp_re, -cp_im], axis=-1))
    blk = lambda n: pl.BlockSpec((None, n, LANES, LANES), lambda j: (j, 0, 0, 0))
    w_col, w_st, w_out = pl.pallas_call(
        _s5_expand_body,
        out_shape=(jax.ShapeDtypeStruct((J, (T + 1) * LANES, LANES), BF16),
                   jax.ShapeDtypeStruct((J, T * LANES, 2 * E * N), BF16),
                   jax.ShapeDtypeStruct((J, T // 2 + 1, 2 * E * N, 2 * LANES), BF16)),
        grid=(J,),
        in_specs=[blk(T), blk(T), blk(T + 1)],
        out_specs=(pl.BlockSpec((None, (T + 1) * LANES, LANES), lambda j: (j, 0, 0)),
                   pl.BlockSpec((None, T * LANES, 2 * E * N), lambda j: (j, 0, 0)),
                   pl.BlockSpec((None, T // 2 + 1, 2 * E * N, 2 * LANES), lambda j: (j, 0, 0, 0))),
        compiler_params=_cparams(("parallel",)),
        name="s5_expand",
    )(pst, ktile, cpn)
    return {
        "w_col": w_col, "w_st": w_st, "w_out": w_out,
        "a_re": pw_re[T].reshape(J, 1, E * N), "a_im": pw_im[T].reshape(J, 1, E * N),
        "a1_re": pw_re[1].reshape(J, 1, E * N), "a1_im": pw_im[1].reshape(J, 1, E * N),
        "d": p["ssm_d"][l].reshape(J, 1, LANES).astype(F32),
    }


def _s5_prompt_body(u_ref, wcol_ref, wst_ref, are_ref, aim_ref, wout_ref, d_ref,
                    y_ref, hre_ref, him_ref, xs_ref, hp_ref, *, n_chunks):
    T = SSM_CHUNK
    ns = SSM_SLAB_GROUPS * SSM_STATE
    u_pos = [u_ref[pl.ds(s, n_chunks, stride=T), :] for s in range(T)]
    ub = jnp.concatenate(u_pos, axis=-1).astype(BF16)
    xs_ref[...] = _dot(ub, wst_ref[...])
    a_re = are_ref[...]
    a_im = aim_ref[...]

    def step(c, carry):
        hr, hi = carry
        hp_ref[pl.ds(c, 1), 0:ns] = hr
        hp_ref[pl.ds(c, 1), ns:2 * ns] = hi
        xr = xs_ref[pl.ds(c, 1), 0:ns]
        xi = xs_ref[pl.ds(c, 1), ns:2 * ns]
        return a_re * hr - a_im * hi + xr, a_re * hi + a_im * hr + xi

    zero = jnp.zeros((1, ns), F32)
    hr, hi = lax.fori_loop(0, n_chunks, step, (zero, zero))
    hre_ref[...] = jnp.broadcast_to(hr, hre_ref.shape)
    him_ref[...] = jnp.broadcast_to(hi, him_ref.shape)
    hpb = hp_ref[...].astype(BF16)
    for t in range(0, T, 2):
        k_rows = (t + 2) * LANES
        w_pair = jnp.concatenate([wcol_ref[(T - 1 - t) * LANES:(T - 1 - t) * LANES + k_rows, :],
                                  wcol_ref[(T - 2 - t) * LANES:T * LANES, :]], axis=1)
        y_pair = _dot(ub[:, :k_rows], w_pair) + _dot(hpb, wout_ref[t // 2 + 1])
        for i in range(2):
            y_ref[pl.ds(t + i, n_chunks, stride=T), :] = (
                y_pair[:, i * LANES:(i + 1) * LANES] + d_ref[...] * u_pos[t + i])


def _s5_prompt_call(u_slab, sw, batch, seq):
    T, J = SSM_CHUNK, SSM_SLABS
    n_chunks = seq // T
    ns = SSM_SLAB_GROUPS * SSM_STATE
    row_spec = pl.BlockSpec((None, seq, LANES), lambda j, b: (j, b, 0))
    slab_spec = lambda a: pl.BlockSpec((None,) + a.shape[1:], lambda j, b: (j,) + (0,) * (a.ndim - 1))
    st_spec = pl.BlockSpec((None, None, SUBLANES, ns), lambda j, b: (b, j, 0, 0))
    st_sds = jax.ShapeDtypeStruct((batch, J, SUBLANES, ns), F32)
    y, hre, him = pl.pallas_call(
        functools.partial(_s5_prompt_body, n_chunks=n_chunks),
        out_shape=(jax.ShapeDtypeStruct(u_slab.shape, F32), st_sds, st_sds),
        grid=(J, batch),
        in_specs=[row_spec, slab_spec(sw["w_col"]), slab_spec(sw["w_st"]), slab_spec(sw["a_re"]),
                  slab_spec(sw["a_im"]), slab_spec(sw["w_out"]), slab_spec(sw["d"])],
        out_specs=(row_spec, st_spec, st_spec),
        scratch_shapes=[pltpu.VMEM((n_chunks, 2 * ns), F32), pltpu.VMEM((n_chunks, 2 * ns), F32)],
        compiler_params=_cparams(("parallel", "parallel")),
        name="s5_prompt",
    )(u_slab, sw["w_col"], sw["w_st"], sw["a_re"], sw["a_im"], sw["w_out"], sw["d"])
    n_groups = J * SSM_SLAB_GROUPS
    state = lambda h: h[:, :, 0, :].reshape(batch, n_groups, SSM_STATE)
    return y, state(hre), state(him)


def _s5_step_body(u_ref, wx_ref, are_ref, aim_ref, h0re_ref, h0im_ref, wy_ref, d_ref,
                  y_ref, hre_ref, him_ref):
    ns = SSM_SLAB_GROUPS * SSM_STATE
    for j in range(SSM_SLABS):
        sl = slice(j * ns, (j + 1) * ns)
        u = u_ref[j]
        x = _dot(u.astype(BF16), wx_ref[j])
        a_re, a_im = are_ref[j], aim_ref[j]
        h0r, h0i = h0re_ref[:, sl], h0im_ref[:, sl]
        hr = a_re * h0r - a_im * h0i + x[:, :ns]
        hi = a_re * h0i + a_im * h0r + x[:, ns:]
        hre_ref[:, sl] = hr
        him_ref[:, sl] = hi
        y_ref[j] = _dot(jnp.concatenate([hr, hi], axis=-1).astype(BF16), wy_ref[j]) + d_ref[j] * u


def _s5_step_call(u_slab, h0_re, h0_im, sw):
    J, n_tok, _ = u_slab.shape
    T = SSM_CHUNK
    full = lambda a: pl.BlockSpec(a.shape, lambda i: (0,) * a.ndim)
    wx_spec = pl.BlockSpec((J, LANES, sw["w_st"].shape[2]), lambda i: (0, T - 1, 0))
    wy_spec = pl.BlockSpec((J, None, sw["w_out"].shape[2], LANES), lambda i: (0, 0, 0, 0))
    st_sds = jax.ShapeDtypeStruct(h0_re.shape, F32)
    return pl.pallas_call(
        _s5_step_body,
        out_shape=(jax.ShapeDtypeStruct(u_slab.shape, F32), st_sds, st_sds),
        grid=(1,),
        in_specs=[full(u_slab), wx_spec, full(sw["a1_re"]), full(sw["a1_im"]), full(h0_re), full(h0_im),
                  wy_spec, full(sw["d"])],
        out_specs=(full(u_slab), full(h0_re), full(h0_re)),
        compiler_params=_cparams(("arbitrary",)),
        name="s5_step",
    )(u_slab, sw["w_st"], sw["a1_re"], sw["a1_im"], h0_re, h0_im, sw["w_out"], sw["d"])


def _compress_sample_body(pt_ref, *refs, n_pages, page_rows, nch):
    del pt_ref
    page_refs = refs[:n_pages]
    pe_ref, wa_ref, wb_ref, b1_ref, w2_ref, kc_ref, vc_ref = refs[n_pages:n_pages + 7]
    buffers = refs[n_pages + 7:]
    b = pl.program_id(0)

    @pl.when(b == 0)
    def _():
        for buf in buffers[2:]:
            buf[...] = jnp.zeros(buf.shape, F32)

    def step(stage, done):
        for i, page_ref in enumerate(page_refs):
            for rows_ref, slot in zip(stage, range(2)):
                rows_ref[i * page_rows:(i + 1) * page_rows, :] = page_ref[slot].reshape(LANES, page_rows).T
        _compress_rows(done, pe_ref, wa_ref, wb_ref, b1_ref, w2_ref, kc_ref, vc_ref, nch)

    @pl.when(b % 2 == 0)
    def _():
        step(buffers[:2], buffers[2:])

    @pl.when(b % 2 == 1)
    def _():
        step(buffers[2:], buffers[:2])


def _compress_sample_call(cache, page_table, cw):
    n_seq, n_pages = page_table.shape
    page_rows = cache.shape[-1]
    assert page_rows == LANES
    nch = n_pages * page_rows // CMP_STRIDE
    staged = lambda b: jnp.minimum(b, n_seq - 1)
    page_spec = lambda i: pl.BlockSpec((None, 2, KV_HEADS, HEAD_DIM, page_rows),
                                       lambda b, pt: (pt[staged(b), i], 0, 0, 0, 0))
    full = lambda a: pl.BlockSpec(a.shape, lambda b, pt: (0,) * a.ndim)
    out_of = lambda b: jnp.maximum(b - 1, 0)
    out_spec = pl.BlockSpec((KV_HEADS, nch, LANES), lambda b, pt: (0, out_of(b), 0))
    out_sds = jax.ShapeDtypeStruct((KV_HEADS, n_seq * nch, LANES), BF16)
    out_t_spec = pl.BlockSpec((KV_HEADS, None, LANES, nch), lambda b, pt: (0, out_of(b), 0, 0))
    out_t_sds = jax.ShapeDtypeStruct((KV_HEADS, n_seq, LANES, nch), BF16)
    weights = (cw["pe"], cw["wa"], cw["wb"], cw["b1"], cw["w2"])
    return pl.pallas_call(
        functools.partial(_compress_sample_body, n_pages=n_pages, page_rows=page_rows, nch=nch),
        out_shape=(out_sds, out_t_sds),
        grid_spec=pltpu.PrefetchScalarGridSpec(
            num_scalar_prefetch=1,
            grid=(n_seq + 1,),
            in_specs=[page_spec(i) for i in range(n_pages)] + [full(a) for a in weights],
            out_specs=(out_spec, out_t_spec),
            scratch_shapes=[pltpu.VMEM((n_pages * page_rows, LANES), F32)] * 4),
        compiler_params=_cparams(("arbitrary",)),
        name="compress_sample",
    )(page_table, *([cache] * n_pages), *weights)


def _group_rows(x0, x1):
    row = lax.broadcasted_iota(I32, x0.shape, 0)
    return jnp.where(row < Q_PER_KV, x0, x1)


def _sample_select_body(q_ref, kc_ref, vct_ref, ovl_ref, tri_ref, ocmp_ref, idx_ref, *, ncp, qpos, nbp):
    q8 = q_ref[...]
    c_i = lax.broadcasted_iota(I32, (N_HEADS, ncp), 1)
    cmask = c_i * CMP_STRIDE + (CMP_BLOCK - 1) <= qpos
    s = _group_rows(_dot_nt(q8, kc_ref[0]), _dot_nt(q8, kc_ref[1]))
    s = jnp.where(cmask, s, NEG_INF)
    e = jnp.where(cmask, jnp.exp2(s - jnp.max(s, axis=-1, keepdims=True)), 0.0)
    l = jnp.sum(e, axis=-1, keepdims=True)
    p = (e / jnp.where(l > 0.0, l, 1.0)).astype(BF16)
    ocmp_ref[...] = _group_rows(_dot_nt(p, vct_ref[0]), _dot_nt(p, vct_ref[1]))
    imp8 = _dot(p, ovl_ref[...])

    n_row = lax.broadcasted_iota(I32, (1, nbp), 1)
    qblk = qpos // SLC_BLOCK
    causal = n_row <= qblk
    forced = (n_row == 0) | (n_row >= qblk - (N_LOCAL_BLOCKS - 1))
    m_i = lax.broadcasted_iota(I32, (nbp, nbp), 0)
    n_i = lax.broadcasted_iota(I32, (nbp, nbp), 1)
    lane = lax.broadcasted_iota(I32, (1, LANES), 1)
    idx_rows = []
    for g in range(KV_HEADS):
        imp = jnp.sum(imp8[g * Q_PER_KV:(g + 1) * Q_PER_KV], axis=0, keepdims=True)
        w = jnp.where(causal, jnp.where(forced, jnp.inf, imp), -jnp.inf)
        w_sq = jnp.broadcast_to(w, (nbp, nbp))
        w_col = w_sq.T
        beats = jnp.where(n_i > m_i, jnp.where(w_col >= w_sq, 1.0, 0.0), jnp.where(w_col > w_sq, 1.0, 0.0))
        rank = jnp.sum(beats, axis=0, keepdims=True)
        sel = causal & (rank < TOP_N)
        self_f = jnp.where(sel, 1.0, 0.0)
        before = _dot(self_f.astype(BF16), tri_ref[...])
        idx = jnp.full((1, LANES), -1, I32)
        for k in range(TOP_N):
            hit = sel & (before == float(k))
            val = jnp.sum(jnp.where(hit, n_row.astype(F32) + 1.0, 0.0), axis=-1, keepdims=True) - 1.0
            idx = jnp.where(lane == k, val.astype(I32), idx)
        idx_rows.append(idx)
    idx_ref[...] = jnp.concatenate(idx_rows + [jnp.full((SUBLANES - KV_HEADS, LANES), -1, I32)], axis=0)


def _sample_select_call(q8, kc, vc_t, ovl, tri, qpos):
    n_seq = q8.shape[0]
    ncp = kc.shape[1] // n_seq
    nbp = ovl.shape[1]
    cmp_spec = pl.BlockSpec((KV_HEADS, ncp, LANES), lambda b: (0, b, 0))
    row_spec = pl.BlockSpec((None, N_HEADS, LANES), lambda b: (b, 0, 0))
    full = lambda a: pl.BlockSpec(a.shape, lambda b: (0,) * a.ndim)
    return pl.pallas_call(
        functools.partial(_sample_select_body, ncp=ncp, qpos=qpos, nbp=nbp),
        out_shape=(jax.ShapeDtypeStruct((n_seq, N_HEADS, LANES), F32),
                   jax.ShapeDtypeStruct((n_seq, SUBLANES, LANES), I32)),
        grid=(n_seq,),
        in_specs=[row_spec, cmp_spec, pl.BlockSpec((KV_HEADS, None, LANES, ncp), lambda b: (0, b, 0, 0)),
                  full(ovl), full(tri)],
        out_specs=(row_spec, pl.BlockSpec((None, SUBLANES, LANES), lambda b: (b, 0, 0))),
        compiler_params=_cparams(("parallel",)),
        name="nsa_sample_select",
    )(q8, kc, vc_t, ovl, tri)


def _sample_attend_body(idx_ref, pt_ref, q_ref, ocmp_ref, gate_ref, ksn_ref, vsn_ref, kwn_ref, vwn_ref,
                        win_ref, *refs, n_cache_blocks, blocks_per_page, win_skip):
    del pt_ref
    n_blk = KV_HEADS * TOP_N
    kv_refs, o_ref = refs[:n_blk], refs[n_blk]
    b = pl.program_id(0)
    q = q_ref[...]
    qf = q.astype(F32)
    row_g = (lax.broadcasted_iota(I32, (N_HEADS, 1), 0) >= Q_PER_KV).astype(I32)

    def attend(s_list, v_list, kn_ref, vn_ref):
        s_self = jnp.sum(qf * kn_ref[...].astype(F32), axis=-1, keepdims=True)
        m = s_self
        for s in s_list:
            m = jnp.maximum(m, jnp.max(s, axis=-1, keepdims=True))
        p_self = jnp.exp2(s_self - m)
        l = p_self
        acc = p_self.astype(BF16).astype(F32) * vn_ref[...].astype(F32)
        for s, v in zip(s_list, v_list):
            p = jnp.exp2(s - m)
            l = l + jnp.sum(p, axis=-1, keepdims=True)
            acc = acc + _dot_nt(p.astype(BF16), v().astype(BF16))
        return acc / l

    s_list, v_list = [], []
    for j in range(n_blk):
        s = _dot(q, kv_refs[j][0].astype(BF16))
        col = lax.broadcasted_iota(I32, s.shape, 1)
        n = idx_ref[b, j]
        first = (n & (blocks_per_page - 1)) * SLC_BLOCK
        ok = ((row_g == j // TOP_N) & (col >= first) & (col < first + SLC_BLOCK)
              & (n >= 0) & (n < n_cache_blocks))
        s_list.append(jnp.where(ok, s, NEG_INF))
        v_list.append(lambda j=j: kv_refs[j][1])
    o_slc = attend(s_list, v_list, ksn_ref, vsn_ref)

    s_list, v_list = [], []
    for g in range(KV_HEADS):
        s = _dot(q, win_ref[0, g].astype(BF16))
        col = lax.broadcasted_iota(I32, s.shape, 1)
        s_list.append(jnp.where((row_g == g) & (col >= win_skip), s, NEG_INF))
        v_list.append(lambda g=g: win_ref[1, g])
    o_win = attend(s_list, v_list, kwn_ref, vwn_ref)

    gates = gate_ref[...]
    o_ref[...] = gates[:, 0:1] * ocmp_ref[:, 0:HEAD_DIM] + gates[:, 1:2] * o_slc + gates[:, 2:3] * o_win


def _sample_attend_call(idx, page_table, q64, ocmp, gates8, ksn, vsn, kwn, vwn, cache_win_t, cache_t, win_skip):
    n_seq, n_pages = page_table.shape
    page_rows = cache_t.shape[-1]
    bpp = page_rows // SLC_BLOCK
    n_cache_blocks = n_pages * bpp
    row_spec = lambda a: pl.BlockSpec((None,) + a.shape[1:], lambda b, ix, pt: (b,) + (0,) * (a.ndim - 1))

    assert bpp & (bpp - 1) == 0
    bpp_shift = bpp.bit_length() - 1

    def blk_spec(j):
        def index_map(b, ix, pt):
            n = jnp.minimum(jnp.maximum(ix[b, j], 0), n_cache_blocks - 1)
            return (pt[b, lax.shift_right_logical(n, bpp_shift)], 1, j // TOP_N, 0, 0)
        return pl.BlockSpec((None, 2, None, HEAD_DIM, page_rows), index_map)

    n_blk = KV_HEADS * TOP_N
    small = (q64, ocmp, gates8, ksn, vsn, kwn, vwn, cache_win_t)
    return pl.pallas_call(
        functools.partial(_sample_attend_body, n_cache_blocks=n_cache_blocks, blocks_per_page=bpp,
                          win_skip=win_skip),
        out_shape=jax.ShapeDtypeStruct((n_seq, N_HEADS, HEAD_DIM), F32),
        grid_spec=pltpu.PrefetchScalarGridSpec(
            num_scalar_prefetch=2,
            grid=(n_seq,),
            in_specs=[row_spec(a) for a in small] + [blk_spec(j) for j in range(n_blk)],
            out_specs=pl.BlockSpec((None, N_HEADS, HEAD_DIM), lambda b, ix, pt: (b, 0, 0))),
        compiler_params=_cparams(("parallel",)),
        name="nsa_sample_attend",
    )(idx, page_table, *small, *([cache_t] * n_blk))


def _round_up(x, m):
    return -(-x // m) * m


def _prompt_layer(h, prm, cw, sw, batch, seq):
    tabs = _rope_tables(jnp.arange(seq))
    u, gs, gn, q_t, kv_t, win_t, gates_t, ks, vs_t, kw, vw_t = _proj_call(
        h, prm["w_in"], prm["norm_w"], prm["qnw"], prm["knw"], prm["gb"], tabs, batch, seq)
    y_ssm, h_re, h_im = _s5_prompt_call(u, sw, batch, seq)
    kc, vc_t = _compress_prompt_call(kv_t, cw, batch, seq)
    nch = seq // CMP_STRIDE
    ovl_t = _overlap_matrix(nch, nch - 1, seq // SLC_BLOCK).T
    o = _attn_call(q_t, kc, vc_t, ovl_t, ks, vs_t, kw, vw_t, gates_t, batch, seq)
    h_new = _outmix_call(h, y_ssm, gs, o, gn, prm["w_glu"], prm["w_out"])
    rows = lambda x_t, slots: x_t.reshape(batch, slots, KV_HEADS, HEAD_DIM, seq).transpose(0, 4, 1, 2, 3)
    return h_new, rows(kv_t, 4), rows(win_t, 2)[:, seq - min(WINDOW, seq):], h_re, h_im


def _sample_layer(h, prm, cw, sw, cache_kv, cache_win, st_re, st_im, page_table):
    n_seq = h.shape[0]
    n_phys, page_rows = cache_kv.shape[:2]
    n_pages = page_table.shape[1]
    past_len = n_pages * page_rows
    win_buf = cache_win.shape[1]
    n_pad = _round_up(n_seq, LANES)
    tabs = _rope_tables(jnp.full((n_pad,), past_len, I32))
    h_pad = jnp.pad(h, ((0, n_pad - n_seq), (0, 0)))
    u, gs, gn, q_t, kv_t, win_t, gates_t, ks, vs_t, kw, vw_t = _proj_call(
        h_pad, prm["w_in"], prm["norm_w"], prm["qnw"], prm["knw"], prm["gb"], tabs, 1, n_pad)
    u, gs, gn = u[:, :n_seq], gs[:n_seq], gn[:n_seq]
    n_state = st_re.shape[1] * st_re.shape[2]
    y_ssm, h_re, h_im = _s5_step_call(u, st_re.reshape(n_seq, n_state), st_im.reshape(n_seq, n_state), sw)
    cache_t = cache_kv.transpose(0, 2, 3, 4, 1)
    kc, vc_t = _compress_sample_call(cache_t, page_table, cw)
    ncp = past_len // CMP_STRIDE
    n_blk = -(-(past_len + 1) // SLC_BLOCK)
    nbp = _round_up(n_blk, LANES)
    ovl = _overlap_matrix(ncp, ncp - 1, n_blk, nbp)
    tri = (jnp.arange(nbp)[:, None] < jnp.arange(nbp)[None, :]).astype(BF16)
    q8 = q_t[:, :, :n_seq].transpose(2, 0, 1)
    ocmp, idx = _sample_select_call(q8, kc, vc_t, ovl, tri, past_len)
    idx = idx[:, :KV_HEADS, :TOP_N].reshape(n_seq, KV_HEADS * TOP_N)
    gates8 = gates_t[:, :3 * Q_PER_KV, :n_seq].reshape(KV_HEADS, Q_PER_KV, 3, n_seq).transpose(3, 0, 1, 2)
    gates8 = jnp.pad(gates8.reshape(n_seq, N_HEADS, 3), ((0, 0), (0, 0), (0, LANES - 3)))
    per_head = lambda a: jnp.repeat(a.transpose(1, 0, 2), Q_PER_KV, axis=1)
    new_k = lambda k: per_head(k[:, :n_seq, HEAD_DIM:])
    new_v = lambda v_t: per_head(v_t[:, 0, :HEAD_DIM, :n_seq].transpose(0, 2, 1))
    o8 = _sample_attend_call(idx, page_table, q8[:, :, HEAD_DIM:], ocmp, gates8, new_k(ks), new_v(vs_t),
                             new_k(kw), new_v(vw_t), cache_win.transpose(0, 2, 3, 4, 1), cache_t,
                             max(win_buf + 1 - WINDOW, 0))
    o = o8.reshape(n_seq, NSA_W)
    h_new = _outmix_call(h, y_ssm, gs, o, gn, prm["w_glu"], prm["w_out"])
    kv_rows = kv_t[0, :, :n_seq].T.reshape(n_seq, 1, 4, KV_HEADS, HEAD_DIM)
    win_new = win_t[0, :, :n_seq].T.reshape(n_seq, 1, 2, KV_HEADS, HEAD_DIM)
    wrows = jnp.concatenate([cache_win, win_new], axis=1)
    wrows = wrows[:, wrows.shape[1] - min(WINDOW, wrows.shape[1]):]
    state = lambda s: s.reshape(st_re.shape)
    return h_new, kv_rows, wrows, state(h_re), state(h_im)


def kernel(x_prompt, x_sample, cache_kv, cache_win, state_ssm_re, state_ssm_im, page_table, norm_w, w_in, gate_b,
           q_norm_w, k_norm_w, cmp_pe, cmp_w1, cmp_b1, cmp_w2, ssm_lam_re, ssm_lam_im, ssm_log_step, ssm_b_re,
           ssm_b_im, ssm_c_re, ssm_c_im, ssm_d, w_glu, w_out):
    p = dict(norm_w=norm_w, w_in=w_in, gate_b=gate_b, q_norm_w=q_norm_w, k_norm_w=k_norm_w, cmp_pe=cmp_pe,
             cmp_w1=cmp_w1, cmp_b1=cmp_b1, cmp_w2=cmp_w2, ssm_lam_re=ssm_lam_re, ssm_lam_im=ssm_lam_im,
             ssm_log_step=ssm_log_step, ssm_b_re=ssm_b_re, ssm_b_im=ssm_b_im, ssm_c_re=ssm_c_re,
             ssm_c_im=ssm_c_im, ssm_d=ssm_d, w_glu=w_glu, w_out=w_out)
    b_p, s_p, d_model = x_prompt.shape
    b_s, s_s, _ = x_sample.shape
    assert s_s == 1, "the sample group decodes one token per sequence"
    h_p = x_prompt.reshape(b_p * s_p, d_model)
    h_s = x_sample.reshape(b_s, d_model)
    outs_p, outs_s = [], []
    for l in range(norm_w.shape[0]):
        weights = (_prep_params(p, l), _prep_compress(p, l), _prep_s5_chunked(p, l))
        h_p, *rest_p = _prompt_layer(h_p, *weights, b_p, s_p)
        h_s, *rest_s = _sample_layer(h_s, *weights, cache_kv[l], cache_win[l], state_ssm_re[l],
                                     state_ssm_im[l], page_table)
        outs_p.append(rest_p)
        outs_s.append(rest_s)
    stack = lambda outs, i: jnp.stack([o[i] for o in outs])
    return (h_p.reshape(x_prompt.shape), h_s.reshape(x_sample.shape),
            stack(outs_p, 0), stack(outs_s, 0), stack(outs_p, 1), stack(outs_s, 1),
            stack(outs_p, 2), stack(outs_p, 3), stack(outs_s, 2), stack(outs_s, 3))
```

```python
import functools
import math

import jax
import jax.numpy as jnp
from jax import lax
from jax.experimental import pallas as pl
from jax.experimental.pallas import tpu as pltpu

F32 = jnp.float32
BF16 = jnp.bfloat16
I32 = jnp.int32

LANES = 128
SUBLANES = 8
VMEM_LIMIT_BYTES = 56 * 1024 * 1024

HEAD_DIM = 64
N_HEADS = 8
KV_HEADS = 2
Q_PER_KV = N_HEADS // KV_HEADS
SSM_W = 512
SSM_GROUP = 16
SSM_STATE = 64
NSA_W = N_HEADS * HEAD_DIM
CMP_BLOCK = 32
CMP_STRIDE = 16
CMP_HID = 2 * HEAD_DIM
SLC_BLOCK = 64
TOP_N = 16
N_LOCAL_BLOCKS = 2
WINDOW = 512
ROPE_THETA = 500000.0
ROPE_DIM = HEAD_DIM // 4
RMS_EPS = 1e-6
NEG_INF = -1e30

COL_U = 0
COL_GS = SSM_W
COL_Q = 2 * SSM_W
COL_GN = 2 * SSM_W + NSA_W
COL_KV = 2 * SSM_W + 2 * NSA_W
COL_GL = COL_KV + 6 * KV_HEADS * HEAD_DIM
IN_W_PAD = COL_GL + LANES

PROJ_TILE = 512
ATTN_TILE = 256
V_ROWS = HEAD_DIM + 16
SSM_CHUNK = 16
SSM_SLAB_GROUPS = LANES // SSM_GROUP
SSM_SLABS = SSM_W // LANES


def _cparams(sem):
    return pltpu.CompilerParams(dimension_semantics=sem, vmem_limit_bytes=VMEM_LIMIT_BYTES)


def _sigmoid(x):
    return 1.0 / (1.0 + jnp.exp(-x))


def _dot(a, b):
    return jnp.dot(a, b, preferred_element_type=F32)


def _dot_nt(a, b):
    return lax.dot_general(a, b, (((1,), (1,)), ((), ())), preferred_element_type=F32)


def _proj_body(x_ref, nw_ref, w_ref, qnw_ref, knw_ref, gb_ref, ra_ref, rb_ref, rc_ref,
               u_ref, gs_ref, gn_ref, qt_ref, kvt_ref, wint_ref, gt_ref,
               ks_ref, vst_ref, kw_ref, vwt_ref, *, tm, tv, tiles_per_seq):
    x = x_ref[...]
    ms = jnp.mean(x * x, axis=-1, keepdims=True)
    h = (x * lax.rsqrt(ms + RMS_EPS) * nw_ref[...]).astype(BF16)

    def mm(c0, c1):
        return _dot(h, w_ref[:, c0:c1])

    lane = lax.broadcasted_iota(I32, (tm, LANES), 1)
    lo = lane < HEAD_DIM
    ra = ra_ref[...]
    rb = rb_ref[...]
    rc = rc_ref[...]

    def norm_rope(s, wrow):
        s2 = s * s
        slo = jnp.sum(jnp.where(lo, s2, 0.0), axis=-1, keepdims=True)
        shi = jnp.sum(jnp.where(lo, 0.0, s2), axis=-1, keepdims=True)
        msq = jnp.where(lo, slo, shi) * (1.0 / HEAD_DIM)
        y = s * lax.rsqrt(msq + RMS_EPS) * wrow
        half = ROPE_DIM // 2
        return y * ra + pltpu.roll(y, LANES - half, 1) * rb + pltpu.roll(y, half, 1) * rc

    def hi_half(y, head):
        src = pltpu.roll(y, HEAD_DIM, 1) if head == 0 else y
        return jnp.where(lo, 0.0, src)

    zq = mm(COL_Q, COL_GN)
    zkv = mm(COL_KV, COL_GL)
    zgl = mm(COL_GL, IN_W_PAD)

    qnw = qnw_ref[...]
    scale = HEAD_DIM ** -0.5 * math.log2(math.e)
    zeros_t = jnp.zeros((HEAD_DIM, tm), F32)
    for j in range(N_HEADS // 2):
        y_t = (norm_rope(zq[:, j * LANES:(j + 1) * LANES], qnw) * scale).T
        for head in range(2):
            q_t = jnp.concatenate([zeros_t, y_t[head * HEAD_DIM:(head + 1) * HEAD_DIM]], axis=0)
            qt_ref[2 * j + head] = q_t.astype(BF16)

    kc = norm_rope(zkv[:, 0:LANES], knw_ref[0:1, :])
    vc = zkv[:, LANES:2 * LANES]
    ks = norm_rope(zkv[:, 2 * LANES:3 * LANES], knw_ref[1:2, :])
    vs = zkv[:, 3 * LANES:4 * LANES]
    kw = norm_rope(zkv[:, 4 * LANES:5 * LANES], knw_ref[2:3, :])
    vw = zkv[:, 5 * LANES:6 * LANES]
    vs_t, vw_t = vs.T, vw.T
    for i, rows_t in enumerate((kc.T, vc.T, ks.T, vs_t)):
        kvt_ref[i * LANES:(i + 1) * LANES, :] = rows_t
    for i, rows_t in enumerate((kw.T, vw_t)):
        wint_ref[i * LANES:(i + 1) * LANES, :] = rows_t

    row = lax.broadcasted_iota(I32, (tm, LANES), 0)
    pos = (pl.program_id(0) % tiles_per_seq) * tm + row
    onehot = jnp.where(lane == lax.shift_right_logical(pos, 6), 1.0, 0.0)
    ones_t = jnp.where(lax.broadcasted_iota(I32, (V_ROWS - HEAD_DIM, tm), 0) == 0, 1.0, 0.0)
    for g in range(KV_HEADS):
        ks_ref[g] = jnp.where(lo, onehot, hi_half(ks, g)).astype(BF16)
        kw_ref[g] = hi_half(kw, g).astype(BF16)
        for v_t, vt_ref in ((vs_t, vst_ref), (vw_t, vwt_ref)):
            v_aug = jnp.concatenate([v_t[g * HEAD_DIM:(g + 1) * HEAD_DIM], ones_t], axis=0).astype(BF16)
            for t in range(tm // tv):
                vt_ref[g, t] = v_aug[:, t * tv:(t + 1) * tv]

    gates_t = _sigmoid(zgl + gb_ref[...]).T
    for g in range(KV_HEADS):
        gt_ref[g] = gates_t[g * 3 * Q_PER_KV:g * 3 * Q_PER_KV + 2 * SUBLANES]

    zu = mm(COL_U, COL_GS)
    for j in range(SSM_SLABS):
        u_ref[j] = zu[:, j * LANES:(j + 1) * LANES]
    for g_ref, cols in ((gs_ref, (COL_GS, COL_Q)), (gn_ref, (COL_GN, COL_KV))):
        g = mm(*cols)
        g_ref[...] = (g * _sigmoid(g)).astype(BF16)


def _proj_call(x2d, w_pad, norm_w, qnw, knw, gb, tabs, batch, seq):
    T, D = x2d.shape
    tm = min(PROJ_TILE, seq)
    tv = min(ATTN_TILE, seq)
    assert T == batch * seq and seq % tm == 0 and tm % tv == 0 and tv % LANES == 0
    tps = seq // tm
    row_spec = lambda w: pl.BlockSpec((tm, w), lambda i: (i, 0))
    full = lambda a: pl.BlockSpec(a.shape, lambda i: (0,) * a.ndim)
    tab_spec = pl.BlockSpec((tm, LANES), lambda i: (i % tps, 0))
    head_spec = lambda n: pl.BlockSpec((n, tm, LANES), lambda i: (0, i, 0))
    head_t_spec = lambda n, rows: pl.BlockSpec((n, rows, tm), lambda i: (0, 0, i))
    cache_t_spec = lambda rows: pl.BlockSpec((None, rows, tm), lambda i: (i // tps, 0, i % tps))
    tile_t_spec = pl.BlockSpec((KV_HEADS, tm // tv, V_ROWS, tv), lambda i: (0, i, 0, 0))
    tile_t_sds = jax.ShapeDtypeStruct((KV_HEADS, T // tv, V_ROWS, tv), BF16)
    out_shape = (
        jax.ShapeDtypeStruct((SSM_SLABS, T, LANES), F32),
        jax.ShapeDtypeStruct((T, SSM_W), BF16),
        jax.ShapeDtypeStruct((T, NSA_W), BF16),
        jax.ShapeDtypeStruct((N_HEADS, LANES, T), BF16),
        jax.ShapeDtypeStruct((batch, 4 * LANES, seq), F32),
        jax.ShapeDtypeStruct((batch, 2 * LANES, seq), F32),
        jax.ShapeDtypeStruct((KV_HEADS, 2 * SUBLANES, T), F32),
        jax.ShapeDtypeStruct((KV_HEADS, T, LANES), BF16),
        tile_t_sds,
        jax.ShapeDtypeStruct((KV_HEADS, T, LANES), BF16),
        tile_t_sds,
    )
    out_specs = (head_spec(SSM_SLABS), row_spec(SSM_W), row_spec(NSA_W), head_t_spec(N_HEADS, LANES),
                 cache_t_spec(4 * LANES), cache_t_spec(2 * LANES), head_t_spec(KV_HEADS, 2 * SUBLANES),
                 head_spec(KV_HEADS), tile_t_spec, head_spec(KV_HEADS), tile_t_spec)
    return pl.pallas_call(
        functools.partial(_proj_body, tm=tm, tv=tv, tiles_per_seq=tps),
        out_shape=out_shape,
        grid=(T // tm,),
        in_specs=[row_spec(D), full(norm_w), full(w_pad), full(qnw), full(knw), full(gb),
                  tab_spec, tab_spec, tab_spec],
        out_specs=out_specs,
        compiler_params=_cparams(("parallel",)),
        name="proj",
    )(x2d, norm_w, w_pad, qnw, knw, gb, *tabs)


def _prep_params(p, l):
    w_in = p["w_in"][l]
    d_model, in_w = w_in.shape
    tile2 = lambda v: jnp.tile(v, (1, LANES // HEAD_DIM))
    return {
        "w_in": jnp.pad(w_in.astype(BF16), ((0, 0), (0, IN_W_PAD - in_w))),
        "norm_w": p["norm_w"][l].reshape(1, d_model).astype(F32),
        "qnw": tile2(p["q_norm_w"][l].reshape(1, HEAD_DIM)).astype(F32),
        "knw": tile2(p["k_norm_w"][l]).astype(F32),
        "gb": jnp.pad(p["gate_b"][l].reshape(1, -1).astype(F32), ((0, 0), (0, LANES - 3 * N_HEADS))),
        "w_glu": p["w_glu"][l].astype(BF16),
        "w_out": p["w_out"][l].astype(BF16),
    }


def _rope_tables(pos):
    half = ROPE_DIM // 2
    inv = ROPE_THETA ** (-jnp.arange(half, dtype=F32) / half)
    ang = pos.astype(F32)[:, None] * inv
    cos, sin = jnp.cos(ang), jnp.sin(ang)
    n = pos.shape[0]
    rest = HEAD_DIM - ROPE_DIM
    a = jnp.concatenate([cos, cos, jnp.ones((n, rest), F32)], axis=-1)
    b = jnp.concatenate([-sin, jnp.zeros((n, HEAD_DIM - half), F32)], axis=-1)
    c = jnp.concatenate([jnp.zeros((n, half), F32), sin, jnp.zeros((n, rest), F32)], axis=-1)
    return tuple(jnp.tile(t, (1, LANES // HEAD_DIM)) for t in (a, b, c))


def _outmix_body(x_ref, y_ref, gs_ref, o_ref, gn_ref, wg_ref, wo_ref, out_ref):
    y = jnp.concatenate([y_ref[j] for j in range(SSM_SLABS)], axis=-1)
    ab = _dot(y.astype(BF16), wg_ref[...])
    ssm = ab[:, :SSM_W] * _sigmoid(ab[:, SSM_W:]) * gs_ref[...].astype(F32)
    nsa = o_ref[...].astype(F32) * gn_ref[...].astype(F32)
    acc = _dot(ssm.astype(BF16), wo_ref[0:SSM_W, :])
    acc += _dot(nsa.astype(BF16), wo_ref[SSM_W:, :])
    out_ref[...] = x_ref[...] + acc


def _outmix_call(x2d, y_ssm, g_ssm, o_nsa, g_nsa, w_glu, w_out):
    T, D = x2d.shape
    tm = min(512, T)
    row_spec = lambda w: pl.BlockSpec((tm, w), lambda i: (i, 0))
    full = lambda a: pl.BlockSpec(a.shape, lambda i: (0,) * a.ndim)
    return pl.pallas_call(
        _outmix_body,
        out_shape=jax.ShapeDtypeStruct((T, D), F32),
        grid=(T // tm,),
        in_specs=[row_spec(D), pl.BlockSpec((SSM_SLABS, tm, LANES), lambda i: (0, i, 0)),
                  row_spec(SSM_W), row_spec(NSA_W), row_spec(NSA_W),
                  full(w_glu), full(w_out)],
        out_specs=row_spec(D),
        compiler_params=_cparams(("parallel",)),
        name="outmix",
    )(x2d, y_ssm, g_ssm, o_nsa, g_nsa, w_glu, w_out)


def _gelu_tanh(x):
    c = math.sqrt(2.0 / math.pi)
    return 0.5 * x * (1.0 + jnp.tanh(c * (x + 0.044715 * (x * x * x))))


def _compress_rows(rows_refs, pe_ref, wa_ref, wb_ref, b1_ref, w2_ref, kc_ref, vc_ref, nch):
    lane = lax.broadcasted_iota(I32, (nch, LANES), 1)
    for kvi, out_ref in ((0, kc_ref), (1, vc_ref)):
        rows_ref = rows_refs[kvi]
        pa = jnp.zeros((nch, 2 * CMP_HID), F32)
        pb = jnp.zeros((nch, 2 * CMP_HID), F32)
        for j0 in range(0, CMP_STRIDE, 2):
            xs = [rows_ref[pl.ds(j, nch, stride=CMP_STRIDE), :] for j in (j0, j0 + 1)]
            xa = jnp.concatenate([xs[i] + pe_ref[kvi, 0, j0 + i:j0 + i + 1, :] for i in range(2)], axis=-1)
            xb = jnp.concatenate([xs[i] + pe_ref[kvi, 1, j0 + i:j0 + i + 1, :] for i in range(2)], axis=-1)
            wsl = slice(j0 * LANES, (j0 + 2) * LANES)
            pa += _dot(xa.astype(BF16), wa_ref[kvi, wsl, :])
            pb += _dot(xb.astype(BF16), wb_ref[kvi, wsl, :])
        hid = _gelu_tanh(pa + pltpu.roll(pb, nch - 1, 0) + b1_ref[kvi]).astype(BF16)
        for g in range(KV_HEADS):
            o = _dot(hid, w2_ref[kvi, g])
            if kvi == 1:
                o = jnp.where(lane == HEAD_DIM, 1.0, o).T
            out_ref[g] = o.astype(BF16)


def _compress_prompt_body(kvt_ref, pe_ref, wa_ref, wb_ref, b1_ref, w2_ref, kc_ref, vc_ref,
                          krows_ref, vrows_ref, *, nch):
    for c in range(kvt_ref.shape[1] // LANES):
        cs = slice(c * LANES, (c + 1) * LANES)
        krows_ref[cs, :] = kvt_ref[0:LANES, cs].T
        vrows_ref[cs, :] = kvt_ref[LANES:2 * LANES, cs].T
    _compress_rows((krows_ref, vrows_ref), pe_ref, wa_ref, wb_ref, b1_ref, w2_ref, kc_ref, vc_ref, nch)


def _compress_prompt_call(kv_t, cw, batch, seq):
    nch = seq // CMP_STRIDE
    full = lambda a: pl.BlockSpec(a.shape, lambda b: (0,) * a.ndim)
    out_spec = pl.BlockSpec((KV_HEADS, nch, LANES), lambda b: (0, b, 0))
    out_sds = jax.ShapeDtypeStruct((KV_HEADS, batch * nch, LANES), BF16)
    out_t_spec = pl.BlockSpec((KV_HEADS, None, LANES, nch), lambda b: (0, b, 0, 0))
    out_t_sds = jax.ShapeDtypeStruct((KV_HEADS, batch, LANES, nch), BF16)
    return pl.pallas_call(
        functools.partial(_compress_prompt_body, nch=nch),
        out_shape=(out_sds, out_t_sds),
        grid=(batch,),
        in_specs=[pl.BlockSpec((None, 2 * LANES, seq), lambda b: (b, 0, 0)),
                  full(cw["pe"]), full(cw["wa"]), full(cw["wb"]), full(cw["b1"]), full(cw["w2"])],
        out_specs=(out_spec, out_t_spec),
        scratch_shapes=[pltpu.VMEM((seq, LANES), F32)] * 2,
        compiler_params=_cparams(("parallel",)),
        name="compress_prompt",
    )(kv_t, cw["pe"], cw["wa"], cw["wb"], cw["b1"], cw["w2"])


def _prep_compress(p, l):
    eye = jnp.eye(KV_HEADS, dtype=F32)
    w1 = p["cmp_w1"][l].reshape(2, 2, CMP_STRIDE, HEAD_DIM, CMP_HID)
    wexp = jnp.einsum("khjdn,ge->khjgden", w1, eye).reshape(2, 2, CMP_STRIDE * LANES, KV_HEADS * CMP_HID)
    pe = p["cmp_pe"][l].reshape(2, 2, CMP_STRIDE, HEAD_DIM)
    w2 = p["cmp_w2"][l]
    zeros = jnp.zeros_like(w2[0])
    w2k = jnp.concatenate([zeros, w2[0]], axis=-1)
    w2v = jnp.concatenate([w2[1], zeros], axis=-1)
    w2e = jnp.stack([jnp.einsum("hd,ge->gehd", w, eye).reshape(KV_HEADS, KV_HEADS * CMP_HID, LANES)
                     for w in (w2k, w2v)])
    return {
        "pe": jnp.tile(pe, (1, 1, 1, KV_HEADS)).astype(F32),
        "wa": wexp[:, 0].astype(BF16),
        "wb": wexp[:, 1].astype(BF16),
        "b1": jnp.tile(p["cmp_b1"][l].reshape(2, 1, CMP_HID), (1, 1, KV_HEADS)).astype(F32),
        "w2": w2e.astype(BF16),
    }


def _overlap_matrix(n_tok_pad, n_tok, n_blk, n_cols=LANES):
    c_start = jnp.arange(n_tok_pad) * CMP_STRIDE
    blk = jnp.arange(n_cols)
    ov = ((c_start[:, None] < (blk[None, :] + 1) * SLC_BLOCK)
          & (c_start[:, None] + CMP_BLOCK > blk[None, :] * SLC_BLOCK)
          & (jnp.arange(n_tok_pad)[:, None] < n_tok) & (blk[None, :] < n_blk))
    return ov.astype(BF16)


def _topk_select_t(w_ref, imp_t, q0, tq):
    nb = imp_t.shape[0]
    n_i = lax.broadcasted_iota(I32, (nb, tq), 0)
    qblk = lax.shift_right_logical(q0 + lax.broadcasted_iota(I32, (nb, tq), 1), 6)
    causal = n_i <= qblk
    forced = (n_i == 0) | (n_i >= qblk - (N_LOCAL_BLOCKS - 1))
    w_ref[...] = jnp.where(causal, jnp.where(forced, jnp.inf, imp_t), -jnp.inf)
    last_blk = lax.shift_right_logical(q0 + tq - 1, 6)
    n_grp = nb // SUBLANES
    rank = [jnp.zeros((SUBLANES, tq), F32) for _ in range(n_grp)]
    grp_i = lax.broadcasted_iota(I32, (SUBLANES, tq), 0)

    def count_group(mg, rank):
        rank = list(rank)
        for mi in range(SUBLANES):
            m = mg * SUBLANES + mi
            wm = w_ref[m:m + 1, :]
            for ng in range(n_grp):
                w = w_ref[ng * SUBLANES:(ng + 1) * SUBLANES, :]
                if ng > mg:
                    beats = jnp.where(wm >= w, 1.0, 0.0)
                elif ng < mg:
                    beats = jnp.where(wm > w, 1.0, 0.0)
                else:
                    beats = jnp.where(grp_i > mi, jnp.where(wm >= w, 1.0, 0.0), jnp.where(wm > w, 1.0, 0.0))
                rank[ng] = rank[ng] + beats
        return tuple(rank)

    rank = tuple(rank)
    for mg in range(n_grp):
        rank = lax.cond(mg * SUBLANES <= last_blk, functools.partial(count_group, mg), lambda r: r, rank)
    return causal & (jnp.concatenate(rank, axis=0) < TOP_N)


def _flash_units(units, states):
    states = [(list(ms), list(accs)) for ms, accs in states]

    def scores(i):
        k, _, mask, q_ts, _ = units[i]
        s = [_dot(k, q_t) for q_t in q_ts]
        return s if mask is None else [jnp.where(mask, x, NEG_INF) for x in s]

    ahead = 2
    pending = {i: scores(i) for i in range(min(ahead, len(units)))}
    for i, (_, v_t, _, _, sid) in enumerate(units):
        ms, accs = states[sid]
        s = pending.pop(i)
        m_new = [jnp.maximum(m, jnp.max(x, axis=0, keepdims=True)) for m, x in zip(ms, s)]
        alpha = [jnp.exp2(m - mn) for m, mn in zip(ms, m_new)]
        p = [jnp.exp2(x - mn).astype(BF16) for x, mn in zip(s, m_new)]
        pv = [_dot(v_t, x) for x in p]
        if i + ahead < len(units):
            pending[i + ahead] = scores(i + ahead)
        accs[:] = [a * acc + x for a, acc, x in zip(alpha, accs, pv)]
        ms[:] = m_new
    return tuple((tuple(ms), tuple(accs)) for ms, accs in states)


def _attn_body(qt_ref, kc_ref, vct_ref, ovlt_ref, ks_ref, vst_ref, kw_ref, vwt_ref, gate_ref, o_ref,
               qa_ref, ocmp_ref, w_ref, *, tq, ncp):
    R = Q_PER_KV
    qt = pl.program_id(2)
    q0 = qt * tq
    nbs = LANES // 2

    def compressed(n_c):
        c_i = lax.broadcasted_iota(I32, (n_c, tq), 0)
        qpos_c = q0 + lax.broadcasted_iota(I32, (n_c, tq), 1)
        cmask = c_i * CMP_STRIDE + (CMP_BLOCK - 1) <= qpos_c
        kc = kc_ref[0:n_c, :]
        s = [jnp.where(cmask, _dot(kc, qt_ref[r]), NEG_INF) for r in range(R)]
        e = [jnp.where(cmask, jnp.exp2(x - jnp.max(x, axis=0, keepdims=True)), 0.0) for x in s]
        l = [jnp.sum(x, axis=0, keepdims=True) for x in e]
        p = [(x * (1.0 / jnp.where(y > 0.0, y, 1.0))).astype(BF16) for x, y in zip(e, l)]
        for r in range(R):
            ocmp_ref[r] = _dot(vct_ref[:, 0:n_c], p[r])
        return sum(_dot(ovlt_ref[:, 0:n_c], x) for x in p)

    half = ncp // 2
    if half % LANES == 0:
        imp = lax.cond((q0 + tq) // CMP_STRIDE <= half, lambda: compressed(half), lambda: compressed(ncp))
    else:
        imp = compressed(ncp)

    sel = _topk_select_t(w_ref, imp[:nbs], q0, tq)
    bias = jnp.concatenate([jnp.where(sel, 0.0, NEG_INF), jnp.zeros((LANES - nbs, tq), F32)], axis=0)
    for r in range(R):
        qa_ref[r] = (qt_ref[r].astype(F32) + bias).astype(BF16)

    key_i = lax.broadcasted_iota(I32, (tq, tq), 0)
    qry_i = lax.broadcasted_iota(I32, (tq, tq), 1)
    init = (tuple(jnp.full((1, tq), -jnp.inf, F32) for _ in range(R)),
            tuple(jnp.zeros((V_ROWS, tq), F32) for _ in range(R)))

    slc_q = [qa_ref[r] for r in range(R)]
    win_q = [qt_ref[r] for r in range(R)]
    diag = key_i <= qry_i

    def unit(k_ref, vt_ref, q_ts, sid, j, mask):
        return (k_ref[pl.ds(pl.multiple_of(j * tq, tq), tq), :], vt_ref[j], mask, q_ts, sid)

    GROUP = 4

    def slc_group(i, st):
        units = [unit(ks_ref, vst_ref, slc_q, 0, GROUP * i + t, None) for t in range(GROUP)]
        return _flash_units(units, [st])[0]

    slc_state = lax.fori_loop(0, qt // GROUP, slc_group, init)

    nwin = WINDOW // tq

    def tail(n_s, n_w, st):
        slc = [unit(ks_ref, vst_ref, slc_q, 0, qt - (n_s - 1 - t), diag if t == n_s - 1 else None)
               for t in range(n_s)]
        win = [unit(kw_ref, vwt_ref, win_q, 1, qt - (n_w - 1 - t),
                    diag if t == n_w - 1 else ((key_i > qry_i) if (t == 0 and n_w == nwin + 1) else None))
               for t in range(n_w)]
        units = []
        for t in range(max(n_s, n_w)):
            units += slc[t:t + 1] + win[t:t + 1]
        return _flash_units(units, [st, init])

    assert nwin < GROUP
    tails = ([functools.partial(tail, n, n) for n in range(1, nwin + 1)]
             + [functools.partial(tail, n, nwin + 1) for n in range(1, GROUP + 1)])
    which = jnp.where(qt < nwin, qt, nwin + qt % GROUP)
    (_, acc_s), (_, acc_w) = lax.switch(which, tails, slc_state)

    outs = []
    for r in range(R):
        a_s = acc_s[r]
        a_w = acc_w[r]
        g = lambda k: gate_ref[3 * r + k:3 * r + k + 1, :]
        head = lambda a: a[:HEAD_DIM] * (1.0 / a[HEAD_DIM:HEAD_DIM + 1, :])
        outs.append(g(0) * ocmp_ref[r][:HEAD_DIM] + g(1) * head(a_s) + g(2) * head(a_w))
    o_ref[...] = jnp.concatenate(outs, axis=0).T.astype(o_ref.dtype)


def _attn_call(q_t, kc, vc_t, ovl_t, ks, vs_t, kw, vw_t, gates_t, batch, seq):
    tq = min(ATTN_TILE, seq)
    nq = seq // tq
    ncp = kc.shape[1] // batch
    R = Q_PER_KV
    assert seq // SLC_BLOCK <= LANES // 2 and WINDOW % tq == 0
    k_spec = pl.BlockSpec((None, seq, LANES), lambda b, g, t: (g, b, 0))
    vt_spec = pl.BlockSpec((None, nq, V_ROWS, tq), lambda b, g, t: (g, b, 0, 0))
    acc = pltpu.VMEM((R, LANES, tq), F32)
    return pl.pallas_call(
        functools.partial(_attn_body, tq=tq, ncp=ncp),
        out_shape=jax.ShapeDtypeStruct((batch * seq, NSA_W), BF16),
        grid=(batch, KV_HEADS, nq),
        in_specs=[pl.BlockSpec((R, LANES, tq), lambda b, g, t: (g, 0, b * nq + t)),
                  pl.BlockSpec((None, ncp, LANES), lambda b, g, t: (g, b, 0)),
                  pl.BlockSpec((None, None, LANES, ncp), lambda b, g, t: (g, b, 0, 0)),
                  pl.BlockSpec(ovl_t.shape, lambda b, g, t: (0, 0)),
                  k_spec, vt_spec, k_spec, vt_spec,
                  pl.BlockSpec((None, 2 * SUBLANES, tq), lambda b, g, t: (g, 0, b * nq + t))],
        out_specs=pl.BlockSpec((tq, R * HEAD_DIM), lambda b, g, t: (b * nq + t, g)),
        scratch_shapes=[pltpu.VMEM((R, LANES, tq), BF16), acc, pltpu.VMEM((LANES // 2, tq), F32)],
        compiler_params=_cparams(("parallel", "parallel", "arbitrary")),
        name="nsa_prompt",
    )(q_t, kc, vc_t, ovl_t, ks, vs_t, kw, vw_t, gates_t)


def _s5_discretise(p, l):
    lr = p["ssm_lam_re"][l].astype(F32)
    li = p["ssm_lam_im"][l].astype(F32)
    dt = jnp.exp(p["ssm_log_step"][l].astype(F32))[:, None]

    def apow(t):
        mag, ang = jnp.exp(lr * dt * t), li * dt * t
        return mag * jnp.cos(ang), mag * jnp.sin(ang)

    a_re, a_im = apow(1.0)
    den = lr * lr + li * li
    nr, ni = a_re - 1.0, a_im
    f_re, f_im = (nr * lr + ni * li) / den, (ni * lr - nr * li) / den
    br, bi = p["ssm_b_re"][l].astype(F32), p["ssm_b_im"][l].astype(F32)
    bb_re = f_re[..., None] * br - f_im[..., None] * bi
    bb_im = f_re[..., None] * bi + f_im[..., None] * br
    return apow, bb_re, bb_im


def _s5_expand_body(pst_ref, kt_ref, cp_ref, wcol_ref, wst_ref, wout_ref):
    T, E, N, C = SSM_CHUNK, SSM_SLAB_GROUPS, SSM_STATE, SSM_GROUP
    row = lax.broadcasted_iota(I32, (LANES, LANES), 0)
    lane = lax.broadcasted_iota(I32, (LANES, LANES), 1)
    row_grp = row // C
    low = lane < N
    own_grp = row_grp == lane // C
    lane_n = lax.broadcasted_iota(I32, (N, LANES), 1)
    wcol_ref[T * LANES:(T + 1) * LANES, :] = jnp.zeros((LANES, LANES), BF16)
    wout_ref[0, :, LANES:2 * LANES] = jnp.zeros((2 * E * N, LANES), BF16)
    for i in range(T):
        wcol_ref[i * LANES:(i + 1) * LANES, :] = jnp.where(own_grp, kt_ref[i], 0.0).astype(BF16)
        x = pst_ref[i]
        x_sw = pltpu.roll(x, N, 1)
        halves = (jnp.where(low, x, x_sw), jnp.where(low, x_sw, x))
        for ri, x2 in enumerate(halves):
            for q in range(E // 2):
                own = row_grp == 2 * q + jnp.where(low, 0, 1)
                c0 = ri * E * N + q * LANES
                wst_ref[i * LANES:(i + 1) * LANES, c0:c0 + LANES] = jnp.where(own, x2, 0.0).astype(BF16)
    for i in range(T + 1):
        pair, half = (0, 0) if i == 0 else ((i + 1) // 2, (i - 1) % 2)
        m_t = cp_ref[i].T
        for ri in range(2):
            rows = m_t[ri * N:(ri + 1) * N]
            for g in range(E):
                r0 = ri * E * N + g * N
                wout_ref[pair, r0:r0 + N, half * LANES:(half + 1) * LANES] = (
                    jnp.where(lane_n // C == g, rows, 0.0).astype(BF16))


def _prep_s5_chunked(p, l):
    hi = lax.Precision.HIGHEST
    apow, bb_re, bb_im = _s5_discretise(p, l)
    c_re, c_im = p["ssm_c_re"][l].astype(F32), p["ssm_c_im"][l].astype(F32)
    T, J, E, N, C = SSM_CHUNK, SSM_SLABS, SSM_SLAB_GROUPS, SSM_STATE, SSM_GROUP
    pw_re, pw_im = apow(jnp.arange(T + 1, dtype=F32)[:, None, None])
    bt_re, bt_im = bb_re.transpose(0, 2, 1), bb_im.transpose(0, 2, 1)
    p_re = pw_re[:, :, None, :] * bt_re - pw_im[:, :, None, :] * bt_im
    p_im = pw_re[:, :, None, :] * bt_im + pw_im[:, :, None, :] * bt_re
    kt = (jnp.einsum("tgkn,gcn->tgkc", p_re[:T], jnp.tile(c_re, (1, E, 1)), precision=hi)
          - jnp.einsum("tgkn,gcn->tgkc", p_im[:T], jnp.tile(c_im, (1, E, 1)), precision=hi))
    cp_re = c_re * pw_re[:, :, None, :] - c_im * pw_im[:, :, None, :]
    cp_im = c_re * pw_im[:, :, None, :] + c_im * pw_re[:, :, None, :]
    slab = lambda x: jnp.moveaxis(x.reshape(x.shape[0], J, E * C, LANES), 1, 0)
    pst = slab(jnp.concatenate([p_re[:T][::-1], p_im[:T][::-1]], axis=-1))
    ktile = slab(kt[::-1])
    cpn = slab(jnp.concatenate([cp_re, -cp_im], axis=-1))
    blk = lambda n: pl.BlockSpec((None, n, LANES, LANES), lambda j: (j, 0, 0, 0))
    w_col, w_st, w_out = pl.pallas_call(
        _s5_expand_body,
        out_shape=(jax.ShapeDtypeStruct((J, (T + 1) * LANES, LANES), BF16),
                   jax.ShapeDtypeStruct((J, T * LANES, 2 * E * N), BF16),
                   jax.ShapeDtypeStruct((J, T // 2 + 1, 2 * E * N, 2 * LANES), BF16)),
        grid=(J,),
        in_specs=[blk(T), blk(T), blk(T + 1)],
        out_specs=(pl.BlockSpec((None, (T + 1) * LANES, LANES), lambda j: (j, 0, 0)),
                   pl.BlockSpec((None, T * LANES, 2 * E * N), lambda j: (j, 0, 0)),
                   pl.BlockSpec((None, T // 2 + 1, 2 * E * N, 2 * LANES), lambda j: (j, 0, 0, 0))),
        compiler_params=_cparams(("parallel",)),
        name="s5_expand",
    )(pst, ktile, cpn)
    return {
        "w_col": w_col, "w_st": w_st, "w_out": w_out,
        "a_re": pw_re[T].reshape(J, 1, E * N), "a_im": pw_im[T].reshape(J, 1, E * N),
        "a1_re": pw_re[1].reshape(J, 1, E * N), "a1_im": pw_im[1].reshape(J, 1, E * N),
        "d": p["ssm_d"][l].reshape(J, 1, LANES).astype(F32),
    }


def _s5_prompt_body(u_ref, wcol_ref, wst_ref, are_ref, aim_ref, wout_ref, d_ref,
                    y_ref, hre_ref, him_ref, xs_ref, hp_ref, *, n_chunks):
    T = SSM_CHUNK
    ns = SSM_SLAB_GROUPS * SSM_STATE
    u_pos = [u_ref[pl.ds(s, n_chunks, stride=T), :] for s in range(T)]
    ub = jnp.concatenate(u_pos, axis=-1).astype(BF16)
    xs_ref[...] = _dot(ub, wst_ref[...])
    a_re = are_ref[...]
    a_im = aim_ref[...]

    def step(c, carry):
        hr, hi = carry
        hp_ref[pl.ds(c, 1), 0:ns] = hr
        hp_ref[pl.ds(c, 1), ns:2 * ns] = hi
        xr = xs_ref[pl.ds(c, 1), 0:ns]
        xi = xs_ref[pl.ds(c, 1), ns:2 * ns]
        return a_re * hr - a_im * hi + xr, a_re * hi + a_im * hr + xi

    zero = jnp.zeros((1, ns), F32)
    hr, hi = lax.fori_loop(0, n_chunks, step, (zero, zero))
    hre_ref[...] = jnp.broadcast_to(hr, hre_ref.shape)
    him_ref[...] = jnp.broadcast_to(hi, him_ref.shape)
    hpb = hp_ref[...].astype(BF16)
    for t in range(0, T, 2):
        k_rows = (t + 2) * LANES
        w_pair = jnp.concatenate([wcol_ref[(T - 1 - t) * LANES:(T - 1 - t) * LANES + k_rows, :],
                                  wcol_ref[(T - 2 - t) * LANES:T * LANES, :]], axis=1)
        y_pair = _dot(ub[:, :k_rows], w_pair) + _dot(hpb, wout_ref[t // 2 + 1])
        for i in range(2):
            y_ref[pl.ds(t + i, n_chunks, stride=T), :] = (
                y_pair[:, i * LANES:(i + 1) * LANES] + d_ref[...] * u_pos[t + i])


def _s5_prompt_call(u_slab, sw, batch, seq):
    T, J = SSM_CHUNK, SSM_SLABS
    n_chunks = seq // T
    ns = SSM_SLAB_GROUPS * SSM_STATE
    row_spec = pl.BlockSpec((None, seq, LANES), lambda j, b: (j, b, 0))
    slab_spec = lambda a: pl.BlockSpec((None,) + a.shape[1:], lambda j, b: (j,) + (0,) * (a.ndim - 1))
    st_spec = pl.BlockSpec((None, None, SUBLANES, ns), lambda j, b: (b, j, 0, 0))
    st_sds = jax.ShapeDtypeStruct((batch, J, SUBLANES, ns), F32)
    y, hre, him = pl.pallas_call(
        functools.partial(_s5_prompt_body, n_chunks=n_chunks),
        out_shape=(jax.ShapeDtypeStruct(u_slab.shape, F32), st_sds, st_sds),
        grid=(J, batch),
        in_specs=[row_spec, slab_spec(sw["w_col"]), slab_spec(sw["w_st"]), slab_spec(sw["a_re"]),
                  slab_spec(sw["a_im"]), slab_spec(sw["w_out"]), slab_spec(sw["d"])],
        out_specs=(row_spec, st_spec, st_spec),
        scratch_shapes=[pltpu.VMEM((n_chunks, 2 * ns), F32), pltpu.VMEM((n_chunks, 2 * ns), F32)],
        compiler_params=_cparams(("parallel", "parallel")),
        name="s5_prompt",
    )(u_slab, sw["w_col"], sw["w_st"], sw["a_re"], sw["a_im"], sw["w_out"], sw["d"])
    n_groups = J * SSM_SLAB_GROUPS
    state = lambda h: h[:, :, 0, :].reshape(batch, n_groups, SSM_STATE)
    return y, state(hre), state(him)


def _s5_step_body(u_ref, wx_ref, are_ref, aim_ref, h0re_ref, h0im_ref, wy_ref, d_ref,
                  y_ref, hre_ref, him_ref):
    ns = SSM_SLAB_GROUPS * SSM_STATE
    for j in range(SSM_SLABS):
        sl = slice(j * ns, (j + 1) * ns)
        u = u_ref[j]
        x = _dot(u.astype(BF16), wx_ref[j])
        a_re, a_im = are_ref[j], aim_ref[j]
        h0r, h0i = h0re_ref[:, sl], h0im_ref[:, sl]
        hr = a_re * h0r - a_im * h0i + x[:, :ns]
        hi = a_re * h0i + a_im * h0r + x[:, ns:]
        hre_ref[:, sl] = hr
        him_ref[:, sl] = hi
        y_ref[j] = _dot(jnp.concatenate([hr, hi], axis=-1).astype(BF16), wy_ref[j]) + d_ref[j] * u


def _s5_step_call(u_slab, h0_re, h0_im, sw):
    J, n_tok, _ = u_slab.shape
    T = SSM_CHUNK
    full = lambda a: pl.BlockSpec(a.shape, lambda i: (0,) * a.ndim)
    wx_spec = pl.BlockSpec((J, LANES, sw["w_st"].shape[2]), lambda i: (0, T - 1, 0))
    wy_spec = pl.BlockSpec((J, None, sw["w_out"].shape[2], LANES), lambda i: (0, 0, 0, 0))
    st_sds = jax.ShapeDtypeStruct(h0_re.shape, F32)
    return pl.pallas_call(
        _s5_step_body,
        out_shape=(jax.ShapeDtypeStruct(u_slab.shape, F32), st_sds, st_sds),
        grid=(1,),
        in_specs=[full(u_slab), wx_spec, full(sw["a1_re"]), full(sw["a1_im"]), full(h0_re), full(h0_im),
                  wy_spec, full(sw["d"])],
        out_specs=(full(u_slab), full(h0_re), full(h0_re)),
        compiler_params=_cparams(("arbitrary",)),
        name="s5_step",
    )(u_slab, sw["w_st"], sw["a1_re"], sw["a1_im"], h0_re, h0_im, sw["w_out"], sw["d"])


def _compress_sample_body(pt_ref, *refs, n_pages, page_rows, nch):
    del pt_ref
    page_refs = refs[:n_pages]
    pe_ref, wa_ref, wb_ref, b1_ref, w2_ref, kc_ref, vc_ref = refs[n_pages:n_pages + 7]
    buffers = refs[n_pages + 7:]
    b = pl.program_id(0)

    @pl.when(b == 0)
    def _():
        for buf in buffers[2:]:
            buf[...] = jnp.zeros(buf.shape, F32)

    def step(stage, done):
        for i, page_ref in enumerate(page_refs):
            for rows_ref, slot in zip(stage, range(2)):
                rows_ref[i * page_rows:(i + 1) * page_rows, :] = page_ref[slot].reshape(LANES, page_rows).T
        _compress_rows(done, pe_ref, wa_ref, wb_ref, b1_ref, w2_ref, kc_ref, vc_ref, nch)

    @pl.when(b % 2 == 0)
    def _():
        step(buffers[:2], buffers[2:])

    @pl.when(b % 2 == 1)
    def _():
        step(buffers[2:], buffers[:2])


def _compress_sample_call(cache, page_table, cw):
    n_seq, n_pages = page_table.shape
    page_rows = cache.shape[-1]
    assert page_rows == LANES
    nch = n_pages * page_rows // CMP_STRIDE
    staged = lambda b: jnp.minimum(b, n_seq - 1)
    page_spec = lambda i: pl.BlockSpec((None, 2, KV_HEADS, HEAD_DIM, page_rows),
                                       lambda b, pt: (pt[staged(b), i], 0, 0, 0, 0))
    full = lambda a: pl.BlockSpec(a.shape, lambda b, pt: (0,) * a.ndim)
    out_of = lambda b: jnp.maximum(b - 1, 0)
    out_spec = pl.BlockSpec((KV_HEADS, nch, LANES), lambda b, pt: (0, out_of(b), 0))
    out_sds = jax.ShapeDtypeStruct((KV_HEADS, n_seq * nch, LANES), BF16)
    out_t_spec = pl.BlockSpec((KV_HEADS, None, LANES, nch), lambda b, pt: (0, out_of(b), 0, 0))
    out_t_sds = jax.ShapeDtypeStruct((KV_HEADS, n_seq, LANES, nch), BF16)
    weights = (cw["pe"], cw["wa"], cw["wb"], cw["b1"], cw["w2"])
    return pl.pallas_call(
        functools.partial(_compress_sample_body, n_pages=n_pages, page_rows=page_rows, nch=nch),
        out_shape=(out_sds, out_t_sds),
        grid_spec=pltpu.PrefetchScalarGridSpec(
            num_scalar_prefetch=1,
            grid=(n_seq + 1,),
            in_specs=[page_spec(i) for i in range(n_pages)] + [full(a) for a in weights],
            out_specs=(out_spec, out_t_spec),
            scratch_shapes=[pltpu.VMEM((n_pages * page_rows, LANES), F32)] * 4),
        compiler_params=_cparams(("arbitrary",)),
        name="compress_sample",
    )(page_table, *([cache] * n_pages), *weights)


def _group_rows(x0, x1):
    row = lax.broadcasted_iota(I32, x0.shape, 0)
    return jnp.where(row < Q_PER_KV, x0, x1)


def _sample_select_body(q_ref, kc_ref, vct_ref, ovl_ref, tri_ref, ocmp_ref, idx_ref, *, ncp, qpos, nbp):
    q8 = q_ref[...]
    c_i = lax.broadcasted_iota(I32, (N_HEADS, ncp), 1)
    cmask = c_i * CMP_STRIDE + (CMP_BLOCK - 1) <= qpos
    s = _group_rows(_dot_nt(q8, kc_ref[0]), _dot_nt(q8, kc_ref[1]))
    s = jnp.where(cmask, s, NEG_INF)
    e = jnp.where(cmask, jnp.exp2(s - jnp.max(s, axis=-1, keepdims=True)), 0.0)
    l = jnp.sum(e, axis=-1, keepdims=True)
    p = (e / jnp.where(l > 0.0, l, 1.0)).astype(BF16)
    ocmp_ref[...] = _group_rows(_dot_nt(p, vct_ref[0]), _dot_nt(p, vct_ref[1]))
    imp8 = _dot(p, ovl_ref[...])

    n_row = lax.broadcasted_iota(I32, (1, nbp), 1)
    qblk = qpos // SLC_BLOCK
    causal = n_row <= qblk
    forced = (n_row == 0) | (n_row >= qblk - (N_LOCAL_BLOCKS - 1))
    m_i = lax.broadcasted_iota(I32, (nbp, nbp), 0)
    n_i = lax.broadcasted_iota(I32, (nbp, nbp), 1)
    lane = lax.broadcasted_iota(I32, (1, LANES), 1)
    idx_rows = []
    for g in range(KV_HEADS):
        imp = jnp.sum(imp8[g * Q_PER_KV:(g + 1) * Q_PER_KV], axis=0, keepdims=True)
        w = jnp.where(causal, jnp.where(forced, jnp.inf, imp), -jnp.inf)
        w_sq = jnp.broadcast_to(w, (nbp, nbp))
        w_col = w_sq.T
        beats = jnp.where(n_i > m_i, jnp.where(w_col >= w_sq, 1.0, 0.0), jnp.where(w_col > w_sq, 1.0, 0.0))
        rank = jnp.sum(beats, axis=0, keepdims=True)
        sel = causal & (rank < TOP_N)
        self_f = jnp.where(sel, 1.0, 0.0)
        before = _dot(self_f.astype(BF16), tri_ref[...])
        idx = jnp.full((1, LANES), -1, I32)
        for k in range(TOP_N):
            hit = sel & (before == float(k))
            val = jnp.sum(jnp.where(hit, n_row.astype(F32) + 1.0, 0.0), axis=-1, keepdims=True) - 1.0
            idx = jnp.where(lane == k, val.astype(I32), idx)
        idx_rows.append(idx)
    idx_ref[...] = jnp.concatenate(idx_rows + [jnp.full((SUBLANES - KV_HEADS, LANES), -1, I32)], axis=0)


def _sample_select_call(q8, kc, vc_t, ovl, tri, qpos):
    n_seq = q8.shape[0]
    ncp = kc.shape[1] // n_seq
    nbp = ovl.shape[1]
    cmp_spec = pl.BlockSpec((KV_HEADS, ncp, LANES), lambda b: (0, b, 0))
    row_spec = pl.BlockSpec((None, N_HEADS, LANES), lambda b: (b, 0, 0))
    full = lambda a: pl.BlockSpec(a.shape, lambda b: (0,) * a.ndim)
    return pl.pallas_call(
        functools.partial(_sample_select_body, ncp=ncp, qpos=qpos, nbp=nbp),
        out_shape=(jax.ShapeDtypeStruct((n_seq, N_HEADS, LANES), F32),
                   jax.ShapeDtypeStruct((n_seq, SUBLANES, LANES), I32)),
        grid=(n_seq,),
        in_specs=[row_spec, cmp_spec, pl.BlockSpec((KV_HEADS, None, LANES, ncp), lambda b: (0, b, 0, 0)),
                  full(ovl), full(tri)],
        out_specs=(row_spec, pl.BlockSpec((None, SUBLANES, LANES), lambda b: (b, 0, 0))),
        compiler_params=_cparams(("parallel",)),
        name="nsa_sample_select",
    )(q8, kc, vc_t, ovl, tri)


def _sample_attend_body(idx_ref, pt_ref, q_ref, ocmp_ref, gate_ref, ksn_ref, vsn_ref, kwn_ref, vwn_ref,
                        win_ref, *refs, n_cache_blocks, blocks_per_page, win_skip):
    del pt_ref
    n_blk = KV_HEADS * TOP_N
    kv_refs, o_ref = refs[:n_blk], refs[n_blk]
    b = pl.program_id(0)
    q = q_ref[...]
    qf = q.astype(F32)
    row_g = (lax.broadcasted_iota(I32, (N_HEADS, 1), 0) >= Q_PER_KV).astype(I32)

    def attend(s_list, v_list, kn_ref, vn_ref):
        s_self = jnp.sum(qf * kn_ref[...].astype(F32), axis=-1, keepdims=True)
        m = s_self
        for s in s_list:
            m = jnp.maximum(m, jnp.max(s, axis=-1, keepdims=True))
        p_self = jnp.exp2(s_self - m)
        l = p_self
        acc = p_self.astype(BF16).astype(F32) * vn_ref[...].astype(F32)
        for s, v in zip(s_list, v_list):
            p = jnp.exp2(s - m)
            l = l + jnp.sum(p, axis=-1, keepdims=True)
            acc = acc + _dot_nt(p.astype(BF16), v().astype(BF16))
        return acc / l

    s_list, v_list = [], []
    for j in range(n_blk):
        s = _dot(q, kv_refs[j][0].astype(BF16))
        col = lax.broadcasted_iota(I32, s.shape, 1)
        n = idx_ref[b, j]
        first = (n & (blocks_per_page - 1)) * SLC_BLOCK
        ok = ((row_g == j // TOP_N) & (col >= first) & (col < first + SLC_BLOCK)
              & (n >= 0) & (n < n_cache_blocks))
        s_list.append(jnp.where(ok, s, NEG_INF))
        v_list.append(lambda j=j: kv_refs[j][1])
    o_slc = attend(s_list, v_list, ksn_ref, vsn_ref)

    s_list, v_list = [], []
    for g in range(KV_HEADS):
        s = _dot(q, win_ref[0, g].astype(BF16))
        col = lax.broadcasted_iota(I32, s.shape, 1)
        s_list.append(jnp.where((row_g == g) & (col >= win_skip), s, NEG_INF))
        v_list.append(lambda g=g: win_ref[1, g])
    o_win = attend(s_list, v_list, kwn_ref, vwn_ref)

    gates = gate_ref[...]
    o_ref[...] = gates[:, 0:1] * ocmp_ref[:, 0:HEAD_DIM] + gates[:, 1:2] * o_slc + gates[:, 2:3] * o_win


def _sample_attend_call(idx, page_table, q64, ocmp, gates8, ksn, vsn, kwn, vwn, cache_win_t, cache_t, win_skip):
    n_seq, n_pages = page_table.shape
    page_rows = cache_t.shape[-1]
    bpp = page_rows // SLC_BLOCK
    n_cache_blocks = n_pages * bpp
    row_spec = lambda a: pl.BlockSpec((None,) + a.shape[1:], lambda b, ix, pt: (b,) + (0,) * (a.ndim - 1))

    assert bpp & (bpp - 1) == 0
    bpp_shift = bpp.bit_length() - 1

    def blk_spec(j):
        def index_map(b, ix, pt):
            n = jnp.minimum(jnp.maximum(ix[b, j], 0), n_cache_blocks - 1)
            return (pt[b, lax.shift_right_logical(n, bpp_shift)], 1, j // TOP_N, 0, 0)
        return pl.BlockSpec((None, 2, None, HEAD_DIM, page_rows), index_map)

    n_blk = KV_HEADS * TOP_N
    small = (q64, ocmp, gates8, ksn, vsn, kwn, vwn, cache_win_t)
    return pl.pallas_call(
        functools.partial(_sample_attend_body, n_cache_blocks=n_cache_blocks, blocks_per_page=bpp,
                          win_skip=win_skip),
        out_shape=jax.ShapeDtypeStruct((n_seq, N_HEADS, HEAD_DIM), F32),
        grid_spec=pltpu.PrefetchScalarGridSpec(
            num_scalar_prefetch=2,
            grid=(n_seq,),
            in_specs=[row_spec(a) for a in small] + [blk_spec(j) for j in range(n_blk)],
            out_specs=pl.BlockSpec((None, N_HEADS, HEAD_DIM), lambda b, ix, pt: (b, 0, 0))),
        compiler_params=_cparams(("parallel",)),
        name="nsa_sample_attend",
    )(idx, page_table, *small, *([cache_t] * n_blk))


def _round_up(x, m):
    return -(-x // m) * m


def _prompt_layer(h, prm, cw, sw, batch, seq):
    tabs = _rope_tables(jnp.arange(seq))
    u, gs, gn, q_t, kv_t, win_t, gates_t, ks, vs_t, kw, vw_t = _proj_call(
        h, prm["w_in"], prm["norm_w"], prm["qnw"], prm["knw"], prm["gb"], tabs, batch, seq)
    y_ssm, h_re, h_im = _s5_prompt_call(u, sw, batch, seq)
    kc, vc_t = _compress_prompt_call(kv_t, cw, batch, seq)
    nch = seq // CMP_STRIDE
    ovl_t = _overlap_matrix(nch, nch - 1, seq // SLC_BLOCK).T
    o = _attn_call(q_t, kc, vc_t, ovl_t, ks, vs_t, kw, vw_t, gates_t, batch, seq)
    h_new = _outmix_call(h, y_ssm, gs, o, gn, prm["w_glu"], prm["w_out"])
    rows = lambda x_t, slots: x_t.reshape(batch, slots, KV_HEADS, HEAD_DIM, seq).transpose(0, 4, 1, 2, 3)
    return h_new, rows(kv_t, 4), rows(win_t, 2)[:, seq - min(WINDOW, seq):], h_re, h_im


def _sample_layer(h, prm, cw, sw, cache_kv, cache_win, st_re, st_im, page_table):
    n_seq = h.shape[0]
    n_phys, page_rows = cache_kv.shape[:2]
    n_pages = page_table.shape[1]
    past_len = n_pages * page_rows
    win_buf = cache_win.shape[1]
    n_pad = _round_up(n_seq, LANES)
    tabs = _rope_tables(jnp.full((n_pad,), past_len, I32))
    h_pad = jnp.pad(h, ((0, n_pad - n_seq), (0, 0)))
    u, gs, gn, q_t, kv_t, win_t, gates_t, ks, vs_t, kw, vw_t = _proj_call(
        h_pad, prm["w_in"], prm["norm_w"], prm["qnw"], prm["knw"], prm["gb"], tabs, 1, n_pad)
    u, gs, gn = u[:, :n_seq], gs[:n_seq], gn[:n_seq]
    n_state = st_re.shape[1] * st_re.shape[2]
    y_ssm, h_re, h_im = _s5_step_call(u, st_re.reshape(n_seq, n_state), st_im.reshape(n_seq, n_state), sw)
    cache_t = cache_kv.transpose(0, 2, 3, 4, 1)
    kc, vc_t = _compress_sample_call(cache_t, page_table, cw)
    ncp = past_len // CMP_STRIDE
    n_blk = -(-(past_len + 1) // SLC_BLOCK)
    nbp = _round_up(n_blk, LANES)
    ovl = _overlap_matrix(ncp, ncp - 1, n_blk, nbp)
    tri = (jnp.arange(nbp)[:, None] < jnp.arange(nbp)[None, :]).astype(BF16)
    q8 = q_t[:, :, :n_seq].transpose(2, 0, 1)
    ocmp, idx = _sample_select_call(q8, kc, vc_t, ovl, tri, past_len)
    idx = idx[:, :KV_HEADS, :TOP_N].reshape(n_seq, KV_HEADS * TOP_N)
    gates8 = gates_t[:, :3 * Q_PER_KV, :n_seq].reshape(KV_HEADS, Q_PER_KV, 3, n_seq).transpose(3, 0, 1, 2)
    gates8 = jnp.pad(gates8.reshape(n_seq, N_HEADS, 3), ((0, 0), (0, 0), (0, LANES - 3)))
    per_head = lambda a: jnp.repeat(a.transpose(1, 0, 2), Q_PER_KV, axis=1)
    new_k = lambda k: per_head(k[:, :n_seq, HEAD_DIM:])
    new_v = lambda v_t: per_head(v_t[:, 0, :HEAD_DIM, :n_seq].transpose(0, 2, 1))
    o8 = _sample_attend_call(idx, page_table, q8[:, :, HEAD_DIM:], ocmp, gates8, new_k(ks), new_v(vs_t),
                             new_k(kw), new_v(vw_t), cache_win.transpose(0, 2, 3, 4, 1), cache_t,
                             max(win_buf + 1 - WINDOW, 0))
    o = o8.reshape(n_seq, NSA_W)
    h_new = _outmix_call(h, y_ssm, gs, o, gn, prm["w_glu"], prm["w_out"])
    kv_rows = kv_t[0, :, :n_seq].T.reshape(n_seq, 1, 4, KV_HEADS, HEAD_DIM)
    win_new = win_t[0, :, :n_seq].T.reshape(n_seq, 1, 2, KV_HEADS, HEAD_DIM)
    wrows = jnp.concatenate([cache_win, win_new], axis=1)
    wrows = wrows[:, wrows.shape[1] - min(WINDOW, wrows.shape[1]):]
    state = lambda s: s.reshape(st_re.shape)
    return h_new, kv_rows, wrows, state(h_re), state(h_im)


def kernel(x_prompt, x_sample, cache_kv, cache_win, state_ssm_re, state_ssm_im, page_table, norm_w, w_in, gate_b,
           q_norm_w, k_norm_w, cmp_pe, cmp_w1, cmp_b1, cmp_w2, ssm_lam_re, ssm_lam_im, ssm_log_step, ssm_b_re,
           ssm_b_im, ssm_c_re, ssm_c_im, ssm_d, w_glu, w_out):
    p = dict(norm_w=norm_w, w_in=w_in, gate_b=gate_b, q_norm_w=q_norm_w, k_norm_w=k_norm_w, cmp_pe=cmp_pe,
             cmp_w1=cmp_w1, cmp_b1=cmp_b1, cmp_w2=cmp_w2, ssm_lam_re=ssm_lam_re, ssm_lam_im=ssm_lam_im,
             ssm_log_step=ssm_log_step, ssm_b_re=ssm_b_re, ssm_b_im=ssm_b_im, ssm_c_re=ssm_c_re,
             ssm_c_im=ssm_c_im, ssm_d=ssm_d, w_glu=w_glu, w_out=w_out)
    b_p, s_p, d_model = x_prompt.shape
    b_s, s_s, _ = x_sample.shape
    assert s_s == 1, "the sample group decodes one token per sequence"
    h_p = x_prompt.reshape(b_p * s_p, d_model)
    h_s = x_sample.reshape(b_s, d_model)
    outs_p, outs_s = [], []
    for l in range(norm_w.shape[0]):
        weights = (_prep_params(p, l), _prep_compress(p, l), _prep_s5_chunked(p, l))
        h_p, *rest_p = _prompt_layer(h_p, *weights, b_p, s_p)
        h_s, *rest_s = _sample_layer(h_s, *weights, cache_kv[l], cache_win[l], state_ssm_re[l],
                                     state_ssm_im[l], page_table)
        outs_p.append(rest_p)
        outs_s.append(rest_s)
    stack = lambda outs, i: jnp.stack([o[i] for o in outs])
    return (h_p.reshape(x_prompt.shape), h_s.reshape(x_sample.shape),
            stack(outs_p, 0), stack(outs_s, 0), stack(outs_p, 1), stack(outs_s, 1),
            stack(outs_p, 2), stack(outs_p, 3), stack(outs_s, 2), stack(outs_s, 3))
```

```python
import functools
import math

import jax
import jax.numpy as jnp
from jax import lax
from jax.experimental import pallas as pl
from jax.experimental.pallas import tpu as pltpu

F32 = jnp.float32
BF16 = jnp.bfloat16
I32 = jnp.int32

LANES = 128
SUBLANES = 8
VMEM_LIMIT_BYTES = 56 * 1024 * 1024

HEAD_DIM = 64
N_HEADS = 8
KV_HEADS = 2
Q_PER_KV = N_HEADS // KV_HEADS
SSM_W = 512
SSM_GROUP = 16
SSM_STATE = 64
NSA_W = N_HEADS * HEAD_DIM
CMP_BLOCK = 32
CMP_STRIDE = 16
CMP_HID = 2 * HEAD_DIM
SLC_BLOCK = 64
TOP_N = 16
N_LOCAL_BLOCKS = 2
WINDOW = 512
ROPE_THETA = 500000.0
ROPE_DIM = HEAD_DIM // 4
RMS_EPS = 1e-6
NEG_INF = -1e30

COL_U = 0
COL_GS = SSM_W
COL_Q = 2 * SSM_W
COL_GN = 2 * SSM_W + NSA_W
COL_KV = 2 * SSM_W + 2 * NSA_W
COL_GL = COL_KV + 6 * KV_HEADS * HEAD_DIM
IN_W_PAD = COL_GL + LANES

PROJ_TILE = 512
ATTN_TILE = 256
V_ROWS = HEAD_DIM + 16
SSM_CHUNK = 16
SSM_SLAB_GROUPS = LANES // SSM_GROUP
SSM_SLABS = SSM_W // LANES


def _cparams(sem):
    return pltpu.CompilerParams(dimension_semantics=sem, vmem_limit_bytes=VMEM_LIMIT_BYTES)


def _sigmoid(x):
    return 1.0 / (1.0 + jnp.exp(-x))


def _dot(a, b):
    return jnp.dot(a, b, preferred_element_type=F32)


def _dot_nt(a, b):
    return lax.dot_general(a, b, (((1,), (1,)), ((), ())), preferred_element_type=F32)


def _proj_body(x_ref, nw_ref, w_ref, qnw_ref, knw_ref, gb_ref, ra_ref, rb_ref, rc_ref,
               u_ref, gs_ref, gn_ref, qt_ref, kvt_ref, wint_ref, gt_ref,
               ks_ref, vst_ref, kw_ref, vwt_ref, *, tm, tv, tiles_per_seq):
    x = x_ref[...]
    ms = jnp.mean(x * x, axis=-1, keepdims=True)
    h = (x * lax.rsqrt(ms + RMS_EPS) * nw_ref[...]).astype(BF16)

    def mm(c0, c1):
        return _dot(h, w_ref[:, c0:c1])

    lane = lax.broadcasted_iota(I32, (tm, LANES), 1)
    lo = lane < HEAD_DIM
    ra = ra_ref[...]
    rb = rb_ref[...]
    rc = rc_ref[...]

    def norm_rope(s, wrow):
        s2 = s * s
        slo = jnp.sum(jnp.where(lo, s2, 0.0), axis=-1, keepdims=True)
        shi = jnp.sum(jnp.where(lo, 0.0, s2), axis=-1, keepdims=True)
        msq = jnp.where(lo, slo, shi) * (1.0 / HEAD_DIM)
        y = s * lax.rsqrt(msq + RMS_EPS) * wrow
        half = ROPE_DIM // 2
        return y * ra + pltpu.roll(y, LANES - half, 1) * rb + pltpu.roll(y, half, 1) * rc

    def hi_half(y, head):
        src = pltpu.roll(y, HEAD_DIM, 1) if head == 0 else y
        return jnp.where(lo, 0.0, src)

    zq = mm(COL_Q, COL_GN)
    zkv = mm(COL_KV, COL_GL)
    zgl = mm(COL_GL, IN_W_PAD)

    qnw = qnw_ref[...]
    scale = HEAD_DIM ** -0.5 * math.log2(math.e)
    zeros_t = jnp.zeros((HEAD_DIM, tm), F32)
    for j in range(N_HEADS // 2):
        y_t = (norm_rope(zq[:, j * LANES:(j + 1) * LANES], qnw) * scale).T
        for head in range(2):
            q_t = jnp.concatenate([zeros_t, y_t[head * HEAD_DIM:(head + 1) * HEAD_DIM]], axis=0)
            qt_ref[2 * j + head] = q_t.astype(BF16)

    kc = norm_rope(zkv[:, 0:LANES], knw_ref[0:1, :])
    vc = zkv[:, LANES:2 * LANES]
    ks = norm_rope(zkv[:, 2 * LANES:3 * LANES], knw_ref[1:2, :])
    vs = zkv[:, 3 * LANES:4 * LANES]
    kw = norm_rope(zkv[:, 4 * LANES:5 * LANES], knw_ref[2:3, :])
    vw = zkv[:, 5 * LANES:6 * LANES]
    vs_t, vw_t = vs.T, vw.T
    for i, rows_t in enumerate((kc.T, vc.T, ks.T, vs_t)):
        kvt_ref[i * LANES:(i + 1) * LANES, :] = rows_t
    for i, rows_t in enumerate((kw.T, vw_t)):
        wint_ref[i * LANES:(i + 1) * LANES, :] = rows_t

    row = lax.broadcasted_iota(I32, (tm, LANES), 0)
    pos = (pl.program_id(0) % tiles_per_seq) * tm + row
    onehot = jnp.where(lane == lax.shift_right_logical(pos, 6), 1.0, 0.0)
    ones_t = jnp.where(lax.broadcasted_iota(I32, (V_ROWS - HEAD_DIM, tm), 0) == 0, 1.0, 0.0)
    for g in range(KV_HEADS):
        ks_ref[g] = jnp.where(lo, onehot, hi_half(ks, g)).astype(BF16)
        kw_ref[g] = hi_half(kw, g).astype(BF16)
        for v_t, vt_ref in ((vs_t, vst_ref), (vw_t, vwt_ref)):
            v_aug = jnp.concatenate([v_t[g * HEAD_DIM:(g + 1) * HEAD_DIM], ones_t], axis=0).astype(BF16)
            for t in range(tm // tv):
                vt_ref[g, t] = v_aug[:, t * tv:(t + 1) * tv]

    gates_t = _sigmoid(zgl + gb_ref[...]).T
    for g in range(KV_HEADS):
        gt_ref[g] = gates_t[g * 3 * Q_PER_KV:g * 3 * Q_PER_KV + 2 * SUBLANES]

    zu = mm(COL_U, COL_GS)
    for j in range(SSM_SLABS):
        u_ref[j] = zu[:, j * LANES:(j + 1) * LANES]
    for g_ref, cols in ((gs_ref, (COL_GS, COL_Q)), (gn_ref, (COL_GN, COL_KV))):
        g = mm(*cols)
        g_ref[...] = (g * _sigmoid(g)).astype(BF16)


def _proj_call(x2d, w_pad, norm_w, qnw, knw, gb, tabs, batch, seq):
    T, D = x2d.shape
    tm = min(PROJ_TILE, seq)
    tv = min(ATTN_TILE, seq)
    assert T == batch * seq and seq % tm == 0 and tm % tv == 0 and tv % LANES == 0
    tps = seq // tm
    row_spec = lambda w: pl.BlockSpec((tm, w), lambda i: (i, 0))
    full = lambda a: pl.BlockSpec(a.shape, lambda i: (0,) * a.ndim)
    tab_spec = pl.BlockSpec((tm, LANES), lambda i: (i % tps, 0))
    head_spec = lambda n: pl.BlockSpec((n, tm, LANES), lambda i: (0, i, 0))
    head_t_spec = lambda n, rows: pl.BlockSpec((n, rows, tm), lambda i: (0, 0, i))
    cache_t_spec = lambda rows: pl.BlockSpec((None, rows, tm), lambda i: (i // tps, 0, i % tps))
    tile_t_spec = pl.BlockSpec((KV_HEADS, tm // tv, V_ROWS, tv), lambda i: (0, i, 0, 0))
    tile_t_sds = jax.ShapeDtypeStruct((KV_HEADS, T // tv, V_ROWS, tv), BF16)
    out_shape = (
        jax.ShapeDtypeStruct((SSM_SLABS, T, LANES), F32),
        jax.ShapeDtypeStruct((T, SSM_W), BF16),
        jax.ShapeDtypeStruct((T, NSA_W), BF16),
        jax.ShapeDtypeStruct((N_HEADS, LANES, T), BF16),
        jax.ShapeDtypeStruct((batch, 4 * LANES, seq), F32),
        jax.ShapeDtypeStruct((batch, 2 * LANES, seq), F32),
        jax.ShapeDtypeStruct((KV_HEADS, 2 * SUBLANES, T), F32),
        jax.ShapeDtypeStruct((KV_HEADS, T, LANES), BF16),
        tile_t_sds,
        jax.ShapeDtypeStruct((KV_HEADS, T, LANES), BF16),
        tile_t_sds,
    )
    out_specs = (head_spec(SSM_SLABS), row_spec(SSM_W), row_spec(NSA_W), head_t_spec(N_HEADS, LANES),
                 cache_t_spec(4 * LANES), cache_t_spec(2 * LANES), head_t_spec(KV_HEADS, 2 * SUBLANES),
                 head_spec(KV_HEADS), tile_t_spec, head_spec(KV_HEADS), tile_t_spec)
    return pl.pallas_call(
        functools.partial(_proj_body, tm=tm, tv=tv, tiles_per_seq=tps),
        out_shape=out_shape,
        grid=(T // tm,),
        in_specs=[row_spec(D), full(norm_w), full(w_pad), full(qnw), full(knw), full(gb),
                  tab_spec, tab_spec, tab_spec],
        out_specs=out_specs,
        compiler_params=_cparams(("parallel",)),
        name="proj",
    )(x2d, norm_w, w_pad, qnw, knw, gb, *tabs)


def _prep_params(p, l):
    w_in = p["w_in"][l]
    d_model, in_w = w_in.shape
    tile2 = lambda v: jnp.tile(v, (1, LANES // HEAD_DIM))
    return {
        "w_in": jnp.pad(w_in.astype(BF16), ((0, 0), (0, IN_W_PAD - in_w))),
        "norm_w": p["norm_w"][l].reshape(1, d_model).astype(F32),
        "qnw": tile2(p["q_norm_w"][l].reshape(1, HEAD_DIM)).astype(F32),
        "knw": tile2(p["k_norm_w"][l]).astype(F32),
        "gb": jnp.pad(p["gate_b"][l].reshape(1, -1).astype(F32), ((0, 0), (0, LANES - 3 * N_HEADS))),
        "w_glu": p["w_glu"][l].astype(BF16),
        "w_out": p["w_out"][l].astype(BF16),
    }


def _rope_tables(pos):
    half = ROPE_DIM // 2
    inv = ROPE_THETA ** (-jnp.arange(half, dtype=F32) / half)
    ang = pos.astype(F32)[:, None] * inv
    cos, sin = jnp.cos(ang), jnp.sin(ang)
    n = pos.shape[0]
    rest = HEAD_DIM - ROPE_DIM
    a = jnp.concatenate([cos, cos, jnp.ones((n, rest), F32)], axis=-1)
    b = jnp.concatenate([-sin, jnp.zeros((n, HEAD_DIM - half), F32)], axis=-1)
    c = jnp.concatenate([jnp.zeros((n, half), F32), sin, jnp.zeros((n, rest), F32)], axis=-1)
    return tuple(jnp.tile(t, (1, LANES // HEAD_DIM)) for t in (a, b, c))


def _outmix_body(x_ref, y_ref, gs_ref, o_ref, gn_ref, wg_ref, wo_ref, out_ref):
    y = jnp.concatenate([y_ref[j] for j in range(SSM_SLABS)], axis=-1)
    ab = _dot(y.astype(BF16), wg_ref[...])
    ssm = ab[:, :SSM_W] * _sigmoid(ab[:, SSM_W:]) * gs_ref[...].astype(F32)
    nsa = o_ref[...].astype(F32) * gn_ref[...].astype(F32)
    acc = _dot(ssm.astype(BF16), wo_ref[0:SSM_W, :])
    acc += _dot(nsa.astype(BF16), wo_ref[SSM_W:, :])
    out_ref[...] = x_ref[...] + acc


def _outmix_call(x2d, y_ssm, g_ssm, o_nsa, g_nsa, w_glu, w_out):
    T, D = x2d.shape
    tm = min(512, T)
    row_spec = lambda w: pl.BlockSpec((tm, w), lambda i: (i, 0))
    full = lambda a: pl.BlockSpec(a.shape, lambda i: (0,) * a.ndim)
    return pl.pallas_call(
        _outmix_body,
        out_shape=jax.ShapeDtypeStruct((T, D), F32),
        grid=(T // tm,),
        in_specs=[row_spec(D), pl.BlockSpec((SSM_SLABS, tm, LANES), lambda i: (0, i, 0)),
                  row_spec(SSM_W), row_spec(NSA_W), row_spec(NSA_W),
                  full(w_glu), full(w_out)],
        out_specs=row_spec(D),
        compiler_params=_cparams(("parallel",)),
        name="outmix",
    )(x2d, y_ssm, g_ssm, o_nsa, g_nsa, w_glu, w_out)


def _gelu_tanh(x):
    c = math.sqrt(2.0 / math.pi)
    return 0.5 * x * (1.0 + jnp.tanh(c * (x + 0.044715 * (x * x * x))))


def _compress_rows(rows_refs, pe_ref, wa_ref, wb_ref, b1_ref, w2_ref, kc_ref, vc_ref, nch):
    lane = lax.broadcasted_iota(I32, (nch, LANES), 1)
    for kvi, out_ref in ((0, kc_ref), (1, vc_ref)):
        rows_ref = rows_refs[kvi]
        pa = jnp.zeros((nch, 2 * CMP_HID), F32)
        pb = jnp.zeros((nch, 2 * CMP_HID), F32)
        for j0 in range(0, CMP_STRIDE, 2):
            xs = [rows_ref[pl.ds(j, nch, stride=CMP_STRIDE), :] for j in (j0, j0 + 1)]
            xa = jnp.concatenate([xs[i] + pe_ref[kvi, 0, j0 + i:j0 + i + 1, :] for i in range(2)], axis=-1)
            xb = jnp.concatenate([xs[i] + pe_ref[kvi, 1, j0 + i:j0 + i + 1, :] for i in range(2)], axis=-1)
            wsl = slice(j0 * LANES, (j0 + 2) * LANES)
            pa += _dot(xa.astype(BF16), wa_ref[kvi, wsl, :])
            pb += _dot(xb.astype(BF16), wb_ref[kvi, wsl, :])
        hid = _gelu_tanh(pa + pltpu.roll(pb, nch - 1, 0) + b1_ref[kvi]).astype(BF16)
        for g in range(KV_HEADS):
            o = _dot(hid, w2_ref[kvi, g])
            if kvi == 1:
                o = jnp.where(lane == HEAD_DIM, 1.0, o).T
            out_ref[g] = o.astype(BF16)


def _compress_prompt_body(kvt_ref, pe_ref, wa_ref, wb_ref, b1_ref, w2_ref, kc_ref, vc_ref,
                          krows_ref, vrows_ref, *, nch):
    for c in range(kvt_ref.shape[1] // LANES):
        cs = slice(c * LANES, (c + 1) * LANES)
        krows_ref[cs, :] = kvt_ref[0:LANES, cs].T
        vrows_ref[cs, :] = kvt_ref[LANES:2 * LANES, cs].T
    _compress_rows((krows_ref, vrows_ref), pe_ref, wa_ref, wb_ref, b1_ref, w2_ref, kc_ref, vc_ref, nch)


def _compress_prompt_call(kv_t, cw, batch, seq):
    nch = seq // CMP_STRIDE
    full = lambda a: pl.BlockSpec(a.shape, lambda b: (0,) * a.ndim)
    out_spec = pl.BlockSpec((KV_HEADS, nch, LANES), lambda b: (0, b, 0))
    out_sds = jax.ShapeDtypeStruct((KV_HEADS, batch * nch, LANES), BF16)
    out_t_spec = pl.BlockSpec((KV_HEADS, None, LANES, nch), lambda b: (0, b, 0, 0))
    out_t_sds = jax.ShapeDtypeStruct((KV_HEADS, batch, LANES, nch), BF16)
    return pl.pallas_call(
        functools.partial(_compress_prompt_body, nch=nch),
        out_shape=(out_sds, out_t_sds),
        grid=(batch,),
        in_specs=[pl.BlockSpec((None, 2 * LANES, seq), lambda b: (b, 0, 0)),
                  full(cw["pe"]), full(cw["wa"]), full(cw["wb"]), full(cw["b1"]), full(cw["w2"])],
        out_specs=(out_spec, out_t_spec),
        scratch_shapes=[pltpu.VMEM((seq, LANES), F32)] * 2,
        compiler_params=_cparams(("parallel",)),
        name="compress_prompt",
    )(kv_t, cw["pe"], cw["wa"], cw["wb"], cw["b1"], cw["w2"])


def _prep_compress(p, l):
    assert KV_HEADS == 2
    w1 = p["cmp_w1"][l].reshape(2, 2, CMP_STRIDE, HEAD_DIM, CMP_HID).astype(BF16)
    z1 = jnp.zeros_like(w1)
    wexp = jnp.stack([jnp.concatenate([w1, z1], axis=-1), jnp.concatenate([z1, w1], axis=-1)], axis=3)
    wexp = wexp.reshape(2, 2, CMP_STRIDE * LANES, KV_HEADS * CMP_HID)
    pe = p["cmp_pe"][l].reshape(2, 2, CMP_STRIDE, HEAD_DIM)
    w2 = p["cmp_w2"][l]
    zeros = jnp.zeros_like(w2[0])
    w2k = jnp.concatenate([zeros, w2[0]], axis=-1)
    w2v = jnp.concatenate([w2[1], zeros], axis=-1)
    z2 = jnp.zeros_like(w2k)
    w2e = jnp.stack([jnp.stack([jnp.concatenate([w, z2], axis=0), jnp.concatenate([z2, w], axis=0)])
                     for w in (w2k, w2v)])
    return {
        "pe": jnp.tile(pe, (1, 1, 1, KV_HEADS)).astype(F32),
        "wa": wexp[:, 0],
        "wb": wexp[:, 1],
        "b1": jnp.tile(p["cmp_b1"][l].reshape(2, 1, CMP_HID), (1, 1, KV_HEADS)).astype(F32),
        "w2": w2e.astype(BF16),
    }


def _overlap_matrix(n_tok_pad, n_tok, n_blk, n_cols=LANES):
    c_start = jnp.arange(n_tok_pad) * CMP_STRIDE
    blk = jnp.arange(n_cols)
    ov = ((c_start[:, None] < (blk[None, :] + 1) * SLC_BLOCK)
          & (c_start[:, None] + CMP_BLOCK > blk[None, :] * SLC_BLOCK)
          & (jnp.arange(n_tok_pad)[:, None] < n_tok) & (blk[None, :] < n_blk))
    return ov.astype(BF16)


def _topk_select_t(w_ref, imp_t, q0, tq):
    nb = imp_t.shape[0]
    n_i = lax.broadcasted_iota(I32, (nb, tq), 0)
    qblk = lax.shift_right_logical(q0 + lax.broadcasted_iota(I32, (nb, tq), 1), 6)
    causal = n_i <= qblk
    forced = (n_i == 0) | (n_i >= qblk - (N_LOCAL_BLOCKS - 1))
    w_ref[...] = jnp.where(causal, jnp.where(forced, jnp.inf, imp_t), -jnp.inf)
    last_blk = lax.shift_right_logical(q0 + tq - 1, 6)
    n_grp = nb // SUBLANES
    rank = [jnp.zeros((SUBLANES, tq), F32) for _ in range(n_grp)]
    grp_i = lax.broadcasted_iota(I32, (SUBLANES, tq), 0)

    def count_group(mg, rank):
        rank = list(rank)
        for mi in range(SUBLANES):
            m = mg * SUBLANES + mi
            wm = w_ref[m:m + 1, :]
            for ng in range(n_grp):
                w = w_ref[ng * SUBLANES:(ng + 1) * SUBLANES, :]
                if ng > mg:
                    beats = jnp.where(wm >= w, 1.0, 0.0)
                elif ng < mg:
                    beats = jnp.where(wm > w, 1.0, 0.0)
                else:
                    beats = jnp.where(grp_i > mi, jnp.where(wm >= w, 1.0, 0.0), jnp.where(wm > w, 1.0, 0.0))
                rank[ng] = rank[ng] + beats
        return tuple(rank)

    rank = tuple(rank)
    for mg in range(n_grp):
        rank = lax.cond(mg * SUBLANES <= last_blk, functools.partial(count_group, mg), lambda r: r, rank)
    return causal & (jnp.concatenate(rank, axis=0) < TOP_N)


def _flash_tiles_t(tiles, q_ts, ms, accs):
    def scores(i):
        k, _, mask = tiles[i]
        s = [_dot(k, q_t) for q_t in q_ts]
        return s if mask is None else [jnp.where(mask, x, NEG_INF) for x in s]

    ahead = 2
    pending = {i: scores(i) for i in range(min(ahead, len(tiles)))}
    for i, (_, v_t, _) in enumerate(tiles):
        s = pending.pop(i)
        m_new = [jnp.maximum(m, jnp.max(x, axis=0, keepdims=True)) for m, x in zip(ms, s)]
        alpha = [jnp.exp2(m - mn) for m, mn in zip(ms, m_new)]
        p = [jnp.exp2(x - mn).astype(BF16) for x, mn in zip(s, m_new)]
        pv = [_dot(v_t, x) for x in p]
        if i + ahead < len(tiles):
            pending[i + ahead] = scores(i + ahead)
        accs = [a * acc + x for a, acc, x in zip(alpha, accs, pv)]
        ms = m_new
    return tuple(ms), tuple(accs)


def _attn_body(qt_ref, kc_ref, vct_ref, ovlt_ref, ks_ref, vst_ref, kw_ref, vwt_ref, gate_ref, o_ref,
               qa_ref, ocmp_ref, w_ref, *, tq, ncp):
    R = Q_PER_KV
    qt = pl.program_id(2)
    q0 = qt * tq
    nbs = LANES // 2

    def compressed(n_c):
        c_i = lax.broadcasted_iota(I32, (n_c, tq), 0)
        qpos_c = q0 + lax.broadcasted_iota(I32, (n_c, tq), 1)
        cmask = c_i * CMP_STRIDE + (CMP_BLOCK - 1) <= qpos_c
        kc = kc_ref[0:n_c, :]
        s = [jnp.where(cmask, _dot(kc, qt_ref[r]), NEG_INF) for r in range(R)]
        e = [jnp.where(cmask, jnp.exp2(x - jnp.max(x, axis=0, keepdims=True)), 0.0) for x in s]
        l = [jnp.sum(x, axis=0, keepdims=True) for x in e]
        p = [(x * (1.0 / jnp.where(y > 0.0, y, 1.0))).astype(BF16) for x, y in zip(e, l)]
        for r in range(R):
            ocmp_ref[r] = _dot(vct_ref[:, 0:n_c], p[r])
        return sum(_dot(ovlt_ref[:, 0:n_c], x) for x in p)

    half = ncp // 2
    if half % LANES == 0:
        imp = lax.cond((q0 + tq) // CMP_STRIDE <= half, lambda: compressed(half), lambda: compressed(ncp))
    else:
        imp = compressed(ncp)

    sel = _topk_select_t(w_ref, imp[:nbs], q0, tq)
    bias = jnp.concatenate([jnp.where(sel, 0.0, NEG_INF), jnp.zeros((LANES - nbs, tq), F32)], axis=0)
    for r in range(R):
        qa_ref[r] = (qt_ref[r].astype(F32) + bias).astype(BF16)

    key_i = lax.broadcasted_iota(I32, (tq, tq), 0)
    qry_i = lax.broadcasted_iota(I32, (tq, tq), 1)
    init = (tuple(jnp.full((1, tq), -jnp.inf, F32) for _ in range(R)),
            tuple(jnp.zeros((V_ROWS, tq), F32) for _ in range(R)))

    def tiles(k_ref, vt_ref, q_ref, js, masks, state):
        ts = [(k_ref[pl.ds(pl.multiple_of(j * tq, tq), tq), :], vt_ref[j], mask) for j, mask in zip(js, masks)]
        return _flash_tiles_t(ts, [q_ref[r] for r in range(R)], *state)

    def last_tiles(k_ref, vt_ref, q_ref, n, first_mask, state):
        js = [qt - (n - 1 - t) for t in range(n)]
        masks = [first_mask] + [None] * (n - 2) + [key_i <= qry_i] if n > 1 else [key_i <= qry_i]
        return tiles(k_ref, vt_ref, q_ref, js, masks, state)

    GROUP = 4

    def slc_group(i, st):
        return tiles(ks_ref, vst_ref, qa_ref, [GROUP * i + t for t in range(GROUP)], [None] * GROUP, st)

    state = lax.fori_loop(0, qt // GROUP, slc_group, init)
    tails = [functools.partial(last_tiles, ks_ref, vst_ref, qa_ref, n, None) for n in range(1, GROUP + 1)]
    _, acc_s = lax.switch(qt % GROUP, tails, state)

    nwin = WINDOW // tq
    wins = [functools.partial(last_tiles, kw_ref, vwt_ref, qt_ref, n, (key_i > qry_i) if n == nwin + 1 else None)
            for n in range(1, nwin + 2)]
    _, acc_w = lax.switch(jnp.minimum(qt, nwin), wins, init)

    outs = []
    for r in range(R):
        a_s = acc_s[r]
        a_w = acc_w[r]
        g = lambda k: gate_ref[3 * r + k:3 * r + k + 1, :]
        head = lambda a: a[:HEAD_DIM] * (1.0 / a[HEAD_DIM:HEAD_DIM + 1, :])
        outs.append(g(0) * ocmp_ref[r][:HEAD_DIM] + g(1) * head(a_s) + g(2) * head(a_w))
    o_ref[...] = jnp.concatenate(outs, axis=0).T.astype(o_ref.dtype)


def _attn_call(q_t, kc, vc_t, ovl_t, ks, vs_t, kw, vw_t, gates_t, batch, seq):
    tq = min(ATTN_TILE, seq)
    nq = seq // tq
    ncp = kc.shape[1] // batch
    R = Q_PER_KV
    assert seq // SLC_BLOCK <= LANES // 2 and WINDOW % tq == 0
    k_spec = pl.BlockSpec((None, seq, LANES), lambda b, g, t: (g, b, 0))
    vt_spec = pl.BlockSpec((None, nq, V_ROWS, tq), lambda b, g, t: (g, b, 0, 0))
    acc = pltpu.VMEM((R, LANES, tq), F32)
    return pl.pallas_call(
        functools.partial(_attn_body, tq=tq, ncp=ncp),
        out_shape=jax.ShapeDtypeStruct((batch * seq, NSA_W), BF16),
        grid=(batch, KV_HEADS, nq),
        in_specs=[pl.BlockSpec((R, LANES, tq), lambda b, g, t: (g, 0, b * nq + t)),
                  pl.BlockSpec((None, ncp, LANES), lambda b, g, t: (g, b, 0)),
                  pl.BlockSpec((None, None, LANES, ncp), lambda b, g, t: (g, b, 0, 0)),
                  pl.BlockSpec(ovl_t.shape, lambda b, g, t: (0, 0)),
                  k_spec, vt_spec, k_spec, vt_spec,
                  pl.BlockSpec((None, 2 * SUBLANES, tq), lambda b, g, t: (g, 0, b * nq + t))],
        out_specs=pl.BlockSpec((tq, R * HEAD_DIM), lambda b, g, t: (b * nq + t, g)),
        scratch_shapes=[pltpu.VMEM((R, LANES, tq), BF16), acc, pltpu.VMEM((LANES // 2, tq), F32)],
        compiler_params=_cparams(("parallel", "parallel", "arbitrary")),
        name="nsa_prompt",
    )(q_t, kc, vc_t, ovl_t, ks, vs_t, kw, vw_t, gates_t)


def _s5_discretise(p, l):
    lr = p["ssm_lam_re"][l].astype(F32)
    li = p["ssm_lam_im"][l].astype(F32)
    dt = jnp.exp(p["ssm_log_step"][l].astype(F32))[:, None]

    def apow(t):
        mag, ang = jnp.exp(lr * dt * t), li * dt * t
        return mag * jnp.cos(ang), mag * jnp.sin(ang)

    a_re, a_im = apow(1.0)
    den = lr * lr + li * li
    nr, ni = a_re - 1.0, a_im
    f_re, f_im = (nr * lr + ni * li) / den, (ni * lr - nr * li) / den
    br, bi = p["ssm_b_re"][l].astype(F32), p["ssm_b_im"][l].astype(F32)
    bb_re = f_re[..., None] * br - f_im[..., None] * bi
    bb_im = f_re[..., None] * bi + f_im[..., None] * br
    return apow, bb_re, bb_im


def _s5_expand_body(pst_ref, kt_ref, cp_ref, wcol_ref, wst_ref, wout_ref):
    T, E, N, C = SSM_CHUNK, SSM_SLAB_GROUPS, SSM_STATE, SSM_GROUP
    row = lax.broadcasted_iota(I32, (LANES, LANES), 0)
    lane = lax.broadcasted_iota(I32, (LANES, LANES), 1)
    row_grp = row // C
    low = lane < N
    own_grp = row_grp == lane // C
    lane_n = lax.broadcasted_iota(I32, (N, LANES), 1)
    wcol_ref[T * LANES:(T + 1) * LANES, :] = jnp.zeros((LANES, LANES), BF16)
    wout_ref[0, :, LANES:2 * LANES] = jnp.zeros((2 * E * N, LANES), BF16)
    for i in range(T):
        wcol_ref[i * LANES:(i + 1) * LANES, :] = jnp.where(own_grp, kt_ref[T - 1 - i], 0.0).astype(BF16)
        x = pst_ref[T - 1 - i]
        x_sw = pltpu.roll(x, N, 1)
        halves = (jnp.where(low, x, x_sw), jnp.where(low, x_sw, x))
        for ri, x2 in enumerate(halves):
            for q in range(E // 2):
                own = row_grp == 2 * q + jnp.where(low, 0, 1)
                c0 = ri * E * N + q * LANES
                wst_ref[i * LANES:(i + 1) * LANES, c0:c0 + LANES] = jnp.where(own, x2, 0.0).astype(BF16)
    for i in range(T + 1):
        pair, half = (0, 0) if i == 0 else ((i + 1) // 2, (i - 1) % 2)
        m_t = cp_ref[i].T
        for ri in range(2):
            rows = m_t[ri * N:(ri + 1) * N]
            for g in range(E):
                r0 = ri * E * N + g * N
                wout_ref[pair, r0:r0 + N, half * LANES:(half + 1) * LANES] = (
                    jnp.where(lane_n // C == g, rows, 0.0).astype(BF16))


def _prep_s5_chunked(p, l):
    hi = lax.Precision.HIGHEST
    apow, bb_re, bb_im = _s5_discretise(p, l)
    c_re, c_im = p["ssm_c_re"][l].astype(F32), p["ssm_c_im"][l].astype(F32)
    T, J, E, N, C = SSM_CHUNK, SSM_SLABS, SSM_SLAB_GROUPS, SSM_STATE, SSM_GROUP
    pw_re, pw_im = apow(jnp.arange(T + 1, dtype=F32)[:, None, None])
    bt_re, bt_im = bb_re.transpose(0, 2, 1), bb_im.transpose(0, 2, 1)
    p_re = pw_re[:, :, None, :] * bt_re - pw_im[:, :, None, :] * bt_im
    p_im = pw_re[:, :, None, :] * bt_im + pw_im[:, :, None, :] * bt_re
    kt = (jnp.einsum("tgkn,gcn->tgkc", p_re[:T], jnp.tile(c_re, (1, E, 1)), precision=hi)
          - jnp.einsum("tgkn,gcn->tgkc", p_im[:T], jnp.tile(c_im, (1, E, 1)), precision=hi))
    cp_re = c_re * pw_re[:, :, None, :] - c_im * pw_im[:, :, None, :]
    cp_im = c_re * pw_im[:, :, None, :] + c_im * pw_re[:, :, None, :]
    slab = lambda x: jnp.moveaxis(x.reshape(x.shape[0], J, E * C, LANES), 1, 0)
    pst = slab(jnp.concatenate([p_re[:T], p_im[:T]], axis=-1))
    ktile = slab(kt)
    cpn = slab(jnp.concatenate([cp_re, -cp_im], axis=-1))
    blk = lambda n: pl.BlockSpec((None, n, LANES, LANES), lambda j: (j, 0, 0, 0))
    w_col, w_st, w_out = pl.pallas_call(
        _s5_expand_body,
        out_shape=(jax.ShapeDtypeStruct((J, (T + 1) * LANES, LANES), BF16),
                   jax.ShapeDtypeStruct((J, T * LANES, 2 * E * N), BF16),
                   jax.ShapeDtypeStruct((J, T // 2 + 1, 2 * E * N, 2 * LANES), BF16)),
        grid=(J,),
        in_specs=[blk(T), blk(T), blk(T + 1)],
        out_specs=(pl.BlockSpec((None, (T + 1) * LANES, LANES), lambda j: (j, 0, 0)),
                   pl.BlockSpec((None, T * LANES, 2 * E * N), lambda j: (j, 0, 0)),
                   pl.BlockSpec((None, T // 2 + 1, 2 * E * N, 2 * LANES), lambda j: (j, 0, 0, 0))),
        compiler_params=_cparams(("parallel",)),
        name="s5_expand",
    )(pst, ktile, cpn)
    return {
        "w_col": w_col, "w_st": w_st, "w_out": w_out,
        "a_re": pw_re[T].reshape(J, 1, E * N), "a_im": pw_im[T].reshape(J, 1, E * N),
        "a1_re": pw_re[1].reshape(J, 1, E * N), "a1_im": pw_im[1].reshape(J, 1, E * N),
        "d": p["ssm_d"][l].reshape(J, 1, LANES).astype(F32),
    }


def _s5_prompt_body(u_ref, wcol_ref, wst_ref, are_ref, aim_ref, wout_ref, d_ref,
                    y_ref, hre_ref, him_ref, xs_ref, hp_ref, *, n_chunks):
    T = SSM_CHUNK
    ns = SSM_SLAB_GROUPS * SSM_STATE
    u_pos = [u_ref[pl.ds(s, n_chunks, stride=T), :] for s in range(T)]
    ub = jnp.concatenate(u_pos, axis=-1).astype(BF16)
    xs_ref[...] = _dot(ub, wst_ref[...])
    a_re = are_ref[...]
    a_im = aim_ref[...]

    def step(c, carry):
        hr, hi = carry
        hp_ref[pl.ds(c, 1), 0:ns] = hr
        hp_ref[pl.ds(c, 1), ns:2 * ns] = hi
        xr = xs_ref[pl.ds(c, 1), 0:ns]
        xi = xs_ref[pl.ds(c, 1), ns:2 * ns]
        return a_re * hr - a_im * hi + xr, a_re * hi + a_im * hr + xi

    zero = jnp.zeros((1, ns), F32)
    hr, hi = lax.fori_loop(0, n_chunks, step, (zero, zero))
    hre_ref[...] = jnp.broadcast_to(hr, hre_ref.shape)
    him_ref[...] = jnp.broadcast_to(hi, him_ref.shape)
    hpb = hp_ref[...].astype(BF16)
    for t in range(0, T, 2):
        k_rows = (t + 2) * LANES
        w_pair = jnp.concatenate([wcol_ref[(T - 1 - t) * LANES:(T - 1 - t) * LANES + k_rows, :],
                                  wcol_ref[(T - 2 - t) * LANES:T * LANES, :]], axis=1)
        y_pair = _dot(ub[:, :k_rows], w_pair) + _dot(hpb, wout_ref[t // 2 + 1])
        for i in range(2):
            y_ref[pl.ds(t + i, n_chunks, stride=T), :] = (
                y_pair[:, i * LANES:(i + 1) * LANES] + d_ref[...] * u_pos[t + i])


def _s5_prompt_call(u_slab, sw, batch, seq):
    T, J = SSM_CHUNK, SSM_SLABS
    n_chunks = seq // T
    ns = SSM_SLAB_GROUPS * SSM_STATE
    row_spec = pl.BlockSpec((None, seq, LANES), lambda j, b: (j, b, 0))
    slab_spec = lambda a: pl.BlockSpec((None,) + a.shape[1:], lambda j, b: (j,) + (0,) * (a.ndim - 1))
    st_spec = pl.BlockSpec((None, None, SUBLANES, ns), lambda j, b: (b, j, 0, 0))
    st_sds = jax.ShapeDtypeStruct((batch, J, SUBLANES, ns), F32)
    y, hre, him = pl.pallas_call(
        functools.partial(_s5_prompt_body, n_chunks=n_chunks),
        out_shape=(jax.ShapeDtypeStruct(u_slab.shape, F32), st_sds, st_sds),
        grid=(J, batch),
        in_specs=[row_spec, slab_spec(sw["w_col"]), slab_spec(sw["w_st"]), slab_spec(sw["a_re"]),
                  slab_spec(sw["a_im"]), slab_spec(sw["w_out"]), slab_spec(sw["d"])],
        out_specs=(row_spec, st_spec, st_spec),
        scratch_shapes=[pltpu.VMEM((n_chunks, 2 * ns), F32), pltpu.VMEM((n_chunks, 2 * ns), F32)],
        compiler_params=_cparams(("parallel", "parallel")),
        name="s5_prompt",
    )(u_slab, sw["w_col"], sw["w_st"], sw["a_re"], sw["a_im"], sw["w_out"], sw["d"])
    n_groups = J * SSM_SLAB_GROUPS
    state = lambda h: h[:, :, 0, :].reshape(batch, n_groups, SSM_STATE)
    return y, state(hre), state(him)


def _s5_step_body(u_ref, wx_ref, are_ref, aim_ref, h0re_ref, h0im_ref, wy_ref, d_ref,
                  y_ref, hre_ref, him_ref):
    ns = SSM_SLAB_GROUPS * SSM_STATE
    for j in range(SSM_SLABS):
        sl = slice(j * ns, (j + 1) * ns)
        u = u_ref[j]
        x = _dot(u.astype(BF16), wx_ref[j])
        a_re, a_im = are_ref[j], aim_ref[j]
        h0r, h0i = h0re_ref[:, sl], h0im_ref[:, sl]
        hr = a_re * h0r - a_im * h0i + x[:, :ns]
        hi = a_re * h0i + a_im * h0r + x[:, ns:]
        hre_ref[:, sl] = hr
        him_ref[:, sl] = hi
        y_ref[j] = _dot(jnp.concatenate([hr, hi], axis=-1).astype(BF16), wy_ref[j]) + d_ref[j] * u


def _s5_step_call(u_slab, h0_re, h0_im, sw):
    J, n_tok, _ = u_slab.shape
    T = SSM_CHUNK
    full = lambda a: pl.BlockSpec(a.shape, lambda i: (0,) * a.ndim)
    wx_spec = pl.BlockSpec((J, LANES, sw["w_st"].shape[2]), lambda i: (0, T - 1, 0))
    wy_spec = pl.BlockSpec((J, None, sw["w_out"].shape[2], LANES), lambda i: (0, 0, 0, 0))
    st_sds = jax.ShapeDtypeStruct(h0_re.shape, F32)
    return pl.pallas_call(
        _s5_step_body,
        out_shape=(jax.ShapeDtypeStruct(u_slab.shape, F32), st_sds, st_sds),
        grid=(1,),
        in_specs=[full(u_slab), wx_spec, full(sw["a1_re"]), full(sw["a1_im"]), full(h0_re), full(h0_im),
                  wy_spec, full(sw["d"])],
        out_specs=(full(u_slab), full(h0_re), full(h0_re)),
        compiler_params=_cparams(("arbitrary",)),
        name="s5_step",
    )(u_slab, sw["w_st"], sw["a1_re"], sw["a1_im"], h0_re, h0_im, sw["w_out"], sw["d"])


def _compress_sample_body(pt_ref, *refs, n_pages, page_rows, nch):
    del pt_ref
    page_refs = refs[:n_pages]
    pe_ref, wa_ref, wb_ref, b1_ref, w2_ref, kc_ref, vc_ref = refs[n_pages:n_pages + 7]
    buffers = refs[n_pages + 7:]
    b = pl.program_id(0)

    @pl.when(b == 0)
    def _():
        for buf in buffers[2:]:
            buf[...] = jnp.zeros(buf.shape, F32)

    def step(stage, done):
        for i, page_ref in enumerate(page_refs):
            for rows_ref, slot in zip(stage, range(2)):
                rows_ref[i * page_rows:(i + 1) * page_rows, :] = page_ref[slot].reshape(LANES, page_rows).T
        _compress_rows(done, pe_ref, wa_ref, wb_ref, b1_ref, w2_ref, kc_ref, vc_ref, nch)

    @pl.when(b % 2 == 0)
    def _():
        step(buffers[:2], buffers[2:])

    @pl.when(b % 2 == 1)
    def _():
        step(buffers[2:], buffers[:2])


def _compress_sample_call(cache, page_table, cw):
    n_seq, n_pages = page_table.shape
    page_rows = cache.shape[-1]
    assert page_rows == LANES
    nch = n_pages * page_rows // CMP_STRIDE
    staged = lambda b: jnp.minimum(b, n_seq - 1)
    page_spec = lambda i: pl.BlockSpec((None, 2, KV_HEADS, HEAD_DIM, page_rows),
                                       lambda b, pt: (pt[staged(b), i], 0, 0, 0, 0))
    full = lambda a: pl.BlockSpec(a.shape, lambda b, pt: (0,) * a.ndim)
    out_of = lambda b: jnp.maximum(b - 1, 0)
    out_spec = pl.BlockSpec((KV_HEADS, nch, LANES), lambda b, pt: (0, out_of(b), 0))
    out_sds = jax.ShapeDtypeStruct((KV_HEADS, n_seq * nch, LANES), BF16)
    out_t_spec = pl.BlockSpec((KV_HEADS, None, LANES, nch), lambda b, pt: (0, out_of(b), 0, 0))
    out_t_sds = jax.ShapeDtypeStruct((KV_HEADS, n_seq, LANES, nch), BF16)
    weights = (cw["pe"], cw["wa"], cw["wb"], cw["b1"], cw["w2"])
    return pl.pallas_call(
        functools.partial(_compress_sample_body, n_pages=n_pages, page_rows=page_rows, nch=nch),
        out_shape=(out_sds, out_t_sds),
        grid_spec=pltpu.PrefetchScalarGridSpec(
            num_scalar_prefetch=1,
            grid=(n_seq + 1,),
            in_specs=[page_spec(i) for i in range(n_pages)] + [full(a) for a in weights],
            out_specs=(out_spec, out_t_spec),
            scratch_shapes=[pltpu.VMEM((n_pages * page_rows, LANES), F32)] * 4),
        compiler_params=_cparams(("arbitrary",)),
        name="compress_sample",
    )(page_table, *([cache] * n_pages), *weights)


def _group_rows(x0, x1):
    row = lax.broadcasted_iota(I32, x0.shape, 0)
    return jnp.where(row < Q_PER_KV, x0, x1)


def _sample_select_body(q_ref, kc_ref, vct_ref, ovl_ref, tri_ref, ocmp_ref, idx_ref, *, ncp, qpos, nbp):
    q8 = q_ref[...]
    c_i = lax.broadcasted_iota(I32, (N_HEADS, ncp), 1)
    cmask = c_i * CMP_STRIDE + (CMP_BLOCK - 1) <= qpos
    s = _group_rows(_dot_nt(q8, kc_ref[0]), _dot_nt(q8, kc_ref[1]))
    s = jnp.where(cmask, s, NEG_INF)
    e = jnp.where(cmask, jnp.exp2(s - jnp.max(s, axis=-1, keepdims=True)), 0.0)
    l = jnp.sum(e, axis=-1, keepdims=True)
    p = (e / jnp.where(l > 0.0, l, 1.0)).astype(BF16)
    ocmp_ref[...] = _group_rows(_dot_nt(p, vct_ref[0]), _dot_nt(p, vct_ref[1]))
    imp8 = _dot(p, ovl_ref[...])

    n_row = lax.broadcasted_iota(I32, (1, nbp), 1)
    qblk = qpos // SLC_BLOCK
    causal = n_row <= qblk
    forced = (n_row == 0) | (n_row >= qblk - (N_LOCAL_BLOCKS - 1))
    m_i = lax.broadcasted_iota(I32, (nbp, nbp), 0)
    n_i = lax.broadcasted_iota(I32, (nbp, nbp), 1)
    lane = lax.broadcasted_iota(I32, (1, LANES), 1)
    idx_rows = []
    for g in range(KV_HEADS):
        imp = jnp.sum(imp8[g * Q_PER_KV:(g + 1) * Q_PER_KV], axis=0, keepdims=True)
        w = jnp.where(causal, jnp.where(forced, jnp.inf, imp), -jnp.inf)
        w_sq = jnp.broadcast_to(w, (nbp, nbp))
        w_col = w_sq.T
        beats = jnp.where(n_i > m_i, jnp.where(w_col >= w_sq, 1.0, 0.0), jnp.where(w_col > w_sq, 1.0, 0.0))
        rank = jnp.sum(beats, axis=0, keepdims=True)
        sel = causal & (rank < TOP_N)
        self_f = jnp.where(sel, 1.0, 0.0)
        before = _dot(self_f.astype(BF16), tri_ref[...])
        idx = jnp.full((1, LANES), -1, I32)
        for k in range(TOP_N):
            hit = sel & (before == float(k))
            val = jnp.sum(jnp.where(hit, n_row.astype(F32) + 1.0, 0.0), axis=-1, keepdims=True) - 1.0
            idx = jnp.where(lane == k, val.astype(I32), idx)
        idx_rows.append(idx)
    idx_ref[...] = jnp.concatenate(idx_rows + [jnp.full((SUBLANES - KV_HEADS, LANES), -1, I32)], axis=0)


def _sample_select_call(q8, kc, vc_t, ovl, tri, qpos):
    n_seq = q8.shape[0]
    ncp = kc.shape[1] // n_seq
    nbp = ovl.shape[1]
    cmp_spec = pl.BlockSpec((KV_HEADS, ncp, LANES), lambda b: (0, b, 0))
    row_spec = pl.BlockSpec((None, N_HEADS, LANES), lambda b: (b, 0, 0))
    full = lambda a: pl.BlockSpec(a.shape, lambda b: (0,) * a.ndim)
    return pl.pallas_call(
        functools.partial(_sample_select_body, ncp=ncp, qpos=qpos, nbp=nbp),
        out_shape=(jax.ShapeDtypeStruct((n_seq, N_HEADS, LANES), F32),
                   jax.ShapeDtypeStruct((n_seq, SUBLANES, LANES), I32)),
        grid=(n_seq,),
        in_specs=[row_spec, cmp_spec, pl.BlockSpec((KV_HEADS, None, LANES, ncp), lambda b: (0, b, 0, 0)),
                  full(ovl), full(tri)],
        out_specs=(row_spec, pl.BlockSpec((None, SUBLANES, LANES), lambda b: (b, 0, 0))),
        compiler_params=_cparams(("parallel",)),
        name="nsa_sample_select",
    )(q8, kc, vc_t, ovl, tri)


def _sample_attend_body(idx_ref, pt_ref, q_ref, ocmp_ref, gate_ref, ksn_ref, vsn_ref, kwn_ref, vwn_ref,
                        win_ref, *refs, n_cache_blocks, blocks_per_page, win_skip):
    del pt_ref
    n_blk = KV_HEADS * TOP_N
    kv_refs, o_ref = refs[:n_blk], refs[n_blk]
    b = pl.program_id(0)
    q = q_ref[...]
    qf = q.astype(F32)
    row_g = (lax.broadcasted_iota(I32, (N_HEADS, 1), 0) >= Q_PER_KV).astype(I32)

    def attend(s_list, v_list, kn_ref, vn_ref):
        s_self = jnp.sum(qf * kn_ref[...].astype(F32), axis=-1, keepdims=True)
        m = s_self
        for s in s_list:
            m = jnp.maximum(m, jnp.max(s, axis=-1, keepdims=True))
        p_self = jnp.exp2(s_self - m)
        l = p_self
        acc = p_self.astype(BF16).astype(F32) * vn_ref[...].astype(F32)
        for s, v in zip(s_list, v_list):
            p = jnp.exp2(s - m)
            l = l + jnp.sum(p, axis=-1, keepdims=True)
            acc = acc + _dot_nt(p.astype(BF16), v().astype(BF16))
        return acc / l

    s_list, v_list = [], []
    for j in range(n_blk):
        s = _dot(q, kv_refs[j][0].astype(BF16))
        col = lax.broadcasted_iota(I32, s.shape, 1)
        n = idx_ref[b, j]
        first = (n & (blocks_per_page - 1)) * SLC_BLOCK
        ok = ((row_g == j // TOP_N) & (col >= first) & (col < first + SLC_BLOCK)
              & (n >= 0) & (n < n_cache_blocks))
        s_list.append(jnp.where(ok, s, NEG_INF))
        v_list.append(lambda j=j: kv_refs[j][1])
    o_slc = attend(s_list, v_list, ksn_ref, vsn_ref)

    s_list, v_list = [], []
    for g in range(KV_HEADS):
        s = _dot(q, win_ref[0, g].astype(BF16))
        col = lax.broadcasted_iota(I32, s.shape, 1)
        s_list.append(jnp.where((row_g == g) & (col >= win_skip), s, NEG_INF))
        v_list.append(lambda g=g: win_ref[1, g])
    o_win = attend(s_list, v_list, kwn_ref, vwn_ref)

    gates = gate_ref[...]
    o_ref[...] = gates[:, 0:1] * ocmp_ref[:, 0:HEAD_DIM] + gates[:, 1:2] * o_slc + gates[:, 2:3] * o_win


def _sample_attend_call(idx, page_table, q64, ocmp, gates8, ksn, vsn, kwn, vwn, cache_win_t, cache_t, win_skip):
    n_seq, n_pages = page_table.shape
    page_rows = cache_t.shape[-1]
    bpp = page_rows // SLC_BLOCK
    n_cache_blocks = n_pages * bpp
    row_spec = lambda a: pl.BlockSpec((None,) + a.shape[1:], lambda b, ix, pt: (b,) + (0,) * (a.ndim - 1))

    assert bpp & (bpp - 1) == 0
    bpp_shift = bpp.bit_length() - 1

    def blk_spec(j):
        def index_map(b, ix, pt):
            n = jnp.minimum(jnp.maximum(ix[b, j], 0), n_cache_blocks - 1)
            return (pt[b, lax.shift_right_logical(n, bpp_shift)], 1, j // TOP_N, 0, 0)
        return pl.BlockSpec((None, 2, None, HEAD_DIM, page_rows), index_map)

    n_blk = KV_HEADS * TOP_N
    small = (q64, ocmp, gates8, ksn, vsn, kwn, vwn, cache_win_t)
    return pl.pallas_call(
        functools.partial(_sample_attend_body, n_cache_blocks=n_cache_blocks, blocks_per_page=bpp,
                          win_skip=win_skip),
        out_shape=jax.ShapeDtypeStruct((n_seq, N_HEADS, HEAD_DIM), F32),
        grid_spec=pltpu.PrefetchScalarGridSpec(
            num_scalar_prefetch=2,
            grid=(n_seq,),
            in_specs=[row_spec(a) for a in small] + [blk_spec(j) for j in range(n_blk)],
            out_specs=pl.BlockSpec((None, N_HEADS, HEAD_DIM), lambda b, ix, pt: (b, 0, 0))),
        compiler_params=_cparams(("parallel",)),
        name="nsa_sample_attend",
    )(idx, page_table, *small, *([cache_t] * n_blk))


def _round_up(x, m):
    return -(-x // m) * m


def _prompt_layer(h, prm, cw, sw, batch, seq):
    tabs = _rope_tables(jnp.arange(seq))
    u, gs, gn, q_t, kv_t, win_t, gates_t, ks, vs_t, kw, vw_t = _proj_call(
        h, prm["w_in"], prm["norm_w"], prm["qnw"], prm["knw"], prm["gb"], tabs, batch, seq)
    y_ssm, h_re, h_im = _s5_prompt_call(u, sw, batch, seq)
    kc, vc_t = _compress_prompt_call(kv_t, cw, batch, seq)
    nch = seq // CMP_STRIDE
    ovl_t = _overlap_matrix(nch, nch - 1, seq // SLC_BLOCK).T
    o = _attn_call(q_t, kc, vc_t, ovl_t, ks, vs_t, kw, vw_t, gates_t, batch, seq)
    h_new = _outmix_call(h, y_ssm, gs, o, gn, prm["w_glu"], prm["w_out"])
    rows = lambda x_t, slots: x_t.reshape(batch, slots, KV_HEADS, HEAD_DIM, seq).transpose(0, 4, 1, 2, 3)
    return h_new, rows(kv_t, 4), rows(win_t, 2)[:, seq - min(WINDOW, seq):], h_re, h_im


def _sample_layer(h, prm, cw, sw, cache_kv, cache_win, st_re, st_im, page_table):
    n_seq = h.shape[0]
    n_phys, page_rows = cache_kv.shape[:2]
    n_pages = page_table.shape[1]
    past_len = n_pages * page_rows
    win_buf = cache_win.shape[1]
    n_pad = _round_up(n_seq, LANES)
    tabs = _rope_tables(jnp.full((n_pad,), past_len, I32))
    h_pad = jnp.pad(h, ((0, n_pad - n_seq), (0, 0)))
    u, gs, gn, q_t, kv_t, win_t, gates_t, ks, vs_t, kw, vw_t = _proj_call(
        h_pad, prm["w_in"], prm["norm_w"], prm["qnw"], prm["knw"], prm["gb"], tabs, 1, n_pad)
    u, gs, gn = u[:, :n_seq], gs[:n_seq], gn[:n_seq]
    n_state = st_re.shape[1] * st_re.shape[2]
    y_ssm, h_re, h_im = _s5_step_call(u, st_re.reshape(n_seq, n_state), st_im.reshape(n_seq, n_state), sw)
    cache_t = cache_kv.transpose(0, 2, 3, 4, 1)
    kc, vc_t = _compress_sample_call(cache_t, page_table, cw)
    ncp = past_len // CMP_STRIDE
    n_blk = -(-(past_len + 1) // SLC_BLOCK)
    nbp = _round_up(n_blk, LANES)
    ovl = _overlap_matrix(ncp, ncp - 1, n_blk, nbp)
    tri = (jnp.arange(nbp)[:, None] < jnp.arange(nbp)[None, :]).astype(BF16)
    q8 = q_t[:, :, :n_seq].transpose(2, 0, 1)
    ocmp, idx = _sample_select_call(q8, kc, vc_t, ovl, tri, past_len)
    idx = idx[:, :KV_HEADS, :TOP_N].reshape(n_seq, KV_HEADS * TOP_N)
    gates8 = gates_t[:, :3 * Q_PER_KV, :n_seq].reshape(KV_HEADS, Q_PER_KV, 3, n_seq).transpose(3, 0, 1, 2)
    gates8 = jnp.pad(gates8.reshape(n_seq, N_HEADS, 3), ((0, 0), (0, 0), (0, LANES - 3)))
    per_head = lambda a: jnp.repeat(a.transpose(1, 0, 2), Q_PER_KV, axis=1)
    new_k = lambda k: per_head(k[:, :n_seq, HEAD_DIM:])
    new_v = lambda v_t: per_head(v_t[:, 0, :HEAD_DIM, :n_seq].transpose(0, 2, 1))
    o8 = _sample_attend_call(idx, page_table, q8[:, :, HEAD_DIM:], ocmp, gates8, new_k(ks), new_v(vs_t),
                             new_k(kw), new_v(vw_t), cache_win.transpose(0, 2, 3, 4, 1), cache_t,
                             max(win_buf + 1 - WINDOW, 0))
    o = o8.reshape(n_seq, NSA_W)
    h_new = _outmix_call(h, y_ssm, gs, o, gn, prm["w_glu"], prm["w_out"])
    kv_rows = kv_t[0, :, :n_seq].T.reshape(n_seq, 1, 4, KV_HEADS, HEAD_DIM)
    win_new = win_t[0, :, :n_seq].T.reshape(n_seq, 1, 2, KV_HEADS, HEAD_DIM)
    wrows = jnp.concatenate([cache_win, win_new], axis=1)
    wrows = wrows[:, wrows.shape[1] - min(WINDOW, wrows.shape[1]):]
    state = lambda s: s.reshape(st_re.shape)
    return h_new, kv_rows, wrows, state(h_re), state(h_im)


def kernel(x_prompt, x_sample, cache_kv, cache_win, state_ssm_re, state_ssm_im, page_table, norm_w, w_in, gate_b,
           q_norm_w, k_norm_w, cmp_pe, cmp_w1, cmp_b1, cmp_w2, ssm_lam_re, ssm_lam_im, ssm_log_step, ssm_b_re,
           ssm_b_im, ssm_c_re, ssm_c_im, ssm_d, w_glu, w_out):
    p = dict(norm_w=norm_w, w_in=w_in, gate_b=gate_b, q_norm_w=q_norm_w, k_norm_w=k_norm_w, cmp_pe=cmp_pe,
             cmp_w1=cmp_w1, cmp_b1=cmp_b1, cmp_w2=cmp_w2, ssm_lam_re=ssm_lam_re, ssm_lam_im=ssm_lam_im,
             ssm_log_step=ssm_log_step, ssm_b_re=ssm_b_re, ssm_b_im=ssm_b_im, ssm_c_re=ssm_c_re,
             ssm_c_im=ssm_c_im, ssm_d=ssm_d, w_glu=w_glu, w_out=w_out)
    b_p, s_p, d_model = x_prompt.shape
    b_s, s_s, _ = x_sample.shape
    assert s_s == 1, "the sample group decodes one token per sequence"
    h_p = x_prompt.reshape(b_p * s_p, d_model)
    h_s = x_sample.reshape(b_s, d_model)
    outs_p, outs_s = [], []
    for l in range(norm_w.shape[0]):
        weights = (_prep_params(p, l), _prep_compress(p, l), _prep_s5_chunked(p, l))
        h_p, *rest_p = _prompt_layer(h_p, *weights, b_p, s_p)
        h_s, *rest_s = _sample_layer(h_s, *weights, cache_kv[l], cache_win[l], state_ssm_re[l],
                                     state_ssm_im[l], page_table)
        outs_p.append(rest_p)
        outs_s.append(rest_s)
    stack = lambda outs, i: jnp.stack([o[i] for o in outs])
    return (h_p.reshape(x_prompt.shape), h_s.reshape(x_sample.shape),
            stack(outs_p, 0), stack(outs_s, 0), stack(outs_p, 1), stack(outs_s, 1),
            stack(outs_p, 2), stack(outs_p, 3), stack(outs_s, 2), stack(outs_s, 3))
```

```python
import functools
import math

import jax
import jax.numpy as jnp
from jax import lax
from jax.experimental import pallas as pl
from jax.experimental.pallas import tpu as pltpu

F32 = jnp.float32
BF16 = jnp.bfloat16
I32 = jnp.int32

LANES = 128
SUBLANES = 8
VMEM_LIMIT_BYTES = 56 * 1024 * 1024

HEAD_DIM = 64
N_HEADS = 8
KV_HEADS = 2
Q_PER_KV = N_HEADS // KV_HEADS
SSM_W = 512
SSM_GROUP = 16
SSM_STATE = 64
NSA_W = N_HEADS * HEAD_DIM
CMP_BLOCK = 32
CMP_STRIDE = 16
CMP_HID = 2 * HEAD_DIM
SLC_BLOCK = 64
SLC_SHIFT = SLC_BLOCK.bit_length() - 1
TOP_N = 16
N_LOCAL_BLOCKS = 2
WINDOW = 512
ROPE_THETA = 500000.0
ROPE_DIM = HEAD_DIM // 4
RMS_EPS = 1e-6
NEG_INF = -1e30

COL_U = 0
COL_GS = SSM_W
COL_Q = 2 * SSM_W
COL_GN = 2 * SSM_W + NSA_W
COL_KV = 2 * SSM_W + 2 * NSA_W
COL_GL = COL_KV + 6 * KV_HEADS * HEAD_DIM
IN_W_PAD = COL_GL + LANES

PROJ_TILE = 512
ATTN_TILE = 256
V_ROWS = HEAD_DIM + 16
SSM_CHUNK = 16
SSM_SLAB_GROUPS = LANES // SSM_GROUP
SSM_SLABS = SSM_W // LANES


def _cparams(sem):
    return pltpu.CompilerParams(dimension_semantics=sem, vmem_limit_bytes=VMEM_LIMIT_BYTES)


def _sigmoid(x):
    return 1.0 / (1.0 + jnp.exp(-x))


def _dot(a, b):
    return jnp.dot(a, b, preferred_element_type=F32)


def _dot_nt(a, b):
    return lax.dot_general(a, b, (((1,), (1,)), ((), ())), preferred_element_type=F32)


def _proj_body(x_ref, nw_ref, w_ref, qnw_ref, knw_ref, gb_ref, ra_ref, rb_ref, rc_ref,
               u_ref, gs_ref, gn_ref, qt_ref, kvt_ref, wint_ref, gt_ref,
               ks_ref, vst_ref, kw_ref, vwt_ref, *, tm, tv, tiles_per_seq):
    x = x_ref[...]
    ms = jnp.mean(x * x, axis=-1, keepdims=True)
    h = (x * lax.rsqrt(ms + RMS_EPS) * nw_ref[...]).astype(BF16)

    def mm(c0, c1):
        return _dot(h, w_ref[:, c0:c1])

    lane = lax.broadcasted_iota(I32, (tm, LANES), 1)
    lo = lane < HEAD_DIM
    ra = ra_ref[...]
    rb = rb_ref[...]
    rc = rc_ref[...]

    def norm_rope(s, wrow):
        s2 = s * s
        slo = jnp.sum(jnp.where(lo, s2, 0.0), axis=-1, keepdims=True)
        shi = jnp.sum(jnp.where(lo, 0.0, s2), axis=-1, keepdims=True)
        msq = jnp.where(lo, slo, shi) * (1.0 / HEAD_DIM)
        y = s * lax.rsqrt(msq + RMS_EPS) * wrow
        half = ROPE_DIM // 2
        return y * ra + pltpu.roll(y, LANES - half, 1) * rb + pltpu.roll(y, half, 1) * rc

    def hi_half(y, head):
        src = pltpu.roll(y, HEAD_DIM, 1) if head == 0 else y
        return jnp.where(lo, 0.0, src)

    zq = mm(COL_Q, COL_GN)
    zkv = mm(COL_KV, COL_GL)
    zgl = mm(COL_GL, IN_W_PAD)

    qnw = qnw_ref[...]
    scale = HEAD_DIM ** -0.5 * math.log2(math.e)
    zeros_t = jnp.zeros((HEAD_DIM, tm), F32)
    for j in range(N_HEADS // 2):
        y_t = (norm_rope(zq[:, j * LANES:(j + 1) * LANES], qnw) * scale).T
        for head in range(2):
            q_t = jnp.concatenate([zeros_t, y_t[head * HEAD_DIM:(head + 1) * HEAD_DIM]], axis=0)
            qt_ref[2 * j + head] = q_t.astype(BF16)

    kc = norm_rope(zkv[:, 0:LANES], knw_ref[0:1, :])
    vc = zkv[:, LANES:2 * LANES]
    ks = norm_rope(zkv[:, 2 * LANES:3 * LANES], knw_ref[1:2, :])
    vs = zkv[:, 3 * LANES:4 * LANES]
    kw = norm_rope(zkv[:, 4 * LANES:5 * LANES], knw_ref[2:3, :])
    vw = zkv[:, 5 * LANES:6 * LANES]
    vs_t, vw_t = vs.T, vw.T
    for i, rows_t in enumerate((kc.T, vc.T, ks.T, vs_t)):
        kvt_ref[i * LANES:(i + 1) * LANES, :] = rows_t
    for i, rows_t in enumerate((kw.T, vw_t)):
        wint_ref[i * LANES:(i + 1) * LANES, :] = rows_t

    row = lax.broadcasted_iota(I32, (tm, LANES), 0)
    pos = (pl.program_id(0) % tiles_per_seq) * tm + row
    onehot = jnp.where(lane == lax.shift_right_logical(pos, SLC_SHIFT), 1.0, 0.0)
    ones_t = jnp.where(lax.broadcasted_iota(I32, (V_ROWS - HEAD_DIM, tm), 0) == 0, 1.0, 0.0)
    for g in range(KV_HEADS):
        ks_ref[g] = jnp.where(lo, onehot, hi_half(ks, g)).astype(BF16)
        kw_ref[g] = hi_half(kw, g).astype(BF16)
        for v_t, vt_ref in ((vs_t, vst_ref), (vw_t, vwt_ref)):
            v_aug = jnp.concatenate([v_t[g * HEAD_DIM:(g + 1) * HEAD_DIM], ones_t], axis=0).astype(BF16)
            for t in range(tm // tv):
                vt_ref[g, t] = v_aug[:, t * tv:(t + 1) * tv]

    gates_t = _sigmoid(zgl + gb_ref[...]).T
    for g in range(KV_HEADS):
        gt_ref[g] = gates_t[g * 3 * Q_PER_KV:g * 3 * Q_PER_KV + 2 * SUBLANES]

    zu = mm(COL_U, COL_GS)
    for j in range(SSM_SLABS):
        u_ref[j] = zu[:, j * LANES:(j + 1) * LANES]
    for g_ref, cols in ((gs_ref, (COL_GS, COL_Q)), (gn_ref, (COL_GN, COL_KV))):
        g = mm(*cols)
        g_ref[...] = (g * _sigmoid(g)).astype(BF16)


def _proj_call(x2d, w_pad, norm_w, qnw, knw, gb, tabs, batch, seq):
    T, D = x2d.shape
    tm = min(PROJ_TILE, seq)
    tv = min(ATTN_TILE, seq)
    assert T == batch * seq and seq % tm == 0 and tm % tv == 0 and tv % LANES == 0
    tps = seq // tm
    row_spec = lambda w: pl.BlockSpec((tm, w), lambda i: (i, 0))
    full = lambda a: pl.BlockSpec(a.shape, lambda i: (0,) * a.ndim)
    tab_spec = pl.BlockSpec((tm, LANES), lambda i: (i % tps, 0))
    head_spec = lambda n: pl.BlockSpec((n, tm, LANES), lambda i: (0, i, 0))
    head_t_spec = lambda n, rows: pl.BlockSpec((n, rows, tm), lambda i: (0, 0, i))
    cache_t_spec = lambda rows: pl.BlockSpec((None, rows, tm), lambda i: (i // tps, 0, i % tps))
    tile_t_spec = pl.BlockSpec((KV_HEADS, tm // tv, V_ROWS, tv), lambda i: (0, i, 0, 0))
    tile_t_sds = jax.ShapeDtypeStruct((KV_HEADS, T // tv, V_ROWS, tv), BF16)
    out_shape = (
        jax.ShapeDtypeStruct((SSM_SLABS, T, LANES), F32),
        jax.ShapeDtypeStruct((T, SSM_W), BF16),
        jax.ShapeDtypeStruct((T, NSA_W), BF16),
        jax.ShapeDtypeStruct((N_HEADS, LANES, T), BF16),
        jax.ShapeDtypeStruct((batch, 4 * LANES, seq), F32),
        jax.ShapeDtypeStruct((batch, 2 * LANES, seq), F32),
        jax.ShapeDtypeStruct((KV_HEADS, 2 * SUBLANES, T), F32),
        jax.ShapeDtypeStruct((KV_HEADS, T, LANES), BF16),
        tile_t_sds,
        jax.ShapeDtypeStruct((KV_HEADS, T, LANES), BF16),
        tile_t_sds,
    )
    out_specs = (head_spec(SSM_SLABS), row_spec(SSM_W), row_spec(NSA_W), head_t_spec(N_HEADS, LANES),
                 cache_t_spec(4 * LANES), cache_t_spec(2 * LANES), head_t_spec(KV_HEADS, 2 * SUBLANES),
                 head_spec(KV_HEADS), tile_t_spec, head_spec(KV_HEADS), tile_t_spec)
    return pl.pallas_call(
        functools.partial(_proj_body, tm=tm, tv=tv, tiles_per_seq=tps),
        out_shape=out_shape,
        grid=(T // tm,),
        in_specs=[row_spec(D), full(norm_w), full(w_pad), full(qnw), full(knw), full(gb),
                  tab_spec, tab_spec, tab_spec],
        out_specs=out_specs,
        compiler_params=_cparams(("parallel",)),
        name="proj",
    )(x2d, norm_w, w_pad, qnw, knw, gb, *tabs)


def _prep_params(p, l):
    w_in = p["w_in"][l]
    d_model, in_w = w_in.shape
    tile2 = lambda v: jnp.tile(v, (1, LANES // HEAD_DIM))
    return {
        "w_in": jnp.pad(w_in.astype(BF16), ((0, 0), (0, IN_W_PAD - in_w))),
        "norm_w": p["norm_w"][l].reshape(1, d_model).astype(F32),
        "qnw": tile2(p["q_norm_w"][l].reshape(1, HEAD_DIM)).astype(F32),
        "knw": tile2(p["k_norm_w"][l]).astype(F32),
        "gb": jnp.pad(p["gate_b"][l].reshape(1, -1).astype(F32), ((0, 0), (0, LANES - 3 * N_HEADS))),
        "w_glu": p["w_glu"][l].astype(BF16),
        "w_out": p["w_out"][l].astype(BF16),
    }


def _rope_tables(pos):
    half = ROPE_DIM // 2
    inv = ROPE_THETA ** (-jnp.arange(half, dtype=F32) / half)
    ang = pos.astype(F32)[:, None] * inv
    cos, sin = jnp.cos(ang), jnp.sin(ang)
    n = pos.shape[0]
    rest = HEAD_DIM - ROPE_DIM
    a = jnp.concatenate([cos, cos, jnp.ones((n, rest), F32)], axis=-1)
    b = jnp.concatenate([-sin, jnp.zeros((n, HEAD_DIM - half), F32)], axis=-1)
    c = jnp.concatenate([jnp.zeros((n, half), F32), sin, jnp.zeros((n, rest), F32)], axis=-1)
    return tuple(jnp.tile(t, (1, LANES // HEAD_DIM)) for t in (a, b, c))


def _outmix_body(x_ref, y_ref, gs_ref, o_ref, gn_ref, wg_ref, wo_ref, out_ref):
    y = jnp.concatenate([y_ref[j] for j in range(SSM_SLABS)], axis=-1)
    ab = _dot(y.astype(BF16), wg_ref[...])
    ssm = ab[:, :SSM_W] * _sigmoid(ab[:, SSM_W:]) * gs_ref[...].astype(F32)
    nsa = o_ref[...].astype(F32) * gn_ref[...].astype(F32)
    acc = _dot(ssm.astype(BF16), wo_ref[0:SSM_W, :])
    acc += _dot(nsa.astype(BF16), wo_ref[SSM_W:, :])
    out_ref[...] = x_ref[...] + acc


def _outmix_call(x2d, y_ssm, g_ssm, o_nsa, g_nsa, w_glu, w_out):
    T, D = x2d.shape
    tm = min(PROJ_TILE, T)
    row_spec = lambda w: pl.BlockSpec((tm, w), lambda i: (i, 0))
    full = lambda a: pl.BlockSpec(a.shape, lambda i: (0,) * a.ndim)
    return pl.pallas_call(
        _outmix_body,
        out_shape=jax.ShapeDtypeStruct((T, D), F32),
        grid=(T // tm,),
        in_specs=[row_spec(D), pl.BlockSpec((SSM_SLABS, tm, LANES), lambda i: (0, i, 0)),
                  row_spec(SSM_W), row_spec(NSA_W), row_spec(NSA_W),
                  full(w_glu), full(w_out)],
        out_specs=row_spec(D),
        compiler_params=_cparams(("parallel",)),
        name="outmix",
    )(x2d, y_ssm, g_ssm, o_nsa, g_nsa, w_glu, w_out)


def _gelu_tanh(x):
    c = math.sqrt(2.0 / math.pi)
    return 0.5 * x * (1.0 + jnp.tanh(c * (x + 0.044715 * (x * x * x))))


def _compress_rows(rows_refs, pe_ref, wa_ref, wb_ref, b1_ref, w2_ref, kc_ref, vc_ref, nch):
    lane = lax.broadcasted_iota(I32, (nch, LANES), 1)
    for kvi, out_ref in ((0, kc_ref), (1, vc_ref)):
        rows_ref = rows_refs[kvi]
        pa = jnp.zeros((nch, 2 * CMP_HID), F32)
        pb = jnp.zeros((nch, 2 * CMP_HID), F32)
        for j0 in range(0, CMP_STRIDE, 2):
            xs = [rows_ref[pl.ds(j, nch, stride=CMP_STRIDE), :] for j in (j0, j0 + 1)]
            xa = jnp.concatenate([xs[i] + pe_ref[kvi, 0, j0 + i:j0 + i + 1, :] for i in range(2)], axis=-1)
            xb = jnp.concatenate([xs[i] + pe_ref[kvi, 1, j0 + i:j0 + i + 1, :] for i in range(2)], axis=-1)
            wsl = slice(j0 * LANES, (j0 + 2) * LANES)
            pa += _dot(xa.astype(BF16), wa_ref[kvi, wsl, :])
            pb += _dot(xb.astype(BF16), wb_ref[kvi, wsl, :])
        hid = _gelu_tanh(pa + pltpu.roll(pb, nch - 1, 0) + b1_ref[kvi]).astype(BF16)
        for g in range(KV_HEADS):
            o = _dot(hid, w2_ref[kvi, g])
            if kvi == 1:
                o = jnp.where(lane == HEAD_DIM, 1.0, o).T
            out_ref[g] = o.astype(BF16)


def _compress_prompt_body(kvt_ref, pe_ref, wa_ref, wb_ref, b1_ref, w2_ref, kc_ref, vc_ref,
                          krows_ref, vrows_ref, *, nch):
    for c in range(kvt_ref.shape[1] // LANES):
        cs = slice(c * LANES, (c + 1) * LANES)
        krows_ref[cs, :] = kvt_ref[0:LANES, cs].T
        vrows_ref[cs, :] = kvt_ref[LANES:2 * LANES, cs].T
    _compress_rows((krows_ref, vrows_ref), pe_ref, wa_ref, wb_ref, b1_ref, w2_ref, kc_ref, vc_ref, nch)


def _compress_prompt_call(kv_t, cw, batch, seq):
    nch = seq // CMP_STRIDE
    full = lambda a: pl.BlockSpec(a.shape, lambda b: (0,) * a.ndim)
    out_spec = pl.BlockSpec((KV_HEADS, nch, LANES), lambda b: (0, b, 0))
    out_sds = jax.ShapeDtypeStruct((KV_HEADS, batch * nch, LANES), BF16)
    out_t_spec = pl.BlockSpec((KV_HEADS, None, LANES, nch), lambda b: (0, b, 0, 0))
    out_t_sds = jax.ShapeDtypeStruct((KV_HEADS, batch, LANES, nch), BF16)
    return pl.pallas_call(
        functools.partial(_compress_prompt_body, nch=nch),
        out_shape=(out_sds, out_t_sds),
        grid=(batch,),
        in_specs=[pl.BlockSpec((None, 2 * LANES, seq), lambda b: (b, 0, 0)),
                  full(cw["pe"]), full(cw["wa"]), full(cw["wb"]), full(cw["b1"]), full(cw["w2"])],
        out_specs=(out_spec, out_t_spec),
        scratch_shapes=[pltpu.VMEM((seq, LANES), F32)] * 2,
        compiler_params=_cparams(("parallel",)),
        name="compress_prompt",
    )(kv_t, cw["pe"], cw["wa"], cw["wb"], cw["b1"], cw["w2"])


def _prep_compress(p, l):
    assert KV_HEADS == 2
    w1 = p["cmp_w1"][l].reshape(2, 2, CMP_STRIDE, HEAD_DIM, CMP_HID).astype(BF16)
    z1 = jnp.zeros_like(w1)
    wexp = jnp.stack([jnp.concatenate([w1, z1], axis=-1), jnp.concatenate([z1, w1], axis=-1)], axis=3)
    wexp = wexp.reshape(2, 2, CMP_STRIDE * LANES, KV_HEADS * CMP_HID)
    pe = p["cmp_pe"][l].reshape(2, 2, CMP_STRIDE, HEAD_DIM)
    w2 = p["cmp_w2"][l]
    zeros = jnp.zeros_like(w2[0])
    w2k = jnp.concatenate([zeros, w2[0]], axis=-1)
    w2v = jnp.concatenate([w2[1], zeros], axis=-1)
    z2 = jnp.zeros_like(w2k)
    w2e = jnp.stack([jnp.stack([jnp.concatenate([w, z2], axis=0), jnp.concatenate([z2, w], axis=0)])
                     for w in (w2k, w2v)])
    return {
        "pe": jnp.tile(pe, (1, 1, 1, KV_HEADS)).astype(F32),
        "wa": wexp[:, 0],
        "wb": wexp[:, 1],
        "b1": jnp.tile(p["cmp_b1"][l].reshape(2, 1, CMP_HID), (1, 1, KV_HEADS)).astype(F32),
        "w2": w2e.astype(BF16),
    }


def _overlap_matrix(n_tok_pad, n_tok, n_blk, n_cols=LANES):
    c_start = jnp.arange(n_tok_pad) * CMP_STRIDE
    blk = jnp.arange(n_cols)
    ov = ((c_start[:, None] < (blk[None, :] + 1) * SLC_BLOCK)
          & (c_start[:, None] + CMP_BLOCK > blk[None, :] * SLC_BLOCK)
          & (jnp.arange(n_tok_pad)[:, None] < n_tok) & (blk[None, :] < n_blk))
    return ov.astype(BF16)


def _topk_select_t(w_ref, imp_t, q0, tq):
    nb = imp_t.shape[0]
    n_i = lax.broadcasted_iota(I32, (nb, tq), 0)
    qblk = lax.shift_right_logical(q0 + lax.broadcasted_iota(I32, (nb, tq), 1), SLC_SHIFT)
    causal = n_i <= qblk
    forced = (n_i == 0) | (n_i >= qblk - (N_LOCAL_BLOCKS - 1))
    w_ref[...] = jnp.where(causal, jnp.where(forced, jnp.inf, imp_t), -jnp.inf)
    last_blk = lax.shift_right_logical(q0 + tq - 1, SLC_SHIFT)
    n_grp = nb // SUBLANES
    rank = [jnp.zeros((SUBLANES, tq), F32) for _ in range(n_grp)]
    grp_i = lax.broadcasted_iota(I32, (SUBLANES, tq), 0)

    def count_group(mg, rank):
        rank = list(rank)
        for mi in range(SUBLANES):
            m = mg * SUBLANES + mi
            wm = w_ref[m:m + 1, :]
            for ng in range(n_grp):
                w = w_ref[ng * SUBLANES:(ng + 1) * SUBLANES, :]
                if ng > mg:
                    beats = jnp.where(wm >= w, 1.0, 0.0)
                elif ng < mg:
                    beats = jnp.where(wm > w, 1.0, 0.0)
                else:
                    beats = jnp.where(grp_i > mi, jnp.where(wm >= w, 1.0, 0.0), jnp.where(wm > w, 1.0, 0.0))
                rank[ng] = rank[ng] + beats
        return tuple(rank)

    rank = tuple(rank)
    for mg in range(n_grp):
        rank = lax.cond(mg * SUBLANES <= last_blk, functools.partial(count_group, mg), lambda r: r, rank)
    return causal & (jnp.concatenate(rank, axis=0) < TOP_N)


def _flash_tiles_t(tiles, q_ts, ms, accs):
    def scores(i):
        k, _, mask = tiles[i]
        s = [_dot(k, q_t) for q_t in q_ts]
        return s if mask is None else [jnp.where(mask, x, NEG_INF) for x in s]

    ahead = 2
    pending = {i: scores(i) for i in range(min(ahead, len(tiles)))}
    for i, (_, v_t, _) in enumerate(tiles):
        s = pending.pop(i)
        m_new = [jnp.maximum(m, jnp.max(x, axis=0, keepdims=True)) for m, x in zip(ms, s)]
        alpha = [jnp.exp2(m - mn) for m, mn in zip(ms, m_new)]
        p = [jnp.exp2(x - mn).astype(BF16) for x, mn in zip(s, m_new)]
        pv = [_dot(v_t, x) for x in p]
        if i + ahead < len(tiles):
            pending[i + ahead] = scores(i + ahead)
        accs = [a * acc + x for a, acc, x in zip(alpha, accs, pv)]
        ms = m_new
    return tuple(ms), tuple(accs)


def _attn_body(qt_ref, kc_ref, vct_ref, ovlt_ref, ks_ref, vst_ref, kw_ref, vwt_ref, gate_ref, o_ref,
               qa_ref, ocmp_ref, w_ref, *, tq, ncp):
    R = Q_PER_KV
    qt = pl.program_id(2)
    q0 = qt * tq
    nbs = LANES // 2

    def compressed(n_c):
        c_i = lax.broadcasted_iota(I32, (n_c, tq), 0)
        qpos_c = q0 + lax.broadcasted_iota(I32, (n_c, tq), 1)
        cmask = c_i * CMP_STRIDE + (CMP_BLOCK - 1) <= qpos_c
        kc = kc_ref[0:n_c, :]
        s = [jnp.where(cmask, _dot(kc, qt_ref[r]), NEG_INF) for r in range(R)]
        e = [jnp.where(cmask, jnp.exp2(x - jnp.max(x, axis=0, keepdims=True)), 0.0) for x in s]
        l = [jnp.sum(x, axis=0, keepdims=True) for x in e]
        p = [(x * (1.0 / jnp.where(y > 0.0, y, 1.0))).astype(BF16) for x, y in zip(e, l)]
        for r in range(R):
            ocmp_ref[r] = _dot(vct_ref[:, 0:n_c], p[r])
        return sum(_dot(ovlt_ref[:, 0:n_c], x) for x in p)

    half = ncp // 2
    if half % LANES == 0:
        imp = lax.cond((q0 + tq) // CMP_STRIDE <= half, lambda: compressed(half), lambda: compressed(ncp))
    else:
        imp = compressed(ncp)

    sel = _topk_select_t(w_ref, imp[:nbs], q0, tq)
    bias = jnp.concatenate([jnp.where(sel, 0.0, NEG_INF), jnp.zeros((LANES - nbs, tq), F32)], axis=0)
    for r in range(R):
        qa_ref[r] = (qt_ref[r].astype(F32) + bias).astype(BF16)

    key_i = lax.broadcasted_iota(I32, (tq, tq), 0)
    qry_i = lax.broadcasted_iota(I32, (tq, tq), 1)
    init = (tuple(jnp.full((1, tq), -jnp.inf, F32) for _ in range(R)),
            tuple(jnp.zeros((V_ROWS, tq), F32) for _ in range(R)))

    def tiles(k_ref, vt_ref, q_ref, js, masks, state):
        ts = [(k_ref[pl.ds(pl.multiple_of(j * tq, tq), tq), :], vt_ref[j], mask) for j, mask in zip(js, masks)]
        return _flash_tiles_t(ts, [q_ref[r] for r in range(R)], *state)

    def last_tiles(k_ref, vt_ref, q_ref, n, first_mask, state):
        js = [qt - (n - 1 - t) for t in range(n)]
        masks = [first_mask] + [None] * (n - 2) + [key_i <= qry_i] if n > 1 else [key_i <= qry_i]
        return tiles(k_ref, vt_ref, q_ref, js, masks, state)

    GROUP = 4

    def slc_group(i, st):
        return tiles(ks_ref, vst_ref, qa_ref, [GROUP * i + t for t in range(GROUP)], [None] * GROUP, st)

    state = lax.fori_loop(0, qt // GROUP, slc_group, init)
    tails = [functools.partial(last_tiles, ks_ref, vst_ref, qa_ref, n, None) for n in range(1, GROUP + 1)]
    _, acc_s = lax.switch(qt % GROUP, tails, state)

    nwin = WINDOW // tq
    wins = [functools.partial(last_tiles, kw_ref, vwt_ref, qt_ref, n, (key_i > qry_i) if n == nwin + 1 else None)
            for n in range(1, nwin + 2)]
    _, acc_w = lax.switch(jnp.minimum(qt, nwin), wins, init)

    outs = []
    for r in range(R):
        a_s = acc_s[r]
        a_w = acc_w[r]
        g = lambda k: gate_ref[3 * r + k:3 * r + k + 1, :]
        head = lambda a: a[:HEAD_DIM] * (1.0 / a[HEAD_DIM:HEAD_DIM + 1, :])
        outs.append(g(0) * ocmp_ref[r][:HEAD_DIM] + g(1) * head(a_s) + g(2) * head(a_w))
    o_ref[...] = jnp.concatenate(outs, axis=0).T.astype(o_ref.dtype)


def _attn_call(q_t, kc, vc_t, ovl_t, ks, vs_t, kw, vw_t, gates_t, batch, seq):
    tq = min(ATTN_TILE, seq)
    nq = seq // tq
    ncp = kc.shape[1] // batch
    R = Q_PER_KV
    assert seq // SLC_BLOCK <= LANES // 2 and WINDOW % tq == 0
    k_spec = pl.BlockSpec((None, seq, LANES), lambda b, g, t: (g, b, 0))
    vt_spec = pl.BlockSpec((None, nq, V_ROWS, tq), lambda b, g, t: (g, b, 0, 0))
    acc = pltpu.VMEM((R, LANES, tq), F32)
    return pl.pallas_call(
        functools.partial(_attn_body, tq=tq, ncp=ncp),
        out_shape=jax.ShapeDtypeStruct((batch * seq, NSA_W), BF16),
        grid=(batch, KV_HEADS, nq),
        in_specs=[pl.BlockSpec((R, LANES, tq), lambda b, g, t: (g, 0, b * nq + t)),
                  pl.BlockSpec((None, ncp, LANES), lambda b, g, t: (g, b, 0)),
                  pl.BlockSpec((None, None, LANES, ncp), lambda b, g, t: (g, b, 0, 0)),
                  pl.BlockSpec(ovl_t.shape, lambda b, g, t: (0, 0)),
                  k_spec, vt_spec, k_spec, vt_spec,
                  pl.BlockSpec((None, 2 * SUBLANES, tq), lambda b, g, t: (g, 0, b * nq + t))],
        out_specs=pl.BlockSpec((tq, R * HEAD_DIM), lambda b, g, t: (b * nq + t, g)),
        scratch_shapes=[pltpu.VMEM((R, LANES, tq), BF16), acc, pltpu.VMEM((LANES // 2, tq), F32)],
        compiler_params=_cparams(("parallel", "parallel", "arbitrary")),
        name="nsa_prompt",
    )(q_t, kc, vc_t, ovl_t, ks, vs_t, kw, vw_t, gates_t)


def _s5_discretise(p, l):
    lr = p["ssm_lam_re"][l].astype(F32)
    li = p["ssm_lam_im"][l].astype(F32)
    dt = jnp.exp(p["ssm_log_step"][l].astype(F32))[:, None]

    def apow(t):
        mag, ang = jnp.exp(lr * dt * t), li * dt * t
        return mag * jnp.cos(ang), mag * jnp.sin(ang)

    a_re, a_im = apow(1.0)
    den = lr * lr + li * li
    nr, ni = a_re - 1.0, a_im
    f_re, f_im = (nr * lr + ni * li) / den, (ni * lr - nr * li) / den
    br, bi = p["ssm_b_re"][l].astype(F32), p["ssm_b_im"][l].astype(F32)
    bb_re = f_re[..., None] * br - f_im[..., None] * bi
    bb_im = f_re[..., None] * bi + f_im[..., None] * br
    return apow, bb_re, bb_im


def _s5_expand_body(pst_ref, kt_ref, cp_ref, wcol_ref, wst_ref, wout_ref):
    T, E, N, C = SSM_CHUNK, SSM_SLAB_GROUPS, SSM_STATE, SSM_GROUP
    row = lax.broadcasted_iota(I32, (LANES, LANES), 0)
    lane = lax.broadcasted_iota(I32, (LANES, LANES), 1)
    row_grp = row // C
    low = lane < N
    own_grp = row_grp == lane // C
    lane_n = lax.broadcasted_iota(I32, (N, LANES), 1)
    wcol_ref[T * LANES:(T + 1) * LANES, :] = jnp.zeros((LANES, LANES), BF16)
    wout_ref[0, :, LANES:2 * LANES] = jnp.zeros((2 * E * N, LANES), BF16)
    for i in range(T):
        wcol_ref[i * LANES:(i + 1) * LANES, :] = jnp.where(own_grp, kt_ref[T - 1 - i], 0.0).astype(BF16)
        x = pst_ref[T - 1 - i]
        x_sw = pltpu.roll(x, N, 1)
        halves = (jnp.where(low, x, x_sw), jnp.where(low, x_sw, x))
        for ri, x2 in enumerate(halves):
            for q in range(E // 2):
                own = row_grp == 2 * q + jnp.where(low, 0, 1)
                c0 = ri * E * N + q * LANES
                wst_ref[i * LANES:(i + 1) * LANES, c0:c0 + LANES] = jnp.where(own, x2, 0.0).astype(BF16)
    for i in range(T + 1):
        pair, half = (0, 0) if i == 0 else ((i + 1) // 2, (i - 1) % 2)
        m_t = cp_ref[i].T
        for ri in range(2):
            rows = m_t[ri * N:(ri + 1) * N]
            for g in range(E):
                r0 = ri * E * N + g * N
                wout_ref[pair, r0:r0 + N, half * LANES:(half + 1) * LANES] = (
                    jnp.where(lane_n // C == g, rows, 0.0).astype(BF16))


def _prep_s5_chunked(p, l):
    hi = lax.Precision.HIGHEST
    apow, bb_re, bb_im = _s5_discretise(p, l)
    c_re, c_im = p["ssm_c_re"][l].astype(F32), p["ssm_c_im"][l].astype(F32)
    T, J, E, N, C = SSM_CHUNK, SSM_SLABS, SSM_SLAB_GROUPS, SSM_STATE, SSM_GROUP
    pw_re, pw_im = apow(jnp.arange(T + 1, dtype=F32)[:, None, None])
    bt_re, bt_im = bb_re.transpose(0, 2, 1), bb_im.transpose(0, 2, 1)
    p_re = pw_re[:, :, None, :] * bt_re - pw_im[:, :, None, :] * bt_im
    p_im = pw_re[:, :, None, :] * bt_im + pw_im[:, :, None, :] * bt_re
    kt = (jnp.einsum("tgkn,gcn->tgkc", p_re[:T], jnp.tile(c_re, (1, E, 1)), precision=hi)
          - jnp.einsum("tgkn,gcn->tgkc", p_im[:T], jnp.tile(c_im, (1, E, 1)), precision=hi))
    cp_re = c_re * pw_re[:, :, None, :] - c_im * pw_im[:, :, None, :]
    cp_im = c_re * pw_im[:, :, None, :] + c_im * pw_re[:, :, None, :]
    slab = lambda x: jnp.moveaxis(x.reshape(x.shape[0], J, E * C, LANES), 1, 0)
    pst = slab(jnp.concatenate([p_re[:T], p_im[:T]], axis=-1))
    ktile = slab(kt)
    cpn = slab(jnp.concatenate([cp_re, -cp_im], axis=-1))
    blk = lambda n: pl.BlockSpec((None, n, LANES, LANES), lambda j: (j, 0, 0, 0))
    w_col, w_st, w_out = pl.pallas_call(
        _s5_expand_body,
        out_shape=(jax.ShapeDtypeStruct((J, (T + 1) * LANES, LANES), BF16),
                   jax.ShapeDtypeStruct((J, T * LANES, 2 * E * N), BF16),
                   jax.ShapeDtypeStruct((J, T // 2 + 1, 2 * E * N, 2 * LANES), BF16)),
        grid=(J,),
        in_specs=[blk(T), blk(T), blk(T + 1)],
        out_specs=(pl.BlockSpec((None, (T + 1) * LANES, LANES), lambda j: (j, 0, 0)),
                   pl.BlockSpec((None, T * LANES, 2 * E * N), lambda j: (j, 0, 0)),
                   pl.BlockSpec((None, T // 2 + 1, 2 * E * N, 2 * LANES), lambda j: (j, 0, 0, 0))),
        compiler_params=_cparams(("parallel",)),
        name="s5_expand",
    )(pst, ktile, cpn)
    return {
        "w_col": w_col, "w_st": w_st, "w_out": w_out,
        "a_re": pw_re[T].reshape(J, 1, E * N), "a_im": pw_im[T].reshape(J, 1, E * N),
        "a1_re": pw_re[1].reshape(J, 1, E * N), "a1_im": pw_im[1].reshape(J, 1, E * N),
        "d": p["ssm_d"][l].reshape(J, 1, LANES).astype(F32),
    }


def _s5_prompt_body(u_ref, wcol_ref, wst_ref, are_ref, aim_ref, wout_ref, d_ref,
                    y_ref, hre_ref, him_ref, xs_ref, hp_ref, *, n_chunks):
    T = SSM_CHUNK
    ns = SSM_SLAB_GROUPS * SSM_STATE
    u_pos = [u_ref[pl.ds(s, n_chunks, stride=T), :] for s in range(T)]
    ub = jnp.concatenate(u_pos, axis=-1).astype(BF16)
    xs_ref[...] = _dot(ub, wst_ref[...])
    a_re = are_ref[...]
    a_im = aim_ref[...]

    def step(c, carry):
        hr, hi = carry
        hp_ref[pl.ds(c, 1), 0:ns] = hr
        hp_ref[pl.ds(c, 1), ns:2 * ns] = hi
        xr = xs_ref[pl.ds(c, 1), 0:ns]
        xi = xs_ref[pl.ds(c, 1), ns:2 * ns]
        return a_re * hr - a_im * hi + xr, a_re * hi + a_im * hr + xi

    zero = jnp.zeros((1, ns), F32)
    hr, hi = lax.fori_loop(0, n_chunks, step, (zero, zero))
    hre_ref[...] = jnp.broadcast_to(hr, hre_ref.shape)
    him_ref[...] = jnp.broadcast_to(hi, him_ref.shape)
    hpb = hp_ref[...].astype(BF16)
    for t in range(0, T, 2):
        k_rows = (t + 2) * LANES
        w_pair = jnp.concatenate([wcol_ref[(T - 1 - t) * LANES:(T - 1 - t) * LANES + k_rows, :],
                                  wcol_ref[(T - 2 - t) * LANES:T * LANES, :]], axis=1)
        y_pair = _dot(ub[:, :k_rows], w_pair) + _dot(hpb, wout_ref[t // 2 + 1])
        for i in range(2):
            y_ref[pl.ds(t + i, n_chunks, stride=T), :] = (
                y_pair[:, i * LANES:(i + 1) * LANES] + d_ref[...] * u_pos[t + i])


def _s5_prompt_call(u_slab, sw, batch, seq):
    T, J = SSM_CHUNK, SSM_SLABS
    n_chunks = seq // T
    ns = SSM_SLAB_GROUPS * SSM_STATE
    row_spec = pl.BlockSpec((None, seq, LANES), lambda j, b: (j, b, 0))
    slab_spec = lambda a: pl.BlockSpec((None,) + a.shape[1:], lambda j, b: (j,) + (0,) * (a.ndim - 1))
    st_spec = pl.BlockSpec((None, None, SUBLANES, ns), lambda j, b: (b, j, 0, 0))
    st_sds = jax.ShapeDtypeStruct((batch, J, SUBLANES, ns), F32)
    y, hre, him = pl.pallas_call(
        functools.partial(_s5_prompt_body, n_chunks=n_chunks),
        out_shape=(jax.ShapeDtypeStruct(u_slab.shape, F32), st_sds, st_sds),
        grid=(J, batch),
        in_specs=[row_spec, slab_spec(sw["w_col"]), slab_spec(sw["w_st"]), slab_spec(sw["a_re"]),
                  slab_spec(sw["a_im"]), slab_spec(sw["w_out"]), slab_spec(sw["d"])],
        out_specs=(row_spec, st_spec, st_spec),
        scratch_shapes=[pltpu.VMEM((n_chunks, 2 * ns), F32), pltpu.VMEM((n_chunks, 2 * ns), F32)],
        compiler_params=_cparams(("parallel", "parallel")),
        name="s5_prompt",
    )(u_slab, sw["w_col"], sw["w_st"], sw["a_re"], sw["a_im"], sw["w_out"], sw["d"])
    n_groups = J * SSM_SLAB_GROUPS
    state = lambda h: h[:, :, 0, :].reshape(batch, n_groups, SSM_STATE)
    return y, state(hre), state(him)


def _s5_step_body(u_ref, wx_ref, are_ref, aim_ref, h0re_ref, h0im_ref, wy_ref, d_ref,
                  y_ref, hre_ref, him_ref):
    ns = SSM_SLAB_GROUPS * SSM_STATE
    for j in range(SSM_SLABS):
        sl = slice(j * ns, (j + 1) * ns)
        u = u_ref[j]
        x = _dot(u.astype(BF16), wx_ref[j])
        a_re, a_im = are_ref[j], aim_ref[j]
        h0r, h0i = h0re_ref[:, sl], h0im_ref[:, sl]
        hr = a_re * h0r - a_im * h0i + x[:, :ns]
        hi = a_re * h0i + a_im * h0r + x[:, ns:]
        hre_ref[:, sl] = hr
        him_ref[:, sl] = hi
        y_ref[j] = _dot(jnp.concatenate([hr, hi], axis=-1).astype(BF16), wy_ref[j]) + d_ref[j] * u


def _s5_step_call(u_slab, h0_re, h0_im, sw):
    J, n_tok, _ = u_slab.shape
    T = SSM_CHUNK
    full = lambda a: pl.BlockSpec(a.shape, lambda i: (0,) * a.ndim)
    wx_spec = pl.BlockSpec((J, LANES, sw["w_st"].shape[2]), lambda i: (0, T - 1, 0))
    wy_spec = pl.BlockSpec((J, None, sw["w_out"].shape[2], LANES), lambda i: (0, 0, 0, 0))
    st_sds = jax.ShapeDtypeStruct(h0_re.shape, F32)
    return pl.pallas_call(
        _s5_step_body,
        out_shape=(jax.ShapeDtypeStruct(u_slab.shape, F32), st_sds, st_sds),
        grid=(1,),
        in_specs=[full(u_slab), wx_spec, full(sw["a1_re"]), full(sw["a1_im"]), full(h0_re), full(h0_im),
                  wy_spec, full(sw["d"])],
        out_specs=(full(u_slab), full(h0_re), full(h0_re)),
        compiler_params=_cparams(("arbitrary",)),
        name="s5_step",
    )(u_slab, sw["w_st"], sw["a1_re"], sw["a1_im"], h0_re, h0_im, sw["w_out"], sw["d"])


def _compress_sample_body(pt_ref, *refs, n_pages, page_rows, nch):
    del pt_ref
    page_refs = refs[:n_pages]
    pe_ref, wa_ref, wb_ref, b1_ref, w2_ref, kc_ref, vc_ref = refs[n_pages:n_pages + 7]
    buffers = refs[n_pages + 7:]
    b = pl.program_id(0)

    @pl.when(b == 0)
    def _():
        for buf in buffers[2:]:
            buf[...] = jnp.zeros(buf.shape, F32)

    def step(stage, done):
        for i, page_ref in enumerate(page_refs):
            for rows_ref, slot in zip(stage, range(2)):
                rows_ref[i * page_rows:(i + 1) * page_rows, :] = page_ref[slot].reshape(LANES, page_rows).T
        _compress_rows(done, pe_ref, wa_ref, wb_ref, b1_ref, w2_ref, kc_ref, vc_ref, nch)

    @pl.when(b % 2 == 0)
    def _():
        step(buffers[:2], buffers[2:])

    @pl.when(b % 2 == 1)
    def _():
        step(buffers[2:], buffers[:2])


def _compress_sample_call(cache, page_table, cw):
    n_seq, n_pages = page_table.shape
    page_rows = cache.shape[-1]
    assert page_rows == LANES
    nch = n_pages * page_rows // CMP_STRIDE
    staged = lambda b: jnp.minimum(b, n_seq - 1)
    page_spec = lambda i: pl.BlockSpec((None, 2, KV_HEADS, HEAD_DIM, page_rows),
                                       lambda b, pt: (pt[staged(b), i], 0, 0, 0, 0))
    full = lambda a: pl.BlockSpec(a.shape, lambda b, pt: (0,) * a.ndim)
    out_of = lambda b: jnp.maximum(b - 1, 0)
    out_spec = pl.BlockSpec((KV_HEADS, nch, LANES), lambda b, pt: (0, out_of(b), 0))
    out_sds = jax.ShapeDtypeStruct((KV_HEADS, n_seq * nch, LANES), BF16)
    out_t_spec = pl.BlockSpec((KV_HEADS, None, LANES, nch), lambda b, pt: (0, out_of(b), 0, 0))
    out_t_sds = jax.ShapeDtypeStruct((KV_HEADS, n_seq, LANES, nch), BF16)
    weights = (cw["pe"], cw["wa"], cw["wb"], cw["b1"], cw["w2"])
    return pl.pallas_call(
        functools.partial(_compress_sample_body, n_pages=n_pages, page_rows=page_rows, nch=nch),
        out_shape=(out_sds, out_t_sds),
        grid_spec=pltpu.PrefetchScalarGridSpec(
            num_scalar_prefetch=1,
            grid=(n_seq + 1,),
            in_specs=[page_spec(i) for i in range(n_pages)] + [full(a) for a in weights],
            out_specs=(out_spec, out_t_spec),
            scratch_shapes=[pltpu.VMEM((n_pages * page_rows, LANES), F32)] * 4),
        compiler_params=_cparams(("arbitrary",)),
        name="compress_sample",
    )(page_table, *([cache] * n_pages), *weights)


def _group_rows(x0, x1):
    row = lax.broadcasted_iota(I32, x0.shape, 0)
    return jnp.where(row < Q_PER_KV, x0, x1)


def _sample_select_body(q_ref, kc_ref, vct_ref, ovl_ref, tri_ref, ocmp_ref, idx_ref, *, ncp, qpos, nbp):
    q8 = q_ref[...]
    c_i = lax.broadcasted_iota(I32, (N_HEADS, ncp), 1)
    cmask = c_i * CMP_STRIDE + (CMP_BLOCK - 1) <= qpos
    s = _group_rows(_dot_nt(q8, kc_ref[0]), _dot_nt(q8, kc_ref[1]))
    s = jnp.where(cmask, s, NEG_INF)
    e = jnp.where(cmask, jnp.exp2(s - jnp.max(s, axis=-1, keepdims=True)), 0.0)
    l = jnp.sum(e, axis=-1, keepdims=True)
    p = (e / jnp.where(l > 0.0, l, 1.0)).astype(BF16)
    ocmp_ref[...] = _group_rows(_dot_nt(p, vct_ref[0]), _dot_nt(p, vct_ref[1]))
    imp8 = _dot(p, ovl_ref[...])

    n_row = lax.broadcasted_iota(I32, (1, nbp), 1)
    qblk = qpos // SLC_BLOCK
    causal = n_row <= qblk
    forced = (n_row == 0) | (n_row >= qblk - (N_LOCAL_BLOCKS - 1))
    m_i = lax.broadcasted_iota(I32, (nbp, nbp), 0)
    n_i = lax.broadcasted_iota(I32, (nbp, nbp), 1)
    lane = lax.broadcasted_iota(I32, (1, LANES), 1)
    idx_rows = []
    for g in range(KV_HEADS):
        imp = jnp.sum(imp8[g * Q_PER_KV:(g + 1) * Q_PER_KV], axis=0, keepdims=True)
        w = jnp.where(causal, jnp.where(forced, jnp.inf, imp), -jnp.inf)
        w_sq = jnp.broadcast_to(w, (nbp, nbp))
        w_col = w_sq.T
        beats = jnp.where(n_i > m_i, jnp.where(w_col >= w_sq, 1.0, 0.0), jnp.where(w_col > w_sq, 1.0, 0.0))
        rank = jnp.sum(beats, axis=0, keepdims=True)
        sel = causal & (rank < TOP_N)
        self_f = jnp.where(sel, 1.0, 0.0)
        before = _dot(self_f.astype(BF16), tri_ref[...])
        idx = jnp.full((1, LANES), -1, I32)
        for k in range(TOP_N):
            hit = sel & (before == float(k))
            val = jnp.sum(jnp.where(hit, n_row.astype(F32) + 1.0, 0.0), axis=-1, keepdims=True) - 1.0
            idx = jnp.where(lane == k, val.astype(I32), idx)
        idx_rows.append(idx)
    idx_ref[...] = jnp.concatenate(idx_rows + [jnp.full((SUBLANES - KV_HEADS, LANES), -1, I32)], axis=0)


def _sample_select_call(q8, kc, vc_t, ovl, tri, qpos):
    n_seq = q8.shape[0]
    ncp = kc.shape[1] // n_seq
    nbp = ovl.shape[1]
    cmp_spec = pl.BlockSpec((KV_HEADS, ncp, LANES), lambda b: (0, b, 0))
    row_spec = pl.BlockSpec((None, N_HEADS, LANES), lambda b: (b, 0, 0))
    full = lambda a: pl.BlockSpec(a.shape, lambda b: (0,) * a.ndim)
    return pl.pallas_call(
        functools.partial(_sample_select_body, ncp=ncp, qpos=qpos, nbp=nbp),
        out_shape=(jax.ShapeDtypeStruct((n_seq, N_HEADS, LANES), F32),
                   jax.ShapeDtypeStruct((n_seq, SUBLANES, LANES), I32)),
        grid=(n_seq,),
        in_specs=[row_spec, cmp_spec, pl.BlockSpec((KV_HEADS, None, LANES, ncp), lambda b: (0, b, 0, 0)),
                  full(ovl), full(tri)],
        out_specs=(row_spec, pl.BlockSpec((None, SUBLANES, LANES), lambda b: (b, 0, 0))),
        compiler_params=_cparams(("parallel",)),
        name="nsa_sample_select",
    )(q8, kc, vc_t, ovl, tri)


def _sample_attend_body(idx_ref, pt_ref, q_ref, ocmp_ref, gate_ref, ksn_ref, vsn_ref, kwn_ref, vwn_ref,
                        win_ref, *refs, n_cache_blocks, blocks_per_page, win_skip):
    del pt_ref
    n_blk = KV_HEADS * TOP_N
    kv_refs, o_ref = refs[:n_blk], refs[n_blk]
    b = pl.program_id(0)
    q = q_ref[...]
    qf = q.astype(F32)
    row_g = (lax.broadcasted_iota(I32, (N_HEADS, 1), 0) >= Q_PER_KV).astype(I32)

    def attend(s_list, v_list, kn_ref, vn_ref):
        s_self = jnp.sum(qf * kn_ref[...].astype(F32), axis=-1, keepdims=True)
        m = s_self
        for s in s_list:
            m = jnp.maximum(m, jnp.max(s, axis=-1, keepdims=True))
        p_self = jnp.exp2(s_self - m)
        l = p_self
        acc = p_self.astype(BF16).astype(F32) * vn_ref[...].astype(F32)
        for s, v in zip(s_list, v_list):
            p = jnp.exp2(s - m)
            l = l + jnp.sum(p, axis=-1, keepdims=True)
            acc = acc + _dot_nt(p.astype(BF16), v().astype(BF16))
        return acc / l

    s_list, v_list = [], []
    for j in range(n_blk):
        s = _dot(q, kv_refs[j][0].astype(BF16))
        col = lax.broadcasted_iota(I32, s.shape, 1)
        n = idx_ref[b, j]
        first = (n & (blocks_per_page - 1)) * SLC_BLOCK
        ok = ((row_g == j // TOP_N) & (col >= first) & (col < first + SLC_BLOCK)
              & (n >= 0) & (n < n_cache_blocks))
        s_list.append(jnp.where(ok, s, NEG_INF))
        v_list.append(lambda j=j: kv_refs[j][1])
    o_slc = attend(s_list, v_list, ksn_ref, vsn_ref)

    s_list, v_list = [], []
    for g in range(KV_HEADS):
        s = _dot(q, win_ref[0, g].astype(BF16))
        col = lax.broadcasted_iota(I32, s.shape, 1)
        s_list.append(jnp.where((row_g == g) & (col >= win_skip), s, NEG_INF))
        v_list.append(lambda g=g: win_ref[1, g])
    o_win = attend(s_list, v_list, kwn_ref, vwn_ref)

    gates = gate_ref[...]
    o_ref[...] = gates[:, 0:1] * ocmp_ref[:, 0:HEAD_DIM] + gates[:, 1:2] * o_slc + gates[:, 2:3] * o_win


def _sample_attend_call(idx, page_table, q64, ocmp, gates8, ksn, vsn, kwn, vwn, cache_win_t, cache_t, win_skip):
    n_seq, n_pages = page_table.shape
    page_rows = cache_t.shape[-1]
    bpp = page_rows // SLC_BLOCK
    n_cache_blocks = n_pages * bpp
    row_spec = lambda a: pl.BlockSpec((None,) + a.shape[1:], lambda b, ix, pt: (b,) + (0,) * (a.ndim - 1))

    assert bpp & (bpp - 1) == 0
    bpp_shift = bpp.bit_length() - 1

    def blk_spec(j):
        def index_map(b, ix, pt):
            n = jnp.minimum(jnp.maximum(ix[b, j], 0), n_cache_blocks - 1)
            return (pt[b, lax.shift_right_logical(n, bpp_shift)], 1, j // TOP_N, 0, 0)
        return pl.BlockSpec((None, 2, None, HEAD_DIM, page_rows), index_map)

    n_blk = KV_HEADS * TOP_N
    small = (q64, ocmp, gates8, ksn, vsn, kwn, vwn, cache_win_t)
    return pl.pallas_call(
        functools.partial(_sample_attend_body, n_cache_blocks=n_cache_blocks, blocks_per_page=bpp,
                          win_skip=win_skip),
        out_shape=jax.ShapeDtypeStruct((n_seq, N_HEADS, HEAD_DIM), F32),
        grid_spec=pltpu.PrefetchScalarGridSpec(
            num_scalar_prefetch=2,
            grid=(n_seq,),
            in_specs=[row_spec(a) for a in small] + [blk_spec(j) for j in range(n_blk)],
            out_specs=pl.BlockSpec((None, N_HEADS, HEAD_DIM), lambda b, ix, pt: (b, 0, 0))),
        compiler_params=_cparams(("parallel",)),
        name="nsa_sample_attend",
    )(idx, page_table, *small, *([cache_t] * n_blk))


def _round_up(x, m):
    return -(-x // m) * m


def _prompt_layer(h, prm, cw, sw, batch, seq):
    tabs = _rope_tables(jnp.arange(seq))
    u, gs, gn, q_t, kv_t, win_t, gates_t, ks, vs_t, kw, vw_t = _proj_call(
        h, prm["w_in"], prm["norm_w"], prm["qnw"], prm["knw"], prm["gb"], tabs, batch, seq)
    y_ssm, h_re, h_im = _s5_prompt_call(u, sw, batch, seq)
    kc, vc_t = _compress_prompt_call(kv_t, cw, batch, seq)
    nch = seq // CMP_STRIDE
    ovl_t = _overlap_matrix(nch, nch - 1, seq // SLC_BLOCK).T
    o = _attn_call(q_t, kc, vc_t, ovl_t, ks, vs_t, kw, vw_t, gates_t, batch, seq)
    h_new = _outmix_call(h, y_ssm, gs, o, gn, prm["w_glu"], prm["w_out"])
    rows = lambda x_t, slots: x_t.reshape(batch, slots, KV_HEADS, HEAD_DIM, seq).transpose(0, 4, 1, 2, 3)
    return h_new, rows(kv_t, 4), rows(win_t, 2)[:, seq - min(WINDOW, seq):], h_re, h_im


def _sample_layer(h, prm, cw, sw, cache_kv, cache_win, st_re, st_im, page_table):
    n_seq = h.shape[0]
    n_phys, page_rows = cache_kv.shape[:2]
    n_pages = page_table.shape[1]
    past_len = n_pages * page_rows
    win_buf = cache_win.shape[1]
    n_pad = _round_up(n_seq, LANES)
    tabs = _rope_tables(jnp.full((n_pad,), past_len, I32))
    h_pad = jnp.pad(h, ((0, n_pad - n_seq), (0, 0)))
    u, gs, gn, q_t, kv_t, win_t, gates_t, ks, vs_t, kw, vw_t = _proj_call(
        h_pad, prm["w_in"], prm["norm_w"], prm["qnw"], prm["knw"], prm["gb"], tabs, 1, n_pad)
    u, gs, gn = u[:, :n_seq], gs[:n_seq], gn[:n_seq]
    n_state = st_re.shape[1] * st_re.shape[2]
    y_ssm, h_re, h_im = _s5_step_call(u, st_re.reshape(n_seq, n_state), st_im.reshape(n_seq, n_state), sw)
    cache_t = cache_kv.transpose(0, 2, 3, 4, 1)
    kc, vc_t = _compress_sample_call(cache_t, page_table, cw)
    ncp = past_len // CMP_STRIDE
    n_blk = -(-(past_len + 1) // SLC_BLOCK)
    nbp = _round_up(n_blk, LANES)
    ovl = _overlap_matrix(ncp, ncp - 1, n_blk, nbp)
    tri = (jnp.arange(nbp)[:, None] < jnp.arange(nbp)[None, :]).astype(BF16)
    q8 = q_t[:, :, :n_seq].transpose(2, 0, 1)
    ocmp, idx = _sample_select_call(q8, kc, vc_t, ovl, tri, past_len)
    idx = idx[:, :KV_HEADS, :TOP_N].reshape(n_seq, KV_HEADS * TOP_N)
    gates8 = gates_t[:, :3 * Q_PER_KV, :n_seq].reshape(KV_HEADS, Q_PER_KV, 3, n_seq).transpose(3, 0, 1, 2)
    gates8 = jnp.pad(gates8.reshape(n_seq, N_HEADS, 3), ((0, 0), (0, 0), (0, LANES - 3)))
    per_head = lambda a: jnp.repeat(a.transpose(1, 0, 2), Q_PER_KV, axis=1)
    new_k = lambda k: per_head(k[:, :n_seq, HEAD_DIM:])
    new_v = lambda v_t: per_head(v_t[:, 0, :HEAD_DIM, :n_seq].transpose(0, 2, 1))
    o8 = _sample_attend_call(idx, page_table, q8[:, :, HEAD_DIM:], ocmp, gates8, new_k(ks), new_v(vs_t),
                             new_k(kw), new_v(vw_t), cache_win.transpose(0, 2, 3, 4, 1), cache_t,
                             max(win_buf + 1 - WINDOW, 0))
    o = o8.reshape(n_seq, NSA_W)
    h_new = _outmix_call(h, y_ssm, gs, o, gn, prm["w_glu"], prm["w_out"])
    kv_rows = kv_t[0, :, :n_seq].T.reshape(n_seq, 1, 4, KV_HEADS, HEAD_DIM)
    win_new = win_t[0, :, :n_seq].T.reshape(n_seq, 1, 2, KV_HEADS, HEAD_DIM)
    wrows = jnp.concatenate([cache_win, win_new], axis=1)
    wrows = wrows[:, wrows.shape[1] - min(WINDOW, wrows.shape[1]):]
    state = lambda s: s.reshape(st_re.shape)
    return h_new, kv_rows, wrows, state(h_re), state(h_im)


def kernel(x_prompt, x_sample, cache_kv, cache_win, state_ssm_re, state_ssm_im, page_table, norm_w, w_in, gate_b,
           q_norm_w, k_norm_w, cmp_pe, cmp_w1, cmp_b1, cmp_w2, ssm_lam_re, ssm_lam_im, ssm_log_step, ssm_b_re,
           ssm_b_im, ssm_c_re, ssm_c_im, ssm_d, w_glu, w_out):
    p = dict(norm_w=norm_w, w_in=w_in, gate_b=gate_b, q_norm_w=q_norm_w, k_norm_w=k_norm_w, cmp_pe=cmp_pe,
             cmp_w1=cmp_w1, cmp_b1=cmp_b1, cmp_w2=cmp_w2, ssm_lam_re=ssm_lam_re, ssm_lam_im=ssm_lam_im,
             ssm_log_step=ssm_log_step, ssm_b_re=ssm_b_re, ssm_b_im=ssm_b_im, ssm_c_re=ssm_c_re,
             ssm_c_im=ssm_c_im, ssm_d=ssm_d, w_glu=w_glu, w_out=w_out)
    b_p, s_p, d_model = x_prompt.shape
    b_s, s_s, _ = x_sample.shape
    assert s_s == 1, "the sample group decodes one token per sequence"
    h_p = x_prompt.reshape(b_p * s_p, d_model)
    h_s = x_sample.reshape(b_s, d_model)
    outs_p, outs_s = [], []
    for l in range(norm_w.shape[0]):
        weights = (_prep_params(p, l), _prep_compress(p, l), _prep_s5_chunked(p, l))
        h_p, *rest_p = _prompt_layer(h_p, *weights, b_p, s_p)
        h_s, *rest_s = _sample_layer(h_s, *weights, cache_kv[l], cache_win[l], state_ssm_re[l],
                                     state_ssm_im[l], page_table)
        outs_p.append(rest_p)
        outs_s.append(rest_s)
    stack = lambda outs, i: jnp.stack([o[i] for o in outs])
    return (h_p.reshape(x_prompt.shape), h_s.reshape(x_sample.shape),
            stack(outs_p, 0), stack(outs_s, 0), stack(outs_p, 1), stack(outs_s, 1),
            stack(outs_p, 2), stack(outs_p, 3), stack(outs_s, 2), stack(outs_s, 3))
```

```python
import functools
import math

import jax
import jax.numpy as jnp
from jax import lax
from jax.experimental import pallas as pl
from jax.experimental.pallas import tpu as pltpu

F32 = jnp.float32
BF16 = jnp.bfloat16
I32 = jnp.int32

LANES = 128
SUBLANES = 8
VMEM_LIMIT_BYTES = 56 * 1024 * 1024

HEAD_DIM = 64
N_HEADS = 8
KV_HEADS = 2
Q_PER_KV = N_HEADS // KV_HEADS
SSM_W = 512
SSM_GROUP = 16
SSM_STATE = 64
NSA_W = N_HEADS * HEAD_DIM
CMP_BLOCK = 32
CMP_STRIDE = 16
CMP_HID = 2 * HEAD_DIM
SLC_BLOCK = 64
SLC_SHIFT = SLC_BLOCK.bit_length() - 1
TOP_N = 16
N_LOCAL_BLOCKS = 2
WINDOW = 512
ROPE_THETA = 500000.0
ROPE_DIM = HEAD_DIM // 4
RMS_EPS = 1e-6
NEG_INF = -1e30

COL_U = 0
COL_GS = SSM_W
COL_Q = 2 * SSM_W
COL_GN = 2 * SSM_W + NSA_W
COL_KV = 2 * SSM_W + 2 * NSA_W
COL_GL = COL_KV + 6 * KV_HEADS * HEAD_DIM
IN_W_PAD = COL_GL + LANES

PROJ_TILE = 512
ATTN_TILE = 256
V_ROWS = HEAD_DIM + 16
SSM_CHUNK = 16
SSM_SLAB_GROUPS = LANES // SSM_GROUP
SSM_SLABS = SSM_W // LANES


def _cparams(sem):
    return pltpu.CompilerParams(dimension_semantics=sem, vmem_limit_bytes=VMEM_LIMIT_BYTES)


def _sigmoid(x):
    return 1.0 / (1.0 + jnp.exp(-x))


def _dot(a, b):
    return jnp.dot(a, b, preferred_element_type=F32)


def _dot_nt(a, b):
    return lax.dot_general(a, b, (((1,), (1,)), ((), ())), preferred_element_type=F32)


def _proj_body(x_ref, nw_ref, w_ref, qnw_ref, knw_ref, gb_ref, ra_ref, rb_ref, rc_ref,
               u_ref, gs_ref, gn_ref, qt_ref, kvt_ref, wint_ref, gt_ref,
               ks_ref, vst_ref, kw_ref, vwt_ref, *, tm, tv, tiles_per_seq):
    x = x_ref[...]
    ms = jnp.mean(x * x, axis=-1, keepdims=True)
    h = (x * lax.rsqrt(ms + RMS_EPS) * nw_ref[...]).astype(BF16)

    def mm(c0, c1):
        return _dot(h, w_ref[:, c0:c1])

    lane = lax.broadcasted_iota(I32, (tm, LANES), 1)
    lo = lane < HEAD_DIM
    ra = ra_ref[...]
    rb = rb_ref[...]
    rc = rc_ref[...]

    def norm_rope(s, wrow):
        s2 = s * s
        slo = jnp.sum(jnp.where(lo, s2, 0.0), axis=-1, keepdims=True)
        shi = jnp.sum(jnp.where(lo, 0.0, s2), axis=-1, keepdims=True)
        msq = jnp.where(lo, slo, shi) * (1.0 / HEAD_DIM)
        y = s * lax.rsqrt(msq + RMS_EPS) * wrow
        half = ROPE_DIM // 2
        return y * ra + pltpu.roll(y, LANES - half, 1) * rb + pltpu.roll(y, half, 1) * rc

    def hi_half(y, head):
        src = pltpu.roll(y, HEAD_DIM, 1) if head == 0 else y
        return jnp.where(lo, 0.0, src)

    zq = mm(COL_Q, COL_GN)
    zkv = mm(COL_KV, COL_GL)
    zgl = mm(COL_GL, IN_W_PAD)

    qnw = qnw_ref[...]
    scale = HEAD_DIM ** -0.5 * math.log2(math.e)
    zeros_t = jnp.zeros((HEAD_DIM, tm), F32)
    for j in range(N_HEADS // 2):
        y_t = (norm_rope(zq[:, j * LANES:(j + 1) * LANES], qnw) * scale).T
        for head in range(2):
            q_t = jnp.concatenate([zeros_t, y_t[head * HEAD_DIM:(head + 1) * HEAD_DIM]], axis=0)
            qt_ref[2 * j + head] = q_t.astype(BF16)

    kc = norm_rope(zkv[:, 0:LANES], knw_ref[0:1, :])
    vc = zkv[:, LANES:2 * LANES]
    ks = norm_rope(zkv[:, 2 * LANES:3 * LANES], knw_ref[1:2, :])
    vs = zkv[:, 3 * LANES:4 * LANES]
    kw = norm_rope(zkv[:, 4 * LANES:5 * LANES], knw_ref[2:3, :])
    vw = zkv[:, 5 * LANES:6 * LANES]
    vs_t, vw_t = vs.T, vw.T
    for i, rows_t in enumerate((kc.T, vc.T, ks.T, vs_t)):
        kvt_ref[i * LANES:(i + 1) * LANES, :] = rows_t
    for i, rows_t in enumerate((kw.T, vw_t)):
        wint_ref[i * LANES:(i + 1) * LANES, :] = rows_t

    row = lax.broadcasted_iota(I32, (tm, LANES), 0)
    pos = (pl.program_id(0) % tiles_per_seq) * tm + row
    onehot = jnp.where(lane == lax.shift_right_logical(pos, SLC_SHIFT), 1.0, 0.0)
    ones_t = jnp.where(lax.broadcasted_iota(I32, (V_ROWS - HEAD_DIM, tm), 0) == 0, 1.0, 0.0)
    for g in range(KV_HEADS):
        ks_ref[g] = jnp.where(lo, onehot, hi_half(ks, g)).astype(BF16)
        kw_ref[g] = hi_half(kw, g).astype(BF16)
        for v_t, vt_ref in ((vs_t, vst_ref), (vw_t, vwt_ref)):
            v_aug = jnp.concatenate([v_t[g * HEAD_DIM:(g + 1) * HEAD_DIM], ones_t], axis=0).astype(BF16)
            for t in range(tm // tv):
                vt_ref[g, t] = v_aug[:, t * tv:(t + 1) * tv]

    gates_t = _sigmoid(zgl + gb_ref[...]).T
    for g in range(KV_HEADS):
        gt_ref[g] = gates_t[g * 3 * Q_PER_KV:g * 3 * Q_PER_KV + 2 * SUBLANES]

    zu = mm(COL_U, COL_GS)
    for j in range(SSM_SLABS):
        u_ref[j] = zu[:, j * LANES:(j + 1) * LANES]
    for g_ref, cols in ((gs_ref, (COL_GS, COL_Q)), (gn_ref, (COL_GN, COL_KV))):
        g = mm(*cols)
        g_ref[...] = (g * _sigmoid(g)).astype(BF16)


def _proj_call(x2d, w_pad, norm_w, qnw, knw, gb, tabs, batch, seq):
    T, D = x2d.shape
    tm = min(PROJ_TILE, seq)
    tv = min(ATTN_TILE, seq)
    assert T == batch * seq and seq % tm == 0 and tm % tv == 0 and tv % LANES == 0
    tps = seq // tm
    row_spec = lambda w: pl.BlockSpec((tm, w), lambda i: (i, 0))
    full = lambda a: pl.BlockSpec(a.shape, lambda i: (0,) * a.ndim)
    tab_spec = pl.BlockSpec((tm, LANES), lambda i: (i % tps, 0))
    head_spec = lambda n: pl.BlockSpec((n, tm, LANES), lambda i: (0, i, 0))
    head_t_spec = lambda n, rows: pl.BlockSpec((n, rows, tm), lambda i: (0, 0, i))
    cache_t_spec = lambda rows: pl.BlockSpec((None, rows, tm), lambda i: (i // tps, 0, i % tps))
    tile_t_spec = pl.BlockSpec((KV_HEADS, tm // tv, V_ROWS, tv), lambda i: (0, i, 0, 0))
    tile_t_sds = jax.ShapeDtypeStruct((KV_HEADS, T // tv, V_ROWS, tv), BF16)
    out_shape = (
        jax.ShapeDtypeStruct((SSM_SLABS, T, LANES), F32),
        jax.ShapeDtypeStruct((T, SSM_W), BF16),
        jax.ShapeDtypeStruct((T, NSA_W), BF16),
        jax.ShapeDtypeStruct((N_HEADS, LANES, T), BF16),
        jax.ShapeDtypeStruct((batch, 4 * LANES, seq), F32),
        jax.ShapeDtypeStruct((batch, 2 * LANES, seq), F32),
        jax.ShapeDtypeStruct((KV_HEADS, 2 * SUBLANES, T), F32),
        jax.ShapeDtypeStruct((KV_HEADS, T, LANES), BF16),
        tile_t_sds,
        jax.ShapeDtypeStruct((KV_HEADS, T, LANES), BF16),
        tile_t_sds,
    )
    out_specs = (head_spec(SSM_SLABS), row_spec(SSM_W), row_spec(NSA_W), head_t_spec(N_HEADS, LANES),
                 cache_t_spec(4 * LANES), cache_t_spec(2 * LANES), head_t_spec(KV_HEADS, 2 * SUBLANES),
                 head_spec(KV_HEADS), tile_t_spec, head_spec(KV_HEADS), tile_t_spec)
    return pl.pallas_call(
        functools.partial(_proj_body, tm=tm, tv=tv, tiles_per_seq=tps),
        out_shape=out_shape,
        grid=(T // tm,),
        in_specs=[row_spec(D), full(norm_w), full(w_pad), full(qnw), full(knw), full(gb),
                  tab_spec, tab_spec, tab_spec],
        out_specs=out_specs,
        compiler_params=_cparams(("parallel",)),
        name="proj",
    )(x2d, norm_w, w_pad, qnw, knw, gb, *tabs)


def _prep_params(p, l):
    w_in = p["w_in"][l]
    d_model, in_w = w_in.shape
    tile2 = lambda v: jnp.tile(v, (1, LANES // HEAD_DIM))
    return {
        "w_in": jnp.pad(w_in.astype(BF16), ((0, 0), (0, IN_W_PAD - in_w))),
        "norm_w": p["norm_w"][l].reshape(1, d_model).astype(F32),
        "qnw": tile2(p["q_norm_w"][l].reshape(1, HEAD_DIM)).astype(F32),
        "knw": tile2(p["k_norm_w"][l]).astype(F32),
        "gb": jnp.pad(p["gate_b"][l].reshape(1, -1).astype(F32), ((0, 0), (0, LANES - 3 * N_HEADS))),
        "w_glu": p["w_glu"][l].astype(BF16),
        "w_out": p["w_out"][l].astype(BF16),
    }


def _rope_tables(pos):
    half = ROPE_DIM // 2
    inv = ROPE_THETA ** (-jnp.arange(half, dtype=F32) / half)
    ang = pos.astype(F32)[:, None] * inv
    cos, sin = jnp.cos(ang), jnp.sin(ang)
    n = pos.shape[0]
    rest = HEAD_DIM - ROPE_DIM
    a = jnp.concatenate([cos, cos, jnp.ones((n, rest), F32)], axis=-1)
    b = jnp.concatenate([-sin, jnp.zeros((n, HEAD_DIM - half), F32)], axis=-1)
    c = jnp.concatenate([jnp.zeros((n, half), F32), sin, jnp.zeros((n, rest), F32)], axis=-1)
    return tuple(jnp.tile(t, (1, LANES // HEAD_DIM)) for t in (a, b, c))


def _outmix_body(x_ref, y_ref, gs_ref, o_ref, gn_ref, wg_ref, wo_ref, out_ref):
    y = jnp.concatenate([y_ref[j] for j in range(SSM_SLABS)], axis=-1)
    ab = _dot(y.astype(BF16), wg_ref[...])
    ssm = ab[:, :SSM_W] * _sigmoid(ab[:, SSM_W:]) * gs_ref[...].astype(F32)
    nsa = o_ref[...].astype(F32) * gn_ref[...].astype(F32)
    acc = _dot(ssm.astype(BF16), wo_ref[0:SSM_W, :])
    acc += _dot(nsa.astype(BF16), wo_ref[SSM_W:, :])
    out_ref[...] = x_ref[...] + acc


def _outmix_call(x2d, y_ssm, g_ssm, o_nsa, g_nsa, w_glu, w_out):
    T, D = x2d.shape
    tm = min(PROJ_TILE, T)
    row_spec = lambda w: pl.BlockSpec((tm, w), lambda i: (i, 0))
    full = lambda a: pl.BlockSpec(a.shape, lambda i: (0,) * a.ndim)
    return pl.pallas_call(
        _outmix_body,
        out_shape=jax.ShapeDtypeStruct((T, D), F32),
        grid=(T // tm,),
        in_specs=[row_spec(D), pl.BlockSpec((SSM_SLABS, tm, LANES), lambda i: (0, i, 0)),
                  row_spec(SSM_W), row_spec(NSA_W), row_spec(NSA_W),
                  full(w_glu), full(w_out)],
        out_specs=row_spec(D),
        compiler_params=_cparams(("parallel",)),
        name="outmix",
    )(x2d, y_ssm, g_ssm, o_nsa, g_nsa, w_glu, w_out)


def _gelu_tanh(x):
    c = math.sqrt(2.0 / math.pi)
    return 0.5 * x * (1.0 + jnp.tanh(c * (x + 0.044715 * (x * x * x))))


def _compress_rows(rows_refs, pe_ref, wa_ref, wb_ref, b1_ref, w2_ref, kc_ref, vc_ref, nch):
    lane = lax.broadcasted_iota(I32, (nch, LANES), 1)
    for kvi, out_ref in ((0, kc_ref), (1, vc_ref)):
        rows_ref = rows_refs[kvi]
        pa = jnp.zeros((nch, 2 * CMP_HID), F32)
        pb = jnp.zeros((nch, 2 * CMP_HID), F32)
        for j0 in range(0, CMP_STRIDE, 2):
            xs = [rows_ref[pl.ds(j, nch, stride=CMP_STRIDE), :] for j in (j0, j0 + 1)]
            xa = jnp.concatenate([xs[i] + pe_ref[kvi, 0, j0 + i:j0 + i + 1, :] for i in range(2)], axis=-1)
            xb = jnp.concatenate([xs[i] + pe_ref[kvi, 1, j0 + i:j0 + i + 1, :] for i in range(2)], axis=-1)
            wsl = slice(j0 * LANES, (j0 + 2) * LANES)
            pa += _dot(xa.astype(BF16), wa_ref[kvi, wsl, :])
            pb += _dot(xb.astype(BF16), wb_ref[kvi, wsl, :])
        hid = _gelu_tanh(pa + pltpu.roll(pb, nch - 1, 0) + b1_ref[kvi]).astype(BF16)
        for g in range(KV_HEADS):
            o = _dot(hid, w2_ref[kvi, g])
            if kvi == 1:
                o = jnp.where(lane == HEAD_DIM, 1.0, o).T
            out_ref[g] = o.astype(BF16)


def _compress_prompt_body(kvt_ref, pe_ref, wa_ref, wb_ref, b1_ref, w2_ref, kc_ref, vc_ref,
                          krows_ref, vrows_ref, *, nch):
    for c in range(kvt_ref.shape[1] // LANES):
        cs = slice(c * LANES, (c + 1) * LANES)
        krows_ref[cs, :] = kvt_ref[0:LANES, cs].T
        vrows_ref[cs, :] = kvt_ref[LANES:2 * LANES, cs].T
    _compress_rows((krows_ref, vrows_ref), pe_ref, wa_ref, wb_ref, b1_ref, w2_ref, kc_ref, vc_ref, nch)


def _compress_prompt_call(kv_t, cw, batch, seq):
    nch = seq // CMP_STRIDE
    full = lambda a: pl.BlockSpec(a.shape, lambda b: (0,) * a.ndim)
    out_spec = pl.BlockSpec((KV_HEADS, nch, LANES), lambda b: (0, b, 0))
    out_sds = jax.ShapeDtypeStruct((KV_HEADS, batch * nch, LANES), BF16)
    out_t_spec = pl.BlockSpec((KV_HEADS, None, LANES, nch), lambda b: (0, b, 0, 0))
    out_t_sds = jax.ShapeDtypeStruct((KV_HEADS, batch, LANES, nch), BF16)
    return pl.pallas_call(
        functools.partial(_compress_prompt_body, nch=nch),
        out_shape=(out_sds, out_t_sds),
        grid=(batch,),
        in_specs=[pl.BlockSpec((None, 2 * LANES, seq), lambda b: (b, 0, 0)),
                  full(cw["pe"]), full(cw["wa"]), full(cw["wb"]), full(cw["b1"]), full(cw["w2"])],
        out_specs=(out_spec, out_t_spec),
        scratch_shapes=[pltpu.VMEM((seq, LANES), F32)] * 2,
        compiler_params=_cparams(("parallel",)),
        name="compress_prompt",
    )(kv_t, cw["pe"], cw["wa"], cw["wb"], cw["b1"], cw["w2"])


def _prep_compress(p, l):
    assert KV_HEADS == 2
    w1 = p["cmp_w1"][l].reshape(2, 2, CMP_STRIDE, HEAD_DIM, CMP_HID).astype(BF16)
    z1 = jnp.zeros_like(w1)
    wexp = jnp.stack([jnp.concatenate([w1, z1], axis=-1), jnp.concatenate([z1, w1], axis=-1)], axis=3)
    wexp = wexp.reshape(2, 2, CMP_STRIDE * LANES, KV_HEADS * CMP_HID)
    pe = p["cmp_pe"][l].reshape(2, 2, CMP_STRIDE, HEAD_DIM)
    w2 = p["cmp_w2"][l]
    zeros = jnp.zeros_like(w2[0])
    w2k = jnp.concatenate([zeros, w2[0]], axis=-1)
    w2v = jnp.concatenate([w2[1], zeros], axis=-1)
    z2 = jnp.zeros_like(w2k)
    w2e = jnp.stack([jnp.stack([jnp.concatenate([w, z2], axis=0), jnp.concatenate([z2, w], axis=0)])
                     for w in (w2k, w2v)])
    return {
        "pe": jnp.tile(pe, (1, 1, 1, KV_HEADS)).astype(F32),
        "wa": wexp[:, 0],
        "wb": wexp[:, 1],
        "b1": jnp.tile(p["cmp_b1"][l].reshape(2, 1, CMP_HID), (1, 1, KV_HEADS)).astype(F32),
        "w2": w2e.astype(BF16),
    }


def _overlap_matrix(n_tok_pad, n_tok, n_blk, n_cols=LANES):
    c_start = jnp.arange(n_tok_pad) * CMP_STRIDE
    blk = jnp.arange(n_cols)
    ov = ((c_start[:, None] < (blk[None, :] + 1) * SLC_BLOCK)
          & (c_start[:, None] + CMP_BLOCK > blk[None, :] * SLC_BLOCK)
          & (jnp.arange(n_tok_pad)[:, None] < n_tok) & (blk[None, :] < n_blk))
    return ov.astype(BF16)


def _topk_select_t(w_ref, imp_t, q0, tq):
    nb = imp_t.shape[0]
    n_i = lax.broadcasted_iota(I32, (nb, tq), 0)
    qblk = lax.shift_right_logical(q0 + lax.broadcasted_iota(I32, (nb, tq), 1), SLC_SHIFT)
    causal = n_i <= qblk
    forced = (n_i == 0) | (n_i >= qblk - (N_LOCAL_BLOCKS - 1))
    w_ref[...] = jnp.where(causal, jnp.where(forced, jnp.inf, imp_t), -jnp.inf)
    last_blk = lax.shift_right_logical(q0 + tq - 1, SLC_SHIFT)
    n_grp = nb // SUBLANES
    rank = [jnp.zeros((SUBLANES, tq), F32) for _ in range(n_grp)]
    grp_i = lax.broadcasted_iota(I32, (SUBLANES, tq), 0)

    def count_group(mg, rank):
        rank = list(rank)
        for mi in range(SUBLANES):
            m = mg * SUBLANES + mi
            wm = w_ref[m:m + 1, :]
            for ng in range(n_grp):
                w = w_ref[ng * SUBLANES:(ng + 1) * SUBLANES, :]
                if ng > mg:
                    beats = jnp.where(wm >= w, 1.0, 0.0)
                elif ng < mg:
                    beats = jnp.where(wm > w, 1.0, 0.0)
                else:
                    beats = jnp.where(grp_i > mi, jnp.where(wm >= w, 1.0, 0.0), jnp.where(wm > w, 1.0, 0.0))
                rank[ng] = rank[ng] + beats
        return tuple(rank)

    rank = tuple(rank)
    for mg in range(n_grp):
        rank = lax.cond(mg * SUBLANES <= last_blk, functools.partial(count_group, mg), lambda r: r, rank)
    return causal & (jnp.concatenate(rank, axis=0) < TOP_N)


def _flash_tiles_t(tiles, q_ts, ms, accs):
    def scores(i):
        k, _, mask = tiles[i]
        s = [_dot(k, q_t) for q_t in q_ts]
        return s if mask is None else [jnp.where(mask, x, NEG_INF) for x in s]

    ahead = 2
    pending = {i: scores(i) for i in range(min(ahead, len(tiles)))}
    for i, (_, v_t, _) in enumerate(tiles):
        s = pending.pop(i)
        m_new = [jnp.maximum(m, jnp.max(x, axis=0, keepdims=True)) for m, x in zip(ms, s)]
        alpha = [jnp.exp2(m - mn) for m, mn in zip(ms, m_new)]
        p = [jnp.exp2(x - mn).astype(BF16) for x, mn in zip(s, m_new)]
        pv = [_dot(v_t, x) for x in p]
        if i + ahead < len(tiles):
            pending[i + ahead] = scores(i + ahead)
        accs = [a * acc + x for a, acc, x in zip(alpha, accs, pv)]
        ms = m_new
    return tuple(ms), tuple(accs)


def _attn_body(qt_ref, kc_ref, vct_ref, ovlt_ref, ks_ref, vst_ref, kw_ref, vwt_ref, gate_ref, o_ref,
               qa_ref, ocmp_ref, w_ref, *, tq, ncp):
    R = Q_PER_KV
    qt = pl.program_id(2)
    q0 = qt * tq
    nbs = LANES // 2

    def compressed(n_c):
        c_i = lax.broadcasted_iota(I32, (n_c, tq), 0)
        qpos_c = q0 + lax.broadcasted_iota(I32, (n_c, tq), 1)
        cmask = c_i * CMP_STRIDE + (CMP_BLOCK - 1) <= qpos_c
        kc = kc_ref[0:n_c, :]
        s = [jnp.where(cmask, _dot(kc, qt_ref[r]), NEG_INF) for r in range(R)]
        e = [jnp.where(cmask, jnp.exp2(x - jnp.max(x, axis=0, keepdims=True)), 0.0) for x in s]
        l = [jnp.sum(x, axis=0, keepdims=True) for x in e]
        p = [(x * (1.0 / jnp.where(y > 0.0, y, 1.0))).astype(BF16) for x, y in zip(e, l)]
        for r in range(R):
            ocmp_ref[r] = _dot(vct_ref[:, 0:n_c], p[r])
        return sum(_dot(ovlt_ref[:, 0:n_c], x) for x in p)

    half = ncp // 2
    if half % LANES == 0:
        imp = lax.cond((q0 + tq) // CMP_STRIDE <= half, lambda: compressed(half), lambda: compressed(ncp))
    else:
        imp = compressed(ncp)

    sel = _topk_select_t(w_ref, imp[:nbs], q0, tq)
    bias = jnp.concatenate([jnp.where(sel, 0.0, NEG_INF), jnp.zeros((LANES - nbs, tq), F32)], axis=0)
    for r in range(R):
        qa_ref[r] = (qt_ref[r].astype(F32) + bias).astype(BF16)

    key_i = lax.broadcasted_iota(I32, (tq, tq), 0)
    qry_i = lax.broadcasted_iota(I32, (tq, tq), 1)
    init = (tuple(jnp.full((1, tq), -jnp.inf, F32) for _ in range(R)),
            tuple(jnp.zeros((V_ROWS, tq), F32) for _ in range(R)))

    def tiles(k_ref, vt_ref, q_ref, js, masks, state):
        ts = [(k_ref[pl.ds(pl.multiple_of(j * tq, tq), tq), :], vt_ref[j], mask) for j, mask in zip(js, masks)]
        return _flash_tiles_t(ts, [q_ref[r] for r in range(R)], *state)

    def last_tiles(k_ref, vt_ref, q_ref, n, first_mask, state):
        js = [qt - (n - 1 - t) for t in range(n)]
        masks = [first_mask] + [None] * (n - 2) + [key_i <= qry_i] if n > 1 else [key_i <= qry_i]
        return tiles(k_ref, vt_ref, q_ref, js, masks, state)

    GROUP = 4

    def slc_group(i, st):
        return tiles(ks_ref, vst_ref, qa_ref, [GROUP * i + t for t in range(GROUP)], [None] * GROUP, st)

    state = lax.fori_loop(0, qt // GROUP, slc_group, init)
    tails = [functools.partial(last_tiles, ks_ref, vst_ref, qa_ref, n, None) for n in range(1, GROUP + 1)]
    _, acc_s = lax.switch(qt % GROUP, tails, state)

    nwin = WINDOW // tq
    wins = [functools.partial(last_tiles, kw_ref, vwt_ref, qt_ref, n, (key_i > qry_i) if n == nwin + 1 else None)
            for n in range(1, nwin + 2)]
    _, acc_w = lax.switch(jnp.minimum(qt, nwin), wins, init)

    outs = []
    for r in range(R):
        a_s = acc_s[r]
        a_w = acc_w[r]
        g = lambda k: gate_ref[3 * r + k:3 * r + k + 1, :]
        head = lambda a: a[:HEAD_DIM] * (1.0 / a[HEAD_DIM:HEAD_DIM + 1, :])
        outs.append(g(0) * ocmp_ref[r][:HEAD_DIM] + g(1) * head(a_s) + g(2) * head(a_w))
    o_ref[...] = jnp.concatenate(outs, axis=0).T.astype(o_ref.dtype)


def _attn_call(q_t, kc, vc_t, ovl_t, ks, vs_t, kw, vw_t, gates_t, batch, seq):
    tq = min(ATTN_TILE, seq)
    nq = seq // tq
    ncp = kc.shape[1] // batch
    R = Q_PER_KV
    assert seq // SLC_BLOCK <= LANES // 2 and WINDOW % tq == 0
    k_spec = pl.BlockSpec((None, seq, LANES), lambda b, g, t: (g, b, 0))
    vt_spec = pl.BlockSpec((None, nq, V_ROWS, tq), lambda b, g, t: (g, b, 0, 0))
    acc = pltpu.VMEM((R, LANES, tq), F32)
    return pl.pallas_call(
        functools.partial(_attn_body, tq=tq, ncp=ncp),
        out_shape=jax.ShapeDtypeStruct((batch * seq, NSA_W), BF16),
        grid=(batch, KV_HEADS, nq),
        in_specs=[pl.BlockSpec((R, LANES, tq), lambda b, g, t: (g, 0, b * nq + t)),
                  pl.BlockSpec((None, ncp, LANES), lambda b, g, t: (g, b, 0)),
                  pl.BlockSpec((None, None, LANES, ncp), lambda b, g, t: (g, b, 0, 0)),
                  pl.BlockSpec(ovl_t.shape, lambda b, g, t: (0, 0)),
                  k_spec, vt_spec, k_spec, vt_spec,
                  pl.BlockSpec((None, 2 * SUBLANES, tq), lambda b, g, t: (g, 0, b * nq + t))],
        out_specs=pl.BlockSpec((tq, R * HEAD_DIM), lambda b, g, t: (b * nq + t, g)),
        scratch_shapes=[pltpu.VMEM((R, LANES, tq), BF16), acc, pltpu.VMEM((LANES // 2, tq), F32)],
        compiler_params=_cparams(("parallel", "parallel", "arbitrary")),
        name="nsa_prompt",
    )(q_t, kc, vc_t, ovl_t, ks, vs_t, kw, vw_t, gates_t)


def _s5_discretise(p, l):
    lr = p["ssm_lam_re"][l].astype(F32)
    li = p["ssm_lam_im"][l].astype(F32)
    dt = jnp.exp(p["ssm_log_step"][l].astype(F32))[:, None]

    def apow(t):
        mag, ang = jnp.exp(lr * dt * t), li * dt * t
        return mag * jnp.cos(ang), mag * jnp.sin(ang)

    a_re, a_im = apow(1.0)
    den = lr * lr + li * li
    nr, ni = a_re - 1.0, a_im
    f_re, f_im = (nr * lr + ni * li) / den, (ni * lr - nr * li) / den
    br, bi = p["ssm_b_re"][l].astype(F32), p["ssm_b_im"][l].astype(F32)
    bb_re = f_re[..., None] * br - f_im[..., None] * bi
    bb_im = f_re[..., None] * bi + f_im[..., None] * br
    return apow, bb_re, bb_im


def _s5_expand_body(pst_ref, kt_ref, cp_ref, wcol_ref, wst_ref, wout_ref):
    T, E, N, C = SSM_CHUNK, SSM_SLAB_GROUPS, SSM_STATE, SSM_GROUP
    row = lax.broadcasted_iota(I32, (LANES, LANES), 0)
    lane = lax.broadcasted_iota(I32, (LANES, LANES), 1)
    row_grp = row // C
    low = lane < N
    own_grp = row_grp == lane // C
    lane_n = lax.broadcasted_iota(I32, (N, LANES), 1)
    wcol_ref[T * LANES:(T + 1) * LANES, :] = jnp.zeros((LANES, LANES), BF16)
    wout_ref[0, :, LANES:2 * LANES] = jnp.zeros((2 * E * N, LANES), BF16)
    for i in range(T):
        wcol_ref[i * LANES:(i + 1) * LANES, :] = jnp.where(own_grp, kt_ref[T - 1 - i], 0.0).astype(BF16)
        x = pst_ref[T - 1 - i]
        x_sw = pltpu.roll(x, N, 1)
        halves = (jnp.where(low, x, x_sw), jnp.where(low, x_sw, x))
        for ri, x2 in enumerate(halves):
            for q in range(E // 2):
                own = row_grp == 2 * q + jnp.where(low, 0, 1)
                c0 = ri * E * N + q * LANES
                wst_ref[i * LANES:(i + 1) * LANES, c0:c0 + LANES] = jnp.where(own, x2, 0.0).astype(BF16)
    for i in range(T + 1):
        pair, half = (0, 0) if i == 0 else ((i + 1) // 2, (i - 1) % 2)
        m_t = cp_ref[i].T
        for ri in range(2):
            rows = m_t[ri * N:(ri + 1) * N]
            for g in range(E):
                r0 = ri * E * N + g * N
                wout_ref[pair, r0:r0 + N, half * LANES:(half + 1) * LANES] = (
                    jnp.where(lane_n // C == g, rows, 0.0).astype(BF16))


def _prep_s5_chunked(p, l):
    hi = lax.Precision.HIGHEST
    apow, bb_re, bb_im = _s5_discretise(p, l)
    c_re, c_im = p["ssm_c_re"][l].astype(F32), p["ssm_c_im"][l].astype(F32)
    T, J, E, N, C = SSM_CHUNK, SSM_SLABS, SSM_SLAB_GROUPS, SSM_STATE, SSM_GROUP
    pw_re, pw_im = apow(jnp.arange(T + 1, dtype=F32)[:, None, None])
    bt_re, bt_im = bb_re.transpose(0, 2, 1), bb_im.transpose(0, 2, 1)
    p_re = pw_re[:, :, None, :] * bt_re - pw_im[:, :, None, :] * bt_im
    p_im = pw_re[:, :, None, :] * bt_im + pw_im[:, :, None, :] * bt_re
    kt = (jnp.einsum("tgkn,gcn->tgkc", p_re[:T], jnp.tile(c_re, (1, E, 1)), precision=hi)
          - jnp.einsum("tgkn,gcn->tgkc", p_im[:T], jnp.tile(c_im, (1, E, 1)), precision=hi))
    cp_re = c_re * pw_re[:, :, None, :] - c_im * pw_im[:, :, None, :]
    cp_im = c_re * pw_im[:, :, None, :] + c_im * pw_re[:, :, None, :]
    slab = lambda x: jnp.moveaxis(x.reshape(x.shape[0], J, E * C, LANES), 1, 0)
    pst = slab(jnp.concatenate([p_re[:T], p_im[:T]], axis=-1))
    ktile = slab(kt)
    cpn = slab(jnp.concatenate([cp_re, -cp_im], axis=-1))
    blk = lambda n: pl.BlockSpec((None, n, LANES, LANES), lambda j: (j, 0, 0, 0))
    w_col, w_st, w_out = pl.pallas_call(
        _s5_expand_body,
        out_shape=(jax.ShapeDtypeStruct((J, (T + 1) * LANES, LANES), BF16),
                   jax.ShapeDtypeStruct((J, T * LANES, 2 * E * N), BF16),
                   jax.ShapeDtypeStruct((J, T // 2 + 1, 2 * E * N, 2 * LANES), BF16)),
        grid=(J,),
        in_specs=[blk(T), blk(T), blk(T + 1)],
        out_specs=(pl.BlockSpec((None, (T + 1) * LANES, LANES), lambda j: (j, 0, 0)),
                   pl.BlockSpec((None, T * LANES, 2 * E * N), lambda j: (j, 0, 0)),
                   pl.BlockSpec((None, T // 2 + 1, 2 * E * N, 2 * LANES), lambda j: (j, 0, 0, 0))),
        compiler_params=_cparams(("parallel",)),
        name="s5_expand",
    )(pst, ktile, cpn)
    return {
        "w_col": w_col, "w_st": w_st, "w_out": w_out,
        "a_re": pw_re[T].reshape(J, 1, E * N), "a_im": pw_im[T].reshape(J, 1, E * N),
        "a1_re": pw_re[1].reshape(J, 1, E * N), "a1_im": pw_im[1].reshape(J, 1, E * N),
        "d": p["ssm_d"][l].reshape(J, 1, LANES).astype(F32),
    }


def _s5_prompt_body(u_ref, wcol_ref, wst_ref, are_ref, aim_ref, wout_ref, d_ref,
                    y_ref, hre_ref, him_ref, xs_ref, hp_ref, *, n_chunks):
    T = SSM_CHUNK
    ns = SSM_SLAB_GROUPS * SSM_STATE
    u_pos = [u_ref[pl.ds(s, n_chunks, stride=T), :] for s in range(T)]
    ub = jnp.concatenate(u_pos, axis=-1).astype(BF16)
    xs_ref[...] = _dot(ub, wst_ref[...])
    a_re = are_ref[...]
    a_im = aim_ref[...]

    def step(c, carry):
        hr, hi = carry
        hp_ref[pl.ds(c, 1), 0:ns] = hr
        hp_ref[pl.ds(c, 1), ns:2 * ns] = hi
        xr = xs_ref[pl.ds(c, 1), 0:ns]
        xi = xs_ref[pl.ds(c, 1), ns:2 * ns]
        return a_re * hr - a_im * hi + xr, a_re * hi + a_im * hr + xi

    zero = jnp.zeros((1, ns), F32)
    hr, hi = lax.fori_loop(0, n_chunks, step, (zero, zero))
    hre_ref[...] = jnp.broadcast_to(hr, hre_ref.shape)
    him_ref[...] = jnp.broadcast_to(hi, him_ref.shape)
    hpb = hp_ref[...].astype(BF16)
    for t in range(0, T, 2):
        k_rows = (t + 2) * LANES
        w_pair = jnp.concatenate([wcol_ref[(T - 1 - t) * LANES:(T - 1 - t) * LANES + k_rows, :],
                                  wcol_ref[(T - 2 - t) * LANES:T * LANES, :]], axis=1)
        y_pair = _dot(ub[:, :k_rows], w_pair) + _dot(hpb, wout_ref[t // 2 + 1])
        for i in range(2):
            y_ref[pl.ds(t + i, n_chunks, stride=T), :] = (
                y_pair[:, i * LANES:(i + 1) * LANES] + d_ref[...] * u_pos[t + i])


def _s5_prompt_call(u_slab, sw, batch, seq):
    T, J = SSM_CHUNK, SSM_SLABS
    n_chunks = seq // T
    ns = SSM_SLAB_GROUPS * SSM_STATE
    row_spec = pl.BlockSpec((None, seq, LANES), lambda j, b: (j, b, 0))
    slab_spec = lambda a: pl.BlockSpec((None,) + a.shape[1:], lambda j, b: (j,) + (0,) * (a.ndim - 1))
    st_spec = pl.BlockSpec((None, None, SUBLANES, ns), lambda j, b: (b, j, 0, 0))
    st_sds = jax.ShapeDtypeStruct((batch, J, SUBLANES, ns), F32)
    y, hre, him = pl.pallas_call(
        functools.partial(_s5_prompt_body, n_chunks=n_chunks),
        out_shape=(jax.ShapeDtypeStruct(u_slab.shape, F32), st_sds, st_sds),
        grid=(J, batch),
        in_specs=[row_spec, slab_spec(sw["w_col"]), slab_spec(sw["w_st"]), slab_spec(sw["a_re"]),
                  slab_spec(sw["a_im"]), slab_spec(sw["w_out"]), slab_spec(sw["d"])],
        out_specs=(row_spec, st_spec, st_spec),
        scratch_shapes=[pltpu.VMEM((n_chunks, 2 * ns), F32), pltpu.VMEM((n_chunks, 2 * ns), F32)],
        compiler_params=_cparams(("parallel", "parallel")),
        name="s5_prompt",
    )(u_slab, sw["w_col"], sw["w_st"], sw["a_re"], sw["a_im"], sw["w_out"], sw["d"])
    n_groups = J * SSM_SLAB_GROUPS
    state = lambda h: h[:, :, 0, :].reshape(batch, n_groups, SSM_STATE)
    return y, state(hre), state(him)


def _s5_step_body(u_ref, wx_ref, are_ref, aim_ref, h0re_ref, h0im_ref, wy_ref, d_ref,
                  y_ref, hre_ref, him_ref):
    ns = SSM_SLAB_GROUPS * SSM_STATE
    for j in range(SSM_SLABS):
        sl = slice(j * ns, (j + 1) * ns)
        u = u_ref[j]
        x = _dot(u.astype(BF16), wx_ref[j])
        a_re, a_im = are_ref[j], aim_ref[j]
        h0r, h0i = h0re_ref[:, sl], h0im_ref[:, sl]
        hr = a_re * h0r - a_im * h0i + x[:, :ns]
        hi = a_re * h0i + a_im * h0r + x[:, ns:]
        hre_ref[:, sl] = hr
        him_ref[:, sl] = hi
        y_ref[j] = _dot(jnp.concatenate([hr, hi], axis=-1).astype(BF16), wy_ref[j]) + d_ref[j] * u


def _s5_step_call(u_slab, h0_re, h0_im, sw):
    J, n_tok, _ = u_slab.shape
    T = SSM_CHUNK
    full = lambda a: pl.BlockSpec(a.shape, lambda i: (0,) * a.ndim)
    wx_spec = pl.BlockSpec((J, LANES, sw["w_st"].shape[2]), lambda i: (0, T - 1, 0))
    wy_spec = pl.BlockSpec((J, None, sw["w_out"].shape[2], LANES), lambda i: (0, 0, 0, 0))
    st_sds = jax.ShapeDtypeStruct(h0_re.shape, F32)
    return pl.pallas_call(
        _s5_step_body,
        out_shape=(jax.ShapeDtypeStruct(u_slab.shape, F32), st_sds, st_sds),
        grid=(1,),
        in_specs=[full(u_slab), wx_spec, full(sw["a1_re"]), full(sw["a1_im"]), full(h0_re), full(h0_im),
                  wy_spec, full(sw["d"])],
        out_specs=(full(u_slab), full(h0_re), full(h0_re)),
        compiler_params=_cparams(("arbitrary",)),
        name="s5_step",
    )(u_slab, sw["w_st"], sw["a1_re"], sw["a1_im"], h0_re, h0_im, sw["w_out"], sw["d"])


def _compress_sample_body(pt_ref, *refs, n_pages, page_rows, nch):
    del pt_ref
    page_refs = refs[:n_pages]
    pe_ref, wa_ref, wb_ref, b1_ref, w2_ref, kc_ref, vc_ref = refs[n_pages:n_pages + 7]
    buffers = refs[n_pages + 7:]
    b = pl.program_id(0)

    @pl.when(b == 0)
    def _():
        for buf in buffers[2:]:
            buf[...] = jnp.zeros(buf.shape, F32)

    def step(stage, done):
        for i, page_ref in enumerate(page_refs):
            for rows_ref, slot in zip(stage, range(2)):
                rows_ref[i * page_rows:(i + 1) * page_rows, :] = page_ref[slot].reshape(LANES, page_rows).T
        _compress_rows(done, pe_ref, wa_ref, wb_ref, b1_ref, w2_ref, kc_ref, vc_ref, nch)

    @pl.when(b % 2 == 0)
    def _():
        step(buffers[:2], buffers[2:])

    @pl.when(b % 2 == 1)
    def _():
        step(buffers[2:], buffers[:2])


def _compress_sample_call(cache, page_table, cw):
    n_seq, n_pages = page_table.shape
    page_rows = cache.shape[-1]
    assert page_rows == LANES
    nch = n_pages * page_rows // CMP_STRIDE
    staged = lambda b: jnp.minimum(b, n_seq - 1)
    page_spec = lambda i: pl.BlockSpec((None, 2, KV_HEADS, HEAD_DIM, page_rows),
                                       lambda b, pt: (pt[staged(b), i], 0, 0, 0, 0))
    full = lambda a: pl.BlockSpec(a.shape, lambda b, pt: (0,) * a.ndim)
    out_of = lambda b: jnp.maximum(b - 1, 0)
    out_spec = pl.BlockSpec((KV_HEADS, nch, LANES), lambda b, pt: (0, out_of(b), 0))
    out_sds = jax.ShapeDtypeStruct((KV_HEADS, n_seq * nch, LANES), BF16)
    out_t_spec = pl.BlockSpec((KV_HEADS, None, LANES, nch), lambda b, pt: (0, out_of(b), 0, 0))
    out_t_sds = jax.ShapeDtypeStruct((KV_HEADS, n_seq, LANES, nch), BF16)
    weights = (cw["pe"], cw["wa"], cw["wb"], cw["b1"], cw["w2"])
    return pl.pallas_call(
        functools.partial(_compress_sample_body, n_pages=n_pages, page_rows=page_rows, nch=nch),
        out_shape=(out_sds, out_t_sds),
        grid_spec=pltpu.PrefetchScalarGridSpec(
            num_scalar_prefetch=1,
            grid=(n_seq + 1,),
            in_specs=[page_spec(i) for i in range(n_pages)] + [full(a) for a in weights],
            out_specs=(out_spec, out_t_spec),
            scratch_shapes=[pltpu.VMEM((n_pages * page_rows, LANES), F32)] * 4),
        compiler_params=_cparams(("arbitrary",)),
        name="compress_sample",
    )(page_table, *([cache] * n_pages), *weights)


def _group_rows(x0, x1):
    row = lax.broadcasted_iota(I32, x0.shape, 0)
    return jnp.where(row < Q_PER_KV, x0, x1)


def _sample_select_body(q_ref, kc_ref, vct_ref, ovl_ref, tri_ref, ocmp_ref, idx_ref, *, ncp, qpos, nbp, n_sub):
    c_i = lax.broadcasted_iota(I32, (N_HEADS, ncp), 1)
    cmask = c_i * CMP_STRIDE + (CMP_BLOCK - 1) <= qpos
    n_row = lax.broadcasted_iota(I32, (1, nbp), 1)
    qblk = qpos // SLC_BLOCK
    causal = n_row <= qblk
    forced = (n_row == 0) | (n_row >= qblk - (N_LOCAL_BLOCKS - 1))
    m_i = lax.broadcasted_iota(I32, (nbp, nbp), 0)
    n_i = lax.broadcasted_iota(I32, (nbp, nbp), 1)
    lane = lax.broadcasted_iota(I32, (1, LANES), 1)
    for i in range(n_sub):
        q8 = q_ref[i]
        rows = slice(i * ncp, (i + 1) * ncp)
        s = _group_rows(_dot_nt(q8, kc_ref[0, rows, :]), _dot_nt(q8, kc_ref[1, rows, :]))
        s = jnp.where(cmask, s, NEG_INF)
        e = jnp.where(cmask, jnp.exp2(s - jnp.max(s, axis=-1, keepdims=True)), 0.0)
        l = jnp.sum(e, axis=-1, keepdims=True)
        p = (e / jnp.where(l > 0.0, l, 1.0)).astype(BF16)
        ocmp_ref[i] = _group_rows(_dot_nt(p, vct_ref[0, i]), _dot_nt(p, vct_ref[1, i]))
        imp8 = _dot(p, ovl_ref[...])
        idx_rows = []
        for g in range(KV_HEADS):
            imp = jnp.sum(imp8[g * Q_PER_KV:(g + 1) * Q_PER_KV], axis=0, keepdims=True)
            w = jnp.where(causal, jnp.where(forced, jnp.inf, imp), -jnp.inf)
            w_sq = jnp.broadcast_to(w, (nbp, nbp))
            w_col = w_sq.T
            beats = jnp.where(n_i > m_i, jnp.where(w_col >= w_sq, 1.0, 0.0), jnp.where(w_col > w_sq, 1.0, 0.0))
            rank = jnp.sum(beats, axis=0, keepdims=True)
            sel = causal & (rank < TOP_N)
            self_f = jnp.where(sel, 1.0, 0.0)
            before = _dot(self_f.astype(BF16), tri_ref[...])
            idx = jnp.full((1, LANES), -1, I32)
            for k in range(TOP_N):
                hit = sel & (before == float(k))
                val = jnp.sum(jnp.where(hit, n_row.astype(F32) + 1.0, 0.0), axis=-1, keepdims=True) - 1.0
                idx = jnp.where(lane == k, val.astype(I32), idx)
            idx_rows.append(idx)
        idx_ref[i] = jnp.concatenate(idx_rows + [jnp.full((SUBLANES - KV_HEADS, LANES), -1, I32)], axis=0)


def _sample_select_call(q8, kc, vc_t, ovl, tri, qpos):
    n_seq = q8.shape[0]
    ncp = kc.shape[1] // n_seq
    nbp = ovl.shape[1]
    n_sub = math.gcd(n_seq, 4)
    cmp_spec = pl.BlockSpec((KV_HEADS, n_sub * ncp, LANES), lambda b: (0, b, 0))
    row_spec = pl.BlockSpec((n_sub, N_HEADS, LANES), lambda b: (b, 0, 0))
    full = lambda a: pl.BlockSpec(a.shape, lambda b: (0,) * a.ndim)
    return pl.pallas_call(
        functools.partial(_sample_select_body, ncp=ncp, qpos=qpos, nbp=nbp, n_sub=n_sub),
        out_shape=(jax.ShapeDtypeStruct((n_seq, N_HEADS, LANES), F32),
                   jax.ShapeDtypeStruct((n_seq, SUBLANES, LANES), I32)),
        grid=(n_seq // n_sub,),
        in_specs=[row_spec, cmp_spec, pl.BlockSpec((KV_HEADS, n_sub, LANES, ncp), lambda b: (0, b, 0, 0)),
                  full(ovl), full(tri)],
        out_specs=(row_spec, pl.BlockSpec((n_sub, SUBLANES, LANES), lambda b: (b, 0, 0))),
        compiler_params=_cparams(("parallel",)),
        name="nsa_sample_select",
    )(q8, kc, vc_t, ovl, tri)


def _sample_attend_body(idx_ref, pt_ref, q_ref, ocmp_ref, gate_ref, ksn_ref, vsn_ref, kwn_ref, vwn_ref,
                        win_ref, *refs, n_cache_blocks, blocks_per_page, win_skip):
    del pt_ref
    n_blk = KV_HEADS * TOP_N
    kv_refs, o_ref = refs[:n_blk], refs[n_blk]
    b = pl.program_id(0)
    q = q_ref[...]
    qf = q.astype(F32)
    row_g = (lax.broadcasted_iota(I32, (N_HEADS, 1), 0) >= Q_PER_KV).astype(I32)

    def attend(s_list, v_list, kn_ref, vn_ref):
        s_self = jnp.sum(qf * kn_ref[...].astype(F32), axis=-1, keepdims=True)
        m = s_self
        for s in s_list:
            m = jnp.maximum(m, jnp.max(s, axis=-1, keepdims=True))
        p_self = jnp.exp2(s_self - m)
        l = p_self
        acc = p_self.astype(BF16).astype(F32) * vn_ref[...].astype(F32)
        for s, v in zip(s_list, v_list):
            p = jnp.exp2(s - m)
            l = l + jnp.sum(p, axis=-1, keepdims=True)
            acc = acc + _dot_nt(p.astype(BF16), v().astype(BF16))
        return acc / l

    s_list, v_list = [], []
    for j in range(n_blk):
        s = _dot(q, kv_refs[j][0].astype(BF16))
        col = lax.broadcasted_iota(I32, s.shape, 1)
        n = idx_ref[b, j]
        first = (n & (blocks_per_page - 1)) * SLC_BLOCK
        ok = ((row_g == j // TOP_N) & (col >= first) & (col < first + SLC_BLOCK)
              & (n >= 0) & (n < n_cache_blocks))
        s_list.append(jnp.where(ok, s, NEG_INF))
        v_list.append(lambda j=j: kv_refs[j][1])
    o_slc = attend(s_list, v_list, ksn_ref, vsn_ref)

    s_list, v_list = [], []
    for g in range(KV_HEADS):
        s = _dot(q, win_ref[0, g].astype(BF16))
        col = lax.broadcasted_iota(I32, s.shape, 1)
        s_list.append(jnp.where((row_g == g) & (col >= win_skip), s, NEG_INF))
        v_list.append(lambda g=g: win_ref[1, g])
    o_win = attend(s_list, v_list, kwn_ref, vwn_ref)

    gates = gate_ref[...]
    o_ref[...] = gates[:, 0:1] * ocmp_ref[:, 0:HEAD_DIM] + gates[:, 1:2] * o_slc + gates[:, 2:3] * o_win


def _sample_attend_call(idx, page_table, q64, ocmp, gates8, ksn, vsn, kwn, vwn, cache_win_t, cache_t, win_skip):
    n_seq, n_pages = page_table.shape
    page_rows = cache_t.shape[-1]
    bpp = page_rows // SLC_BLOCK
    n_cache_blocks = n_pages * bpp
    row_spec = lambda a: pl.BlockSpec((None,) + a.shape[1:], lambda b, ix, pt: (b,) + (0,) * (a.ndim - 1))

    assert bpp & (bpp - 1) == 0
    bpp_shift = bpp.bit_length() - 1

    def blk_spec(j):
        def index_map(b, ix, pt):
            n = jnp.minimum(jnp.maximum(ix[b, j], 0), n_cache_blocks - 1)
            return (pt[b, lax.shift_right_logical(n, bpp_shift)], 1, j // TOP_N, 0, 0)
        return pl.BlockSpec((None, 2, None, HEAD_DIM, page_rows), index_map)

    n_blk = KV_HEADS * TOP_N
    small = (q64, ocmp, gates8, ksn, vsn, kwn, vwn, cache_win_t)
    return pl.pallas_call(
        functools.partial(_sample_attend_body, n_cache_blocks=n_cache_blocks, blocks_per_page=bpp,
                          win_skip=win_skip),
        out_shape=jax.ShapeDtypeStruct((n_seq, N_HEADS, HEAD_DIM), F32),
        grid_spec=pltpu.PrefetchScalarGridSpec(
            num_scalar_prefetch=2,
            grid=(n_seq,),
            in_specs=[row_spec(a) for a in small] + [blk_spec(j) for j in range(n_blk)],
            out_specs=pl.BlockSpec((None, N_HEADS, HEAD_DIM), lambda b, ix, pt: (b, 0, 0))),
        compiler_params=_cparams(("parallel",)),
        name="nsa_sample_attend",
    )(idx, page_table, *small, *([cache_t] * n_blk))


def _round_up(x, m):
    return -(-x // m) * m


def _prompt_layer(h, prm, cw, sw, batch, seq):
    tabs = _rope_tables(jnp.arange(seq))
    u, gs, gn, q_t, kv_t, win_t, gates_t, ks, vs_t, kw, vw_t = _proj_call(
        h, prm["w_in"], prm["norm_w"], prm["qnw"], prm["knw"], prm["gb"], tabs, batch, seq)
    y_ssm, h_re, h_im = _s5_prompt_call(u, sw, batch, seq)
    kc, vc_t = _compress_prompt_call(kv_t, cw, batch, seq)
    nch = seq // CMP_STRIDE
    ovl_t = _overlap_matrix(nch, nch - 1, seq // SLC_BLOCK).T
    o = _attn_call(q_t, kc, vc_t, ovl_t, ks, vs_t, kw, vw_t, gates_t, batch, seq)
    h_new = _outmix_call(h, y_ssm, gs, o, gn, prm["w_glu"], prm["w_out"])
    rows = lambda x_t, slots: x_t.reshape(batch, slots, KV_HEADS, HEAD_DIM, seq).transpose(0, 4, 1, 2, 3)
    return h_new, rows(kv_t, 4), rows(win_t, 2)[:, seq - min(WINDOW, seq):], h_re, h_im


def _sample_layer(h, prm, cw, sw, cache_kv, cache_win, st_re, st_im, page_table):
    n_seq = h.shape[0]
    n_phys, page_rows = cache_kv.shape[:2]
    n_pages = page_table.shape[1]
    past_len = n_pages * page_rows
    win_buf = cache_win.shape[1]
    n_pad = _round_up(n_seq, LANES)
    tabs = _rope_tables(jnp.full((n_pad,), past_len, I32))
    h_pad = jnp.pad(h, ((0, n_pad - n_seq), (0, 0)))
    u, gs, gn, q_t, kv_t, win_t, gates_t, ks, vs_t, kw, vw_t = _proj_call(
        h_pad, prm["w_in"], prm["norm_w"], prm["qnw"], prm["knw"], prm["gb"], tabs, 1, n_pad)
    u, gs, gn = u[:, :n_seq], gs[:n_seq], gn[:n_seq]
    n_state = st_re.shape[1] * st_re.shape[2]
    y_ssm, h_re, h_im = _s5_step_call(u, st_re.reshape(n_seq, n_state), st_im.reshape(n_seq, n_state), sw)
    cache_t = cache_kv.transpose(0, 2, 3, 4, 1)
    kc, vc_t = _compress_sample_call(cache_t, page_table, cw)
    ncp = past_len // CMP_STRIDE
    n_blk = -(-(past_len + 1) // SLC_BLOCK)
    nbp = _round_up(n_blk, LANES)
    ovl = _overlap_matrix(ncp, ncp - 1, n_blk, nbp)
    tri = (jnp.arange(nbp)[:, None] < jnp.arange(nbp)[None, :]).astype(BF16)
    q8 = q_t[:, :, :n_seq].transpose(2, 0, 1)
    ocmp, idx = _sample_select_call(q8, kc, vc_t, ovl, tri, past_len)
    idx = idx[:, :KV_HEADS, :TOP_N].reshape(n_seq, KV_HEADS * TOP_N)
    gates8 = gates_t[:, :3 * Q_PER_KV, :n_seq].reshape(KV_HEADS, Q_PER_KV, 3, n_seq).transpose(3, 0, 1, 2)
    gates8 = jnp.pad(gates8.reshape(n_seq, N_HEADS, 3), ((0, 0), (0, 0), (0, LANES - 3)))
    per_head = lambda a: jnp.repeat(a.transpose(1, 0, 2), Q_PER_KV, axis=1)
    new_k = lambda k: per_head(k[:, :n_seq, HEAD_DIM:])
    new_v = lambda v_t: per_head(v_t[:, 0, :HEAD_DIM, :n_seq].transpose(0, 2, 1))
    o8 = _sample_attend_call(idx, page_table, q8[:, :, HEAD_DIM:], ocmp, gates8, new_k(ks), new_v(vs_t),
                             new_k(kw), new_v(vw_t), cache_win.transpose(0, 2, 3, 4, 1), cache_t,
                             max(win_buf + 1 - WINDOW, 0))
    o = o8.reshape(n_seq, NSA_W)
    h_new = _outmix_call(h, y_ssm, gs, o, gn, prm["w_glu"], prm["w_out"])
    kv_rows = kv_t[0, :, :n_seq].T.reshape(n_seq, 1, 4, KV_HEADS, HEAD_DIM)
    win_new = win_t[0, :, :n_seq].T.reshape(n_seq, 1, 2, KV_HEADS, HEAD_DIM)
    wrows = jnp.concatenate([cache_win, win_new], axis=1)
    wrows = wrows[:, wrows.shape[1] - min(WINDOW, wrows.shape[1]):]
    state = lambda s: s.reshape(st_re.shape)
    return h_new, kv_rows, wrows, state(h_re), state(h_im)


def kernel(x_prompt, x_sample, cache_kv, cache_win, state_ssm_re, state_ssm_im, page_table, norm_w, w_in, gate_b,
           q_norm_w, k_norm_w, cmp_pe, cmp_w1, cmp_b1, cmp_w2, ssm_lam_re, ssm_lam_im, ssm_log_step, ssm_b_re,
           ssm_b_im, ssm_c_re, ssm_c_im, ssm_d, w_glu, w_out):
    p = dict(norm_w=norm_w, w_in=w_in, gate_b=gate_b, q_norm_w=q_norm_w, k_norm_w=k_norm_w, cmp_pe=cmp_pe,
             cmp_w1=cmp_w1, cmp_b1=cmp_b1, cmp_w2=cmp_w2, ssm_lam_re=ssm_lam_re, ssm_lam_im=ssm_lam_im,
             ssm_log_step=ssm_log_step, ssm_b_re=ssm_b_re, ssm_b_im=ssm_b_im, ssm_c_re=ssm_c_re,
             ssm_c_im=ssm_c_im, ssm_d=ssm_d, w_glu=w_glu, w_out=w_out)
    b_p, s_p, d_model = x_prompt.shape
    b_s, s_s, _ = x_sample.shape
    assert s_s == 1, "the sample group decodes one token per sequence"
    h_p = x_prompt.reshape(b_p * s_p, d_model)
    h_s = x_sample.reshape(b_s, d_model)
    outs_p, outs_s = [], []
    for l in range(norm_w.shape[0]):
        weights = (_prep_params(p, l), _prep_compress(p, l), _prep_s5_chunked(p, l))
        h_p, *rest_p = _prompt_layer(h_p, *weights, b_p, s_p)
        h_s, *rest_s = _sample_layer(h_s, *weights, cache_kv[l], cache_win[l], state_ssm_re[l],
                                     state_ssm_im[l], page_table)
        outs_p.append(rest_p)
        outs_s.append(rest_s)
    stack = lambda outs, i: jnp.stack([o[i] for o in outs])
    return (h_p.reshape(x_prompt.shape), h_s.reshape(x_sample.shape),
            stack(outs_p, 0), stack(outs_s, 0), stack(outs_p, 1), stack(outs_s, 1),
            stack(outs_p, 2), stack(outs_p, 3), stack(outs_s, 2), stack(outs_s, 3))
```

```python
import functools
import math

import jax
import jax.numpy as jnp
from jax import lax
from jax.experimental import pallas as pl
from jax.experimental.pallas import tpu as pltpu

F32 = jnp.float32
BF16 = jnp.bfloat16
I32 = jnp.int32

LANES = 128
SUBLANES = 8
VMEM_LIMIT_BYTES = 56 * 1024 * 1024

HEAD_DIM = 64
N_HEADS = 8
KV_HEADS = 2
Q_PER_KV = N_HEADS // KV_HEADS
SSM_W = 512
SSM_GROUP = 16
SSM_STATE = 64
NSA_W = N_HEADS * HEAD_DIM
CMP_BLOCK = 32
CMP_STRIDE = 16
CMP_HID = 2 * HEAD_DIM
SLC_BLOCK = 64
SLC_SHIFT = SLC_BLOCK.bit_length() - 1
TOP_N = 16
N_LOCAL_BLOCKS = 2
WINDOW = 512
ROPE_THETA = 500000.0
ROPE_DIM = HEAD_DIM // 4
RMS_EPS = 1e-6
NEG_INF = -1e30

COL_U = 0
COL_GS = SSM_W
COL_Q = 2 * SSM_W
COL_GN = 2 * SSM_W + NSA_W
COL_KV = 2 * SSM_W + 2 * NSA_W
COL_GL = COL_KV + 6 * KV_HEADS * HEAD_DIM
IN_W_PAD = COL_GL + LANES

PROJ_TILE = 512
ATTN_TILE = 256
V_ROWS = HEAD_DIM + 16
SSM_CHUNK = 16
SSM_SLAB_GROUPS = LANES // SSM_GROUP
SSM_SLABS = SSM_W // LANES


def _cparams(sem):
    return pltpu.CompilerParams(dimension_semantics=sem, vmem_limit_bytes=VMEM_LIMIT_BYTES)


def _sigmoid(x):
    return 1.0 / (1.0 + jnp.exp(-x))


def _dot(a, b):
    return jnp.dot(a, b, preferred_element_type=F32)


def _dot_nt(a, b):
    return lax.dot_general(a, b, (((1,), (1,)), ((), ())), preferred_element_type=F32)


def _proj_body(x_ref, nw_ref, w_ref, qnw_ref, knw_ref, gb_ref, ra_ref, rb_ref, rc_ref,
               u_ref, gs_ref, gn_ref, qt_ref, kvt_ref, wint_ref, gt_ref,
               ks_ref, vst_ref, kw_ref, vwt_ref, *, tm, tv, tiles_per_seq):
    x = x_ref[...]
    ms = jnp.mean(x * x, axis=-1, keepdims=True)
    h = (x * lax.rsqrt(ms + RMS_EPS) * nw_ref[...]).astype(BF16)

    def mm(c0, c1):
        return _dot(h, w_ref[:, c0:c1])

    lane = lax.broadcasted_iota(I32, (tm, LANES), 1)
    lo = lane < HEAD_DIM
    ra = ra_ref[...]
    rb = rb_ref[...]
    rc = rc_ref[...]

    def norm_rope(s, wrow):
        s2 = s * s
        slo = jnp.sum(jnp.where(lo, s2, 0.0), axis=-1, keepdims=True)
        shi = jnp.sum(jnp.where(lo, 0.0, s2), axis=-1, keepdims=True)
        msq = jnp.where(lo, slo, shi) * (1.0 / HEAD_DIM)
        y = s * lax.rsqrt(msq + RMS_EPS) * wrow
        half = ROPE_DIM // 2
        return y * ra + pltpu.roll(y, LANES - half, 1) * rb + pltpu.roll(y, half, 1) * rc

    def hi_half(y, head):
        src = pltpu.roll(y, HEAD_DIM, 1) if head == 0 else y
        return jnp.where(lo, 0.0, src)

    zq = mm(COL_Q, COL_GN)
    zkv = mm(COL_KV, COL_GL)
    zgl = mm(COL_GL, IN_W_PAD)

    qnw = qnw_ref[...]
    scale = HEAD_DIM ** -0.5 * math.log2(math.e)
    zeros_t = jnp.zeros((HEAD_DIM, tm), F32)
    for j in range(N_HEADS // 2):
        y_t = (norm_rope(zq[:, j * LANES:(j + 1) * LANES], qnw) * scale).T
        for head in range(2):
            q_t = jnp.concatenate([zeros_t, y_t[head * HEAD_DIM:(head + 1) * HEAD_DIM]], axis=0)
            qt_ref[2 * j + head] = q_t.astype(BF16)

    kc = norm_rope(zkv[:, 0:LANES], knw_ref[0:1, :])
    vc = zkv[:, LANES:2 * LANES]
    ks = norm_rope(zkv[:, 2 * LANES:3 * LANES], knw_ref[1:2, :])
    vs = zkv[:, 3 * LANES:4 * LANES]
    kw = norm_rope(zkv[:, 4 * LANES:5 * LANES], knw_ref[2:3, :])
    vw = zkv[:, 5 * LANES:6 * LANES]
    vs_t, vw_t = vs.T, vw.T
    for i, rows_t in enumerate((kc.T, vc.T, ks.T, vs_t)):
        kvt_ref[i * LANES:(i + 1) * LANES, :] = rows_t
    for i, rows_t in enumerate((kw.T, vw_t)):
        wint_ref[i * LANES:(i + 1) * LANES, :] = rows_t

    row = lax.broadcasted_iota(I32, (tm, LANES), 0)
    pos = (pl.program_id(0) % tiles_per_seq) * tm + row
    onehot = jnp.where(lane == lax.shift_right_logical(pos, SLC_SHIFT), 1.0, 0.0)
    ones_t = jnp.where(lax.broadcasted_iota(I32, (V_ROWS - HEAD_DIM, tm), 0) == 0, 1.0, 0.0)
    for g in range(KV_HEADS):
        ks_ref[g] = jnp.where(lo, onehot, hi_half(ks, g)).astype(BF16)
        kw_ref[g] = hi_half(kw, g).astype(BF16)
        for v_t, vt_ref in ((vs_t, vst_ref), (vw_t, vwt_ref)):
            v_aug = jnp.concatenate([v_t[g * HEAD_DIM:(g + 1) * HEAD_DIM], ones_t], axis=0).astype(BF16)
            for t in range(tm // tv):
                vt_ref[g, t] = v_aug[:, t * tv:(t + 1) * tv]

    gates_t = _sigmoid(zgl + gb_ref[...]).T
    for g in range(KV_HEADS):
        gt_ref[g] = gates_t[g * 3 * Q_PER_KV:g * 3 * Q_PER_KV + 2 * SUBLANES]

    zu = mm(COL_U, COL_GS)
    for j in range(SSM_SLABS):
        u_ref[j] = zu[:, j * LANES:(j + 1) * LANES]
    for g_ref, cols in ((gs_ref, (COL_GS, COL_Q)), (gn_ref, (COL_GN, COL_KV))):
        g = mm(*cols)
        g_ref[...] = (g * _sigmoid(g)).astype(BF16)


def _proj_call(x2d, w_pad, norm_w, qnw, knw, gb, tabs, batch, seq):
    T, D = x2d.shape
    tm = min(PROJ_TILE, seq)
    tv = min(ATTN_TILE, seq)
    assert T == batch * seq and seq % tm == 0 and tm % tv == 0 and tv % LANES == 0
    tps = seq // tm
    row_spec = lambda w: pl.BlockSpec((tm, w), lambda i: (i, 0))
    full = lambda a: pl.BlockSpec(a.shape, lambda i: (0,) * a.ndim)
    tab_spec = pl.BlockSpec((tm, LANES), lambda i: (i % tps, 0))
    head_spec = lambda n: pl.BlockSpec((n, tm, LANES), lambda i: (0, i, 0))
    head_t_spec = lambda n, rows: pl.BlockSpec((n, rows, tm), lambda i: (0, 0, i))
    cache_t_spec = lambda rows: pl.BlockSpec((None, rows, tm), lambda i: (i // tps, 0, i % tps))
    tile_t_spec = pl.BlockSpec((KV_HEADS, tm // tv, V_ROWS, tv), lambda i: (0, i, 0, 0))
    tile_t_sds = jax.ShapeDtypeStruct((KV_HEADS, T // tv, V_ROWS, tv), BF16)
    out_shape = (
        jax.ShapeDtypeStruct((SSM_SLABS, T, LANES), F32),
        jax.ShapeDtypeStruct((T, SSM_W), BF16),
        jax.ShapeDtypeStruct((T, NSA_W), BF16),
        jax.ShapeDtypeStruct((N_HEADS, LANES, T), BF16),
        jax.ShapeDtypeStruct((batch, 4 * LANES, seq), F32),
        jax.ShapeDtypeStruct((batch, 2 * LANES, seq), F32),
        jax.ShapeDtypeStruct((KV_HEADS, 2 * SUBLANES, T), F32),
        jax.ShapeDtypeStruct((KV_HEADS, T, LANES), BF16),
        tile_t_sds,
        jax.ShapeDtypeStruct((KV_HEADS, T, LANES), BF16),
        tile_t_sds,
    )
    out_specs = (head_spec(SSM_SLABS), row_spec(SSM_W), row_spec(NSA_W), head_t_spec(N_HEADS, LANES),
                 cache_t_spec(4 * LANES), cache_t_spec(2 * LANES), head_t_spec(KV_HEADS, 2 * SUBLANES),
                 head_spec(KV_HEADS), tile_t_spec, head_spec(KV_HEADS), tile_t_spec)
    return pl.pallas_call(
        functools.partial(_proj_body, tm=tm, tv=tv, tiles_per_seq=tps),
        out_shape=out_shape,
        grid=(T // tm,),
        in_specs=[row_spec(D), full(norm_w), full(w_pad), full(qnw), full(knw), full(gb),
                  tab_spec, tab_spec, tab_spec],
        out_specs=out_specs,
        compiler_params=_cparams(("parallel",)),
        name="proj",
    )(x2d, norm_w, w_pad, qnw, knw, gb, *tabs)


def _prep_params(p, l):
    w_in = p["w_in"][l]
    d_model, in_w = w_in.shape
    tile2 = lambda v: jnp.tile(v, (1, LANES // HEAD_DIM))
    return {
        "w_in": jnp.pad(w_in.astype(BF16), ((0, 0), (0, IN_W_PAD - in_w))),
        "norm_w": p["norm_w"][l].reshape(1, d_model).astype(F32),
        "qnw": tile2(p["q_norm_w"][l].reshape(1, HEAD_DIM)).astype(F32),
        "knw": tile2(p["k_norm_w"][l]).astype(F32),
        "gb": jnp.pad(p["gate_b"][l].reshape(1, -1).astype(F32), ((0, 0), (0, LANES - 3 * N_HEADS))),
        "w_glu": p["w_glu"][l].astype(BF16),
        "w_out": p["w_out"][l].astype(BF16),
    }


def _rope_tables(pos):
    half = ROPE_DIM // 2
    inv = ROPE_THETA ** (-jnp.arange(half, dtype=F32) / half)
    ang = pos.astype(F32)[:, None] * inv
    cos, sin = jnp.cos(ang), jnp.sin(ang)
    n = pos.shape[0]
    rest = HEAD_DIM - ROPE_DIM
    a = jnp.concatenate([cos, cos, jnp.ones((n, rest), F32)], axis=-1)
    b = jnp.concatenate([-sin, jnp.zeros((n, HEAD_DIM - half), F32)], axis=-1)
    c = jnp.concatenate([jnp.zeros((n, half), F32), sin, jnp.zeros((n, rest), F32)], axis=-1)
    return tuple(jnp.tile(t, (1, LANES // HEAD_DIM)) for t in (a, b, c))


def _outmix_body(x_ref, y_ref, gs_ref, o_ref, gn_ref, wg_ref, wo_ref, out_ref):
    y = jnp.concatenate([y_ref[j] for j in range(SSM_SLABS)], axis=-1)
    ab = _dot(y.astype(BF16), wg_ref[...])
    ssm = ab[:, :SSM_W] * _sigmoid(ab[:, SSM_W:]) * gs_ref[...].astype(F32)
    nsa = o_ref[...].astype(F32) * gn_ref[...].astype(F32)
    acc = _dot(ssm.astype(BF16), wo_ref[0:SSM_W, :])
    acc += _dot(nsa.astype(BF16), wo_ref[SSM_W:, :])
    out_ref[...] = x_ref[...] + acc


def _outmix_call(x2d, y_ssm, g_ssm, o_nsa, g_nsa, w_glu, w_out):
    T, D = x2d.shape
    tm = min(PROJ_TILE, T)
    row_spec = lambda w: pl.BlockSpec((tm, w), lambda i: (i, 0))
    full = lambda a: pl.BlockSpec(a.shape, lambda i: (0,) * a.ndim)
    return pl.pallas_call(
        _outmix_body,
        out_shape=jax.ShapeDtypeStruct((T, D), F32),
        grid=(T // tm,),
        in_specs=[row_spec(D), pl.BlockSpec((SSM_SLABS, tm, LANES), lambda i: (0, i, 0)),
                  row_spec(SSM_W), row_spec(NSA_W), row_spec(NSA_W),
                  full(w_glu), full(w_out)],
        out_specs=row_spec(D),
        compiler_params=_cparams(("parallel",)),
        name="outmix",
    )(x2d, y_ssm, g_ssm, o_nsa, g_nsa, w_glu, w_out)


def _gelu_tanh(x):
    c = math.sqrt(2.0 / math.pi)
    return 0.5 * x * (1.0 + jnp.tanh(c * (x + 0.044715 * (x * x * x))))


def _compress_rows(rows_refs, pe_ref, wa_ref, wb_ref, b1_ref, w2_ref, kc_ref, vc_ref, nch):
    lane = lax.broadcasted_iota(I32, (nch, LANES), 1)
    for kvi, out_ref in ((0, kc_ref), (1, vc_ref)):
        rows_ref = rows_refs[kvi]
        pa = jnp.zeros((nch, 2 * CMP_HID), F32)
        pb = jnp.zeros((nch, 2 * CMP_HID), F32)
        for j0 in range(0, CMP_STRIDE, 2):
            xs = [rows_ref[pl.ds(j, nch, stride=CMP_STRIDE), :] for j in (j0, j0 + 1)]
            xa = jnp.concatenate([xs[i] + pe_ref[kvi, 0, j0 + i:j0 + i + 1, :] for i in range(2)], axis=-1)
            xb = jnp.concatenate([xs[i] + pe_ref[kvi, 1, j0 + i:j0 + i + 1, :] for i in range(2)], axis=-1)
            wsl = slice(j0 * LANES, (j0 + 2) * LANES)
            pa += _dot(xa.astype(BF16), wa_ref[kvi, wsl, :])
            pb += _dot(xb.astype(BF16), wb_ref[kvi, wsl, :])
        hid = _gelu_tanh(pa + pltpu.roll(pb, nch - 1, 0) + b1_ref[kvi]).astype(BF16)
        for g in range(KV_HEADS):
            o = _dot(hid, w2_ref[kvi, g])
            if kvi == 1:
                o = jnp.where(lane == HEAD_DIM, 1.0, o).T
            out_ref[g] = o.astype(BF16)


def _compress_prompt_body(kvt_ref, pe_ref, wa_ref, wb_ref, b1_ref, w2_ref, kc_ref, vc_ref,
                          krows_ref, vrows_ref, *, nch):
    for c in range(kvt_ref.shape[1] // LANES):
        cs = slice(c * LANES, (c + 1) * LANES)
        krows_ref[cs, :] = kvt_ref[0:LANES, cs].T
        vrows_ref[cs, :] = kvt_ref[LANES:2 * LANES, cs].T
    _compress_rows((krows_ref, vrows_ref), pe_ref, wa_ref, wb_ref, b1_ref, w2_ref, kc_ref, vc_ref, nch)


def _compress_prompt_call(kv_t, cw, batch, seq):
    nch = seq // CMP_STRIDE
    full = lambda a: pl.BlockSpec(a.shape, lambda b: (0,) * a.ndim)
    out_spec = pl.BlockSpec((KV_HEADS, nch, LANES), lambda b: (0, b, 0))
    out_sds = jax.ShapeDtypeStruct((KV_HEADS, batch * nch, LANES), BF16)
    out_t_spec = pl.BlockSpec((KV_HEADS, None, LANES, nch), lambda b: (0, b, 0, 0))
    out_t_sds = jax.ShapeDtypeStruct((KV_HEADS, batch, LANES, nch), BF16)
    return pl.pallas_call(
        functools.partial(_compress_prompt_body, nch=nch),
        out_shape=(out_sds, out_t_sds),
        grid=(batch,),
        in_specs=[pl.BlockSpec((None, 2 * LANES, seq), lambda b: (b, 0, 0)),
                  full(cw["pe"]), full(cw["wa"]), full(cw["wb"]), full(cw["b1"]), full(cw["w2"])],
        out_specs=(out_spec, out_t_spec),
        scratch_shapes=[pltpu.VMEM((seq, LANES), F32)] * 2,
        compiler_params=_cparams(("parallel",)),
        name="compress_prompt",
    )(kv_t, cw["pe"], cw["wa"], cw["wb"], cw["b1"], cw["w2"])


def _prep_compress(p, l):
    assert KV_HEADS == 2
    w1 = p["cmp_w1"][l].reshape(2, 2, CMP_STRIDE, HEAD_DIM, CMP_HID).astype(BF16)
    z1 = jnp.zeros_like(w1)
    wexp = jnp.stack([jnp.concatenate([w1, z1], axis=-1), jnp.concatenate([z1, w1], axis=-1)], axis=3)
    wexp = wexp.reshape(2, 2, CMP_STRIDE * LANES, KV_HEADS * CMP_HID)
    pe = p["cmp_pe"][l].reshape(2, 2, CMP_STRIDE, HEAD_DIM)
    w2 = p["cmp_w2"][l]
    zeros = jnp.zeros_like(w2[0])
    w2k = jnp.concatenate([zeros, w2[0]], axis=-1)
    w2v = jnp.concatenate([w2[1], zeros], axis=-1)
    z2 = jnp.zeros_like(w2k)
    w2e = jnp.stack([jnp.stack([jnp.concatenate([w, z2], axis=0), jnp.concatenate([z2, w], axis=0)])
                     for w in (w2k, w2v)])
    return {
        "pe": jnp.tile(pe, (1, 1, 1, KV_HEADS)).astype(F32),
        "wa": wexp[:, 0],
        "wb": wexp[:, 1],
        "b1": jnp.tile(p["cmp_b1"][l].reshape(2, 1, CMP_HID), (1, 1, KV_HEADS)).astype(F32),
        "w2": w2e.astype(BF16),
    }


def _overlap_matrix(n_tok_pad, n_tok, n_blk, n_cols=LANES):
    c_start = jnp.arange(n_tok_pad) * CMP_STRIDE
    blk = jnp.arange(n_cols)
    ov = ((c_start[:, None] < (blk[None, :] + 1) * SLC_BLOCK)
          & (c_start[:, None] + CMP_BLOCK > blk[None, :] * SLC_BLOCK)
          & (jnp.arange(n_tok_pad)[:, None] < n_tok) & (blk[None, :] < n_blk))
    return ov.astype(BF16)


def _topk_select_t(w_ref, imp_t, q0, tq):
    nb = imp_t.shape[0]
    n_i = lax.broadcasted_iota(I32, (nb, tq), 0)
    qblk = lax.shift_right_logical(q0 + lax.broadcasted_iota(I32, (nb, tq), 1), SLC_SHIFT)
    causal = n_i <= qblk
    forced = (n_i == 0) | (n_i >= qblk - (N_LOCAL_BLOCKS - 1))
    w_ref[...] = jnp.where(causal, jnp.where(forced, jnp.inf, imp_t), -jnp.inf)
    last_blk = lax.shift_right_logical(q0 + tq - 1, SLC_SHIFT)
    n_grp = nb // SUBLANES
    rank = [jnp.zeros((SUBLANES, tq), F32) for _ in range(n_grp)]
    grp_i = lax.broadcasted_iota(I32, (SUBLANES, tq), 0)

    def count_group(mg, rank):
        rank = list(rank)
        for mi in range(SUBLANES):
            m = mg * SUBLANES + mi
            wm = w_ref[m:m + 1, :]
            for ng in range(n_grp):
                w = w_ref[ng * SUBLANES:(ng + 1) * SUBLANES, :]
                if ng > mg:
                    beats = jnp.where(wm >= w, 1.0, 0.0)
                elif ng < mg:
                    beats = jnp.where(wm > w, 1.0, 0.0)
                else:
                    beats = jnp.where(grp_i > mi, jnp.where(wm >= w, 1.0, 0.0), jnp.where(wm > w, 1.0, 0.0))
                rank[ng] = rank[ng] + beats
        return tuple(rank)

    rank = tuple(rank)
    for mg in range(n_grp):
        rank = lax.cond(mg * SUBLANES <= last_blk, functools.partial(count_group, mg), lambda r: r, rank)
    return causal & (jnp.concatenate(rank, axis=0) < TOP_N)


def _flash_tiles_t(tiles, q_ts, ms, accs):
    def scores(i):
        k, _, mask = tiles[i]
        s = [_dot(k, q_t) for q_t in q_ts]
        return s if mask is None else [jnp.where(mask, x, NEG_INF) for x in s]

    ahead = 2
    pending = {i: scores(i) for i in range(min(ahead, len(tiles)))}
    for i, (_, v_t, _) in enumerate(tiles):
        s = pending.pop(i)
        m_new = [jnp.maximum(m, jnp.max(x, axis=0, keepdims=True)) for m, x in zip(ms, s)]
        alpha = [jnp.exp2(m - mn) for m, mn in zip(ms, m_new)]
        p = [jnp.exp2(x - mn).astype(BF16) for x, mn in zip(s, m_new)]
        pv = [_dot(v_t, x) for x in p]
        if i + ahead < len(tiles):
            pending[i + ahead] = scores(i + ahead)
        accs = [a * acc + x for a, acc, x in zip(alpha, accs, pv)]
        ms = m_new
    return tuple(ms), tuple(accs)


def _attn_body(qt_ref, kc_ref, vct_ref, ovlt_ref, ks_ref, vst_ref, kw_ref, vwt_ref, gate_ref, o_ref,
               qa_ref, ocmp_ref, w_ref, *, tq, ncp):
    R = Q_PER_KV
    qt = pl.program_id(2)
    q0 = qt * tq
    nbs = LANES // 2

    def compressed(n_c):
        c_i = lax.broadcasted_iota(I32, (n_c, tq), 0)
        qpos_c = q0 + lax.broadcasted_iota(I32, (n_c, tq), 1)
        cmask = c_i * CMP_STRIDE + (CMP_BLOCK - 1) <= qpos_c
        kc = kc_ref[0:n_c, :]
        s = [jnp.where(cmask, _dot(kc, qt_ref[r]), NEG_INF) for r in range(R)]
        e = [jnp.where(cmask, jnp.exp2(x - jnp.max(x, axis=0, keepdims=True)), 0.0) for x in s]
        l = [jnp.sum(x, axis=0, keepdims=True) for x in e]
        p = [(x * (1.0 / jnp.where(y > 0.0, y, 1.0))).astype(BF16) for x, y in zip(e, l)]
        for r in range(R):
            ocmp_ref[r] = _dot(vct_ref[:, 0:n_c], p[r])
        return sum(_dot(ovlt_ref[:, 0:n_c], x) for x in p)

    half = ncp // 2
    if half % LANES == 0:
        imp = lax.cond((q0 + tq) // CMP_STRIDE <= half, lambda: compressed(half), lambda: compressed(ncp))
    else:
        imp = compressed(ncp)

    sel = _topk_select_t(w_ref, imp[:nbs], q0, tq)
    bias = jnp.concatenate([jnp.where(sel, 0.0, NEG_INF), jnp.zeros((LANES - nbs, tq), F32)], axis=0)
    for r in range(R):
        qa_ref[r] = (qt_ref[r].astype(F32) + bias).astype(BF16)

    key_i = lax.broadcasted_iota(I32, (tq, tq), 0)
    qry_i = lax.broadcasted_iota(I32, (tq, tq), 1)
    init = (tuple(jnp.full((1, tq), -jnp.inf, F32) for _ in range(R)),
            tuple(jnp.zeros((V_ROWS, tq), F32) for _ in range(R)))

    def tiles(k_ref, vt_ref, q_ref, js, masks, state):
        ts = []
        hk = tq // 2
        for j, mask in zip(js, masks):
            k = k_ref[pl.ds(pl.multiple_of(j * tq, tq), tq), :]
            v_t = vt_ref[j]
            for h in range(2):
                ts.append((k[h * hk:(h + 1) * hk], v_t[:, h * hk:(h + 1) * hk],
                           None if mask is None else mask[h * hk:(h + 1) * hk]))
        return _flash_tiles_t(ts, [q_ref[r] for r in range(R)], *state)

    def last_tiles(k_ref, vt_ref, q_ref, n, first_mask, state):
        js = [qt - (n - 1 - t) for t in range(n)]
        masks = [first_mask] + [None] * (n - 2) + [key_i <= qry_i] if n > 1 else [key_i <= qry_i]
        return tiles(k_ref, vt_ref, q_ref, js, masks, state)

    GROUP = 4

    def slc_group(i, st):
        return tiles(ks_ref, vst_ref, qa_ref, [GROUP * i + t for t in range(GROUP)], [None] * GROUP, st)

    state = lax.fori_loop(0, qt // GROUP, slc_group, init)
    tails = [functools.partial(last_tiles, ks_ref, vst_ref, qa_ref, n, None) for n in range(1, GROUP + 1)]
    _, acc_s = lax.switch(qt % GROUP, tails, state)

    nwin = WINDOW // tq
    wins = [functools.partial(last_tiles, kw_ref, vwt_ref, qt_ref, n, (key_i > qry_i) if n == nwin + 1 else None)
            for n in range(1, nwin + 2)]
    _, acc_w = lax.switch(jnp.minimum(qt, nwin), wins, init)

    outs = []
    for r in range(R):
        a_s = acc_s[r]
        a_w = acc_w[r]
        g = lambda k: gate_ref[3 * r + k:3 * r + k + 1, :]
        head = lambda a: a[:HEAD_DIM] * (1.0 / a[HEAD_DIM:HEAD_DIM + 1, :])
        outs.append(g(0) * ocmp_ref[r][:HEAD_DIM] + g(1) * head(a_s) + g(2) * head(a_w))
    o_ref[...] = jnp.concatenate(outs, axis=0).T.astype(o_ref.dtype)


def _attn_call(q_t, kc, vc_t, ovl_t, ks, vs_t, kw, vw_t, gates_t, batch, seq):
    tq = min(ATTN_TILE, seq)
    nq = seq // tq
    ncp = kc.shape[1] // batch
    R = Q_PER_KV
    assert seq // SLC_BLOCK <= LANES // 2 and WINDOW % tq == 0
    k_spec = pl.BlockSpec((None, seq, LANES), lambda b, g, t: (g, b, 0))
    vt_spec = pl.BlockSpec((None, nq, V_ROWS, tq), lambda b, g, t: (g, b, 0, 0))
    acc = pltpu.VMEM((R, LANES, tq), F32)
    return pl.pallas_call(
        functools.partial(_attn_body, tq=tq, ncp=ncp),
        out_shape=jax.ShapeDtypeStruct((batch * seq, NSA_W), BF16),
        grid=(batch, KV_HEADS, nq),
        in_specs=[pl.BlockSpec((R, LANES, tq), lambda b, g, t: (g, 0, b * nq + t)),
                  pl.BlockSpec((None, ncp, LANES), lambda b, g, t: (g, b, 0)),
                  pl.BlockSpec((None, None, LANES, ncp), lambda b, g, t: (g, b, 0, 0)),
                  pl.BlockSpec(ovl_t.shape, lambda b, g, t: (0, 0)),
                  k_spec, vt_spec, k_spec, vt_spec,
                  pl.BlockSpec((None, 2 * SUBLANES, tq), lambda b, g, t: (g, 0, b * nq + t))],
        out_specs=pl.BlockSpec((tq, R * HEAD_DIM), lambda b, g, t: (b * nq + t, g)),
        scratch_shapes=[pltpu.VMEM((R, LANES, tq), BF16), acc, pltpu.VMEM((LANES // 2, tq), F32)],
        compiler_params=_cparams(("parallel", "parallel", "arbitrary")),
        name="nsa_prompt",
    )(q_t, kc, vc_t, ovl_t, ks, vs_t, kw, vw_t, gates_t)


def _s5_discretise(p, l):
    lr = p["ssm_lam_re"][l].astype(F32)
    li = p["ssm_lam_im"][l].astype(F32)
    dt = jnp.exp(p["ssm_log_step"][l].astype(F32))[:, None]

    def apow(t):
        mag, ang = jnp.exp(lr * dt * t), li * dt * t
        return mag * jnp.cos(ang), mag * jnp.sin(ang)

    a_re, a_im = apow(1.0)
    den = lr * lr + li * li
    nr, ni = a_re - 1.0, a_im
    f_re, f_im = (nr * lr + ni * li) / den, (ni * lr - nr * li) / den
    br, bi = p["ssm_b_re"][l].astype(F32), p["ssm_b_im"][l].astype(F32)
    bb_re = f_re[..., None] * br - f_im[..., None] * bi
    bb_im = f_re[..., None] * bi + f_im[..., None] * br
    return apow, bb_re, bb_im


def _s5_expand_body(pst_ref, kt_ref, cp_ref, wcol_ref, wst_ref, wout_ref):
    T, E, N, C = SSM_CHUNK, SSM_SLAB_GROUPS, SSM_STATE, SSM_GROUP
    row = lax.broadcasted_iota(I32, (LANES, LANES), 0)
    lane = lax.broadcasted_iota(I32, (LANES, LANES), 1)
    row_grp = row // C
    low = lane < N
    own_grp = row_grp == lane // C
    lane_n = lax.broadcasted_iota(I32, (N, LANES), 1)
    wcol_ref[T * LANES:(T + 1) * LANES, :] = jnp.zeros((LANES, LANES), BF16)
    wout_ref[0, :, LANES:2 * LANES] = jnp.zeros((2 * E * N, LANES), BF16)
    for i in range(T):
        wcol_ref[i * LANES:(i + 1) * LANES, :] = jnp.where(own_grp, kt_ref[T - 1 - i], 0.0).astype(BF16)
        x = pst_ref[T - 1 - i]
        x_sw = pltpu.roll(x, N, 1)
        halves = (jnp.where(low, x, x_sw), jnp.where(low, x_sw, x))
        for ri, x2 in enumerate(halves):
            for q in range(E // 2):
                own = row_grp == 2 * q + jnp.where(low, 0, 1)
                c0 = ri * E * N + q * LANES
                wst_ref[i * LANES:(i + 1) * LANES, c0:c0 + LANES] = jnp.where(own, x2, 0.0).astype(BF16)
    for i in range(T + 1):
        pair, half = (0, 0) if i == 0 else ((i + 1) // 2, (i - 1) % 2)
        m_t = cp_ref[i].T
        for ri in range(2):
            rows = m_t[ri * N:(ri + 1) * N]
            for g in range(E):
                r0 = ri * E * N + g * N
                wout_ref[pair, r0:r0 + N, half * LANES:(half + 1) * LANES] = (
                    jnp.where(lane_n // C == g, rows, 0.0).astype(BF16))


def _prep_s5_chunked(p, l):
    hi = lax.Precision.HIGHEST
    apow, bb_re, bb_im = _s5_discretise(p, l)
    c_re, c_im = p["ssm_c_re"][l].astype(F32), p["ssm_c_im"][l].astype(F32)
    T, J, E, N, C = SSM_CHUNK, SSM_SLABS, SSM_SLAB_GROUPS, SSM_STATE, SSM_GROUP
    pw_re, pw_im = apow(jnp.arange(T + 1, dtype=F32)[:, None, None])
    bt_re, bt_im = bb_re.transpose(0, 2, 1), bb_im.transpose(0, 2, 1)
    p_re = pw_re[:, :, None, :] * bt_re - pw_im[:, :, None, :] * bt_im
    p_im = pw_re[:, :, None, :] * bt_im + pw_im[:, :, None, :] * bt_re
    kt = (jnp.einsum("tgkn,gcn->tgkc", p_re[:T], jnp.tile(c_re, (1, E, 1)), precision=hi)
          - jnp.einsum("tgkn,gcn->tgkc", p_im[:T], jnp.tile(c_im, (1, E, 1)), precision=hi))
    cp_re = c_re * pw_re[:, :, None, :] - c_im * pw_im[:, :, None, :]
    cp_im = c_re * pw_im[:, :, None, :] + c_im * pw_re[:, :, None, :]
    slab = lambda x: jnp.moveaxis(x.reshape(x.shape[0], J, E * C, LANES), 1, 0)
    pst = slab(jnp.concatenate([p_re[:T], p_im[:T]], axis=-1))
    ktile = slab(kt)
    cpn = slab(jnp.concatenate([cp_re, -cp_im], axis=-1))
    blk = lambda n: pl.BlockSpec((None, n, LANES, LANES), lambda j: (j, 0, 0, 0))
    w_col, w_st, w_out = pl.pallas_call(
        _s5_expand_body,
        out_shape=(jax.ShapeDtypeStruct((J, (T + 1) * LANES, LANES), BF16),
                   jax.ShapeDtypeStruct((J, T * LANES, 2 * E * N), BF16),
                   jax.ShapeDtypeStruct((J, T // 2 + 1, 2 * E * N, 2 * LANES), BF16)),
        grid=(J,),
        in_specs=[blk(T), blk(T), blk(T + 1)],
        out_specs=(pl.BlockSpec((None, (T + 1) * LANES, LANES), lambda j: (j, 0, 0)),
                   pl.BlockSpec((None, T * LANES, 2 * E * N), lambda j: (j, 0, 0)),
                   pl.BlockSpec((None, T // 2 + 1, 2 * E * N, 2 * LANES), lambda j: (j, 0, 0, 0))),
        compiler_params=_cparams(("parallel",)),
        name="s5_expand",
    )(pst, ktile, cpn)
    return {
        "w_col": w_col, "w_st": w_st, "w_out": w_out,
        "a_re": pw_re[T].reshape(J, 1, E * N), "a_im": pw_im[T].reshape(J, 1, E * N),
        "a1_re": pw_re[1].reshape(J, 1, E * N), "a1_im": pw_im[1].reshape(J, 1, E * N),
        "d": p["ssm_d"][l].reshape(J, 1, LANES).astype(F32),
    }


def _s5_prompt_body(u_ref, wcol_ref, wst_ref, are_ref, aim_ref, wout_ref, d_ref,
                    y_ref, hre_ref, him_ref, xs_ref, hp_ref, *, n_chunks):
    T = SSM_CHUNK
    ns = SSM_SLAB_GROUPS * SSM_STATE
    u_pos = [u_ref[pl.ds(s, n_chunks, stride=T), :] for s in range(T)]
    ub = jnp.concatenate(u_pos, axis=-1).astype(BF16)
    xs_ref[...] = _dot(ub, wst_ref[...])
    a_re = are_ref[...]
    a_im = aim_ref[...]

    def step(c, carry):
        hr, hi = carry
        hp_ref[pl.ds(c, 1), 0:ns] = hr
        hp_ref[pl.ds(c, 1), ns:2 * ns] = hi
        xr = xs_ref[pl.ds(c, 1), 0:ns]
        xi = xs_ref[pl.ds(c, 1), ns:2 * ns]
        return a_re * hr - a_im * hi + xr, a_re * hi + a_im * hr + xi

    zero = jnp.zeros((1, ns), F32)
    hr, hi = lax.fori_loop(0, n_chunks, step, (zero, zero))
    hre_ref[...] = jnp.broadcast_to(hr, hre_ref.shape)
    him_ref[...] = jnp.broadcast_to(hi, him_ref.shape)
    hpb = hp_ref[...].astype(BF16)
    for t in range(0, T, 2):
        k_rows = (t + 2) * LANES
        w_pair = jnp.concatenate([wcol_ref[(T - 1 - t) * LANES:(T - 1 - t) * LANES + k_rows, :],
                                  wcol_ref[(T - 2 - t) * LANES:T * LANES, :]], axis=1)
        y_pair = _dot(ub[:, :k_rows], w_pair) + _dot(hpb, wout_ref[t // 2 + 1])
        for i in range(2):
            y_ref[pl.ds(t + i, n_chunks, stride=T), :] = (
                y_pair[:, i * LANES:(i + 1) * LANES] + d_ref[...] * u_pos[t + i])


def _s5_prompt_call(u_slab, sw, batch, seq):
    T, J = SSM_CHUNK, SSM_SLABS
    n_chunks = seq // T
    ns = SSM_SLAB_GROUPS * SSM_STATE
    row_spec = pl.BlockSpec((None, seq, LANES), lambda j, b: (j, b, 0))
    slab_spec = lambda a: pl.BlockSpec((None,) + a.shape[1:], lambda j, b: (j,) + (0,) * (a.ndim - 1))
    st_spec = pl.BlockSpec((None, None, SUBLANES, ns), lambda j, b: (b, j, 0, 0))
    st_sds = jax.ShapeDtypeStruct((batch, J, SUBLANES, ns), F32)
    y, hre, him = pl.pallas_call(
        functools.partial(_s5_prompt_body, n_chunks=n_chunks),
        out_shape=(jax.ShapeDtypeStruct(u_slab.shape, F32), st_sds, st_sds),
        grid=(J, batch),
        in_specs=[row_spec, slab_spec(sw["w_col"]), slab_spec(sw["w_st"]), slab_spec(sw["a_re"]),
                  slab_spec(sw["a_im"]), slab_spec(sw["w_out"]), slab_spec(sw["d"])],
        out_specs=(row_spec, st_spec, st_spec),
        scratch_shapes=[pltpu.VMEM((n_chunks, 2 * ns), F32), pltpu.VMEM((n_chunks, 2 * ns), F32)],
        compiler_params=_cparams(("parallel", "parallel")),
        name="s5_prompt",
    )(u_slab, sw["w_col"], sw["w_st"], sw["a_re"], sw["a_im"], sw["w_out"], sw["d"])
    n_groups = J * SSM_SLAB_GROUPS
    state = lambda h: h[:, :, 0, :].reshape(batch, n_groups, SSM_STATE)
    return y, state(hre), state(him)


def _s5_step_body(u_ref, wx_ref, are_ref, aim_ref, h0re_ref, h0im_ref, wy_ref, d_ref,
                  y_ref, hre_ref, him_ref):
    ns = SSM_SLAB_GROUPS * SSM_STATE
    for j in range(SSM_SLABS):
        sl = slice(j * ns, (j + 1) * ns)
        u = u_ref[j]
        x = _dot(u.astype(BF16), wx_ref[j])
        a_re, a_im = are_ref[j], aim_ref[j]
        h0r, h0i = h0re_ref[:, sl], h0im_ref[:, sl]
        hr = a_re * h0r - a_im * h0i + x[:, :ns]
        hi = a_re * h0i + a_im * h0r + x[:, ns:]
        hre_ref[:, sl] = hr
        him_ref[:, sl] = hi
        y_ref[j] = _dot(jnp.concatenate([hr, hi], axis=-1).astype(BF16), wy_ref[j]) + d_ref[j] * u


def _s5_step_call(u_slab, h0_re, h0_im, sw):
    J, n_tok, _ = u_slab.shape
    T = SSM_CHUNK
    full = lambda a: pl.BlockSpec(a.shape, lambda i: (0,) * a.ndim)
    wx_spec = pl.BlockSpec((J, LANES, sw["w_st"].shape[2]), lambda i: (0, T - 1, 0))
    wy_spec = pl.BlockSpec((J, None, sw["w_out"].shape[2], LANES), lambda i: (0, 0, 0, 0))
    st_sds = jax.ShapeDtypeStruct(h0_re.shape, F32)
    return pl.pallas_call(
        _s5_step_body,
        out_shape=(jax.ShapeDtypeStruct(u_slab.shape, F32), st_sds, st_sds),
        grid=(1,),
        in_specs=[full(u_slab), wx_spec, full(sw["a1_re"]), full(sw["a1_im"]), full(h0_re), full(h0_im),
                  wy_spec, full(sw["d"])],
        out_specs=(full(u_slab), full(h0_re), full(h0_re)),
        compiler_params=_cparams(("arbitrary",)),
        name="s5_step",
    )(u_slab, sw["w_st"], sw["a1_re"], sw["a1_im"], h0_re, h0_im, sw["w_out"], sw["d"])


def _compress_sample_body(pt_ref, *refs, n_pages, page_rows, nch):
    del pt_ref
    page_refs = refs[:n_pages]
    pe_ref, wa_ref, wb_ref, b1_ref, w2_ref, kc_ref, vc_ref = refs[n_pages:n_pages + 7]
    buffers = refs[n_pages + 7:]
    b = pl.program_id(0)

    @pl.when(b == 0)
    def _():
        for buf in buffers[2:]:
            buf[...] = jnp.zeros(buf.shape, F32)

    def step(stage, done):
        for i, page_ref in enumerate(page_refs):
            for rows_ref, slot in zip(stage, range(2)):
                rows_ref[i * page_rows:(i + 1) * page_rows, :] = page_ref[slot].reshape(LANES, page_rows).T
        _compress_rows(done, pe_ref, wa_ref, wb_ref, b1_ref, w2_ref, kc_ref, vc_ref, nch)

    @pl.when(b % 2 == 0)
    def _():
        step(buffers[:2], buffers[2:])

    @pl.when(b % 2 == 1)
    def _():
        step(buffers[2:], buffers[:2])


def _compress_sample_call(cache, page_table, cw):
    n_seq, n_pages = page_table.shape
    page_rows = cache.shape[-1]
    assert page_rows == LANES
    nch = n_pages * page_rows // CMP_STRIDE
    staged = lambda b: jnp.minimum(b, n_seq - 1)
    page_spec = lambda i: pl.BlockSpec((None, 2, KV_HEADS, HEAD_DIM, page_rows),
                                       lambda b, pt: (pt[staged(b), i], 0, 0, 0, 0))
    full = lambda a: pl.BlockSpec(a.shape, lambda b, pt: (0,) * a.ndim)
    out_of = lambda b: jnp.maximum(b - 1, 0)
    out_spec = pl.BlockSpec((KV_HEADS, nch, LANES), lambda b, pt: (0, out_of(b), 0))
    out_sds = jax.ShapeDtypeStruct((KV_HEADS, n_seq * nch, LANES), BF16)
    out_t_spec = pl.BlockSpec((KV_HEADS, None, LANES, nch), lambda b, pt: (0, out_of(b), 0, 0))
    out_t_sds = jax.ShapeDtypeStruct((KV_HEADS, n_seq, LANES, nch), BF16)
    weights = (cw["pe"], cw["wa"], cw["wb"], cw["b1"], cw["w2"])
    return pl.pallas_call(
        functools.partial(_compress_sample_body, n_pages=n_pages, page_rows=page_rows, nch=nch),
        out_shape=(out_sds, out_t_sds),
        grid_spec=pltpu.PrefetchScalarGridSpec(
            num_scalar_prefetch=1,
            grid=(n_seq + 1,),
            in_specs=[page_spec(i) for i in range(n_pages)] + [full(a) for a in weights],
            out_specs=(out_spec, out_t_spec),
            scratch_shapes=[pltpu.VMEM((n_pages * page_rows, LANES), F32)] * 4),
        compiler_params=_cparams(("arbitrary",)),
        name="compress_sample",
    )(page_table, *([cache] * n_pages), *weights)


def _group_rows(x0, x1):
    row = lax.broadcasted_iota(I32, x0.shape, 0)
    return jnp.where(row < Q_PER_KV, x0, x1)


def _sample_select_body(q_ref, kc_ref, vct_ref, ovl_ref, tri_ref, ocmp_ref, idx_ref, *, ncp, qpos, nbp, n_sub):
    c_i = lax.broadcasted_iota(I32, (N_HEADS, ncp), 1)
    cmask = c_i * CMP_STRIDE + (CMP_BLOCK - 1) <= qpos
    n_row = lax.broadcasted_iota(I32, (1, nbp), 1)
    qblk = qpos // SLC_BLOCK
    causal = n_row <= qblk
    forced = (n_row == 0) | (n_row >= qblk - (N_LOCAL_BLOCKS - 1))
    m_i = lax.broadcasted_iota(I32, (nbp, nbp), 0)
    n_i = lax.broadcasted_iota(I32, (nbp, nbp), 1)
    lane = lax.broadcasted_iota(I32, (1, LANES), 1)
    for i in range(n_sub):
        q8 = q_ref[i]
        rows = slice(i * ncp, (i + 1) * ncp)
        s = _group_rows(_dot_nt(q8, kc_ref[0, rows, :]), _dot_nt(q8, kc_ref[1, rows, :]))
        s = jnp.where(cmask, s, NEG_INF)
        e = jnp.where(cmask, jnp.exp2(s - jnp.max(s, axis=-1, keepdims=True)), 0.0)
        l = jnp.sum(e, axis=-1, keepdims=True)
        p = (e / jnp.where(l > 0.0, l, 1.0)).astype(BF16)
        ocmp_ref[i] = _group_rows(_dot_nt(p, vct_ref[0, i]), _dot_nt(p, vct_ref[1, i]))
        imp8 = _dot(p, ovl_ref[...])
        idx_rows = []
        for g in range(KV_HEADS):
            imp = jnp.sum(imp8[g * Q_PER_KV:(g + 1) * Q_PER_KV], axis=0, keepdims=True)
            w = jnp.where(causal, jnp.where(forced, jnp.inf, imp), -jnp.inf)
            w_sq = jnp.broadcast_to(w, (nbp, nbp))
            w_col = w_sq.T
            beats = jnp.where(n_i > m_i, jnp.where(w_col >= w_sq, 1.0, 0.0), jnp.where(w_col > w_sq, 1.0, 0.0))
            rank = jnp.sum(beats, axis=0, keepdims=True)
            sel = causal & (rank < TOP_N)
            self_f = jnp.where(sel, 1.0, 0.0)
            before = _dot(self_f.astype(BF16), tri_ref[...])
            idx = jnp.full((1, LANES), -1, I32)
            for k in range(TOP_N):
                hit = sel & (before == float(k))
                val = jnp.sum(jnp.where(hit, n_row.astype(F32) + 1.0, 0.0), axis=-1, keepdims=True) - 1.0
                idx = jnp.where(lane == k, val.astype(I32), idx)
            idx_rows.append(idx)
        idx_ref[i] = jnp.concatenate(idx_rows + [jnp.full((SUBLANES - KV_HEADS, LANES), -1, I32)], axis=0)


def _sample_select_call(q8, kc, vc_t, ovl, tri, qpos):
    n_seq = q8.shape[0]
    ncp = kc.shape[1] // n_seq
    nbp = ovl.shape[1]
    n_sub = math.gcd(n_seq, 4)
    cmp_spec = pl.BlockSpec((KV_HEADS, n_sub * ncp, LANES), lambda b: (0, b, 0))
    row_spec = pl.BlockSpec((n_sub, N_HEADS, LANES), lambda b: (b, 0, 0))
    full = lambda a: pl.BlockSpec(a.shape, lambda b: (0,) * a.ndim)
    return pl.pallas_call(
        functools.partial(_sample_select_body, ncp=ncp, qpos=qpos, nbp=nbp, n_sub=n_sub),
        out_shape=(jax.ShapeDtypeStruct((n_seq, N_HEADS, LANES), F32),
                   jax.ShapeDtypeStruct((n_seq, SUBLANES, LANES), I32)),
        grid=(n_seq // n_sub,),
        in_specs=[row_spec, cmp_spec, pl.BlockSpec((KV_HEADS, n_sub, LANES, ncp), lambda b: (0, b, 0, 0)),
                  full(ovl), full(tri)],
        out_specs=(row_spec, pl.BlockSpec((n_sub, SUBLANES, LANES), lambda b: (b, 0, 0))),
        compiler_params=_cparams(("parallel",)),
        name="nsa_sample_select",
    )(q8, kc, vc_t, ovl, tri)


def _sample_attend_body(idx_ref, pt_ref, q_ref, ocmp_ref, gate_ref, ksn_ref, vsn_ref, kwn_ref, vwn_ref,
                        win_ref, *refs, n_cache_blocks, blocks_per_page, win_skip):
    del pt_ref
    n_blk = KV_HEADS * TOP_N
    kv_refs, o_ref = refs[:n_blk], refs[n_blk]
    b = pl.program_id(0)
    q = q_ref[...]
    qf = q.astype(F32)
    row_g = (lax.broadcasted_iota(I32, (N_HEADS, 1), 0) >= Q_PER_KV).astype(I32)

    def attend(s_list, v_list, kn_ref, vn_ref):
        s_self = jnp.sum(qf * kn_ref[...].astype(F32), axis=-1, keepdims=True)
        m = s_self
        for s in s_list:
            m = jnp.maximum(m, jnp.max(s, axis=-1, keepdims=True))
        p_self = jnp.exp2(s_self - m)
        l = p_self
        acc = p_self.astype(BF16).astype(F32) * vn_ref[...].astype(F32)
        for s, v in zip(s_list, v_list):
            p = jnp.exp2(s - m)
            l = l + jnp.sum(p, axis=-1, keepdims=True)
            acc = acc + _dot_nt(p.astype(BF16), v().astype(BF16))
        return acc / l

    s_list, v_list = [], []
    for j in range(n_blk):
        s = _dot(q, kv_refs[j][0].astype(BF16))
        col = lax.broadcasted_iota(I32, s.shape, 1)
        n = idx_ref[b, j]
        first = (n & (blocks_per_page - 1)) * SLC_BLOCK
        ok = ((row_g == j // TOP_N) & (col >= first) & (col < first + SLC_BLOCK)
              & (n >= 0) & (n < n_cache_blocks))
        s_list.append(jnp.where(ok, s, NEG_INF))
        v_list.append(lambda j=j: kv_refs[j][1])
    o_slc = attend(s_list, v_list, ksn_ref, vsn_ref)

    s_list, v_list = [], []
    for g in range(KV_HEADS):
        s = _dot(q, win_ref[0, g].astype(BF16))
        col = lax.broadcasted_iota(I32, s.shape, 1)
        s_list.append(jnp.where((row_g == g) & (col >= win_skip), s, NEG_INF))
        v_list.append(lambda g=g: win_ref[1, g])
    o_win = attend(s_list, v_list, kwn_ref, vwn_ref)

    gates = gate_ref[...]
    o_ref[...] = gates[:, 0:1] * ocmp_ref[:, 0:HEAD_DIM] + gates[:, 1:2] * o_slc + gates[:, 2:3] * o_win


def _sample_attend_call(idx, page_table, q64, ocmp, gates8, ksn, vsn, kwn, vwn, cache_win_t, cache_t, win_skip):
    n_seq, n_pages = page_table.shape
    page_rows = cache_t.shape[-1]
    bpp = page_rows // SLC_BLOCK
    n_cache_blocks = n_pages * bpp
    row_spec = lambda a: pl.BlockSpec((None,) + a.shape[1:], lambda b, ix, pt: (b,) + (0,) * (a.ndim - 1))

    assert bpp & (bpp - 1) == 0
    bpp_shift = bpp.bit_length() - 1

    def blk_spec(j):
        def index_map(b, ix, pt):
            n = jnp.minimum(jnp.maximum(ix[b, j], 0), n_cache_blocks - 1)
            return (pt[b, lax.shift_right_logical(n, bpp_shift)], 1, j // TOP_N, 0, 0)
        return pl.BlockSpec((None, 2, None, HEAD_DIM, page_rows), index_map)

    n_blk = KV_HEADS * TOP_N
    small = (q64, ocmp, gates8, ksn, vsn, kwn, vwn, cache_win_t)
    return pl.pallas_call(
        functools.partial(_sample_attend_body, n_cache_blocks=n_cache_blocks, blocks_per_page=bpp,
                          win_skip=win_skip),
        out_shape=jax.ShapeDtypeStruct((n_seq, N_HEADS, HEAD_DIM), F32),
        grid_spec=pltpu.PrefetchScalarGridSpec(
            num_scalar_prefetch=2,
            grid=(n_seq,),
            in_specs=[row_spec(a) for a in small] + [blk_spec(j) for j in range(n_blk)],
            out_specs=pl.BlockSpec((None, N_HEADS, HEAD_DIM), lambda b, ix, pt: (b, 0, 0))),
        compiler_params=_cparams(("parallel",)),
        name="nsa_sample_attend",
    )(idx, page_table, *small, *([cache_t] * n_blk))


def _round_up(x, m):
    return -(-x // m) * m


def _prompt_layer(h, prm, cw, sw, batch, seq):
    tabs = _rope_tables(jnp.arange(seq))
    u, gs, gn, q_t, kv_t, win_t, gates_t, ks, vs_t, kw, vw_t = _proj_call(
        h, prm["w_in"], prm["norm_w"], prm["qnw"], prm["knw"], prm["gb"], tabs, batch, seq)
    y_ssm, h_re, h_im = _s5_prompt_call(u, sw, batch, seq)
    kc, vc_t = _compress_prompt_call(kv_t, cw, batch, seq)
    nch = seq // CMP_STRIDE
    ovl_t = _overlap_matrix(nch, nch - 1, seq // SLC_BLOCK).T
    o = _attn_call(q_t, kc, vc_t, ovl_t, ks, vs_t, kw, vw_t, gates_t, batch, seq)
    h_new = _outmix_call(h, y_ssm, gs, o, gn, prm["w_glu"], prm["w_out"])
    rows = lambda x_t, slots: x_t.reshape(batch, slots, KV_HEADS, HEAD_DIM, seq).transpose(0, 4, 1, 2, 3)
    return h_new, rows(kv_t, 4), rows(win_t, 2)[:, seq - min(WINDOW, seq):], h_re, h_im


def _sample_layer(h, prm, cw, sw, cache_kv, cache_win, st_re, st_im, page_table):
    n_seq = h.shape[0]
    n_phys, page_rows = cache_kv.shape[:2]
    n_pages = page_table.shape[1]
    past_len = n_pages * page_rows
    win_buf = cache_win.shape[1]
    n_pad = _round_up(n_seq, LANES)
    tabs = _rope_tables(jnp.full((n_pad,), past_len, I32))
    h_pad = jnp.pad(h, ((0, n_pad - n_seq), (0, 0)))
    u, gs, gn, q_t, kv_t, win_t, gates_t, ks, vs_t, kw, vw_t = _proj_call(
        h_pad, prm["w_in"], prm["norm_w"], prm["qnw"], prm["knw"], prm["gb"], tabs, 1, n_pad)
    u, gs, gn = u[:, :n_seq], gs[:n_seq], gn[:n_seq]
    n_state = st_re.shape[1] * st_re.shape[2]
    y_ssm, h_re, h_im = _s5_step_call(u, st_re.reshape(n_seq, n_state), st_im.reshape(n_seq, n_state), sw)
    cache_t = cache_kv.transpose(0, 2, 3, 4, 1)
    kc, vc_t = _compress_sample_call(cache_t, page_table, cw)
    ncp = past_len // CMP_STRIDE
    n_blk = -(-(past_len + 1) // SLC_BLOCK)
    nbp = _round_up(n_blk, LANES)
    ovl = _overlap_matrix(ncp, ncp - 1, n_blk, nbp)
    tri = (jnp.arange(nbp)[:, None] < jnp.arange(nbp)[None, :]).astype(BF16)
    q8 = q_t[:, :, :n_seq].transpose(2, 0, 1)
    ocmp, idx = _sample_select_call(q8, kc, vc_t, ovl, tri, past_len)
    idx = idx[:, :KV_HEADS, :TOP_N].reshape(n_seq, KV_HEADS * TOP_N)
    gates8 = gates_t[:, :3 * Q_PER_KV, :n_seq].reshape(KV_HEADS, Q_PER_KV, 3, n_seq).transpose(3, 0, 1, 2)
    gates8 = jnp.pad(gates8.reshape(n_seq, N_HEADS, 3), ((0, 0), (0, 0), (0, LANES - 3)))
    per_head = lambda a: jnp.repeat(a.transpose(1, 0, 2), Q_PER_KV, axis=1)
    new_k = lambda k: per_head(k[:, :n_seq, HEAD_DIM:])
    new_v = lambda v_t: per_head(v_t[:, 0, :HEAD_DIM, :n_seq].transpose(0, 2, 1))
    o8 = _sample_attend_call(idx, page_table, q8[:, :, HEAD_DIM:], ocmp, gates8, new_k(ks), new_v(vs_t),
                             new_k(kw), new_v(vw_t), cache_win.transpose(0, 2, 3, 4, 1), cache_t,
                             max(win_buf + 1 - WINDOW, 0))
    o = o8.reshape(n_seq, NSA_W)
    h_new = _outmix_call(h, y_ssm, gs, o, gn, prm["w_glu"], prm["w_out"])
    kv_rows = kv_t[0, :, :n_seq].T.reshape(n_seq, 1, 4, KV_HEADS, HEAD_DIM)
    win_new = win_t[0, :, :n_seq].T.reshape(n_seq, 1, 2, KV_HEADS, HEAD_DIM)
    wrows = jnp.concatenate([cache_win, win_new], axis=1)
    wrows = wrows[:, wrows.shape[1] - min(WINDOW, wrows.shape[1]):]
    state = lambda s: s.reshape(st_re.shape)
    return h_new, kv_rows, wrows, state(h_re), state(h_im)


def kernel(x_prompt, x_sample, cache_kv, cache_win, state_ssm_re, state_ssm_im, page_table, norm_w, w_in, gate_b,
           q_norm_w, k_norm_w, cmp_pe, cmp_w1, cmp_b1, cmp_w2, ssm_lam_re, ssm_lam_im, ssm_log_step, ssm_b_re,
           ssm_b_im, ssm_c_re, ssm_c_im, ssm_d, w_glu, w_out):
    p = dict(norm_w=norm_w, w_in=w_in, gate_b=gate_b, q_norm_w=q_norm_w, k_norm_w=k_norm_w, cmp_pe=cmp_pe,
             cmp_w1=cmp_w1, cmp_b1=cmp_b1, cmp_w2=cmp_w2, ssm_lam_re=ssm_lam_re, ssm_lam_im=ssm_lam_im,
             ssm_log_step=ssm_log_step, ssm_b_re=ssm_b_re, ssm_b_im=ssm_b_im, ssm_c_re=ssm_c_re,
             ssm_c_im=ssm_c_im, ssm_d=ssm_d, w_glu=w_glu, w_out=w_out)
    b_p, s_p, d_model = x_prompt.shape
    b_s, s_s, _ = x_sample.shape
    assert s_s == 1, "the sample group decodes one token per sequence"
    h_p = x_prompt.reshape(b_p * s_p, d_model)
    h_s = x_sample.reshape(b_s, d_model)
    outs_p, outs_s = [], []
    for l in range(norm_w.shape[0]):
        weights = (_prep_params(p, l), _prep_compress(p, l), _prep_s5_chunked(p, l))
        h_p, *rest_p = _prompt_layer(h_p, *weights, b_p, s_p)
        h_s, *rest_s = _sample_layer(h_s, *weights, cache_kv[l], cache_win[l], state_ssm_re[l],
                                     state_ssm_im[l], page_table)
        outs_p.append(rest_p)
        outs_s.append(rest_s)
    stack = lambda outs, i: jnp.stack([o[i] for o in outs])
    return (h_p.reshape(x_prompt.shape), h_s.reshape(x_sample.shape),
            stack(outs_p, 0), stack(outs_s, 0), stack(outs_p, 1), stack(outs_s, 1),
            stack(outs_p, 2), stack(outs_p, 3), stack(outs_s, 2), stack(outs_s, 3))
```
